```python
import jax, jax.numpy as jnp
from jax import lax
import numpy as np

D_MODEL = 1024
BATCH = 8
SEQ = 4096
DEPTH = 1

CHUNK = 64
SB_BLOCK = 128
PLE_DIM = 256
MIX_WIDTH = D_MODEL
SB_WIDTH = MIX_WIDTH // 2
ML_WIDTH = MIX_WIDTH - SB_WIDTH
SB_HEADS = 8
SB_HEAD_DIM = SB_WIDTH // SB_HEADS
ML_HEADS = 4
ML_HEAD_DIM = ML_WIDTH // ML_HEADS
CONV_K = 4
EPS = 1e-6
SPLIT_SIZES = (SB_WIDTH, SB_WIDTH, SB_WIDTH, SB_WIDTH, ML_WIDTH, ML_WIDTH, ML_WIDTH, ML_WIDTH, ML_WIDTH, ML_HEADS, ML_HEADS)
N_IN = 4 * SB_WIDTH + 5 * ML_WIDTH + 2 * ML_HEADS

kernel_name = 'hybrid_stickbreaking_mlstm_block'


def _rmsnorm(x, w):
    xf = x.astype(jnp.float32)
    y = xf * lax.rsqrt(jnp.mean(xf * xf, axis=-1, keepdims=True) + EPS)
    return (y * w.astype(jnp.float32)).astype(x.dtype)


def _head_rmsnorm(y, w, n_heads):
    b, s, width = y.shape
    yf = y.astype(jnp.float32).reshape(b, s, n_heads, width // n_heads)
    yf = yf * lax.rsqrt(jnp.mean(yf * yf, axis=-1, keepdims=True) + EPS)
    return (yf.reshape(b, s, width) * w.astype(jnp.float32)).astype(y.dtype)


def _split_cols(u):
    bounds = []
    acc = 0
    for size in SPLIT_SIZES[:-1]:
        acc += size
        bounds.append(acc)
    return jnp.split(u, bounds, axis=-1)


def _split_heads(u, n_heads):
    b, s, width = u.shape
    return u.reshape(b, s, n_heads, width // n_heads).transpose(0, 2, 1, 3)


def _merge_heads(u):
    b, h, s, d = u.shape
    return u.transpose(0, 2, 1, 3).reshape(b, s, h * d)


def _causal_conv(u, w, bias):
    s = u.shape[1]
    k = w.shape[0]
    up = jnp.pad(u, ((0, 0), (k - 1, 0), (0, 0)))
    out = bias
    for j in range(k):
        out = out + w[j] * up[:, j:j + s]
    return out


def _stick_breaking_attention(q, k, v):
    s = q.shape[2]
    d = q.shape[3]
    scale = d ** -0.5
    outs = []
    for blk in range(s // SB_BLOCK):
        q0 = blk * SB_BLOCK
        k_end = q0 + SB_BLOCK
        qb = q[:, :, q0:k_end]
        kb = k[:, :, :k_end]
        vb = v[:, :, :k_end]
        z = jnp.einsum('bhqd,bhkd->bhqk', qb, kb).astype(jnp.float32) * scale
        t_pos = q0 + jnp.arange(SB_BLOCK)[:, None]
        s_pos = jnp.arange(k_end)[None, :]
        strict = s_pos < t_pos
        log_beta = jax.nn.log_sigmoid(z)
        log_keep = jnp.where(strict, log_beta - z, 0.0)
        after = lax.cumsum(log_keep, axis=3, reverse=True) - log_keep
        a = jnp.where(strict, jnp.exp(log_beta + after), 0.0)
        outs.append(jnp.einsum('bhqk,bhkd->bhqd', a, vb.astype(jnp.float32)))
    return jnp.concatenate(outs, axis=2).astype(q.dtype)


def _mlstm_chunkwise(q, k, v, i_pre, f_pre):
    out_dtype = q.dtype
    f32 = jnp.float32
    bsz, nh, s, d = q.shape
    nc = s // CHUNK

    def chunks(t):
        return t.astype(f32).reshape(bsz, nh, nc, CHUNK, d)

    q = chunks(q)
    k = chunks(k) * (d ** -0.5)
    v = chunks(v)
    ig = i_pre.astype(f32).transpose(0, 2, 1).reshape(bsz, nh, nc, CHUNK)
    lf = jax.nn.log_sigmoid(f_pre.astype(f32)).transpose(0, 2, 1).reshape(bsz, nh, nc, CHUNK)
    b = jnp.cumsum(lf, axis=-1)
    b_last = b[..., -1]

    g = b_last[..., None] - b + ig
    m_loc = jnp.max(g, axis=-1)
    w_loc = jnp.exp(g - m_loc[..., None])
    c_loc = jnp.einsum('bhcs,bhcsv,bhcsd->bhcvd', w_loc, v, k)
    n_loc = jnp.einsum('bhcs,bhcsd->bhcd', w_loc, k)

    def step(carry, xs):
        c_st, n_st, m_st = carry
        bl, ml, cl, nl = xs
        m_new = jnp.maximum(bl + m_st, ml)
        a = jnp.exp(bl + m_st - m_new)
        gg = jnp.exp(ml - m_new)
        c_new = a[..., None, None] * c_st + gg[..., None, None] * cl
        n_new = a[..., None] * n_st + gg[..., None] * nl
        return (c_new, n_new, m_new), (c_st, n_st, m_st)

    init = (jnp.zeros((bsz, nh, d, d), f32), jnp.zeros((bsz, nh, d), f32), jnp.zeros((bsz, nh), f32))
    xs = (jnp.moveaxis(b_last, 2, 0), jnp.moveaxis(m_loc, 2, 0), jnp.moveaxis(c_loc, 2, 0), jnp.moveaxis(n_loc, 2, 0))
    _, (c_prev, n_prev, m_prev) = lax.scan(step, init, xs)
    c_prev = jnp.moveaxis(c_prev, 0, 2)
    n_prev = jnp.moveaxis(n_prev, 0, 2)
    m_prev = jnp.moveaxis(m_prev, 0, 2)

    causal = jnp.tril(jnp.ones((CHUNK, CHUNK), dtype=bool))
    d_log = jnp.where(causal, b[..., :, None] - b[..., None, :] + ig[..., None, :], -jnp.inf)
    m_t = jnp.maximum(b + m_prev[..., None], jnp.max(d_log, axis=-1))
    w_intra = jnp.exp(d_log - m_t[..., None])
    scores = jnp.einsum('bhcld,bhcsd->bhcls', q, k) * w_intra
    carry_scale = jnp.exp(b + m_prev[..., None] - m_t)
    num = jnp.einsum('bhcls,bhcsv->bhclv', scores, v) + carry_scale[..., None] * jnp.einsum('bhcld,bhcvd->bhclv', q, c_prev)
    den = jnp.sum(scores, axis=-1) + carry_scale * jnp.einsum('bhcld,bhcd->bhcl', q, n_prev)
    h = num / jnp.maximum(jnp.abs(den), jnp.exp(-m_t))[..., None]
    return h.reshape(bsz, nh, s, d).astype(out_dtype)


def _fwd_setup_inputs(seed: int = 0) -> dict:
    key = jax.random.key(seed)
    ks = jax.random.split(key, 15)
    nrm = jax.random.normal
    f32 = jnp.float32
    return {
        'x': nrm(ks[0], (BATCH, SEQ, D_MODEL), f32),
        'p': nrm(ks[1], (DEPTH, BATCH, SEQ, PLE_DIM), f32),
        'pre_norm_w': 1.0 + 0.05 * nrm(ks[2], (DEPTH, D_MODEL), f32),
        'w_in': nrm(ks[3], (DEPTH, D_MODEL, N_IN), f32) * D_MODEL ** -0.5,
        'ml_conv_w': nrm(ks[4], (DEPTH, CONV_K, 2 * ML_WIDTH), f32) * CONV_K ** -0.5,
        'ml_conv_b': 0.02 * nrm(ks[5], (DEPTH, 2 * ML_WIDTH), f32),
        'ml_i_bias': 0.1 * nrm(ks[6], (DEPTH, ML_HEADS), f32),
        'ml_f_bias': jnp.linspace(3.0, 6.0, ML_HEADS, dtype=f32)[None, :] + 0.01 * nrm(ks[7], (DEPTH, ML_HEADS), f32),
        'sb_norm_w': 1.0 + 0.05 * nrm(ks[8], (DEPTH, SB_WIDTH), f32),
        'ml_norm_w': 1.0 + 0.05 * nrm(ks[9], (DEPTH, ML_WIDTH), f32),
        'w_out': nrm(ks[10], (DEPTH, MIX_WIDTH, D_MODEL), f32) * MIX_WIDTH ** -0.5,
        'post_norm_w': 1.0 + 0.05 * nrm(ks[11], (DEPTH, D_MODEL), f32),
        'ple_w_up': nrm(ks[12], (DEPTH, PLE_DIM, D_MODEL), f32) * PLE_DIM ** -0.5,
        'ple_w_gate': nrm(ks[13], (DEPTH, D_MODEL, D_MODEL), f32) * D_MODEL ** -0.5,
        'ple_b_gate': 0.02 * nrm(ks[14], (DEPTH, D_MODEL), f32),
    }


def _fwd_reference(x, p, pre_norm_w, w_in, ml_conv_w, ml_conv_b, ml_i_bias, ml_f_bias, sb_norm_w, ml_norm_w, w_out, post_norm_w, ple_w_up, ple_w_gate, ple_b_gate):
    h = x
    for i in range(DEPTH):
        u = _rmsnorm(h, pre_norm_w[i])
        proj = u @ w_in[i]
        sb_q, sb_k, sb_v, sb_z, ml_q, ml_k, ml_v, ml_o, ml_z, ml_i, ml_f = _split_cols(proj)

        y_sb = _stick_breaking_attention(_split_heads(sb_q, SB_HEADS), _split_heads(sb_k, SB_HEADS), _split_heads(sb_v, SB_HEADS))
        y_sb = _head_rmsnorm(_merge_heads(y_sb), sb_norm_w[i], SB_HEADS) * jax.nn.silu(sb_z)

        qk = jax.nn.silu(_causal_conv(jnp.concatenate([ml_q, ml_k], axis=-1), ml_conv_w[i], ml_conv_b[i]))
        ml_q, ml_k = jnp.split(qk, 2, axis=-1)
        y_ml = _mlstm_chunkwise(_split_heads(ml_q, ML_HEADS), _split_heads(ml_k, ML_HEADS), _split_heads(ml_v, ML_HEADS), ml_i + ml_i_bias[i], ml_f + ml_f_bias[i])
        y_ml = jax.nn.sigmoid(ml_o) * _merge_heads(y_ml)
        y_ml = _head_rmsnorm(y_ml, ml_norm_w[i], ML_HEADS) * jax.nn.silu(ml_z)

        y = jnp.concatenate([y_sb, y_ml], axis=-1) @ w_out[i]
        h = h + _rmsnorm(y, post_norm_w[i])

        gate = jax.nn.sigmoid(h @ ple_w_gate[i] + ple_b_gate[i])
        h = h + gate * (p[i] @ ple_w_up[i])
    return h


import jax as _jax
import jax.numpy as _jnp

TWIN_FORMAT = 'train_step'
FWD_PARAMS = ['x', 'p', 'pre_norm_w', 'w_in', 'ml_conv_w', 'ml_conv_b', 'ml_i_bias', 'ml_f_bias', 'sb_norm_w', 'ml_norm_w', 'w_out', 'post_norm_w', 'ple_w_up', 'ple_w_gate', 'ple_b_gate']
TWIN_WEIGHTS = ['pre_norm_w', 'w_in', 'ml_conv_w', 'ml_conv_b', 'ml_i_bias', 'ml_f_bias', 'sb_norm_w', 'ml_norm_w', 'w_out', 'post_norm_w', 'ple_w_up', 'ple_w_gate', 'ple_b_gate']
TWIN_DIFF_INPUT = 'x'
TWIN_INPUTS = ['x', 'p', 'pre_norm_w', 'w_in', 'ml_conv_w', 'ml_conv_b', 'ml_i_bias', 'ml_f_bias', 'sb_norm_w', 'ml_norm_w', 'w_out', 'post_norm_w', 'ple_w_up', 'ple_w_gate', 'ple_b_gate', 'loss_target', 'm_pre_norm_w', 'm_w_in', 'm_ml_conv_w', 'm_ml_conv_b', 'm_ml_i_bias', 'm_ml_f_bias', 'm_sb_norm_w', 'm_ml_norm_w', 'm_w_out', 'm_post_norm_w', 'm_ple_w_up', 'm_ple_w_gate', 'm_ple_b_gate', 'v_pre_norm_w', 'v_w_in', 'v_ml_conv_w', 'v_ml_conv_b', 'v_ml_i_bias', 'v_ml_f_bias', 'v_sb_norm_w', 'v_ml_norm_w', 'v_w_out', 'v_post_norm_w', 'v_ple_w_up', 'v_ple_w_gate', 'v_ple_b_gate']
TWIN_OUTPUTS = ['loss', 'grad_x', 'grad_pre_norm_w', 'grad_w_in', 'grad_ml_conv_w', 'grad_ml_conv_b', 'grad_ml_i_bias', 'grad_ml_f_bias', 'grad_sb_norm_w', 'grad_ml_norm_w', 'grad_w_out', 'grad_post_norm_w', 'grad_ple_w_up', 'grad_ple_w_gate', 'grad_ple_b_gate', 'delta_pre_norm_w', 'delta_w_in', 'delta_ml_conv_w', 'delta_ml_conv_b', 'delta_ml_i_bias', 'delta_ml_f_bias', 'delta_sb_norm_w', 'delta_ml_norm_w', 'delta_w_out', 'delta_post_norm_w', 'delta_ple_w_up', 'delta_ple_w_gate', 'delta_ple_b_gate', 'new_m_pre_norm_w', 'new_m_w_in', 'new_m_ml_conv_w', 'new_m_ml_conv_b', 'new_m_ml_i_bias', 'new_m_ml_f_bias', 'new_m_sb_norm_w', 'new_m_ml_norm_w', 'new_m_w_out', 'new_m_post_norm_w', 'new_m_ple_w_up', 'new_m_ple_w_gate', 'new_m_ple_b_gate', 'new_v_pre_norm_w', 'new_v_w_in', 'new_v_ml_conv_w', 'new_v_ml_conv_b', 'new_v_ml_i_bias', 'new_v_ml_f_bias', 'new_v_sb_norm_w', 'new_v_ml_norm_w', 'new_v_w_out', 'new_v_post_norm_w', 'new_v_ple_w_up', 'new_v_ple_w_gate', 'new_v_ple_b_gate']
TWIN_LEAF_KINDS = {'loss': 'loss', 'grad_x': 'grad_x', 'grad_pre_norm_w': 'grad_w', 'grad_w_in': 'grad_w', 'grad_ml_conv_w': 'grad_w', 'grad_ml_conv_b': 'grad_w', 'grad_ml_i_bias': 'grad_w', 'grad_ml_f_bias': 'grad_w', 'grad_sb_norm_w': 'grad_w', 'grad_ml_norm_w': 'grad_w', 'grad_w_out': 'grad_w', 'grad_post_norm_w': 'grad_w', 'grad_ple_w_up': 'grad_w', 'grad_ple_w_gate': 'grad_w', 'grad_ple_b_gate': 'grad_w', 'delta_pre_norm_w': 'delta_w', 'delta_w_in': 'delta_w', 'delta_ml_conv_w': 'delta_w', 'delta_ml_conv_b': 'delta_w', 'delta_ml_i_bias': 'delta_w', 'delta_ml_f_bias': 'delta_w', 'delta_sb_norm_w': 'delta_w', 'delta_ml_norm_w': 'delta_w', 'delta_w_out': 'delta_w', 'delta_post_norm_w': 'delta_w', 'delta_ple_w_up': 'delta_w', 'delta_ple_w_gate': 'delta_w', 'delta_ple_b_gate': 'delta_w', 'new_m_pre_norm_w': 'new_m', 'new_m_w_in': 'new_m', 'new_m_ml_conv_w': 'new_m', 'new_m_ml_conv_b': 'new_m', 'new_m_ml_i_bias': 'new_m', 'new_m_ml_f_bias': 'new_m', 'new_m_sb_norm_w': 'new_m', 'new_m_ml_norm_w': 'new_m', 'new_m_w_out': 'new_m', 'new_m_post_norm_w': 'new_m', 'new_m_ple_w_up': 'new_m', 'new_m_ple_w_gate': 'new_m', 'new_m_ple_b_gate': 'new_m', 'new_v_pre_norm_w': 'new_v', 'new_v_w_in': 'new_v', 'new_v_ml_conv_w': 'new_v', 'new_v_ml_conv_b': 'new_v', 'new_v_ml_i_bias': 'new_v', 'new_v_ml_f_bias': 'new_v', 'new_v_sb_norm_w': 'new_v', 'new_v_ml_norm_w': 'new_v', 'new_v_w_out': 'new_v', 'new_v_post_norm_w': 'new_v', 'new_v_ple_w_up': 'new_v', 'new_v_ple_w_gate': 'new_v', 'new_v_ple_b_gate': 'new_v'}


def _forward(args):
    return _fwd_reference(*[args[k] for k in FWD_PARAMS])


def _output_shape():
    out = _jax.eval_shape(lambda: _forward(_fwd_setup_inputs(0)))
    return out.shape, out.dtype

N_MICROBATCH = 1
ADAM_LR = 0.001
ADAM_B1 = 0.9
ADAM_B2 = 0.999
ADAM_EPS = 1e-08
ADAM_WD = 0.01
ADAM_STEP = 10
PER_EXAMPLE_BATCH_AXIS = {'x': 0, 'p': 1, 'loss_target': 0}
SHARED_INPUTS = []
_WEIGHT_DTYPES = {'pre_norm_w': _jnp.float32, 'w_in': _jnp.float32, 'ml_conv_w': _jnp.float32, 'ml_conv_b': _jnp.float32, 'ml_i_bias': _jnp.float32, 'ml_f_bias': _jnp.float32, 'sb_norm_w': _jnp.float32, 'ml_norm_w': _jnp.float32, 'w_out': _jnp.float32, 'post_norm_w': _jnp.float32, 'ple_w_up': _jnp.float32, 'ple_w_gate': _jnp.float32, 'ple_b_gate': _jnp.float32}
MOMENT_SCALE = {'pre_norm_w': 5.196910e-01, 'w_in': 2.457846e-01, 'ml_conv_w': 2.172406e-01, 'ml_conv_b': 1.941394e-01, 'ml_i_bias': 1.746743e-02, 'ml_f_bias': 9.927712e-01, 'sb_norm_w': 2.765298e-01, 'ml_norm_w': 3.869963e-01, 'w_out': 3.057350e-01, 'post_norm_w': 3.274073e+01, 'ple_w_up': 4.550891e-01, 'ple_w_gate': 1.280639e-01, 'ple_b_gate': 2.912465e+00}


def _to_microbatches(a, axis):
    t = _jnp.moveaxis(a, axis, 0)
    t = t.reshape((N_MICROBATCH, t.shape[0] // N_MICROBATCH) + t.shape[1:])
    return _jnp.moveaxis(t, 1, axis + 1)


def setup_inputs(seed: int = 0) -> dict:
    inp = _fwd_setup_inputs(seed)
    key = _jax.random.fold_in(_jax.random.key(seed), 7919)
    shape, _ = _output_shape()
    out = dict(inp)
    out["loss_target"] = _jax.random.normal(_jax.random.fold_in(key, 0), shape, _jnp.float32)
    for i, name in enumerate(TWIN_WEIGHTS):
        w = inp[name].astype(_jnp.float32)
        if MOMENT_SCALE is None:
            s = _jnp.sqrt(_jnp.mean(_jnp.square(w)) + 1e-30)
        else:
            s = MOMENT_SCALE[name]
        km, kv = _jax.random.split(_jax.random.fold_in(key, i + 1))
        out[name] = w
        out["m_" + name] = s * _jax.random.normal(km, w.shape, _jnp.float32)
        out["v_" + name] = (s * s) * _jax.random.uniform(kv, w.shape, _jnp.float32, 0.5, 1.5)
    if N_MICROBATCH > 1:
        for name, axis in PER_EXAMPLE_BATCH_AXIS.items():
            out[name] = _to_microbatches(out[name], axis)
    return {'x': out['x'], 'p': out['p'], 'pre_norm_w': out['pre_norm_w'], 'w_in': out['w_in'], 'ml_conv_w': out['ml_conv_w'], 'ml_conv_b': out['ml_conv_b'], 'ml_i_bias': out['ml_i_bias'], 'ml_f_bias': out['ml_f_bias'], 'sb_norm_w': out['sb_norm_w'], 'ml_norm_w': out['ml_norm_w'], 'w_out': out['w_out'], 'post_norm_w': out['post_norm_w'], 'ple_w_up': out['ple_w_up'], 'ple_w_gate': out['ple_w_gate'], 'ple_b_gate': out['ple_b_gate'], 'loss_target': out['loss_target'], 'm_pre_norm_w': out['m_pre_norm_w'], 'm_w_in': out['m_w_in'], 'm_ml_conv_w': out['m_ml_conv_w'], 'm_ml_conv_b': out['m_ml_conv_b'], 'm_ml_i_bias': out['m_ml_i_bias'], 'm_ml_f_bias': out['m_ml_f_bias'], 'm_sb_norm_w': out['m_sb_norm_w'], 'm_ml_norm_w': out['m_ml_norm_w'], 'm_w_out': out['m_w_out'], 'm_post_norm_w': out['m_post_norm_w'], 'm_ple_w_up': out['m_ple_w_up'], 'm_ple_w_gate': out['m_ple_w_gate'], 'm_ple_b_gate': out['m_ple_b_gate'], 'v_pre_norm_w': out['v_pre_norm_w'], 'v_w_in': out['v_w_in'], 'v_ml_conv_w': out['v_ml_conv_w'], 'v_ml_conv_b': out['v_ml_conv_b'], 'v_ml_i_bias': out['v_ml_i_bias'], 'v_ml_f_bias': out['v_ml_f_bias'], 'v_sb_norm_w': out['v_sb_norm_w'], 'v_ml_norm_w': out['v_ml_norm_w'], 'v_w_out': out['v_w_out'], 'v_post_norm_w': out['v_post_norm_w'], 'v_ple_w_up': out['v_ple_w_up'], 'v_ple_w_gate': out['v_ple_w_gate'], 'v_ple_b_gate': out['v_ple_b_gate']}


def _loss(weights, diff, rest, loss_target):
    with _jax.named_scope("forward"):
        args = {**rest, TWIN_DIFF_INPUT: diff, **{k: w.astype(_WEIGHT_DTYPES[k]) for k, w in weights.items()}}
        y = _forward(args)
    with _jax.named_scope("loss_head"):
        err = _jnp.square(y.astype(_jnp.float32) - loss_target)
        return 0.5 * _jnp.sum(_jnp.mean(err, axis=-1)) if err.ndim else 0.5 * err


def _adamw(w, g, m, v):
    m = ADAM_B1 * m + (1.0 - ADAM_B1) * g
    v = ADAM_B2 * v + (1.0 - ADAM_B2) * _jnp.square(g)
    m_hat = m / (1.0 - ADAM_B1 ** ADAM_STEP)
    v_hat = v / (1.0 - ADAM_B2 ** ADAM_STEP)
    delta = -ADAM_LR * (m_hat / (_jnp.sqrt(v_hat) + ADAM_EPS) + ADAM_WD * w)
    return delta, m, v


def reference(x, p, pre_norm_w, w_in, ml_conv_w, ml_conv_b, ml_i_bias, ml_f_bias, sb_norm_w, ml_norm_w, w_out, post_norm_w, ple_w_up, ple_w_gate, ple_b_gate, loss_target, m_pre_norm_w, m_w_in, m_ml_conv_w, m_ml_conv_b, m_ml_i_bias, m_ml_f_bias, m_sb_norm_w, m_ml_norm_w, m_w_out, m_post_norm_w, m_ple_w_up, m_ple_w_gate, m_ple_b_gate, v_pre_norm_w, v_w_in, v_ml_conv_w, v_ml_conv_b, v_ml_i_bias, v_ml_f_bias, v_sb_norm_w, v_ml_norm_w, v_w_out, v_post_norm_w, v_ple_w_up, v_ple_w_gate, v_ple_b_gate):
    given = dict(x=x, p=p, pre_norm_w=pre_norm_w, w_in=w_in, ml_conv_w=ml_conv_w, ml_conv_b=ml_conv_b, ml_i_bias=ml_i_bias, ml_f_bias=ml_f_bias, sb_norm_w=sb_norm_w, ml_norm_w=ml_norm_w, w_out=w_out, post_norm_w=post_norm_w, ple_w_up=ple_w_up, ple_w_gate=ple_w_gate, ple_b_gate=ple_b_gate, loss_target=loss_target, m_pre_norm_w=m_pre_norm_w, m_w_in=m_w_in, m_ml_conv_w=m_ml_conv_w, m_ml_conv_b=m_ml_conv_b, m_ml_i_bias=m_ml_i_bias, m_ml_f_bias=m_ml_f_bias, m_sb_norm_w=m_sb_norm_w, m_ml_norm_w=m_ml_norm_w, m_w_out=m_w_out, m_post_norm_w=m_post_norm_w, m_ple_w_up=m_ple_w_up, m_ple_w_gate=m_ple_w_gate, m_ple_b_gate=m_ple_b_gate, v_pre_norm_w=v_pre_norm_w, v_w_in=v_w_in, v_ml_conv_w=v_ml_conv_w, v_ml_conv_b=v_ml_conv_b, v_ml_i_bias=v_ml_i_bias, v_ml_f_bias=v_ml_f_bias, v_sb_norm_w=v_sb_norm_w, v_ml_norm_w=v_ml_norm_w, v_w_out=v_w_out, v_post_norm_w=v_post_norm_w, v_ple_w_up=v_ple_w_up, v_ple_w_gate=v_ple_w_gate, v_ple_b_gate=v_ple_b_gate)
    weights = {n: given[n] for n in TWIN_WEIGHTS}
    shared = {n: given[n] for n in SHARED_INPUTS}
    per_example = {n: given[n] for n in ['x', 'p']}
    grad_fn = _jax.value_and_grad(_loss, argnums=(0, 1))

    def one_microbatch(ex, loss_target):
        ex = dict(ex)
        diff = ex.pop(TWIN_DIFF_INPUT)
        return grad_fn(weights, diff, {**shared, **ex}, loss_target)

    if N_MICROBATCH == 1:
        loss, (grad_w, grad_x) = one_microbatch(per_example, given["loss_target"])
    else:
        def body(carry, xs):
            loss_sum, grad_sum = carry
            l_k, (gw_k, gx_k) = one_microbatch(xs[0], xs[1])
            with _jax.named_scope("update"):
                return (loss_sum + l_k, _jax.tree.map(_jnp.add, grad_sum, gw_k)), gx_k

        init = (_jnp.zeros((), _jnp.float32), _jax.tree.map(_jnp.zeros_like, weights))
        (loss, grad_w), grad_x = _jax.lax.scan(body, init, (per_example, given["loss_target"]))
    with _jax.named_scope("update"):
        delta_w, new_m, new_v = {}, {}, {}
        for n in TWIN_WEIGHTS:
            delta_w[n], new_m[n], new_v[n] = _adamw(weights[n], grad_w[n], given["m_" + n], given["v_" + n])
    return (loss, grad_x, *[grad_w[n] for n in TWIN_WEIGHTS], *[delta_w[n] for n in TWIN_WEIGHTS],
            *[new_m[n] for n in TWIN_WEIGHTS], *[new_v[n] for n in TWIN_WEIGHTS])
```

```python
import functools

import jax
import jax.numpy as jnp
from jax import lax
from jax.experimental import pallas as pl
from jax.experimental.pallas import tpu as pltpu

F32 = jnp.float32
BF16 = jnp.bfloat16
EPS = 1e-6
D_MODEL = 1024
SB_W = 512
ML_W = 512
N_MAIN = 4608
N_IN = 4616
SHARD_IN = 577
SHARD_IN_PAD = 584
TQ = 128
LCH = 128
VMEM_LIMIT = 56 * 1024 * 1024


def _dot(a, b):
    return jnp.dot(a, b, preferred_element_type=F32)


def _dot_nt(a, b):
    return lax.dot_general(a, b, (((1,), (1,)), ((), ())), preferred_element_type=F32)


def _dot_tn(a, b):
    return lax.dot_general(a, b, (((0,), (0,)), ((), ())), preferred_element_type=F32)


def _split2(x):
    hi = x.astype(BF16)
    lo = (x - hi.astype(F32)).astype(BF16)
    return hi, lo


def _split3(x):
    hi = x.astype(BF16)
    r = x - hi.astype(F32)
    mid = r.astype(BF16)
    lo = (r - mid.astype(F32)).astype(BF16)
    return hi, mid, lo


def _params(sem):
    return pltpu.CompilerParams(dimension_semantics=sem, vmem_limit_bytes=VMEM_LIMIT)


def _log_sigmoid_parts(z):
    e = jnp.exp(-jnp.abs(z))
    return jnp.minimum(z, 0.0) - jnp.log(1.0 + e)


def _sb_fwd(proj):
    S = proj.shape[0]
    nq = S // TQ

    def body(q_ref, k_ref, v_ref, y_ref, t_ref, acc_ref, car_ref):
        i = pl.program_id(1)
        lane = lax.broadcasted_iota(jnp.int32, (TQ, 128), 1)
        low = lane < 64
        row = lax.broadcasted_iota(jnp.int32, (TQ, TQ), 0)
        col = lax.broadcasted_iota(jnp.int32, (TQ, TQ), 1)
        strict = col < row
        uo = jnp.concatenate([(row > col).astype(BF16), jnp.ones((TQ, 128), BF16)], axis=1)
        q = q_ref[...] * 0.125
        qh = (jnp.where(low, q, 0.0).astype(BF16), jnp.where(low, 0.0, q).astype(BF16))
        acc_ref[...] = jnp.zeros_like(acc_ref)
        car_ref[...] = jnp.zeros_like(car_ref)

        def block(j, diag):
            kb = k_ref[pl.ds(pl.multiple_of(j * TQ, TQ), TQ), :].astype(BF16)
            vb = v_ref[pl.ds(pl.multiple_of(j * TQ, TQ), TQ), :].astype(BF16)
            for h in range(2):
                z = _dot_nt(qh[h], kb)
                lb = _log_sigmoid_parts(z)
                lk = lb - z
                if diag:
                    lk = jnp.where(strict, lk, 0.0)
                hi, lo = _split2(lk)
                rr = _dot(hi, uo) + _dot(lo, uo)
                a = jnp.exp(lb + car_ref[h] + rr[:, :128])
                if diag:
                    a = jnp.where(strict, a, 0.0)
                acc_ref[h] += _dot(a.astype(BF16), vb)
                car_ref[h] += rr[:, 128:]

        block(i, True)

        def loop(n, c):
            block(i - 1 - n, False)
            return c

        lax.fori_loop(0, i, loop, 0)
        y_ref[...] = jnp.where(low, acc_ref[0], acc_ref[1])
        t_ref[...] = jnp.where(low, car_ref[0], car_ref[1])

    return pl.pallas_call(
        body, name="sb_fwd", grid=(4, nq),
        in_specs=[pl.BlockSpec((TQ, 128), lambda p, i: (i, p)),
                  pl.BlockSpec((S, 128), lambda p, i: (0, 4 + p)),
                  pl.BlockSpec((S, 128), lambda p, i: (0, 8 + p))],
        out_specs=[pl.BlockSpec((TQ, 128), lambda p, i: (i, p)),
                   pl.BlockSpec((TQ, 128), lambda p, i: (i, p))],
        out_shape=[jax.ShapeDtypeStruct((S, SB_W), F32), jax.ShapeDtypeStruct((S, SB_W), F32)],
        scratch_shapes=[pltpu.VMEM((2, TQ, 128), F32), pltpu.VMEM((2, TQ, 128), F32)],
        compiler_params=_params(("arbitrary", "arbitrary")),
    )(proj, proj, proj)


def _sb_bwd(proj, tot, dy):
    S = proj.shape[0]
    nq = S // TQ

    def body(q_ref, k_ref, v_ref, t_ref, dy_ref, dq_ref, dk_ref, dv_ref, dqa_ref, cp_ref, cg_ref):
        i = pl.program_id(1)
        lane = lax.broadcasted_iota(jnp.int32, (TQ, 128), 1)
        low = lane < 64
        row = lax.broadcasted_iota(jnp.int32, (TQ, TQ), 0)
        col = lax.broadcasted_iota(jnp.int32, (TQ, TQ), 1)
        strict = col < row
        ones = jnp.ones((TQ, 128), BF16)
        u_inc = jnp.concatenate([(row <= col).astype(BF16), ones], axis=1)
        u_exc = jnp.concatenate([(row < col).astype(BF16), ones], axis=1)
        q = q_ref[...] * 0.125
        qh = (jnp.where(low, q, 0.0).astype(BF16), jnp.where(low, 0.0, q).astype(BF16))
        dy_ = dy_ref[...]
        dyh = (jnp.where(low, dy_, 0.0).astype(BF16), jnp.where(low, 0.0, dy_).astype(BF16))
        t_ = t_ref[...]
        t_sw = pltpu.roll(t_, 64, 1)
        th = (jnp.where(low, t_, t_sw), jnp.where(low, t_sw, t_))
        dqa_ref[...] = jnp.zeros_like(dqa_ref)
        cp_ref[...] = jnp.zeros_like(cp_ref)
        cg_ref[...] = jnp.zeros_like(cg_ref)

        @pl.when(i == 0)
        def _():
            dk_ref[...] = jnp.zeros_like(dk_ref)
            dv_ref[...] = jnp.zeros_like(dv_ref)

        def block(j, diag):
            rows = pl.ds(pl.multiple_of(j * TQ, TQ), TQ)
            kb = k_ref[rows, :].astype(BF16)
            vb = v_ref[rows, :].astype(BF16)
            dk_acc = jnp.zeros((TQ, 128), F32)
            dv_acc = jnp.zeros((TQ, 128), F32)
            for h in range(2):
                z = _dot_nt(qh[h], kb)
                lb = _log_sigmoid_parts(z)
                lk = lb - z
                if diag:
                    lk = jnp.where(strict, lk, 0.0)
                hi, lo = _split2(lk)
                pp = _dot(hi, u_inc) + _dot(lo, u_inc)
                a = jnp.exp(lb + ((th[h] - cp_ref[h]) - pp[:, :128]))
                if diag:
                    a = jnp.where(strict, a, 0.0)
                g = _dot_nt(dyh[h], vb) * a
                ghi, glo = _split2(g)
                gg = _dot(ghi, u_exc) + _dot(glo, u_exc)
                beta = jnp.exp(lb)
                dz = g - beta * (g + (cg_ref[h] + gg[:, :128]))
                if diag:
                    dz = jnp.where(strict, dz, 0.0)
                dzb = dz.astype(BF16)
                dqa_ref[h] += _dot(dzb, kb)
                dk_acc += _dot_tn(dzb, qh[h])
                dv_acc += _dot_tn(a.astype(BF16), dyh[h])
                cp_ref[h] += pp[:, 128:]
                cg_ref[h] += gg[:, 128:]
            dk_ref[rows, :] += dk_acc
            dv_ref[rows, :] += dv_acc

        def loop(j, c):
            block(j, False)
            return c

        lax.fori_loop(0, i, loop, 0)
        block(i, True)
        dq_ref[...] = jnp.where(low, dqa_ref[0], dqa_ref[1]) * 0.125

    return pl.pallas_call(
        body, name="sb_bwd", grid=(4, nq),
        in_specs=[pl.BlockSpec((TQ, 128), lambda p, i: (i, p)),
                  pl.BlockSpec((S, 128), lambda p, i: (0, 4 + p)),
                  pl.BlockSpec((S, 128), lambda p, i: (0, 8 + p)),
                  pl.BlockSpec((TQ, 128), lambda p, i: (i, p)),
                  pl.BlockSpec((TQ, 128), lambda p, i: (i, p))],
        out_specs=[pl.BlockSpec((TQ, 128), lambda p, i: (i, p)),
                   pl.BlockSpec((S, 128), lambda p, i: (0, p)),
                   pl.BlockSpec((S, 128), lambda p, i: (0, p))],
        out_shape=[jax.ShapeDtypeStruct((S, SB_W), F32)] * 3,
        scratch_shapes=[pltpu.VMEM((2, TQ, 128), F32)] * 3,
        compiler_params=_params(("arbitrary", "arbitrary")),
    )(proj, proj, proj, tot, dy)


ML_SCALE = 128 ** -0.5
RC = 256


def _conv_taps(cur, prev8, w):
    n = cur.shape[0]
    win = jnp.concatenate([prev8, cur], axis=0)
    out = w[3:4, :] * cur
    for j in range(3):
        out = out + w[j:j + 1, :] * pltpu.roll(win, 3 - j, 0)[8:8 + n]
    return out


def _ml_prep(proj, conv_w, conv_b):
    S = proj.shape[0]

    def body(x_ref, w_ref, b_ref, o_ref):
        c = pl.program_id(0)
        scale = jnp.where(c < 4, 1.0, ML_SCALE).astype(F32)
        w = w_ref[...]
        b = b_ref[...]
        for n in range(S // RC):
            cur = x_ref[n * RC:(n + 1) * RC, :]
            prev8 = x_ref[n * RC - 8:n * RC, :] if n else jnp.zeros((8, 128), F32)
            pre = b + _conv_taps(cur, prev8, w)
            o_ref[n * RC:(n + 1) * RC, :] = (pre * jax.nn.sigmoid(pre) * scale).astype(BF16)

    return pl.pallas_call(
        body, name="ml_prep", grid=(8,),
        in_specs=[pl.BlockSpec((S, 128), lambda c: (0, 16 + c)),
                  pl.BlockSpec((4, 128), lambda c: (0, c)),
                  pl.BlockSpec((1, 128), lambda c: (0, c))],
        out_specs=pl.BlockSpec((S, 128), lambda c: (0, c)),
        out_shape=jax.ShapeDtypeStruct((S, 1024), BF16),
        compiler_params=_params(("arbitrary",)),
    )(proj, conv_w, conv_b)


def _ml_prep_bwd(proj, conv_w, conv_b, dq, dk):
    S = proj.shape[0]

    def body(x_ref, w_ref, b_ref, dq_ref, dk_ref, dx_ref, gw_ref, gb_ref, dp_ref):
        c = pl.program_id(0)
        w = w_ref[...]
        b = b_ref[...]
        gw = [jnp.zeros((1, 128), F32) for _ in range(4)]
        gb = jnp.zeros((1, 128), F32)
        for n in range(S // RC):
            rows = slice(n * RC, (n + 1) * RC)
            cur = x_ref[rows, :]
            prev8 = x_ref[n * RC - 8:n * RC, :] if n else jnp.zeros((8, 128), F32)
            pre = b + _conv_taps(cur, prev8, w)
            s = jax.nn.sigmoid(pre)
            dpost = jnp.where(c < 4, dq_ref[rows, :], dk_ref[rows, :] * ML_SCALE)
            dpre = dpost * (s * (1.0 + pre * (1.0 - s)))
            dp_ref[rows, :] = dpre
            win = jnp.concatenate([prev8, cur], axis=0)
            gb = gb + jnp.sum(dpre, axis=0, keepdims=True)
            gw[3] = gw[3] + jnp.sum(dpre * cur, axis=0, keepdims=True)
            for j in range(3):
                gw[j] = gw[j] + jnp.sum(dpre * pltpu.roll(win, 3 - j, 0)[8:8 + RC], axis=0, keepdims=True)
        dp_ref[S:S + 8, :] = jnp.zeros((8, 128), F32)
        for n in range(S // RC):
            win = dp_ref[n * RC:(n + 1) * RC + 8, :]
            dx = w[3:4, :] * win[:RC]
            for j in range(3):
                dx = dx + w[j:j + 1, :] * pltpu.roll(win, RC + 8 - (3 - j), 0)[:RC]
            dx_ref[n * RC:(n + 1) * RC, :] = dx
        gw_ref[...] = jnp.concatenate(gw, axis=0)
        gb_ref[...] = gb

    return pl.pallas_call(
        body, name="ml_prep_bwd", grid=(8,),
        in_specs=[pl.BlockSpec((S, 128), lambda c: (0, 16 + c)),
                  pl.BlockSpec((4, 128), lambda c: (0, c)),
                  pl.BlockSpec((1, 128), lambda c: (0, c)),
                  pl.BlockSpec((S, 128), lambda c: (0, jnp.minimum(c, 3))),
                  pl.BlockSpec((S, 128), lambda c: (0, jnp.maximum(c - 4, 0)))],
        out_specs=[pl.BlockSpec((S, 128), lambda c: (0, c)),
                   pl.BlockSpec((4, 128), lambda c: (0, c)),
                   pl.BlockSpec((1, 128), lambda c: (0, c))],
        out_shape=[jax.ShapeDtypeStruct((S, 1024), F32), jax.ShapeDtypeStruct((4, 1024), F32),
                   jax.ShapeDtypeStruct((1, 1024), F32)],
        scratch_shapes=[pltpu.VMEM((S + 8, 128), F32)],
        compiler_params=_params(("arbitrary",)),
    )(proj, conv_w, conv_b, dq, dk)


def _gate_prep(proj_g, gbias):
    S = proj_g.shape[0]
    tm = 512

    def body(g_ref, b_ref, ig_ref, lf_ref):
        g = g_ref[...] + b_ref[...]
        for h in range(4):
            ig_ref[h] = jnp.broadcast_to(g[:, h:h + 1], (tm, 128))
            lf_ref[h] = jnp.broadcast_to(_log_sigmoid_parts(g[:, 4 + h:5 + h]), (tm, 128))

    return pl.pallas_call(
        body, name="gate_prep", grid=(S // tm,),
        in_specs=[pl.BlockSpec((tm, 128), lambda i: (i, 0)), pl.BlockSpec((1, 128), lambda i: (0, 0))],
        out_specs=[pl.BlockSpec((4, tm, 128), lambda i: (0, i, 0))] * 2,
        out_shape=[jax.ShapeDtypeStruct((4, S, 128), F32)] * 2,
        compiler_params=_params(("arbitrary",)),
    )(proj_g, gbias)


def _ml_chunk_fwd(q, k, v, ig, lf, ct, n_st, m_st, tri, causal):
    vf = v.astype(F32)
    vb = v.astype(BF16)
    b = sum(_dot(tri, part) for part in _split3(lf))
    b_last = b[LCH - 1:LCH, :]
    g = b_last - b + ig
    m_loc = jnp.max(g, axis=0, keepdims=True)
    w = jnp.exp(g - m_loc)
    vwf = vf * w
    vw = vwf.astype(BF16)
    ct_loc = _dot_tn(k, vw)
    kf = k.astype(F32)
    n_loc = jnp.sum(w * kf, axis=0, keepdims=True)
    r = jnp.transpose(ig - b)
    d_log = jnp.where(causal, b + r, -jnp.inf)
    m_t = jnp.maximum(b + m_st, jnp.max(d_log, axis=1, keepdims=True))
    w_in = jnp.exp(d_log - m_t)
    qk = _dot_nt(q, k)
    scores = qk * w_in
    cs = jnp.exp(b + m_st - m_t)
    ctb = ct.astype(BF16)
    qc = _dot(q, ctb)
    qf = q.astype(F32)
    qn = jnp.sum(qf * n_st, axis=1, keepdims=True)
    num = _dot(scores.astype(BF16), vb) + cs * qc
    den = jnp.sum(scores, axis=1, keepdims=True) + cs * qn
    em = jnp.exp(-m_t)
    dd = jnp.maximum(jnp.abs(den), em)
    h = num / dd
    m_new = jnp.maximum(b_last + m_st, m_loc)
    a = jnp.exp(b_last + m_st - m_new)
    gg = jnp.exp(m_loc - m_new)
    return dict(vf=vf, vb=vb, b=b, w=w, vwf=vwf, vw=vw, kf=kf, qf=qf, ct_loc=ct_loc, n_loc=n_loc, w_in=w_in, qk=qk,
                scores=scores, cs=cs, ctb=ctb, qc=qc, qn=qn, den=den, em=em, dd=dd, h=h, m_new=m_new, a=a, gg=gg)


def _ml_consts():
    row = lax.broadcasted_iota(jnp.int32, (LCH, LCH), 0)
    col = lax.broadcasted_iota(jnp.int32, (LCH, LCH), 1)
    return row, (col <= row), (col <= row).astype(BF16)


def _ml_fwd(qk, proj, ig, lf):
    S = proj.shape[0]
    nc = S // LCH

    def body(q_ref, k_ref, v_ref, ig_ref, lf_ref, h_ref, cst_ref, nm_ref, ct_ref, n_ref, m_ref):
        _, causal, tri = _ml_consts()
        ct_ref[...] = jnp.zeros_like(ct_ref)
        n_ref[...] = jnp.zeros_like(n_ref)
        m_ref[...] = jnp.zeros_like(m_ref)

        def chunk(c, carry):
            rows = pl.ds(pl.multiple_of(c * LCH, LCH), LCH)
            ct, n_st, m_st = ct_ref[...], n_ref[0:1, :], m_ref[0:1, :]
            cst_ref[c] = ct
            nm_ref[c, 0:8, :] = n_ref[...]
            nm_ref[c, 8:16, :] = m_ref[...]
            f = _ml_chunk_fwd(q_ref[rows, :], k_ref[rows, :], v_ref[rows, :], ig_ref[rows, :], lf_ref[rows, :],
                              ct, n_st, m_st, tri, causal)
            h_ref[rows, :] = f["h"]
            ct_ref[...] = f["a"] * ct + f["gg"] * f["ct_loc"]
            n_ref[...] = jnp.broadcast_to(f["a"] * n_st + f["gg"] * f["n_loc"], (8, 128))
            m_ref[...] = jnp.broadcast_to(f["m_new"], (8, 128))
            return carry

        lax.fori_loop(0, nc, chunk, 0)

    return pl.pallas_call(
        body, name="ml_fwd", grid=(4,),
        in_specs=[pl.BlockSpec((S, 128), lambda h: (0, h)),
                  pl.BlockSpec((S, 128), lambda h: (0, 4 + h)),
                  pl.BlockSpec((S, 128), lambda h: (0, 24 + h)),
                  pl.BlockSpec((None, S, 128), lambda h: (h, 0, 0)),
                  pl.BlockSpec((None, S, 128), lambda h: (h, 0, 0))],
        out_specs=[pl.BlockSpec((S, 128), lambda h: (0, h)),
                   pl.BlockSpec((None, nc, 128, 128), lambda h: (h, 0, 0, 0)),
                   pl.BlockSpec((None, nc, 16, 128), lambda h: (h, 0, 0, 0))],
        out_shape=[jax.ShapeDtypeStruct((S, ML_W), F32), jax.ShapeDtypeStruct((4, nc, 128, 128), F32),
                   jax.ShapeDtypeStruct((4, nc, 16, 128), F32)],
        scratch_shapes=[pltpu.VMEM((128, 128), F32), pltpu.VMEM((8, 128), F32), pltpu.VMEM((8, 128), F32)],
        compiler_params=_params(("arbitrary",)),
    )(qk, qk, proj, ig, lf)


def _ml_bwd(qk, proj, ig, lf, cst, nm, dh):
    S = proj.shape[0]
    nc = S // LCH

    def body(q_ref, k_ref, v_ref, ig_ref, lf_ref, cst_ref, nm_ref, dh_ref,
             dq_ref, dk_ref, dv_ref, dif_ref, gsum_ref, dct_ref, dn_ref, gi_ref):
        hd = pl.program_id(0)
        row, causal, tri = _ml_consts()
        lane = lax.broadcasted_iota(jnp.int32, (LCH, 128), 1)
        last_row = row == LCH - 1
        dct_ref[...] = jnp.zeros_like(dct_ref)
        dn_ref[...] = jnp.zeros_like(dn_ref)
        gi_ref[...] = jnp.zeros_like(gi_ref)

        @pl.when(hd == 0)
        def _():
            dif_ref[...] = jnp.zeros_like(dif_ref)

        def chunk(t, carry):
            c = nc - 1 - t
            rows = pl.ds(pl.multiple_of(c * LCH, LCH), LCH)
            q, k = q_ref[rows, :], k_ref[rows, :]
            ig_, lf_ = ig_ref[rows, :], lf_ref[rows, :]
            ct, n_st, m_st = cst_ref[c], nm_ref[c, 0:1, :], nm_ref[c, 8:9, :]
            f = _ml_chunk_fwd(q, k, v_ref[rows, :], ig_, lf_, ct, n_st, m_st, tri, causal)
            dh_ = dh_ref[rows, :]
            dct_new, dn_new = dct_ref[...], dn_ref[0:1, :]
            e_num = dh_ / f["dd"]
            hdh = jnp.sum(f["h"] * dh_, axis=1, keepdims=True)
            e_den = jnp.where(jnp.abs(f["den"]) > f["em"], -hdh / f["dd"] * jnp.sign(f["den"]), 0.0)
            e_num_b = e_num.astype(BF16)
            ds_ = _dot_nt(e_num_b, f["vb"]) + e_den
            dqk = ds_ * f["w_in"]
            gam = dqk * f["qk"]
            dqk_b = dqk.astype(BF16)
            cse = f["cs"] * e_den
            dq = _dot(dqk_b, k) + f["cs"] * _dot_nt(e_num_b, f["ctb"]) + cse * n_st
            dk = _dot_tn(dqk_b, q)
            dv = _dot_tn(f["scores"].astype(BF16), e_num_b)
            dcl = (f["gg"] * dct_new).astype(BF16)
            dnl = f["gg"] * dn_new
            kd = _dot(k, dcl)
            dv = dv + f["w"] * kd
            dk = dk + _dot_nt(f["vw"], dcl) + f["w"] * dnl
            gam_s = jnp.sum(kd * f["vwf"], axis=1, keepdims=True) + f["w"] * jnp.sum(f["kf"] * dnl, axis=1, keepdims=True)
            col_g = jnp.sum(jnp.transpose(gam), axis=1, keepdims=True) + gam_s
            db = (jnp.sum(gam, axis=1, keepdims=True) + jnp.sum(e_num * (f["cs"] * f["qc"]), axis=1, keepdims=True)
                  + cse * f["qn"] - col_g)
            db_last = jnp.sum(gam_s[:, 0:1]) + jnp.sum(f["a"][0:1, 0:1]) * (jnp.sum(dct_new * ct) + jnp.sum(dn_new * n_st))
            db = jnp.where(last_row, db + db_last, db)
            dlf = sum(_dot_tn(tri, part) for part in _split3(db))
            df = dlf * (1.0 - jnp.exp(lf_))
            dq_ref[rows, :] = dq
            dk_ref[rows, :] = dk
            dv_ref[rows, :] = dv
            dif_ref[rows, :] += jnp.where(lane == hd, col_g, 0.0) + jnp.where(lane == hd + 4, df, 0.0)
            clamped = jnp.where(jnp.abs(f["den"]) > f["em"], 0.0, hdh)
            gi_ref[...] += jnp.broadcast_to(jnp.sum(clamped, axis=0, keepdims=True), (8, 128))
            dct_ref[...] = f["a"] * dct_new + _dot_tn(q, (f["cs"] * e_num).astype(BF16))
            dn_ref[...] = jnp.broadcast_to(f["a"] * dn_new + jnp.sum(cse * f["qf"], axis=0, keepdims=True), (8, 128))
            return carry

        lax.fori_loop(0, nc, chunk, 0)
        lane8 = lax.broadcasted_iota(jnp.int32, (8, 128), 1)
        part = jnp.where(lane8 == hd, gi_ref[...], 0.0)

        @pl.when(hd == 0)
        def _():
            gsum_ref[...] = part

        @pl.when(hd > 0)
        def _():
            gsum_ref[...] += part

        @pl.when(hd == 3)
        def _():
            gsum_ref[...] += jnp.where(lane8 >= 4, jnp.sum(dif_ref[...], axis=0, keepdims=True), 0.0)

    return pl.pallas_call(
        body, name="ml_bwd", grid=(4,),
        in_specs=[pl.BlockSpec((S, 128), lambda h: (0, h)),
                  pl.BlockSpec((S, 128), lambda h: (0, 4 + h)),
                  pl.BlockSpec((S, 128), lambda h: (0, 24 + h)),
                  pl.BlockSpec((None, S, 128), lambda h: (h, 0, 0)),
                  pl.BlockSpec((None, S, 128), lambda h: (h, 0, 0)),
                  pl.BlockSpec((None, nc, 128, 128), lambda h: (h, 0, 0, 0)),
                  pl.BlockSpec((None, nc, 16, 128), lambda h: (h, 0, 0, 0)),
                  pl.BlockSpec((S, 128), lambda h: (0, h))],
        out_specs=[pl.BlockSpec((S, 128), lambda h: (0, h))] * 3 + [pl.BlockSpec((S, 128), lambda h: (0, 0)),
                                                                       pl.BlockSpec((8, 128), lambda h: (0, 0))],
        out_shape=[jax.ShapeDtypeStruct((S, ML_W), F32)] * 3 + [jax.ShapeDtypeStruct((S, 128), F32),
                                                                 jax.ShapeDtypeStruct((8, 128), F32)],
        scratch_shapes=[pltpu.VMEM((128, 128), F32), pltpu.VMEM((8, 128), F32), pltpu.VMEM((8, 128), F32)],
        compiler_params=_params(("arbitrary",)),
    )(qk, qk, proj, ig, lf, cst, nm, dh)


def _inproj_fwd(x, pre_w, w_t, wg_t):
    S, D = x.shape
    tm, tn = 512, 512

    def body(x_ref, pw_ref, w_ref, wg_ref, proj_ref, g_ref, u_ref):
        @pl.when(pl.program_id(1) == 0)
        def _():
            xf = x_ref[...]
            r = lax.rsqrt(jnp.mean(xf * xf, axis=-1, keepdims=True) + EPS)
            u = (xf * r * pw_ref[...]).astype(BF16)
            u_ref[...] = u
            g_ref[...] = _dot_nt(u, wg_ref[...])

        proj_ref[...] = _dot_nt(u_ref[...], w_ref[...])

    return pl.pallas_call(
        body, name="inproj_fwd", grid=(S // tm, N_MAIN // tn),
        in_specs=[pl.BlockSpec((tm, D), lambda i, j: (i, 0)),
                  pl.BlockSpec((1, D), lambda i, j: (0, 0)),
                  pl.BlockSpec((tn, D), lambda i, j: (j, 0)),
                  pl.BlockSpec((128, D), lambda i, j: (0, 0))],
        out_specs=[pl.BlockSpec((tm, tn), lambda i, j: (i, j)),
                   pl.BlockSpec((tm, 128), lambda i, j: (i, 0)),
                   pl.BlockSpec((tm, D), lambda i, j: (i, 0))],
        out_shape=[jax.ShapeDtypeStruct((S, N_MAIN), F32), jax.ShapeDtypeStruct((S, 128), F32),
                   jax.ShapeDtypeStruct((S, D), BF16)],
        compiler_params=_params(("arbitrary", "arbitrary")),
    )(x, pre_w, w_t, wg_t)


def _inproj_bwd(d_main, d_if, w_t, wg_t, x, pre_w, dx_tail):
    S, D = x.shape
    tm, tk = 512, 512
    nk = N_MAIN // tk

    def body(d_ref, dg_ref, w_ref, wg_ref, x_ref, pw_ref, dt_ref, dx_ref, gpw_ref, acc_ref):
        i, k = pl.program_id(0), pl.program_id(1)

        @pl.when(k == 0)
        def _():
            acc_ref[...] = _dot(dg_ref[...], wg_ref[...])

        acc_ref[...] += _dot(d_ref[...], w_ref[...])

        @pl.when(k == nk - 1)
        def _():
            xf = x_ref[...]
            r = lax.rsqrt(jnp.mean(xf * xf, axis=-1, keepdims=True) + EPS)
            xn = xf * r
            du = acc_ref[...]
            gw = du * pw_ref[...]
            dx_ref[...] = dt_ref[...] + r * (gw - xn * jnp.mean(gw * xn, axis=-1, keepdims=True))
            part = jnp.sum(du * xn, axis=0, keepdims=True)

            @pl.when(i == 0)
            def _():
                gpw_ref[...] = part

            @pl.when(i > 0)
            def _():
                gpw_ref[...] += part

    return pl.pallas_call(
        body, name="inproj_bwd", grid=(S // tm, nk),
        in_specs=[pl.BlockSpec((tm, tk), lambda i, k: (i, k)),
                  pl.BlockSpec((tm, 128), lambda i, k: (i, 0)),
                  pl.BlockSpec((tk, D), lambda i, k: (k, 0)),
                  pl.BlockSpec((128, D), lambda i, k: (0, 0)),
                  pl.BlockSpec((tm, D), lambda i, k: (i, 0)),
                  pl.BlockSpec((1, D), lambda i, k: (0, 0)),
                  pl.BlockSpec((tm, D), lambda i, k: (i, 0))],
        out_specs=[pl.BlockSpec((tm, D), lambda i, k: (i, 0)),
                   pl.BlockSpec((1, D), lambda i, k: (0, 0))],
        out_shape=[jax.ShapeDtypeStruct((S, D), F32), jax.ShapeDtypeStruct((1, D), F32)],
        scratch_shapes=[pltpu.VMEM((tm, D), F32)],
        compiler_params=_params(("arbitrary", "arbitrary")),
    )(d_main, d_if, w_t, wg_t, x, pre_w, dx_tail)


def _matmul_tn(a, b, name):
    S, M = a.shape
    N = b.shape[1]
    tmm = min(M, 512)
    tk = 512
    nk = S // tk

    def body(a_ref, b_ref, o_ref):
        part = _dot_tn(a_ref[...].astype(BF16), b_ref[...].astype(BF16))

        @pl.when(pl.program_id(1) == 0)
        def _():
            o_ref[...] = part

        @pl.when(pl.program_id(1) > 0)
        def _():
            o_ref[...] += part

    return pl.pallas_call(
        body, name=name, grid=(M // tmm, nk),
        in_specs=[pl.BlockSpec((tk, tmm), lambda i, k: (k, i)),
                  pl.BlockSpec((tk, N), lambda i, k: (k, 0))],
        out_specs=pl.BlockSpec((tmm, N), lambda i, k: (i, 0)),
        out_shape=jax.ShapeDtypeStruct((M, N), F32),
        compiler_params=_params(("arbitrary", "arbitrary")),
    )(a, b)


def _half_mean(v, low):
    s_lo = jnp.sum(jnp.where(low, v, 0.0), axis=1, keepdims=True)
    s_hi = jnp.sum(jnp.where(low, 0.0, v), axis=1, keepdims=True)
    return jnp.where(low, s_lo, s_hi) * (1.0 / 64.0)


def _silu_grad(z, s):
    return s * (1.0 + z * (1.0 - s))


def _tail(y_sb, h_ml, proj, x, p, target, sb_nw, ml_nw, w_out, post_w, w_gate, b_gate, w_up):
    S, D = x.shape
    tm = 256

    def body(ysb_ref, hml_ref, sbz_ref, mlo_ref, mlz_ref, x_ref, p_ref, tg_ref, sbw_ref, mlw_ref, wo_ref, pw_ref,
             wg_ref, bg_ref, wu_ref,
             dx_ref, dysb_ref, dhml_ref, dsbz_ref, dmlo_ref, dmlz_ref, mix_ref, dy_ref, h1_ref, dgp_ref, dpu_ref,
             small_ref):
        lane = lax.broadcasted_iota(jnp.int32, (tm, 128), 1)
        low = lane < 64
        sb_saved, ml_saved, mixed = [], [], []
        for s in range(4):
            sl = slice(128 * s, 128 * s + 128)
            y = ysb_ref[:, sl]
            rs = lax.rsqrt(_half_mean(y * y, low) + EPS)
            n = y * rs
            z = sbz_ref[:, sl]
            sg = jax.nn.sigmoid(z)
            w = sbw_ref[:, sl]
            mixed.append((n * w) * (z * sg))
            sb_saved.append((rs, n, z, sg, w))
        for s in range(4):
            sl = slice(128 * s, 128 * s + 128)
            og = jax.nn.sigmoid(mlo_ref[:, sl])
            hh = hml_ref[:, sl]
            t = og * hh
            rs = lax.rsqrt(jnp.mean(t * t, axis=1, keepdims=True) + EPS)
            n = t * rs
            z = mlz_ref[:, sl]
            sg = jax.nn.sigmoid(z)
            w = mlw_ref[:, sl]
            mixed.append((n * w) * (z * sg))
            ml_saved.append((rs, n, z, sg, w, og, hh))
        mix = jnp.concatenate(mixed, axis=1).astype(BF16)
        mix_ref[...] = mix
        y = _dot(mix, wo_ref[...])
        rs_y = lax.rsqrt(jnp.mean(y * y, axis=1, keepdims=True) + EPS)
        yn = y * rs_y
        pw = pw_ref[...]
        h1 = x_ref[...] + yn * pw
        h1b = h1.astype(BF16)
        h1_ref[...] = h1b
        gate = jax.nn.sigmoid(_dot(h1b, wg_ref[...]) + bg_ref[...])
        pu = _dot(p_ref[...].astype(BF16), wu_ref[...])
        err = (h1 + gate * pu) - tg_ref[...]
        loss = 0.5 * jnp.sum(jnp.sum(err * err, axis=1, keepdims=True) * (1.0 / D))
        d_out = err * (1.0 / D)
        dpu_ref[...] = (d_out * gate).astype(BF16)
        dgp = (d_out * pu) * (gate * (1.0 - gate))
        dgpb = dgp.astype(BF16)
        dgp_ref[...] = dgpb
        d_h1 = d_out + _dot_nt(dgpb, wg_ref[...])
        dx_ref[...] = d_h1
        gwy = d_h1 * pw
        d_y = rs_y * (gwy - yn * jnp.mean(gwy * yn, axis=1, keepdims=True))
        d_yb = d_y.astype(BF16)
        dy_ref[...] = d_yb
        d_mix = _dot_nt(d_yb, wo_ref[...])
        g_nw = []
        for s in range(4):
            sl = slice(128 * s, 128 * s + 128)
            rs, n, z, sg, w = sb_saved[s]
            da = d_mix[:, sl]
            act = z * sg
            dsbz_ref[:, sl] = (da * (n * w) * _silu_grad(z, sg)).astype(BF16)
            dn = da * w * act
            g_nw.append(jnp.sum(da * act * n, axis=0, keepdims=True))
            dysb_ref[:, sl] = rs * (dn - n * _half_mean(dn * n, low))
        for s in range(4):
            sl = slice(128 * s, 128 * s + 128)
            rs, n, z, sg, w, og, hh = ml_saved[s]
            da = d_mix[:, 512 + 128 * s:512 + 128 * s + 128]
            act = z * sg
            dmlz_ref[:, sl] = (da * (n * w) * _silu_grad(z, sg)).astype(BF16)
            dn = da * w * act
            g_nw.append(jnp.sum(da * act * n, axis=0, keepdims=True))
            dt = rs * (dn - n * jnp.mean(dn * n, axis=1, keepdims=True))
            dmlo_ref[:, sl] = (dt * hh * (og * (1.0 - og))).astype(BF16)
            dhml_ref[:, sl] = dt * og
        upd = jnp.concatenate([
            jnp.sum(d_h1 * yn, axis=0, keepdims=True),
            jnp.sum(dgp, axis=0, keepdims=True),
            jnp.concatenate(g_nw, axis=1),
            jnp.full((1, D), loss, F32),
            jnp.zeros((4, D), F32)], axis=0)

        @pl.when(pl.program_id(0) == 0)
        def _():
            small_ref[...] = upd

        @pl.when(pl.program_id(0) > 0)
        def _():
            small_ref[...] += upd

    def rows(width, col=0):
        return pl.BlockSpec((tm, width), lambda i: (i, col))

    def whole(a):
        return pl.BlockSpec(a.shape, lambda i: (0, 0))

    return pl.pallas_call(
        body, name="tail", grid=(S // tm,),
        in_specs=[rows(512), rows(512), rows(512, 3), rows(512, 7), rows(512, 8), rows(D), rows(256), rows(D),
                  whole(sb_nw), whole(ml_nw), whole(w_out), whole(post_w), whole(w_gate), whole(b_gate), whole(w_up)],
        out_specs=[rows(D), rows(512), rows(512), rows(512), rows(512), rows(512), rows(D), rows(D), rows(D), rows(D),
                   rows(D), pl.BlockSpec((8, D), lambda i: (0, 0))],
        out_shape=[jax.ShapeDtypeStruct((S, D), F32), jax.ShapeDtypeStruct((S, 512), F32),
                   jax.ShapeDtypeStruct((S, 512), F32)] + [jax.ShapeDtypeStruct((S, 512), BF16)] * 3
        + [jax.ShapeDtypeStruct((S, D), BF16)] * 5 + [jax.ShapeDtypeStruct((8, D), F32)],
        compiler_params=_params(("arbitrary",)),
    )(y_sb, h_ml, proj, proj, proj, x, p, target, sb_nw, ml_nw, w_out, post_w, w_gate, b_gate, w_up)


def _local_step(x, p, target, pre_w, w_t, wg_t, conv_w, conv_b, gbias, sb_nw, ml_nw, w_out, post_w, w_gate, b_gate,
                w_up):
    proj, proj_g, u = _inproj_fwd(x, pre_w, w_t, wg_t)
    y_sb, tot = _sb_fwd(proj)
    qk = _ml_prep(proj, conv_w, conv_b)
    ig, lf = _gate_prep(proj_g, gbias)
    h_ml, cst, nm = _ml_fwd(qk, proj, ig, lf)
    dx_tail, d_ysb, d_hml, d_sbz, d_mlo, d_mlz, mix, d_y, h1, dgp, dpu, small = _tail(
        y_sb, h_ml, proj, x, p, target, sb_nw, ml_nw, w_out, post_w, w_gate, b_gate, w_up)
    dq, dk, dv = _sb_bwd(proj, tot, d_ysb)
    dqc, dks, dmlv, dif, gif = _ml_bwd(qk, proj, ig, lf, cst, nm, d_hml)
    dmlqk, g_cw, g_cb = _ml_prep_bwd(proj, conv_w, conv_b, dqc, dks)
    d_main = jnp.concatenate([dq.astype(BF16), dk.astype(BF16), dv.astype(BF16), d_sbz, dmlqk.astype(BF16),
                              dmlv.astype(BF16), d_mlo, d_mlz], axis=1)
    d_if = dif.astype(BF16)
    dx, g_pre = _inproj_bwd(d_main, d_if, w_t, wg_t, x, pre_w, dx_tail)
    grads = dict(
        w_t=_matmul_tn(d_main, u, "gw_in"), wg_t=_matmul_tn(d_if, u, "gw_in_gates"),
        w_out=_matmul_tn(mix, d_y, "gw_out"), w_gate=_matmul_tn(h1, dgp, "gw_gate"), w_up=_matmul_tn(p, dpu, "gw_up"),
        conv_w=g_cw, conv_b=g_cb, pre_w=g_pre, gif=gif)
    return dx, grads, small


MESH = pl.DeviceIdType.MESH
ANY = pl.BlockSpec(memory_space=pl.ANY)
N_DEV = 8


def _place():
    return lax.axis_index("x"), lax.axis_index("y"), lax.axis_index("c")


def _block_of(px, py, pc):
    return 4 * px + 2 * py + pc


def _all_gather(a, b):
    def body(a_ref, b_ref, oa_ref, ob_ref, send_sems, recv_sems, local_sems):
        x, y, c = _place()
        me, sibling = (x, y, c), (x, y, 1 - c)
        chips = [(1 - x, y), (x, 1 - y), (1 - x, 1 - y)]
        pairs = ((a_ref, oa_ref), (b_ref, ob_ref))

        def copies(k, block, to, from_input=False):
            slot = _block_of(*block)
            return [pltpu.make_async_remote_copy(
                src_ref=src if from_input else out.at[slot], dst_ref=out.at[slot],
                send_sem=send_sems.at[t, k], recv_sem=recv_sems.at[t, k], device_id=to, device_id_type=MESH)
                for t, (src, out) in enumerate(pairs)]

        mine = [pltpu.make_async_copy(src, out.at[_block_of(*me)], local_sems.at[t])
                for t, (src, out) in enumerate(pairs)]
        for cp in mine:
            cp.start()
        first = copies(0, me, sibling, True)
        for j, chip in enumerate(chips):
            first += copies(1 + j, me, (*chip, c), True)
        for cp in first:
            cp.start()
        passed = []
        for j, chip in enumerate(chips):
            for cp in copies(1 + j, (*chip, c), me):
                cp.wait_recv()
            fwd = copies(4 + j, (*chip, c), sibling)
            for cp in fwd:
                cp.start()
            passed += fwd
        for cp in copies(0, sibling, me):
            cp.wait_recv()
        for j, chip in enumerate(chips):
            for cp in copies(4 + j, (*chip, 1 - c), me):
                cp.wait_recv()
        for cp in first + passed:
            cp.wait_send()
        for cp in mine:
            cp.wait()

    return pl.pallas_call(
        body, name="all_gather",
        in_specs=[ANY, ANY], out_specs=[ANY, ANY],
        out_shape=[jax.ShapeDtypeStruct((N_DEV,) + a.shape, a.dtype), jax.ShapeDtypeStruct((N_DEV,) + b.shape, b.dtype)],
        scratch_shapes=[pltpu.SemaphoreType.DMA((2, 7)), pltpu.SemaphoreType.DMA((2, 7)), pltpu.SemaphoreType.DMA((2,))],
    )(a, b)


def _exchange(g, s):
    def body(g_ref, s_ref, og_ref, os_ref, send_sems, recv_sems, local_sems):
        x, y, c = _place()
        mine = _block_of(x, y, c)
        local = [pltpu.make_async_copy(g_ref.at[mine], og_ref.at[mine], local_sems.at[0]),
                 pltpu.make_async_copy(s_ref, os_ref.at[mine], local_sems.at[1])]
        for cp in local:
            cp.start()
        sent = []
        for k in range(1, N_DEV):
            peer = (x ^ (k >> 2), y ^ ((k >> 1) & 1), c ^ (k & 1))
            sent.append(pltpu.make_async_remote_copy(
                src_ref=g_ref.at[_block_of(*peer)], dst_ref=og_ref.at[mine],
                send_sem=send_sems.at[0, k - 1], recv_sem=recv_sems.at[0, k - 1], device_id=peer, device_id_type=MESH))
            sent.append(pltpu.make_async_remote_copy(
                src_ref=s_ref, dst_ref=os_ref.at[mine],
                send_sem=send_sems.at[1, k - 1], recv_sem=recv_sems.at[1, k - 1], device_id=peer, device_id_type=MESH))
        for cp in sent:
            cp.start()
        for cp in sent:
            cp.wait()
        for cp in local:
            cp.wait()

    return pl.pallas_call(
        body, name="exchange",
        in_specs=[ANY, ANY], out_specs=[ANY, ANY],
        out_shape=[jax.ShapeDtypeStruct(g.shape, g.dtype), jax.ShapeDtypeStruct((N_DEV,) + s.shape, s.dtype)],
        scratch_shapes=[pltpu.SemaphoreType.DMA((2, 7)), pltpu.SemaphoreType.DMA((2, 7)), pltpu.SemaphoreType.DMA((2,))],
    )(g, s)


ADAM_LR, ADAM_B1, ADAM_B2, ADAM_EPS, ADAM_WD, ADAM_STEP = 0.001, 0.9, 0.999, 1e-08, 0.01, 10


def _adamw(w, g, m, v):
    m = ADAM_B1 * m + (1.0 - ADAM_B1) * g
    v = ADAM_B2 * v + (1.0 - ADAM_B2) * (g * g)
    m_hat = m / (1.0 - ADAM_B1 ** ADAM_STEP)
    v_hat = v / (1.0 - ADAM_B2 ** ADAM_STEP)
    return -ADAM_LR * (m_hat / (jnp.sqrt(v_hat) + ADAM_EPS) + ADAM_WD * w), m, v


def _adam(parts, w, m, v, tr, name):
    R, D = w.shape

    def body(p_ref, w_ref, m_ref, v_ref, g_ref, d_ref, nm_ref, nv_ref):
        g = p_ref[0]
        for k in range(1, N_DEV):
            g = g + p_ref[k]
        g_ref[...] = g
        d_ref[...], nm_ref[...], nv_ref[...] = _adamw(w_ref[...], g, m_ref[...], v_ref[...])

    blk = pl.BlockSpec((tr, D), lambda i: (i, 0))
    return pl.pallas_call(
        body, name=name, grid=(R // tr,),
        in_specs=[pl.BlockSpec((N_DEV, tr, D), lambda i: (0, i, 0)), blk, blk, blk],
        out_specs=[blk] * 4, out_shape=[jax.ShapeDtypeStruct((R, D), F32)] * 4,
        compiler_params=_params(("arbitrary",)),
    )(parts, w, m, v)


ROWS_IN = 592
ROWS_BF16 = ROWS_IN + 128 + 128 + 32
ROWS_ALL = ROWS_BF16 + 8


def _pad_rows(a, rows):
    return jnp.pad(a, ((0, rows - a.shape[0]), (0, 0)))


def _pack_shards(w_in, w_out, w_gate, w_up, conv_w):
    return jnp.concatenate([
        _pad_rows(w_in[0].T, ROWS_IN), w_out[0], w_gate[0], w_up[0].reshape(32, D_MODEL),
        _pad_rows(jnp.pad(conv_w[0].reshape(1, 512), ((0, 0), (0, 512))), 8)], axis=0)


def _unpack_shards(a):
    return (a[:SHARD_IN].T[None], a[ROWS_IN:ROWS_IN + 128][None], a[ROWS_IN + 128:ROWS_IN + 256][None],
            a[ROWS_IN + 256:ROWS_BF16].reshape(1, 256, 128), a[ROWS_BF16, :512].reshape(1, 4, 128))


def _pack_small(pre_w, conv_b, i_bias, f_bias, sb_nw, ml_nw, post_w, b_gate):
    gates = jnp.pad(jnp.concatenate([i_bias, f_bias], axis=1), ((0, 0), (0, D_MODEL - 8)))
    return jnp.concatenate([post_w, b_gate, jnp.concatenate([sb_nw, ml_nw], axis=1), jnp.zeros((1, D_MODEL), F32),
                            pre_w, conv_b, gates, jnp.zeros((1, D_MODEL), F32)], axis=0)


def _unpack_small(a):
    return a[4:5], a[5:6], a[6:7, 0:4], a[6:7, 4:8], a[2:3, :512], a[2:3, 512:], a[0:1], a[1:2]


def kernel(x, p, pre_norm_w, w_in, ml_conv_w, ml_conv_b, ml_i_bias, ml_f_bias, sb_norm_w, ml_norm_w, w_out, post_norm_w, ple_w_up, ple_w_gate, ple_b_gate, loss_target, m_pre_norm_w, m_w_in, m_ml_conv_w, m_ml_conv_b, m_ml_i_bias, m_ml_f_bias, m_sb_norm_w, m_ml_norm_w, m_w_out, m_post_norm_w, m_ple_w_up, m_ple_w_gate, m_ple_b_gate, v_pre_norm_w, v_w_in, v_ml_conv_w, v_ml_conv_b, v_ml_i_bias, v_ml_f_bias, v_sb_norm_w, v_ml_norm_w, v_w_out, v_post_norm_w, v_ple_w_up, v_ple_w_gate, v_ple_b_gate):
    D = D_MODEL
    w_pk = _pack_shards(w_in, w_out, ple_w_gate, ple_w_up, ml_conv_w)
    m_pk = _pack_shards(m_w_in, m_w_out, m_ple_w_gate, m_ple_w_up, m_ml_conv_w)
    v_pk = _pack_shards(v_w_in, v_w_out, v_ple_w_gate, v_ple_w_up, v_ml_conv_w)
    w_sm = _pack_small(pre_norm_w, ml_conv_b, ml_i_bias, ml_f_bias, sb_norm_w, ml_norm_w, post_norm_w, ple_b_gate)
    m_sm = _pack_small(m_pre_norm_w, m_ml_conv_b, m_ml_i_bias, m_ml_f_bias, m_sb_norm_w, m_ml_norm_w, m_post_norm_w, m_ple_b_gate)
    v_sm = _pack_small(v_pre_norm_w, v_ml_conv_b, v_ml_i_bias, v_ml_f_bias, v_sb_norm_w, v_ml_norm_w, v_post_norm_w, v_ple_b_gate)

    ga, gb = _all_gather(w_pk[:ROWS_BF16].astype(BF16), w_pk[ROWS_BF16:])
    w_in_t = ga[:, :SHARD_IN].reshape(N_IN, D)
    w_t = w_in_t[:N_MAIN]
    wg_t = _pad_rows(w_in_t[N_MAIN:], 128)
    w_out_f = ga[:, ROWS_IN:ROWS_IN + 128].reshape(D, D)
    w_gate_f = ga[:, ROWS_IN + 128:ROWS_IN + 256].reshape(D, D)
    w_up_f = ga[:, ROWS_IN + 256:].reshape(N_DEV, 256, 128).transpose(1, 0, 2).reshape(256, D)
    conv_w_f = gb[:, 0, :512].reshape(N_DEV, 4, 128).transpose(1, 0, 2).reshape(4, D)
    gbias = jnp.pad(jnp.concatenate([ml_i_bias, ml_f_bias], axis=1), ((0, 0), (0, 120)))

    dx, g, small = _local_step(x[0], p[0, 0], loss_target[0], pre_norm_w, w_t, wg_t, conv_w_f, ml_conv_b, gbias,
                               sb_norm_w, ml_norm_w, w_out_f, post_norm_w, w_gate_f, ple_b_gate, w_up_f)

    g_in = jnp.concatenate([g["w_t"], g["wg_t"][:8]], axis=0).reshape(N_DEV, SHARD_IN, D)
    g_blocks = jnp.concatenate([
        jnp.pad(g_in, ((0, 0), (0, ROWS_IN - SHARD_IN), (0, 0))),
        g["w_out"].reshape(N_DEV, 128, D), g["w_gate"].reshape(N_DEV, 128, D),
        g["w_up"].reshape(256, N_DEV, 128).transpose(1, 0, 2).reshape(N_DEV, 32, D),
        jnp.pad(g["conv_w"].reshape(4, N_DEV, 128).transpose(1, 0, 2).reshape(N_DEV, 1, 512), ((0, 0), (0, 7), (0, 512))),
    ], axis=1)
    g_small = jnp.concatenate([small[0:4], g["pre_w"], g["conv_b"], jnp.pad(g["gif"][0:1], ((0, 0), (0, D - 128))),
                               jnp.zeros((1, D), F32)], axis=0)
    parts, parts_sm = _exchange(g_blocks, g_small)

    grad_pk, delta_pk, nm_pk, nv_pk = _adam(parts, w_pk, m_pk, v_pk, ROWS_ALL // 3, "adam")
    grad_sm, delta_sm, nm_sm, nv_sm = _adam(parts_sm, w_sm, m_sm, v_sm, 8, "adam_small")
    loss = grad_sm[3, 0]

    def ordered(pk, sm):
        win, wout, wgate, wup, convw = _unpack_shards(pk)
        pre_w, conv_b, i_b, f_b, sb_nw, ml_nw, post_w, b_gate = _unpack_small(sm)
        return [pre_w, win, convw, conv_b, i_b, f_b, sb_nw, ml_nw, wout, post_w, wup, wgate, b_gate]

    return (loss, dx[None], *ordered(grad_pk, grad_sm), *ordered(delta_pk, delta_sm), *ordered(nm_pk, nm_sm),
            *ordered(nv_pk, nv_sm))
```

```python
import functools

import jax
import jax.numpy as jnp
from jax import lax
from jax.experimental import pallas as pl
from jax.experimental.pallas import tpu as pltpu

F32 = jnp.float32
BF16 = jnp.bfloat16
EPS = 1e-6
D_MODEL = 1024
SB_W = 512
ML_W = 512
N_MAIN = 4608
N_IN = 4616
SHARD_IN = 577
SHARD_IN_PAD = 584
TQ = 1024
TK = 128
ND = TQ // TK
LCH = 128
VMEM_LIMIT = 56 * 1024 * 1024


def _dot(a, b):
    return jnp.dot(a, b, preferred_element_type=F32)


def _dot_nt(a, b):
    return lax.dot_general(a, b, (((1,), (1,)), ((), ())), preferred_element_type=F32)


def _dot_tn(a, b):
    return lax.dot_general(a, b, (((0,), (0,)), ((), ())), preferred_element_type=F32)


def _split2(x):
    hi = x.astype(BF16)
    lo = (x - hi.astype(F32)).astype(BF16)
    return hi, lo


def _split3(x):
    hi = x.astype(BF16)
    r = x - hi.astype(F32)
    mid = r.astype(BF16)
    lo = (r - mid.astype(F32)).astype(BF16)
    return hi, mid, lo


def _params(sem):
    return pltpu.CompilerParams(dimension_semantics=sem, vmem_limit_bytes=VMEM_LIMIT)


def _log_sigmoid_parts(z):
    e = jnp.exp(-jnp.abs(z))
    return jnp.minimum(z, 0.0) - jnp.log(1.0 + e)


def _sb_fwd(proj):
    S = proj.shape[0]
    nq = S // TQ

    def body(q_ref, k_ref, v_ref, y_ref, t_ref, acc_ref, car_ref):
        i = pl.program_id(1)
        low = lax.broadcasted_iota(jnp.int32, (TQ, 128), 1) < 64
        row = lax.broadcasted_iota(jnp.int32, (TK, TK), 0)
        col = lax.broadcasted_iota(jnp.int32, (TK, TK), 1)
        uo = jnp.concatenate([(row > col).astype(BF16), jnp.ones((TK, 128), BF16)], axis=1)
        uo = jnp.concatenate([uo, uo], axis=0)
        q = q_ref[...] * 0.125
        qh = (jnp.where(low, q, 0.0).astype(BF16), jnp.where(low, 0.0, q).astype(BF16))
        acc_ref[...] = jnp.zeros_like(acc_ref)
        car_ref[...] = jnp.zeros_like(car_ref)

        def block(j, r0):
            diag = r0 is not None
            r0 = r0 or 0
            if diag:
                strict = (lax.broadcasted_iota(jnp.int32, (TQ - r0, TK), 1)
                          < lax.broadcasted_iota(jnp.int32, (TQ - r0, TK), 0))
            rows = pl.ds(pl.multiple_of(j * TK, TK), TK)
            kb = k_ref[rows, :].astype(BF16)
            vb = v_ref[rows, :].astype(BF16)
            for h in range(2):
                z = _dot_nt(qh[h][r0:], kb)
                lb = _log_sigmoid_parts(z)
                lk = lb - z
                if diag:
                    lk = jnp.where(strict, lk, 0.0)
                hi, lo = _split2(lk)
                rr = _dot(jnp.concatenate([hi, lo], axis=1), uo)
                a = jnp.exp(lb + car_ref[h, r0:, :] + rr[:, :128])
                if diag:
                    a = jnp.where(strict, a, 0.0)
                acc_ref[h, r0:, :] += _dot(a.astype(BF16), vb)
                car_ref[h, r0:, :] += rr[:, 128:]

        for d in reversed(range(ND)):
            block(ND * i + d, TK * d)

        def loop(n, c):
            block(ND * i - 1 - n, None)
            return c

        lax.fori_loop(0, ND * i, loop, 0)
        y_ref[...] = jnp.where(low, acc_ref[0], acc_ref[1])
        t_ref[...] = jnp.where(low, car_ref[0], car_ref[1])

    return pl.pallas_call(
        body, name="sb_fwd", grid=(4, nq),
        in_specs=[pl.BlockSpec((TQ, 128), lambda p, i: (i, p)),
                  pl.BlockSpec((S, 128), lambda p, i: (0, 4 + p)),
                  pl.BlockSpec((S, 128), lambda p, i: (0, 8 + p))],
        out_specs=[pl.BlockSpec((TQ, 128), lambda p, i: (i, p)),
                   pl.BlockSpec((TQ, 128), lambda p, i: (i, p))],
        out_shape=[jax.ShapeDtypeStruct((S, SB_W), F32), jax.ShapeDtypeStruct((S, SB_W), F32)],
        scratch_shapes=[pltpu.VMEM((2, TQ, 128), F32), pltpu.VMEM((2, TQ, 128), F32)],
        compiler_params=_params(("arbitrary", "arbitrary")),
    )(proj, proj, proj)


def _sb_bwd(proj, tot, dy):
    S = proj.shape[0]
    nq = S // TQ

    def body(q_ref, k_ref, v_ref, t_ref, dy_ref, dq_ref, dk_ref, dv_ref, dqa_ref, cp_ref, cg_ref):
        i = pl.program_id(1)
        low = lax.broadcasted_iota(jnp.int32, (TQ, 128), 1) < 64
        row = lax.broadcasted_iota(jnp.int32, (TK, TK), 0)
        col = lax.broadcasted_iota(jnp.int32, (TK, TK), 1)
        ones = jnp.ones((TK, 128), BF16)
        u_inc = jnp.concatenate([(row <= col).astype(BF16), ones], axis=1)
        u_exc = jnp.concatenate([(row < col).astype(BF16), ones], axis=1)
        u_inc = jnp.concatenate([u_inc, u_inc], axis=0)
        u_exc = jnp.concatenate([u_exc, u_exc], axis=0)
        q = q_ref[...] * 0.125
        qh = (jnp.where(low, q, 0.0).astype(BF16), jnp.where(low, 0.0, q).astype(BF16))
        dy_ = dy_ref[...]
        dyh = (jnp.where(low, dy_, 0.0).astype(BF16), jnp.where(low, 0.0, dy_).astype(BF16))
        t_ = t_ref[...]
        t_sw = pltpu.roll(t_, 64, 1)
        th = (jnp.where(low, t_, t_sw), jnp.where(low, t_sw, t_))
        dqa_ref[...] = jnp.zeros_like(dqa_ref)
        cp_ref[...] = jnp.zeros_like(cp_ref)
        cg_ref[...] = jnp.zeros_like(cg_ref)

        @pl.when(i == 0)
        def _():
            dk_ref[...] = jnp.zeros_like(dk_ref)
            dv_ref[...] = jnp.zeros_like(dv_ref)

        def block(j, r0):
            diag = r0 is not None
            r0 = r0 or 0
            if diag:
                strict = (lax.broadcasted_iota(jnp.int32, (TQ - r0, TK), 1)
                          < lax.broadcasted_iota(jnp.int32, (TQ - r0, TK), 0))
            rows = pl.ds(pl.multiple_of(j * TK, TK), TK)
            kb = k_ref[rows, :].astype(BF16)
            vb = v_ref[rows, :].astype(BF16)
            dk_acc = jnp.zeros((TK, 128), F32)
            dv_acc = jnp.zeros((TK, 128), F32)
            for h in range(2):
                qr, dyr = qh[h][r0:], dyh[h][r0:]
                z = _dot_nt(qr, kb)
                lb = _log_sigmoid_parts(z)
                lk = lb - z
                if diag:
                    lk = jnp.where(strict, lk, 0.0)
                hi, lo = _split2(lk)
                pp = _dot(jnp.concatenate([hi, lo], axis=1), u_inc)
                a = jnp.exp(lb + ((th[h][r0:] - cp_ref[h, r0:, :]) - pp[:, :128]))
                if diag:
                    a = jnp.where(strict, a, 0.0)
                g = _dot_nt(dyr, vb) * a
                ghi, glo = _split2(g)
                gg = _dot(jnp.concatenate([ghi, glo], axis=1), u_exc)
                beta = jnp.exp(lb)
                dz = g - beta * (g + (cg_ref[h, r0:, :] + gg[:, :128]))
                if diag:
                    dz = jnp.where(strict, dz, 0.0)
                dzb = dz.astype(BF16)
                dqa_ref[h, r0:, :] += _dot(dzb, kb)
                dk_acc += _dot_tn(dzb, qr)
                dv_acc += _dot_tn(a.astype(BF16), dyr)
                cp_ref[h, r0:, :] += pp[:, 128:]
                cg_ref[h, r0:, :] += gg[:, 128:]
            dk_ref[rows, :] += dk_acc
            dv_ref[rows, :] += dv_acc

        def loop(j, c):
            block(j, None)
            return c

        lax.fori_loop(0, ND * i, loop, 0)
        for d in range(ND):
            block(ND * i + d, TK * d)
        dq_ref[...] = jnp.where(low, dqa_ref[0], dqa_ref[1]) * 0.125

    return pl.pallas_call(
        body, name="sb_bwd", grid=(4, nq),
        in_specs=[pl.BlockSpec((TQ, 128), lambda p, i: (i, p)),
                  pl.BlockSpec((S, 128), lambda p, i: (0, 4 + p)),
                  pl.BlockSpec((S, 128), lambda p, i: (0, 8 + p)),
                  pl.BlockSpec((TQ, 128), lambda p, i: (i, p)),
                  pl.BlockSpec((TQ, 128), lambda p, i: (i, p))],
        out_specs=[pl.BlockSpec((TQ, 128), lambda p, i: (i, p)),
                   pl.BlockSpec((S, 128), lambda p, i: (0, p)),
                   pl.BlockSpec((S, 128), lambda p, i: (0, p))],
        out_shape=[jax.ShapeDtypeStruct((S, SB_W), F32)] * 3,
        scratch_shapes=[pltpu.VMEM((2, TQ, 128), F32)] * 3,
        compiler_params=_params(("arbitrary", "arbitrary")),
    )(proj, proj, proj, tot, dy)


ML_SCALE = 128 ** -0.5
RC = 256


def _conv_taps(cur, prev8, w):
    n = cur.shape[0]
    win = jnp.concatenate([prev8, cur], axis=0)
    out = w[3:4, :] * cur
    for j in range(3):
        out = out + w[j:j + 1, :] * pltpu.roll(win, 3 - j, 0)[8:8 + n]
    return out


def _ml_prep(proj, conv_w, conv_b):
    S = proj.shape[0]

    def body(x_ref, w_ref, b_ref, o_ref):
        c = pl.program_id(0)
        scale = jnp.where(c < 4, 1.0, ML_SCALE).astype(F32)
        w = w_ref[...]
        b = b_ref[...]
        for n in range(S // RC):
            cur = x_ref[n * RC:(n + 1) * RC, :]
            prev8 = x_ref[n * RC - 8:n * RC, :] if n else jnp.zeros((8, 128), F32)
            pre = b + _conv_taps(cur, prev8, w)
            o_ref[n * RC:(n + 1) * RC, :] = (pre * jax.nn.sigmoid(pre) * scale).astype(BF16)

    return pl.pallas_call(
        body, name="ml_prep", grid=(8,),
        in_specs=[pl.BlockSpec((S, 128), lambda c: (0, 16 + c)),
                  pl.BlockSpec((4, 128), lambda c: (0, c)),
                  pl.BlockSpec((1, 128), lambda c: (0, c))],
        out_specs=pl.BlockSpec((S, 128), lambda c: (0, c)),
        out_shape=jax.ShapeDtypeStruct((S, 1024), BF16),
        compiler_params=_params(("arbitrary",)),
    )(proj, conv_w, conv_b)


def _ml_prep_bwd(proj, conv_w, conv_b, dq, dk):
    S = proj.shape[0]

    def body(x_ref, w_ref, b_ref, dq_ref, dk_ref, dx_ref, gw_ref, gb_ref, dp_ref):
        c = pl.program_id(0)
        w = w_ref[...]
        b = b_ref[...]
        gw = [jnp.zeros((1, 128), F32) for _ in range(4)]
        gb = jnp.zeros((1, 128), F32)
        for n in range(S // RC):
            rows = slice(n * RC, (n + 1) * RC)
            cur = x_ref[rows, :]
            prev8 = x_ref[n * RC - 8:n * RC, :] if n else jnp.zeros((8, 128), F32)
            pre = b + _conv_taps(cur, prev8, w)
            s = jax.nn.sigmoid(pre)
            dpost = jnp.where(c < 4, dq_ref[rows, :], dk_ref[rows, :] * ML_SCALE)
            dpre = dpost * (s * (1.0 + pre * (1.0 - s)))
            dp_ref[rows, :] = dpre
            win = jnp.concatenate([prev8, cur], axis=0)
            gb = gb + jnp.sum(dpre, axis=0, keepdims=True)
            gw[3] = gw[3] + jnp.sum(dpre * cur, axis=0, keepdims=True)
            for j in range(3):
                gw[j] = gw[j] + jnp.sum(dpre * pltpu.roll(win, 3 - j, 0)[8:8 + RC], axis=0, keepdims=True)
        dp_ref[S:S + 8, :] = jnp.zeros((8, 128), F32)
        for n in range(S // RC):
            win = dp_ref[n * RC:(n + 1) * RC + 8, :]
            dx = w[3:4, :] * win[:RC]
            for j in range(3):
                dx = dx + w[j:j + 1, :] * pltpu.roll(win, RC + 8 - (3 - j), 0)[:RC]
            dx_ref[n * RC:(n + 1) * RC, :] = dx
        gw_ref[...] = jnp.concatenate(gw, axis=0)
        gb_ref[...] = gb

    return pl.pallas_call(
        body, name="ml_prep_bwd", grid=(8,),
        in_specs=[pl.BlockSpec((S, 128), lambda c: (0, 16 + c)),
                  pl.BlockSpec((4, 128), lambda c: (0, c)),
                  pl.BlockSpec((1, 128), lambda c: (0, c)),
                  pl.BlockSpec((S, 128), lambda c: (0, jnp.minimum(c, 3))),
                  pl.BlockSpec((S, 128), lambda c: (0, jnp.maximum(c - 4, 0)))],
        out_specs=[pl.BlockSpec((S, 128), lambda c: (0, c)),
                   pl.BlockSpec((4, 128), lambda c: (0, c)),
                   pl.BlockSpec((1, 128), lambda c: (0, c))],
        out_shape=[jax.ShapeDtypeStruct((S, 1024), F32), jax.ShapeDtypeStruct((4, 1024), F32),
                   jax.ShapeDtypeStruct((1, 1024), F32)],
        scratch_shapes=[pltpu.VMEM((S + 8, 128), F32)],
        compiler_params=_params(("arbitrary",)),
    )(proj, conv_w, conv_b, dq, dk)


def _gate_prep(proj_g, gbias):
    S = proj_g.shape[0]
    tm = 512

    def body(g_ref, b_ref, ig_ref, lf_ref):
        g = g_ref[...] + b_ref[...]
        for h in range(4):
            ig_ref[h] = jnp.broadcast_to(g[:, h:h + 1], (tm, 128))
            lf_ref[h] = jnp.broadcast_to(_log_sigmoid_parts(g[:, 4 + h:5 + h]), (tm, 128))

    return pl.pallas_call(
        body, name="gate_prep", grid=(S // tm,),
        in_specs=[pl.BlockSpec((tm, 128), lambda i: (i, 0)), pl.BlockSpec((1, 128), lambda i: (0, 0))],
        out_specs=[pl.BlockSpec((4, tm, 128), lambda i: (0, i, 0))] * 2,
        out_shape=[jax.ShapeDtypeStruct((4, S, 128), F32)] * 2,
        compiler_params=_params(("arbitrary",)),
    )(proj_g, gbias)


def _ml_chunk_fwd(q, k, v, ig, lf, ct, n_st, m_st, tri, causal):
    vf = v.astype(F32)
    vb = v.astype(BF16)
    b = sum(_dot(tri, part) for part in _split3(lf))
    b_last = b[LCH - 1:LCH, :]
    g = b_last - b + ig
    m_loc = jnp.max(g, axis=0, keepdims=True)
    w = jnp.exp(g - m_loc)
    vwf = vf * w
    vw = vwf.astype(BF16)
    ct_loc = _dot_tn(k, vw)
    kf = k.astype(F32)
    n_loc = jnp.sum(w * kf, axis=0, keepdims=True)
    r = jnp.transpose(ig - b)
    d_log = jnp.where(causal, b + r, -jnp.inf)
    m_t = jnp.maximum(b + m_st, jnp.max(d_log, axis=1, keepdims=True))
    w_in = jnp.exp(d_log - m_t)
    qk = _dot_nt(q, k)
    scores = qk * w_in
    cs = jnp.exp(b + m_st - m_t)
    ctb = ct.astype(BF16)
    qc = _dot(q, ctb)
    qf = q.astype(F32)
    qn = jnp.sum(qf * n_st, axis=1, keepdims=True)
    num = _dot(scores.astype(BF16), vb) + cs * qc
    den = jnp.sum(scores, axis=1, keepdims=True) + cs * qn
    em = jnp.exp(-m_t)
    dd = jnp.maximum(jnp.abs(den), em)
    h = num / dd
    m_new = jnp.maximum(b_last + m_st, m_loc)
    a = jnp.exp(b_last + m_st - m_new)
    gg = jnp.exp(m_loc - m_new)
    return dict(vf=vf, vb=vb, b=b, w=w, vwf=vwf, vw=vw, kf=kf, qf=qf, ct_loc=ct_loc, n_loc=n_loc, w_in=w_in, qk=qk,
                scores=scores, cs=cs, ctb=ctb, qc=qc, qn=qn, den=den, em=em, dd=dd, h=h, m_new=m_new, a=a, gg=gg)


def _ml_consts():
    row = lax.broadcasted_iota(jnp.int32, (LCH, LCH), 0)
    col = lax.broadcasted_iota(jnp.int32, (LCH, LCH), 1)
    return row, (col <= row), (col <= row).astype(BF16)


def _ml_fwd(qk, proj, ig, lf):
    S = proj.shape[0]
    nc = S // LCH

    def body(q_ref, k_ref, v_ref, ig_ref, lf_ref, h_ref, cst_ref, nm_ref, ct_ref, n_ref, m_ref):
        _, causal, tri = _ml_consts()
        ct_ref[...] = jnp.zeros_like(ct_ref)
        n_ref[...] = jnp.zeros_like(n_ref)
        m_ref[...] = jnp.zeros_like(m_ref)

        def chunk(c, carry):
            rows = pl.ds(pl.multiple_of(c * LCH, LCH), LCH)
            ct, n_st, m_st = ct_ref[...], n_ref[0:1, :], m_ref[0:1, :]
            cst_ref[c] = ct
            nm_ref[c, 0:8, :] = n_ref[...]
            nm_ref[c, 8:16, :] = m_ref[...]
            f = _ml_chunk_fwd(q_ref[rows, :], k_ref[rows, :], v_ref[rows, :], ig_ref[rows, :], lf_ref[rows, :],
                              ct, n_st, m_st, tri, causal)
            h_ref[rows, :] = f["h"]
            ct_ref[...] = f["a"] * ct + f["gg"] * f["ct_loc"]
            n_ref[...] = jnp.broadcast_to(f["a"] * n_st + f["gg"] * f["n_loc"], (8, 128))
            m_ref[...] = jnp.broadcast_to(f["m_new"], (8, 128))
            return carry

        lax.fori_loop(0, nc, chunk, 0)

    return pl.pallas_call(
        body, name="ml_fwd", grid=(4,),
        in_specs=[pl.BlockSpec((S, 128), lambda h: (0, h)),
                  pl.BlockSpec((S, 128), lambda h: (0, 4 + h)),
                  pl.BlockSpec((S, 128), lambda h: (0, 24 + h)),
                  pl.BlockSpec((None, S, 128), lambda h: (h, 0, 0)),
                  pl.BlockSpec((None, S, 128), lambda h: (h, 0, 0))],
        out_specs=[pl.BlockSpec((S, 128), lambda h: (0, h)),
                   pl.BlockSpec((None, nc, 128, 128), lambda h: (h, 0, 0, 0)),
                   pl.BlockSpec((None, nc, 16, 128), lambda h: (h, 0, 0, 0))],
        out_shape=[jax.ShapeDtypeStruct((S, ML_W), F32), jax.ShapeDtypeStruct((4, nc, 128, 128), F32),
                   jax.ShapeDtypeStruct((4, nc, 16, 128), F32)],
        scratch_shapes=[pltpu.VMEM((128, 128), F32), pltpu.VMEM((8, 128), F32), pltpu.VMEM((8, 128), F32)],
        compiler_params=_params(("arbitrary",)),
    )(qk, qk, proj, ig, lf)


def _ml_bwd(qk, proj, ig, lf, cst, nm, dh):
    S = proj.shape[0]
    nc = S // LCH

    def body(q_ref, k_ref, v_ref, ig_ref, lf_ref, cst_ref, nm_ref, dh_ref,
             dq_ref, dk_ref, dv_ref, dif_ref, gsum_ref, dct_ref, dn_ref, gi_ref):
        hd = pl.program_id(0)
        row, causal, tri = _ml_consts()
        lane = lax.broadcasted_iota(jnp.int32, (LCH, 128), 1)
        last_row = row == LCH - 1
        dct_ref[...] = jnp.zeros_like(dct_ref)
        dn_ref[...] = jnp.zeros_like(dn_ref)
        gi_ref[...] = jnp.zeros_like(gi_ref)

        @pl.when(hd == 0)
        def _():
            dif_ref[...] = jnp.zeros_like(dif_ref)

        def chunk(t, carry):
            c = nc - 1 - t
            rows = pl.ds(pl.multiple_of(c * LCH, LCH), LCH)
            q, k = q_ref[rows, :], k_ref[rows, :]
            ig_, lf_ = ig_ref[rows, :], lf_ref[rows, :]
            ct, n_st, m_st = cst_ref[c], nm_ref[c, 0:1, :], nm_ref[c, 8:9, :]
            f = _ml_chunk_fwd(q, k, v_ref[rows, :], ig_, lf_, ct, n_st, m_st, tri, causal)
            dh_ = dh_ref[rows, :]
            dct_new, dn_new = dct_ref[...], dn_ref[0:1, :]
            e_num = dh_ / f["dd"]
            hdh = jnp.sum(f["h"] * dh_, axis=1, keepdims=True)
            e_den = jnp.where(jnp.abs(f["den"]) > f["em"], -hdh / f["dd"] * jnp.sign(f["den"]), 0.0)
            e_num_b = e_num.astype(BF16)
            ds_ = _dot_nt(e_num_b, f["vb"]) + e_den
            dqk = ds_ * f["w_in"]
            gam = dqk * f["qk"]
            dqk_b = dqk.astype(BF16)
            cse = f["cs"] * e_den
            dq = _dot(dqk_b, k) + f["cs"] * _dot_nt(e_num_b, f["ctb"]) + cse * n_st
            dk = _dot_tn(dqk_b, q)
            dv = _dot_tn(f["scores"].astype(BF16), e_num_b)
            dcl = (f["gg"] * dct_new).astype(BF16)
            dnl = f["gg"] * dn_new
            kd = _dot(k, dcl)
            dv = dv + f["w"] * kd
            dk = dk + _dot_nt(f["vw"], dcl) + f["w"] * dnl
            gam_s = jnp.sum(kd * f["vwf"], axis=1, keepdims=True) + f["w"] * jnp.sum(f["kf"] * dnl, axis=1, keepdims=True)
            col_g = jnp.sum(jnp.transpose(gam), axis=1, keepdims=True) + gam_s
            db = (jnp.sum(gam, axis=1, keepdims=True) + jnp.sum(e_num * (f["cs"] * f["qc"]), axis=1, keepdims=True)
                  + cse * f["qn"] - col_g)
            db_last = jnp.sum(gam_s[:, 0:1]) + jnp.sum(f["a"][0:1, 0:1]) * (jnp.sum(dct_new * ct) + jnp.sum(dn_new * n_st))
            db = jnp.where(last_row, db + db_last, db)
            dlf = sum(_dot_tn(tri, part) for part in _split3(db))
            df = dlf * (1.0 - jnp.exp(lf_))
            dq_ref[rows, :] = dq
            dk_ref[rows, :] = dk
            dv_ref[rows, :] = dv
            dif_ref[rows, :] += jnp.where(lane == hd, col_g, 0.0) + jnp.where(lane == hd + 4, df, 0.0)
            clamped = jnp.where(jnp.abs(f["den"]) > f["em"], 0.0, hdh)
            gi_ref[...] += jnp.broadcast_to(jnp.sum(clamped, axis=0, keepdims=True), (8, 128))
            dct_ref[...] = f["a"] * dct_new + _dot_tn(q, (f["cs"] * e_num).astype(BF16))
            dn_ref[...] = jnp.broadcast_to(f["a"] * dn_new + jnp.sum(cse * f["qf"], axis=0, keepdims=True), (8, 128))
            return carry

        lax.fori_loop(0, nc, chunk, 0)
        lane8 = lax.broadcasted_iota(jnp.int32, (8, 128), 1)
        part = jnp.where(lane8 == hd, gi_ref[...], 0.0)

        @pl.when(hd == 0)
        def _():
            gsum_ref[...] = part

        @pl.when(hd > 0)
        def _():
            gsum_ref[...] += part

        @pl.when(hd == 3)
        def _():
            gsum_ref[...] += jnp.where(lane8 >= 4, jnp.sum(dif_ref[...], axis=0, keepdims=True), 0.0)

    return pl.pallas_call(
        body, name="ml_bwd", grid=(4,),
        in_specs=[pl.BlockSpec((S, 128), lambda h: (0, h)),
                  pl.BlockSpec((S, 128), lambda h: (0, 4 + h)),
                  pl.BlockSpec((S, 128), lambda h: (0, 24 + h)),
                  pl.BlockSpec((None, S, 128), lambda h: (h, 0, 0)),
                  pl.BlockSpec((None, S, 128), lambda h: (h, 0, 0)),
                  pl.BlockSpec((None, nc, 128, 128), lambda h: (h, 0, 0, 0)),
                  pl.BlockSpec((None, nc, 16, 128), lambda h: (h, 0, 0, 0)),
                  pl.BlockSpec((S, 128), lambda h: (0, h))],
        out_specs=[pl.BlockSpec((S, 128), lambda h: (0, h))] * 3 + [pl.BlockSpec((S, 128), lambda h: (0, 0)),
                                                                       pl.BlockSpec((8, 128), lambda h: (0, 0))],
        out_shape=[jax.ShapeDtypeStruct((S, ML_W), F32)] * 3 + [jax.ShapeDtypeStruct((S, 128), F32),
                                                                 jax.ShapeDtypeStruct((8, 128), F32)],
        scratch_shapes=[pltpu.VMEM((128, 128), F32), pltpu.VMEM((8, 128), F32), pltpu.VMEM((8, 128), F32)],
        compiler_params=_params(("arbitrary",)),
    )(qk, qk, proj, ig, lf, cst, nm, dh)


def _inproj_fwd(x, pre_w, w_t, wg_t):
    S, D = x.shape
    tm, tn = 512, 512

    def body(x_ref, pw_ref, w_ref, wg_ref, proj_ref, g_ref, u_ref):
        @pl.when(pl.program_id(1) == 0)
        def _():
            xf = x_ref[...]
            r = lax.rsqrt(jnp.mean(xf * xf, axis=-1, keepdims=True) + EPS)
            u = (xf * r * pw_ref[...]).astype(BF16)
            u_ref[...] = u
            g_ref[...] = _dot_nt(u, wg_ref[...])

        proj_ref[...] = _dot_nt(u_ref[...], w_ref[...])

    return pl.pallas_call(
        body, name="inproj_fwd", grid=(S // tm, N_MAIN // tn),
        in_specs=[pl.BlockSpec((tm, D), lambda i, j: (i, 0)),
                  pl.BlockSpec((1, D), lambda i, j: (0, 0)),
                  pl.BlockSpec((tn, D), lambda i, j: (j, 0)),
                  pl.BlockSpec((128, D), lambda i, j: (0, 0))],
        out_specs=[pl.BlockSpec((tm, tn), lambda i, j: (i, j)),
                   pl.BlockSpec((tm, 128), lambda i, j: (i, 0)),
                   pl.BlockSpec((tm, D), lambda i, j: (i, 0))],
        out_shape=[jax.ShapeDtypeStruct((S, N_MAIN), F32), jax.ShapeDtypeStruct((S, 128), F32),
                   jax.ShapeDtypeStruct((S, D), BF16)],
        compiler_params=_params(("arbitrary", "arbitrary")),
    )(x, pre_w, w_t, wg_t)


def _inproj_bwd(d_main, d_if, w_t, wg_t, x, pre_w, dx_tail):
    S, D = x.shape
    tm, tk = 512, 512
    nk = N_MAIN // tk

    def body(d_ref, dg_ref, w_ref, wg_ref, x_ref, pw_ref, dt_ref, dx_ref, gpw_ref, acc_ref):
        i, k = pl.program_id(0), pl.program_id(1)

        @pl.when(k == 0)
        def _():
            acc_ref[...] = _dot(dg_ref[...], wg_ref[...])

        acc_ref[...] += _dot(d_ref[...], w_ref[...])

        @pl.when(k == nk - 1)
        def _():
            xf = x_ref[...]
            r = lax.rsqrt(jnp.mean(xf * xf, axis=-1, keepdims=True) + EPS)
            xn = xf * r
            du = acc_ref[...]
            gw = du * pw_ref[...]
            dx_ref[...] = dt_ref[...] + r * (gw - xn * jnp.mean(gw * xn, axis=-1, keepdims=True))
            part = jnp.sum(du * xn, axis=0, keepdims=True)

            @pl.when(i == 0)
            def _():
                gpw_ref[...] = part

            @pl.when(i > 0)
            def _():
                gpw_ref[...] += part

    return pl.pallas_call(
        body, name="inproj_bwd", grid=(S // tm, nk),
        in_specs=[pl.BlockSpec((tm, tk), lambda i, k: (i, k)),
                  pl.BlockSpec((tm, 128), lambda i, k: (i, 0)),
                  pl.BlockSpec((tk, D), lambda i, k: (k, 0)),
                  pl.BlockSpec((128, D), lambda i, k: (0, 0)),
                  pl.BlockSpec((tm, D), lambda i, k: (i, 0)),
                  pl.BlockSpec((1, D), lambda i, k: (0, 0)),
                  pl.BlockSpec((tm, D), lambda i, k: (i, 0))],
        out_specs=[pl.BlockSpec((tm, D), lambda i, k: (i, 0)),
                   pl.BlockSpec((1, D), lambda i, k: (0, 0))],
        out_shape=[jax.ShapeDtypeStruct((S, D), F32), jax.ShapeDtypeStruct((1, D), F32)],
        scratch_shapes=[pltpu.VMEM((tm, D), F32)],
        compiler_params=_params(("arbitrary", "arbitrary")),
    )(d_main, d_if, w_t, wg_t, x, pre_w, dx_tail)


def _matmul_tn(a, b, name):
    S, M = a.shape
    N = b.shape[1]
    tmm = min(M, 512)
    tk = 512
    nk = S // tk

    def body(a_ref, b_ref, o_ref):
        part = _dot_tn(a_ref[...].astype(BF16), b_ref[...].astype(BF16))

        @pl.when(pl.program_id(1) == 0)
        def _():
            o_ref[...] = part

        @pl.when(pl.program_id(1) > 0)
        def _():
            o_ref[...] += part

    return pl.pallas_call(
        body, name=name, grid=(M // tmm, nk),
        in_specs=[pl.BlockSpec((tk, tmm), lambda i, k: (k, i)),
                  pl.BlockSpec((tk, N), lambda i, k: (k, 0))],
        out_specs=pl.BlockSpec((tmm, N), lambda i, k: (i, 0)),
        out_shape=jax.ShapeDtypeStruct((M, N), F32),
        compiler_params=_params(("arbitrary", "arbitrary")),
    )(a, b)


def _half_mean(v, low):
    s_lo = jnp.sum(jnp.where(low, v, 0.0), axis=1, keepdims=True)
    s_hi = jnp.sum(jnp.where(low, 0.0, v), axis=1, keepdims=True)
    return jnp.where(low, s_lo, s_hi) * (1.0 / 64.0)


def _silu_grad(z, s):
    return s * (1.0 + z * (1.0 - s))


def _tail(y_sb, h_ml, proj, x, p, target, sb_nw, ml_nw, w_out, post_w, w_gate, b_gate, w_up):
    S, D = x.shape
    tm = 256

    def body(ysb_ref, hml_ref, sbz_ref, mlo_ref, mlz_ref, x_ref, p_ref, tg_ref, sbw_ref, mlw_ref, wo_ref, pw_ref,
             wg_ref, bg_ref, wu_ref,
             dx_ref, dysb_ref, dhml_ref, dsbz_ref, dmlo_ref, dmlz_ref, mix_ref, dy_ref, h1_ref, dgp_ref, dpu_ref,
             small_ref):
        lane = lax.broadcasted_iota(jnp.int32, (tm, 128), 1)
        low = lane < 64
        sb_saved, ml_saved, mixed = [], [], []
        for s in range(4):
            sl = slice(128 * s, 128 * s + 128)
            y = ysb_ref[:, sl]
            rs = lax.rsqrt(_half_mean(y * y, low) + EPS)
            n = y * rs
            z = sbz_ref[:, sl]
            sg = jax.nn.sigmoid(z)
            w = sbw_ref[:, sl]
            mixed.append((n * w) * (z * sg))
            sb_saved.append((rs, n, z, sg, w))
        for s in range(4):
            sl = slice(128 * s, 128 * s + 128)
            og = jax.nn.sigmoid(mlo_ref[:, sl])
            hh = hml_ref[:, sl]
            t = og * hh
            rs = lax.rsqrt(jnp.mean(t * t, axis=1, keepdims=True) + EPS)
            n = t * rs
            z = mlz_ref[:, sl]
            sg = jax.nn.sigmoid(z)
            w = mlw_ref[:, sl]
            mixed.append((n * w) * (z * sg))
            ml_saved.append((rs, n, z, sg, w, og, hh))
        mix = jnp.concatenate(mixed, axis=1).astype(BF16)
        mix_ref[...] = mix
        y = _dot(mix, wo_ref[...])
        rs_y = lax.rsqrt(jnp.mean(y * y, axis=1, keepdims=True) + EPS)
        yn = y * rs_y
        pw = pw_ref[...]
        h1 = x_ref[...] + yn * pw
        h1b = h1.astype(BF16)
        h1_ref[...] = h1b
        gate = jax.nn.sigmoid(_dot(h1b, wg_ref[...]) + bg_ref[...])
        pu = _dot(p_ref[...].astype(BF16), wu_ref[...])
        err = (h1 + gate * pu) - tg_ref[...]
        loss = 0.5 * jnp.sum(jnp.sum(err * err, axis=1, keepdims=True) * (1.0 / D))
        d_out = err * (1.0 / D)
        dpu_ref[...] = (d_out * gate).astype(BF16)
        dgp = (d_out * pu) * (gate * (1.0 - gate))
        dgpb = dgp.astype(BF16)
        dgp_ref[...] = dgpb
        d_h1 = d_out + _dot_nt(dgpb, wg_ref[...])
        dx_ref[...] = d_h1
        gwy = d_h1 * pw
        d_y = rs_y * (gwy - yn * jnp.mean(gwy * yn, axis=1, keepdims=True))
        d_yb = d_y.astype(BF16)
        dy_ref[...] = d_yb
        d_mix = _dot_nt(d_yb, wo_ref[...])
        g_nw = []
        for s in range(4):
            sl = slice(128 * s, 128 * s + 128)
            rs, n, z, sg, w = sb_saved[s]
            da = d_mix[:, sl]
            act = z * sg
            dsbz_ref[:, sl] = (da * (n * w) * _silu_grad(z, sg)).astype(BF16)
            dn = da * w * act
            g_nw.append(jnp.sum(da * act * n, axis=0, keepdims=True))
            dysb_ref[:, sl] = rs * (dn - n * _half_mean(dn * n, low))
        for s in range(4):
            sl = slice(128 * s, 128 * s + 128)
            rs, n, z, sg, w, og, hh = ml_saved[s]
            da = d_mix[:, 512 + 128 * s:512 + 128 * s + 128]
            act = z * sg
            dmlz_ref[:, sl] = (da * (n * w) * _silu_grad(z, sg)).astype(BF16)
            dn = da * w * act
            g_nw.append(jnp.sum(da * act * n, axis=0, keepdims=True))
            dt = rs * (dn - n * jnp.mean(dn * n, axis=1, keepdims=True))
            dmlo_ref[:, sl] = (dt * hh * (og * (1.0 - og))).astype(BF16)
            dhml_ref[:, sl] = dt * og
        upd = jnp.concatenate([
            jnp.sum(d_h1 * yn, axis=0, keepdims=True),
            jnp.sum(dgp, axis=0, keepdims=True),
            jnp.concatenate(g_nw, axis=1),
            jnp.full((1, D), loss, F32),
            jnp.zeros((4, D), F32)], axis=0)

        @pl.when(pl.program_id(0) == 0)
        def _():
            small_ref[...] = upd

        @pl.when(pl.program_id(0) > 0)
        def _():
            small_ref[...] += upd

    def rows(width, col=0):
        return pl.BlockSpec((tm, width), lambda i: (i, col))

    def whole(a):
        return pl.BlockSpec(a.shape, lambda i: (0, 0))

    return pl.pallas_call(
        body, name="tail", grid=(S // tm,),
        in_specs=[rows(512), rows(512), rows(512, 3), rows(512, 7), rows(512, 8), rows(D), rows(256), rows(D),
                  whole(sb_nw), whole(ml_nw), whole(w_out), whole(post_w), whole(w_gate), whole(b_gate), whole(w_up)],
        out_specs=[rows(D), rows(512), rows(512), rows(512), rows(512), rows(512), rows(D), rows(D), rows(D), rows(D),
                   rows(D), pl.BlockSpec((8, D), lambda i: (0, 0))],
        out_shape=[jax.ShapeDtypeStruct((S, D), F32), jax.ShapeDtypeStruct((S, 512), F32),
                   jax.ShapeDtypeStruct((S, 512), F32)] + [jax.ShapeDtypeStruct((S, 512), BF16)] * 3
        + [jax.ShapeDtypeStruct((S, D), BF16)] * 5 + [jax.ShapeDtypeStruct((8, D), F32)],
        compiler_params=_params(("arbitrary",)),
    )(y_sb, h_ml, proj, proj, proj, x, p, target, sb_nw, ml_nw, w_out, post_w, w_gate, b_gate, w_up)


def _local_step(x, p, target, pre_w, w_t, wg_t, conv_w, conv_b, gbias, sb_nw, ml_nw, w_out, post_w, w_gate, b_gate,
                w_up):
    proj, proj_g, u = _inproj_fwd(x, pre_w, w_t, wg_t)
    y_sb, tot = _sb_fwd(proj)
    qk = _ml_prep(proj, conv_w, conv_b)
    ig, lf = _gate_prep(proj_g, gbias)
    h_ml, cst, nm = _ml_fwd(qk, proj, ig, lf)
    dx_tail, d_ysb, d_hml, d_sbz, d_mlo, d_mlz, mix, d_y, h1, dgp, dpu, small = _tail(
        y_sb, h_ml, proj, x, p, target, sb_nw, ml_nw, w_out, post_w, w_gate, b_gate, w_up)
    dq, dk, dv = _sb_bwd(proj, tot, d_ysb)
    dqc, dks, dmlv, dif, gif = _ml_bwd(qk, proj, ig, lf, cst, nm, d_hml)
    dmlqk, g_cw, g_cb = _ml_prep_bwd(proj, conv_w, conv_b, dqc, dks)
    d_main = jnp.concatenate([dq.astype(BF16), dk.astype(BF16), dv.astype(BF16), d_sbz, dmlqk.astype(BF16),
                              dmlv.astype(BF16), d_mlo, d_mlz], axis=1)
    d_if = dif.astype(BF16)
    dx, g_pre = _inproj_bwd(d_main, d_if, w_t, wg_t, x, pre_w, dx_tail)
    grads = dict(
        w_t=_matmul_tn(d_main, u, "gw_in"), wg_t=_matmul_tn(d_if, u, "gw_in_gates"),
        w_out=_matmul_tn(mix, d_y, "gw_out"), w_gate=_matmul_tn(h1, dgp, "gw_gate"), w_up=_matmul_tn(p, dpu, "gw_up"),
        conv_w=g_cw, conv_b=g_cb, pre_w=g_pre, gif=gif)
    return dx, grads, small


MESH = pl.DeviceIdType.MESH
ANY = pl.BlockSpec(memory_space=pl.ANY)
N_DEV = 8


def _place():
    return lax.axis_index("x"), lax.axis_index("y"), lax.axis_index("c")


def _block_of(px, py, pc):
    return 4 * px + 2 * py + pc


def _all_gather(a, b):
    def body(a_ref, b_ref, oa_ref, ob_ref, send_sems, recv_sems, local_sems):
        x, y, c = _place()
        me, sibling = (x, y, c), (x, y, 1 - c)
        chips = [(1 - x, y), (x, 1 - y), (1 - x, 1 - y)]
        pairs = ((a_ref, oa_ref), (b_ref, ob_ref))

        def copies(k, block, to, from_input=False):
            slot = _block_of(*block)
            return [pltpu.make_async_remote_copy(
                src_ref=src if from_input else out.at[slot], dst_ref=out.at[slot],
                send_sem=send_sems.at[t, k], recv_sem=recv_sems.at[t, k], device_id=to, device_id_type=MESH)
                for t, (src, out) in enumerate(pairs)]

        mine = [pltpu.make_async_copy(src, out.at[_block_of(*me)], local_sems.at[t])
                for t, (src, out) in enumerate(pairs)]
        for cp in mine:
            cp.start()
        first = copies(0, me, sibling, True)
        for j, chip in enumerate(chips):
            first += copies(1 + j, me, (*chip, c), True)
        for cp in first:
            cp.start()
        passed = []
        for j, chip in enumerate(chips):
            for cp in copies(1 + j, (*chip, c), me):
                cp.wait_recv()
            fwd = copies(4 + j, (*chip, c), sibling)
            for cp in fwd:
                cp.start()
            passed += fwd
        for cp in copies(0, sibling, me):
            cp.wait_recv()
        for j, chip in enumerate(chips):
            for cp in copies(4 + j, (*chip, 1 - c), me):
                cp.wait_recv()
        for cp in first + passed:
            cp.wait_send()
        for cp in mine:
            cp.wait()

    return pl.pallas_call(
        body, name="all_gather",
        in_specs=[ANY, ANY], out_specs=[ANY, ANY],
        out_shape=[jax.ShapeDtypeStruct((N_DEV,) + a.shape, a.dtype), jax.ShapeDtypeStruct((N_DEV,) + b.shape, b.dtype)],
        scratch_shapes=[pltpu.SemaphoreType.DMA((2, 7)), pltpu.SemaphoreType.DMA((2, 7)), pltpu.SemaphoreType.DMA((2,))],
    )(a, b)


def _exchange(g, s):
    def body(g_ref, s_ref, og_ref, os_ref, send_sems, recv_sems, local_sems):
        x, y, c = _place()
        mine = _block_of(x, y, c)
        local = [pltpu.make_async_copy(g_ref.at[mine], og_ref.at[mine], local_sems.at[0]),
                 pltpu.make_async_copy(s_ref, os_ref.at[mine], local_sems.at[1])]
        for cp in local:
            cp.start()
        sent = []
        for k in range(1, N_DEV):
            peer = (x ^ (k >> 2), y ^ ((k >> 1) & 1), c ^ (k & 1))
            sent.append(pltpu.make_async_remote_copy(
                src_ref=g_ref.at[_block_of(*peer)], dst_ref=og_ref.at[mine],
                send_sem=send_sems.at[0, k - 1], recv_sem=recv_sems.at[0, k - 1], device_id=peer, device_id_type=MESH))
            sent.append(pltpu.make_async_remote_copy(
                src_ref=s_ref, dst_ref=os_ref.at[mine],
                send_sem=send_sems.at[1, k - 1], recv_sem=recv_sems.at[1, k - 1], device_id=peer, device_id_type=MESH))
        for cp in sent:
            cp.start()
        for cp in sent:
            cp.wait()
        for cp in local:
            cp.wait()

    return pl.pallas_call(
        body, name="exchange",
        in_specs=[ANY, ANY], out_specs=[ANY, ANY],
        out_shape=[jax.ShapeDtypeStruct(g.shape, g.dtype), jax.ShapeDtypeStruct((N_DEV,) + s.shape, s.dtype)],
        scratch_shapes=[pltpu.SemaphoreType.DMA((2, 7)), pltpu.SemaphoreType.DMA((2, 7)), pltpu.SemaphoreType.DMA((2,))],
    )(g, s)


ADAM_LR, ADAM_B1, ADAM_B2, ADAM_EPS, ADAM_WD, ADAM_STEP = 0.001, 0.9, 0.999, 1e-08, 0.01, 10


def _adamw(w, g, m, v):
    m = ADAM_B1 * m + (1.0 - ADAM_B1) * g
    v = ADAM_B2 * v + (1.0 - ADAM_B2) * (g * g)
    m_hat = m / (1.0 - ADAM_B1 ** ADAM_STEP)
    v_hat = v / (1.0 - ADAM_B2 ** ADAM_STEP)
    return -ADAM_LR * (m_hat / (jnp.sqrt(v_hat) + ADAM_EPS) + ADAM_WD * w), m, v


def _adam(parts, w, m, v, tr, name):
    R, D = w.shape

    def body(p_ref, w_ref, m_ref, v_ref, g_ref, d_ref, nm_ref, nv_ref):
        g = p_ref[0]
        for k in range(1, N_DEV):
            g = g + p_ref[k]
        g_ref[...] = g
        d_ref[...], nm_ref[...], nv_ref[...] = _adamw(w_ref[...], g, m_ref[...], v_ref[...])

    blk = pl.BlockSpec((tr, D), lambda i: (i, 0))
    return pl.pallas_call(
        body, name=name, grid=(R // tr,),
        in_specs=[pl.BlockSpec((N_DEV, tr, D), lambda i: (0, i, 0)), blk, blk, blk],
        out_specs=[blk] * 4, out_shape=[jax.ShapeDtypeStruct((R, D), F32)] * 4,
        compiler_params=_params(("arbitrary",)),
    )(parts, w, m, v)


ROWS_IN = 592
ROWS_BF16 = ROWS_IN + 128 + 128 + 32
ROWS_ALL = ROWS_BF16 + 8


def _pad_rows(a, rows):
    return jnp.pad(a, ((0, rows - a.shape[0]), (0, 0)))


def _pack_shards(w_in, w_out, w_gate, w_up, conv_w):
    return jnp.concatenate([
        _pad_rows(w_in[0].T, ROWS_IN), w_out[0], w_gate[0], w_up[0].reshape(32, D_MODEL),
        _pad_rows(jnp.pad(conv_w[0].reshape(1, 512), ((0, 0), (0, 512))), 8)], axis=0)


def _unpack_shards(a):
    return (a[:SHARD_IN].T[None], a[ROWS_IN:ROWS_IN + 128][None], a[ROWS_IN + 128:ROWS_IN + 256][None],
            a[ROWS_IN + 256:ROWS_BF16].reshape(1, 256, 128), a[ROWS_BF16, :512].reshape(1, 4, 128))


def _pack_small(pre_w, conv_b, i_bias, f_bias, sb_nw, ml_nw, post_w, b_gate):
    gates = jnp.pad(jnp.concatenate([i_bias, f_bias], axis=1), ((0, 0), (0, D_MODEL - 8)))
    return jnp.concatenate([post_w, b_gate, jnp.concatenate([sb_nw, ml_nw], axis=1), jnp.zeros((1, D_MODEL), F32),
                            pre_w, conv_b, gates, jnp.zeros((1, D_MODEL), F32)], axis=0)


def _unpack_small(a):
    return a[4:5], a[5:6], a[6:7, 0:4], a[6:7, 4:8], a[2:3, :512], a[2:3, 512:], a[0:1], a[1:2]


def kernel(x, p, pre_norm_w, w_in, ml_conv_w, ml_conv_b, ml_i_bias, ml_f_bias, sb_norm_w, ml_norm_w, w_out, post_norm_w, ple_w_up, ple_w_gate, ple_b_gate, loss_target, m_pre_norm_w, m_w_in, m_ml_conv_w, m_ml_conv_b, m_ml_i_bias, m_ml_f_bias, m_sb_norm_w, m_ml_norm_w, m_w_out, m_post_norm_w, m_ple_w_up, m_ple_w_gate, m_ple_b_gate, v_pre_norm_w, v_w_in, v_ml_conv_w, v_ml_conv_b, v_ml_i_bias, v_ml_f_bias, v_sb_norm_w, v_ml_norm_w, v_w_out, v_post_norm_w, v_ple_w_up, v_ple_w_gate, v_ple_b_gate):
    D = D_MODEL
    w_pk = _pack_shards(w_in, w_out, ple_w_gate, ple_w_up, ml_conv_w)
    m_pk = _pack_shards(m_w_in, m_w_out, m_ple_w_gate, m_ple_w_up, m_ml_conv_w)
    v_pk = _pack_shards(v_w_in, v_w_out, v_ple_w_gate, v_ple_w_up, v_ml_conv_w)
    w_sm = _pack_small(pre_norm_w, ml_conv_b, ml_i_bias, ml_f_bias, sb_norm_w, ml_norm_w, post_norm_w, ple_b_gate)
    m_sm = _pack_small(m_pre_norm_w, m_ml_conv_b, m_ml_i_bias, m_ml_f_bias, m_sb_norm_w, m_ml_norm_w, m_post_norm_w, m_ple_b_gate)
    v_sm = _pack_small(v_pre_norm_w, v_ml_conv_b, v_ml_i_bias, v_ml_f_bias, v_sb_norm_w, v_ml_norm_w, v_post_norm_w, v_ple_b_gate)

    ga, gb = _all_gather(w_pk[:ROWS_BF16].astype(BF16), w_pk[ROWS_BF16:])
    w_in_t = ga[:, :SHARD_IN].reshape(N_IN, D)
    w_t = w_in_t[:N_MAIN]
    wg_t = _pad_rows(w_in_t[N_MAIN:], 128)
    w_out_f = ga[:, ROWS_IN:ROWS_IN + 128].reshape(D, D)
    w_gate_f = ga[:, ROWS_IN + 128:ROWS_IN + 256].reshape(D, D)
    w_up_f = ga[:, ROWS_IN + 256:].reshape(N_DEV, 256, 128).transpose(1, 0, 2).reshape(256, D)
    conv_w_f = gb[:, 0, :512].reshape(N_DEV, 4, 128).transpose(1, 0, 2).reshape(4, D)
    gbias = jnp.pad(jnp.concatenate([ml_i_bias, ml_f_bias], axis=1), ((0, 0), (0, 120)))

    dx, g, small = _local_step(x[0], p[0, 0], loss_target[0], pre_norm_w, w_t, wg_t, conv_w_f, ml_conv_b, gbias,
                               sb_norm_w, ml_norm_w, w_out_f, post_norm_w, w_gate_f, ple_b_gate, w_up_f)

    g_in = jnp.concatenate([g["w_t"], g["wg_t"][:8]], axis=0).reshape(N_DEV, SHARD_IN, D)
    g_blocks = jnp.concatenate([
        jnp.pad(g_in, ((0, 0), (0, ROWS_IN - SHARD_IN), (0, 0))),
        g["w_out"].reshape(N_DEV, 128, D), g["w_gate"].reshape(N_DEV, 128, D),
        g["w_up"].reshape(256, N_DEV, 128).transpose(1, 0, 2).reshape(N_DEV, 32, D),
        jnp.pad(g["conv_w"].reshape(4, N_DEV, 128).transpose(1, 0, 2).reshape(N_DEV, 1, 512), ((0, 0), (0, 7), (0, 512))),
    ], axis=1)
    g_small = jnp.concatenate([small[0:4], g["pre_w"], g["conv_b"], jnp.pad(g["gif"][0:1], ((0, 0), (0, D - 128))),
                               jnp.zeros((1, D), F32)], axis=0)
    parts, parts_sm = _exchange(g_blocks, g_small)

    grad_pk, delta_pk, nm_pk, nv_pk = _adam(parts, w_pk, m_pk, v_pk, ROWS_ALL // 3, "adam")
    grad_sm, delta_sm, nm_sm, nv_sm = _adam(parts_sm, w_sm, m_sm, v_sm, 8, "adam_small")
    loss = grad_sm[3, 0]

    def ordered(pk, sm):
        win, wout, wgate, wup, convw = _unpack_shards(pk)
        pre_w, conv_b, i_b, f_b, sb_nw, ml_nw, post_w, b_gate = _unpack_small(sm)
        return [pre_w, win, convw, conv_b, i_b, f_b, sb_nw, ml_nw, wout, post_w, wup, wgate, b_gate]

    return (loss, dx[None], *ordered(grad_pk, grad_sm), *ordered(delta_pk, delta_sm), *ordered(nm_pk, nm_sm),
            *ordered(nv_pk, nv_sm))
```

```python
import functools

import jax
import jax.numpy as jnp
from jax import lax
from jax.experimental import pallas as pl
from jax.experimental.pallas import tpu as pltpu

F32 = jnp.float32
BF16 = jnp.bfloat16
EPS = 1e-6
D_MODEL = 1024
SB_W = 512
ML_W = 512
N_MAIN = 4608
N_IN = 4616
SHARD_IN = 577
SHARD_IN_PAD = 584
TQ = 1024
TK = 128
ND = TQ // TK
LCH = 128
VMEM_LIMIT = 56 * 1024 * 1024


def _dot(a, b):
    return jnp.dot(a, b, preferred_element_type=F32)


def _dot_nt(a, b):
    return lax.dot_general(a, b, (((1,), (1,)), ((), ())), preferred_element_type=F32)


def _dot_tn(a, b):
    return lax.dot_general(a, b, (((0,), (0,)), ((), ())), preferred_element_type=F32)


def _split2(x):
    hi = x.astype(BF16)
    lo = (x - hi.astype(F32)).astype(BF16)
    return hi, lo


def _split3(x):
    hi = x.astype(BF16)
    r = x - hi.astype(F32)
    mid = r.astype(BF16)
    lo = (r - mid.astype(F32)).astype(BF16)
    return hi, mid, lo


def _params(sem):
    return pltpu.CompilerParams(dimension_semantics=sem, vmem_limit_bytes=VMEM_LIMIT)


def _log_sigmoid_parts(z):
    e = jnp.exp(-jnp.abs(z))
    return jnp.minimum(z, 0.0) - jnp.log(1.0 + e)


def _sb_fwd(proj):
    S = proj.shape[0]
    nq = S // TQ

    def body(q_ref, k_ref, v_ref, y_ref, t_ref, acc_ref, car_ref):
        i = pl.program_id(1)
        low = lax.broadcasted_iota(jnp.int32, (TQ, 128), 1) < 64
        row = lax.broadcasted_iota(jnp.int32, (TK, TK), 0)
        col = lax.broadcasted_iota(jnp.int32, (TK, TK), 1)
        uo = jnp.concatenate([(row > col).astype(BF16), jnp.ones((TK, 128), BF16)], axis=1)
        uo = jnp.concatenate([uo, uo], axis=0)
        q = q_ref[...] * 0.125
        qh = (jnp.where(low, q, 0.0).astype(BF16), jnp.where(low, 0.0, q).astype(BF16))
        acc_ref[...] = jnp.zeros_like(acc_ref)
        car_ref[...] = jnp.zeros_like(car_ref)

        def block(j, r0):
            diag = r0 is not None
            r0 = r0 or 0
            if diag:
                strict = (lax.broadcasted_iota(jnp.int32, (TQ - r0, TK), 1)
                          < lax.broadcasted_iota(jnp.int32, (TQ - r0, TK), 0))
            rows = pl.ds(pl.multiple_of(j * TK, TK), TK)
            kb = k_ref[rows, :].astype(BF16)
            vb = v_ref[rows, :].astype(BF16)
            for h in range(2):
                z = _dot_nt(qh[h][r0:], kb)
                lb = _log_sigmoid_parts(z)
                lk = lb - z
                if diag:
                    lk = jnp.where(strict, lk, 0.0)
                hi, lo = _split2(lk)
                rr = _dot(jnp.concatenate([hi, lo], axis=1), uo)
                a = jnp.exp(lb + car_ref[h, r0:, :] + rr[:, :128])
                if diag:
                    a = jnp.where(strict, a, 0.0)
                acc_ref[h, r0:, :] += _dot(a.astype(BF16), vb)
                car_ref[h, r0:, :] += rr[:, 128:]

        for d in reversed(range(ND)):
            block(ND * i + d, TK * d)

        def loop(n, c):
            block(ND * i - 1 - n, None)
            return c

        lax.fori_loop(0, ND * i, loop, 0)
        y_ref[...] = jnp.where(low, acc_ref[0], acc_ref[1])
        t_ref[...] = jnp.where(low, car_ref[0], car_ref[1])

    return pl.pallas_call(
        body, name="sb_fwd", grid=(4, nq),
        in_specs=[pl.BlockSpec((TQ, 128), lambda p, i: (i, p)),
                  pl.BlockSpec((S, 128), lambda p, i: (0, 4 + p)),
                  pl.BlockSpec((S, 128), lambda p, i: (0, 8 + p))],
        out_specs=[pl.BlockSpec((TQ, 128), lambda p, i: (i, p)),
                   pl.BlockSpec((TQ, 128), lambda p, i: (i, p))],
        out_shape=[jax.ShapeDtypeStruct((S, SB_W), F32), jax.ShapeDtypeStruct((S, SB_W), F32)],
        scratch_shapes=[pltpu.VMEM((2, TQ, 128), F32), pltpu.VMEM((2, TQ, 128), F32)],
        compiler_params=_params(("arbitrary", "arbitrary")),
    )(proj, proj, proj)


def _sb_bwd(proj, tot, dy):
    S = proj.shape[0]
    nq = S // TQ

    def body(q_ref, k_ref, v_ref, t_ref, dy_ref, dq_ref, dk_ref, dv_ref, dqa_ref, cp_ref, cg_ref):
        i = pl.program_id(1)
        low = lax.broadcasted_iota(jnp.int32, (TQ, 128), 1) < 64
        row = lax.broadcasted_iota(jnp.int32, (TK, TK), 0)
        col = lax.broadcasted_iota(jnp.int32, (TK, TK), 1)
        ones = jnp.ones((TK, 128), BF16)
        u_inc = jnp.concatenate([(row <= col).astype(BF16), ones], axis=1)
        u_exc = jnp.concatenate([(row < col).astype(BF16), ones], axis=1)
        u_inc = jnp.concatenate([u_inc, u_inc], axis=0)
        u_exc = jnp.concatenate([u_exc, u_exc], axis=0)
        q = q_ref[...] * 0.125
        qh = (jnp.where(low, q, 0.0).astype(BF16), jnp.where(low, 0.0, q).astype(BF16))
        dy_ = dy_ref[...]
        dyh = (jnp.where(low, dy_, 0.0).astype(BF16), jnp.where(low, 0.0, dy_).astype(BF16))
        t_ = t_ref[...]
        t_sw = pltpu.roll(t_, 64, 1)
        th = (jnp.where(low, t_, t_sw), jnp.where(low, t_sw, t_))
        dqa_ref[...] = jnp.zeros_like(dqa_ref)
        cp_ref[...] = jnp.zeros_like(cp_ref)
        cg_ref[...] = jnp.zeros_like(cg_ref)

        @pl.when(i == 0)
        def _():
            dk_ref[...] = jnp.zeros_like(dk_ref)
            dv_ref[...] = jnp.zeros_like(dv_ref)

        def block(j, r0):
            diag = r0 is not None
            r0 = r0 or 0
            if diag:
                strict = (lax.broadcasted_iota(jnp.int32, (TQ - r0, TK), 1)
                          < lax.broadcasted_iota(jnp.int32, (TQ - r0, TK), 0))
            rows = pl.ds(pl.multiple_of(j * TK, TK), TK)
            kb = k_ref[rows, :].astype(BF16)
            vb = v_ref[rows, :].astype(BF16)
            dk_acc = jnp.zeros((TK, 128), F32)
            dv_acc = jnp.zeros((TK, 128), F32)
            for h in range(2):
                qr, dyr = qh[h][r0:], dyh[h][r0:]
                z = _dot_nt(qr, kb)
                lb = _log_sigmoid_parts(z)
                lk = lb - z
                if diag:
                    lk = jnp.where(strict, lk, 0.0)
                hi, lo = _split2(lk)
                pp = _dot(jnp.concatenate([hi, lo], axis=1), u_inc)
                a = jnp.exp(lb + ((th[h][r0:] - cp_ref[h, r0:, :]) - pp[:, :128]))
                if diag:
                    a = jnp.where(strict, a, 0.0)
                g = _dot_nt(dyr, vb) * a
                ghi, glo = _split2(g)
                gg = _dot(jnp.concatenate([ghi, glo], axis=1), u_exc)
                beta = jnp.exp(lb)
                dz = g - beta * (g + (cg_ref[h, r0:, :] + gg[:, :128]))
                if diag:
                    dz = jnp.where(strict, dz, 0.0)
                dzb = dz.astype(BF16)
                dqa_ref[h, r0:, :] += _dot(dzb, kb)
                dk_acc += _dot_tn(dzb, qr)
                dv_acc += _dot_tn(a.astype(BF16), dyr)
                cp_ref[h, r0:, :] += pp[:, 128:]
                cg_ref[h, r0:, :] += gg[:, 128:]
            dk_ref[rows, :] += dk_acc
            dv_ref[rows, :] += dv_acc

        def loop(j, c):
            block(j, None)
            return c

        lax.fori_loop(0, ND * i, loop, 0)
        for d in range(ND):
            block(ND * i + d, TK * d)
        dq_ref[...] = jnp.where(low, dqa_ref[0], dqa_ref[1]) * 0.125

    return pl.pallas_call(
        body, name="sb_bwd", grid=(4, nq),
        in_specs=[pl.BlockSpec((TQ, 128), lambda p, i: (i, p)),
                  pl.BlockSpec((S, 128), lambda p, i: (0, 4 + p)),
                  pl.BlockSpec((S, 128), lambda p, i: (0, 8 + p)),
                  pl.BlockSpec((TQ, 128), lambda p, i: (i, p)),
                  pl.BlockSpec((TQ, 128), lambda p, i: (i, p))],
        out_specs=[pl.BlockSpec((TQ, 128), lambda p, i: (i, p)),
                   pl.BlockSpec((S, 128), lambda p, i: (0, p)),
                   pl.BlockSpec((S, 128), lambda p, i: (0, p))],
        out_shape=[jax.ShapeDtypeStruct((S, SB_W), F32)] * 3,
        scratch_shapes=[pltpu.VMEM((2, TQ, 128), F32)] * 3,
        compiler_params=_params(("arbitrary", "arbitrary")),
    )(proj, proj, proj, tot, dy)


ML_SCALE = 128 ** -0.5
RC = 256


def _conv_taps(cur, prev8, w):
    n = cur.shape[0]
    win = jnp.concatenate([prev8, cur], axis=0)
    out = w[3:4, :] * cur
    for j in range(3):
        out = out + w[j:j + 1, :] * pltpu.roll(win, 3 - j, 0)[8:8 + n]
    return out


def _ml_prep(proj, conv_w, conv_b):
    S = proj.shape[0]

    def body(x_ref, w_ref, b_ref, o_ref):
        c = pl.program_id(0)
        scale = jnp.where(c < 4, 1.0, ML_SCALE).astype(F32)
        w = w_ref[...]
        b = b_ref[...]
        for n in range(S // RC):
            cur = x_ref[n * RC:(n + 1) * RC, :]
            prev8 = x_ref[n * RC - 8:n * RC, :] if n else jnp.zeros((8, 128), F32)
            pre = b + _conv_taps(cur, prev8, w)
            o_ref[n * RC:(n + 1) * RC, :] = (pre * jax.nn.sigmoid(pre) * scale).astype(BF16)

    return pl.pallas_call(
        body, name="ml_prep", grid=(8,),
        in_specs=[pl.BlockSpec((S, 128), lambda c: (0, 16 + c)),
                  pl.BlockSpec((4, 128), lambda c: (0, c)),
                  pl.BlockSpec((1, 128), lambda c: (0, c))],
        out_specs=pl.BlockSpec((S, 128), lambda c: (0, c)),
        out_shape=jax.ShapeDtypeStruct((S, 1024), BF16),
        compiler_params=_params(("arbitrary",)),
    )(proj, conv_w, conv_b)


def _ml_prep_bwd(proj, conv_w, conv_b, dq, dk):
    S = proj.shape[0]

    def body(x_ref, w_ref, b_ref, dq_ref, dk_ref, dx_ref, gw_ref, gb_ref, dp_ref):
        c = pl.program_id(0)
        w = w_ref[...]
        b = b_ref[...]
        gw = [jnp.zeros((1, 128), F32) for _ in range(4)]
        gb = jnp.zeros((1, 128), F32)
        for n in range(S // RC):
            rows = slice(n * RC, (n + 1) * RC)
            cur = x_ref[rows, :]
            prev8 = x_ref[n * RC - 8:n * RC, :] if n else jnp.zeros((8, 128), F32)
            pre = b + _conv_taps(cur, prev8, w)
            s = jax.nn.sigmoid(pre)
            dpost = jnp.where(c < 4, dq_ref[rows, :], dk_ref[rows, :] * ML_SCALE)
            dpre = dpost * (s * (1.0 + pre * (1.0 - s)))
            dp_ref[rows, :] = dpre
            win = jnp.concatenate([prev8, cur], axis=0)
            gb = gb + jnp.sum(dpre, axis=0, keepdims=True)
            gw[3] = gw[3] + jnp.sum(dpre * cur, axis=0, keepdims=True)
            for j in range(3):
                gw[j] = gw[j] + jnp.sum(dpre * pltpu.roll(win, 3 - j, 0)[8:8 + RC], axis=0, keepdims=True)
        dp_ref[S:S + 8, :] = jnp.zeros((8, 128), F32)
        for n in range(S // RC):
            win = dp_ref[n * RC:(n + 1) * RC + 8, :]
            dx = w[3:4, :] * win[:RC]
            for j in range(3):
                dx = dx + w[j:j + 1, :] * pltpu.roll(win, RC + 8 - (3 - j), 0)[:RC]
            dx_ref[n * RC:(n + 1) * RC, :] = dx
        gw_ref[...] = jnp.concatenate(gw, axis=0)
        gb_ref[...] = gb

    return pl.pallas_call(
        body, name="ml_prep_bwd", grid=(8,),
        in_specs=[pl.BlockSpec((S, 128), lambda c: (0, 16 + c)),
                  pl.BlockSpec((4, 128), lambda c: (0, c)),
                  pl.BlockSpec((1, 128), lambda c: (0, c)),
                  pl.BlockSpec((S, 128), lambda c: (0, jnp.minimum(c, 3))),
                  pl.BlockSpec((S, 128), lambda c: (0, jnp.maximum(c - 4, 0)))],
        out_specs=[pl.BlockSpec((S, 128), lambda c: (0, c)),
                   pl.BlockSpec((4, 128), lambda c: (0, c)),
                   pl.BlockSpec((1, 128), lambda c: (0, c))],
        out_shape=[jax.ShapeDtypeStruct((S, 1024), F32), jax.ShapeDtypeStruct((4, 1024), F32),
                   jax.ShapeDtypeStruct((1, 1024), F32)],
        scratch_shapes=[pltpu.VMEM((S + 8, 128), F32)],
        compiler_params=_params(("arbitrary",)),
    )(proj, conv_w, conv_b, dq, dk)


def _gate_prep(proj_g, gbias):
    S = proj_g.shape[0]
    tm = 512

    def body(g_ref, b_ref, ig_ref, lf_ref):
        g = g_ref[...] + b_ref[...]
        for h in range(4):
            ig_ref[h] = jnp.broadcast_to(g[:, h:h + 1], (tm, 128))
            lf_ref[h] = jnp.broadcast_to(_log_sigmoid_parts(g[:, 4 + h:5 + h]), (tm, 128))

    return pl.pallas_call(
        body, name="gate_prep", grid=(S // tm,),
        in_specs=[pl.BlockSpec((tm, 128), lambda i: (i, 0)), pl.BlockSpec((1, 128), lambda i: (0, 0))],
        out_specs=[pl.BlockSpec((4, tm, 128), lambda i: (0, i, 0))] * 2,
        out_shape=[jax.ShapeDtypeStruct((4, S, 128), F32)] * 2,
        compiler_params=_params(("arbitrary",)),
    )(proj_g, gbias)


def _ml_chunk_fwd(q, k, v, ig, lf, ct, n_st, m_st, tri, causal):
    vf = v.astype(F32)
    vb = v.astype(BF16)
    b = sum(_dot(tri, part) for part in _split3(lf))
    b_last = b[LCH - 1:LCH, :]
    g = b_last - b + ig
    m_loc = jnp.max(g, axis=0, keepdims=True)
    w = jnp.exp(g - m_loc)
    vwf = vf * w
    vw = vwf.astype(BF16)
    ct_loc = _dot_tn(k, vw)
    kf = k.astype(F32)
    n_loc = jnp.sum(w * kf, axis=0, keepdims=True)
    r = jnp.transpose(ig - b)
    d_log = jnp.where(causal, b + r, -jnp.inf)
    m_t = jnp.maximum(b + m_st, jnp.max(d_log, axis=1, keepdims=True))
    w_in = jnp.exp(d_log - m_t)
    qk = _dot_nt(q, k)
    scores = qk * w_in
    cs = jnp.exp(b + m_st - m_t)
    ctb = ct.astype(BF16)
    qc = _dot(q, ctb)
    qf = q.astype(F32)
    qn = jnp.sum(qf * n_st, axis=1, keepdims=True)
    num = _dot(scores.astype(BF16), vb) + cs * qc
    den = jnp.sum(scores, axis=1, keepdims=True) + cs * qn
    em = jnp.exp(-m_t)
    dd = jnp.maximum(jnp.abs(den), em)
    h = num / dd
    m_new = jnp.maximum(b_last + m_st, m_loc)
    a = jnp.exp(b_last + m_st - m_new)
    gg = jnp.exp(m_loc - m_new)
    return dict(vf=vf, vb=vb, b=b, w=w, vwf=vwf, vw=vw, kf=kf, qf=qf, ct_loc=ct_loc, n_loc=n_loc, w_in=w_in, qk=qk,
                scores=scores, cs=cs, ctb=ctb, qc=qc, qn=qn, den=den, em=em, dd=dd, h=h, m_new=m_new, a=a, gg=gg)


def _ml_consts():
    row = lax.broadcasted_iota(jnp.int32, (LCH, LCH), 0)
    col = lax.broadcasted_iota(jnp.int32, (LCH, LCH), 1)
    return row, (col <= row), (col <= row).astype(BF16)


def _ml_fwd(qk, proj, ig, lf):
    S = proj.shape[0]
    nc = S // LCH

    def body(q_ref, k_ref, v_ref, ig_ref, lf_ref, h_ref, cst_ref, nm_ref, ct_ref, n_ref, m_ref):
        _, causal, tri = _ml_consts()
        ct_ref[...] = jnp.zeros_like(ct_ref)
        n_ref[...] = jnp.zeros_like(n_ref)
        m_ref[...] = jnp.zeros_like(m_ref)

        def chunk(c, carry):
            rows = pl.ds(pl.multiple_of(c * LCH, LCH), LCH)
            ct, n_st, m_st = ct_ref[...], n_ref[0:1, :], m_ref[0:1, :]
            cst_ref[c] = ct
            nm_ref[c, 0:8, :] = n_ref[...]
            nm_ref[c, 8:16, :] = m_ref[...]
            f = _ml_chunk_fwd(q_ref[rows, :], k_ref[rows, :], v_ref[rows, :], ig_ref[rows, :], lf_ref[rows, :],
                              ct, n_st, m_st, tri, causal)
            h_ref[rows, :] = f["h"]
            ct_ref[...] = f["a"] * ct + f["gg"] * f["ct_loc"]
            n_ref[...] = jnp.broadcast_to(f["a"] * n_st + f["gg"] * f["n_loc"], (8, 128))
            m_ref[...] = jnp.broadcast_to(f["m_new"], (8, 128))
            return carry

        lax.fori_loop(0, nc, chunk, 0)

    return pl.pallas_call(
        body, name="ml_fwd", grid=(4,),
        in_specs=[pl.BlockSpec((S, 128), lambda h: (0, h)),
                  pl.BlockSpec((S, 128), lambda h: (0, 4 + h)),
                  pl.BlockSpec((S, 128), lambda h: (0, 24 + h)),
                  pl.BlockSpec((None, S, 128), lambda h: (h, 0, 0)),
                  pl.BlockSpec((None, S, 128), lambda h: (h, 0, 0))],
        out_specs=[pl.BlockSpec((S, 128), lambda h: (0, h)),
                   pl.BlockSpec((None, nc, 128, 128), lambda h: (h, 0, 0, 0)),
                   pl.BlockSpec((None, nc, 16, 128), lambda h: (h, 0, 0, 0))],
        out_shape=[jax.ShapeDtypeStruct((S, ML_W), F32), jax.ShapeDtypeStruct((4, nc, 128, 128), F32),
                   jax.ShapeDtypeStruct((4, nc, 16, 128), F32)],
        scratch_shapes=[pltpu.VMEM((128, 128), F32), pltpu.VMEM((8, 128), F32), pltpu.VMEM((8, 128), F32)],
        compiler_params=_params(("arbitrary",)),
    )(qk, qk, proj, ig, lf)


def _ml_bwd(qk, proj, ig, lf, cst, nm, dh):
    S = proj.shape[0]
    nc = S // LCH

    def body(q_ref, k_ref, v_ref, ig_ref, lf_ref, cst_ref, nm_ref, dh_ref,
             dq_ref, dk_ref, dv_ref, dif_ref, gsum_ref, dct_ref, dn_ref, gi_ref):
        hd = pl.program_id(0)
        row, causal, tri = _ml_consts()
        lane = lax.broadcasted_iota(jnp.int32, (LCH, 128), 1)
        last_row = row == LCH - 1
        dct_ref[...] = jnp.zeros_like(dct_ref)
        dn_ref[...] = jnp.zeros_like(dn_ref)
        gi_ref[...] = jnp.zeros_like(gi_ref)

        @pl.when(hd == 0)
        def _():
            dif_ref[...] = jnp.zeros_like(dif_ref)

        def chunk(t, carry):
            c = nc - 1 - t
            rows = pl.ds(pl.multiple_of(c * LCH, LCH), LCH)
            q, k = q_ref[rows, :], k_ref[rows, :]
            ig_, lf_ = ig_ref[rows, :], lf_ref[rows, :]
            ct, n_st, m_st = cst_ref[c], nm_ref[c, 0:1, :], nm_ref[c, 8:9, :]
            f = _ml_chunk_fwd(q, k, v_ref[rows, :], ig_, lf_, ct, n_st, m_st, tri, causal)
            dh_ = dh_ref[rows, :]
            dct_new, dn_new = dct_ref[...], dn_ref[0:1, :]
            e_num = dh_ / f["dd"]
            hdh = jnp.sum(f["h"] * dh_, axis=1, keepdims=True)
            e_den = jnp.where(jnp.abs(f["den"]) > f["em"], -hdh / f["dd"] * jnp.sign(f["den"]), 0.0)
            e_num_b = e_num.astype(BF16)
            ds_ = _dot_nt(e_num_b, f["vb"]) + e_den
            dqk = ds_ * f["w_in"]
            gam = dqk * f["qk"]
            dqk_b = dqk.astype(BF16)
            cse = f["cs"] * e_den
            dq = _dot(dqk_b, k) + f["cs"] * _dot_nt(e_num_b, f["ctb"]) + cse * n_st
            dk = _dot_tn(dqk_b, q)
            dv = _dot_tn(f["scores"].astype(BF16), e_num_b)
            dcl = (f["gg"] * dct_new).astype(BF16)
            dnl = f["gg"] * dn_new
            kd = _dot(k, dcl)
            dv = dv + f["w"] * kd
            dk = dk + _dot_nt(f["vw"], dcl) + f["w"] * dnl
            gam_s = jnp.sum(kd * f["vwf"], axis=1, keepdims=True) + f["w"] * jnp.sum(f["kf"] * dnl, axis=1, keepdims=True)
            col_g = jnp.sum(jnp.transpose(gam), axis=1, keepdims=True) + gam_s
            db = (jnp.sum(gam, axis=1, keepdims=True) + jnp.sum(e_num * (f["cs"] * f["qc"]), axis=1, keepdims=True)
                  + cse * f["qn"] - col_g)
            db_last = jnp.sum(gam_s[:, 0:1]) + jnp.sum(f["a"][0:1, 0:1]) * (jnp.sum(dct_new * ct) + jnp.sum(dn_new * n_st))
            db = jnp.where(last_row, db + db_last, db)
            dlf = sum(_dot_tn(tri, part) for part in _split3(db))
            df = dlf * (1.0 - jnp.exp(lf_))
            dq_ref[rows, :] = dq
            dk_ref[rows, :] = dk
            dv_ref[rows, :] = dv
            dif_ref[rows, :] += jnp.where(lane == hd, col_g, 0.0) + jnp.where(lane == hd + 4, df, 0.0)
            clamped = jnp.where(jnp.abs(f["den"]) > f["em"], 0.0, hdh)
            gi_ref[...] += jnp.broadcast_to(jnp.sum(clamped, axis=0, keepdims=True), (8, 128))
            dct_ref[...] = f["a"] * dct_new + _dot_tn(q, (f["cs"] * e_num).astype(BF16))
            dn_ref[...] = jnp.broadcast_to(f["a"] * dn_new + jnp.sum(cse * f["qf"], axis=0, keepdims=True), (8, 128))
            return carry

        lax.fori_loop(0, nc, chunk, 0)
        lane8 = lax.broadcasted_iota(jnp.int32, (8, 128), 1)
        part = jnp.where(lane8 == hd, gi_ref[...], 0.0)

        @pl.when(hd == 0)
        def _():
            gsum_ref[...] = part

        @pl.when(hd > 0)
        def _():
            gsum_ref[...] += part

        @pl.when(hd == 3)
        def _():
            gsum_ref[...] += jnp.where(lane8 >= 4, jnp.sum(dif_ref[...], axis=0, keepdims=True), 0.0)

    return pl.pallas_call(
        body, name="ml_bwd", grid=(4,),
        in_specs=[pl.BlockSpec((S, 128), lambda h: (0, h)),
                  pl.BlockSpec((S, 128), lambda h: (0, 4 + h)),
                  pl.BlockSpec((S, 128), lambda h: (0, 24 + h)),
                  pl.BlockSpec((None, S, 128), lambda h: (h, 0, 0)),
                  pl.BlockSpec((None, S, 128), lambda h: (h, 0, 0)),
                  pl.BlockSpec((None, nc, 128, 128), lambda h: (h, 0, 0, 0)),
                  pl.BlockSpec((None, nc, 16, 128), lambda h: (h, 0, 0, 0)),
                  pl.BlockSpec((S, 128), lambda h: (0, h))],
        out_specs=[pl.BlockSpec((S, 128), lambda h: (0, h))] * 3 + [pl.BlockSpec((S, 128), lambda h: (0, 0)),
                                                                       pl.BlockSpec((8, 128), lambda h: (0, 0))],
        out_shape=[jax.ShapeDtypeStruct((S, ML_W), F32)] * 3 + [jax.ShapeDtypeStruct((S, 128), F32),
                                                                 jax.ShapeDtypeStruct((8, 128), F32)],
        scratch_shapes=[pltpu.VMEM((128, 128), F32), pltpu.VMEM((8, 128), F32), pltpu.VMEM((8, 128), F32)],
        compiler_params=_params(("arbitrary",)),
    )(qk, qk, proj, ig, lf, cst, nm, dh)


def _inproj_fwd(x, pre_w, w_t, wg_t):
    S, D = x.shape
    tm, tn = min(S, 1024), 1152

    def body(x_ref, pw_ref, w_ref, wg_ref, proj_ref, g_ref, u_ref):
        @pl.when(pl.program_id(1) == 0)
        def _():
            xf = x_ref[...]
            r = lax.rsqrt(jnp.mean(xf * xf, axis=-1, keepdims=True) + EPS)
            u = (xf * r * pw_ref[...]).astype(BF16)
            u_ref[...] = u
            g_ref[...] = _dot_nt(u, wg_ref[...])

        proj_ref[...] = _dot_nt(u_ref[...], w_ref[...])

    return pl.pallas_call(
        body, name="inproj_fwd", grid=(S // tm, N_MAIN // tn),
        in_specs=[pl.BlockSpec((tm, D), lambda i, j: (i, 0)),
                  pl.BlockSpec((1, D), lambda i, j: (0, 0)),
                  pl.BlockSpec((tn, D), lambda i, j: (j, 0)),
                  pl.BlockSpec((128, D), lambda i, j: (0, 0))],
        out_specs=[pl.BlockSpec((tm, tn), lambda i, j: (i, j)),
                   pl.BlockSpec((tm, 128), lambda i, j: (i, 0)),
                   pl.BlockSpec((tm, D), lambda i, j: (i, 0))],
        out_shape=[jax.ShapeDtypeStruct((S, N_MAIN), F32), jax.ShapeDtypeStruct((S, 128), F32),
                   jax.ShapeDtypeStruct((S, D), BF16)],
        compiler_params=_params(("arbitrary", "arbitrary")),
    )(x, pre_w, w_t, wg_t)


def _inproj_bwd(d_main, d_if, w_t, wg_t, x, pre_w, dx_tail):
    S, D = x.shape
    tm, tk = min(S, 1024), 1152
    nk = N_MAIN // tk

    def body(d_ref, dg_ref, w_ref, wg_ref, x_ref, pw_ref, dt_ref, dx_ref, gpw_ref, acc_ref):
        i, k = pl.program_id(0), pl.program_id(1)

        @pl.when(k == 0)
        def _():
            acc_ref[...] = _dot(dg_ref[...], wg_ref[...])

        acc_ref[...] += _dot(d_ref[...], w_ref[...])

        @pl.when(k == nk - 1)
        def _():
            xf = x_ref[...]
            r = lax.rsqrt(jnp.mean(xf * xf, axis=-1, keepdims=True) + EPS)
            xn = xf * r
            du = acc_ref[...]
            gw = du * pw_ref[...]
            dx_ref[...] = dt_ref[...] + r * (gw - xn * jnp.mean(gw * xn, axis=-1, keepdims=True))
            part = jnp.sum(du * xn, axis=0, keepdims=True)

            @pl.when(i == 0)
            def _():
                gpw_ref[...] = part

            @pl.when(i > 0)
            def _():
                gpw_ref[...] += part

    return pl.pallas_call(
        body, name="inproj_bwd", grid=(S // tm, nk),
        in_specs=[pl.BlockSpec((tm, tk), lambda i, k: (i, k)),
                  pl.BlockSpec((tm, 128), lambda i, k: (i, 0)),
                  pl.BlockSpec((tk, D), lambda i, k: (k, 0)),
                  pl.BlockSpec((128, D), lambda i, k: (0, 0)),
                  pl.BlockSpec((tm, D), lambda i, k: (i, 0)),
                  pl.BlockSpec((1, D), lambda i, k: (0, 0)),
                  pl.BlockSpec((tm, D), lambda i, k: (i, 0))],
        out_specs=[pl.BlockSpec((tm, D), lambda i, k: (i, 0)),
                   pl.BlockSpec((1, D), lambda i, k: (0, 0))],
        out_shape=[jax.ShapeDtypeStruct((S, D), F32), jax.ShapeDtypeStruct((1, D), F32)],
        scratch_shapes=[pltpu.VMEM((tm, D), F32)],
        compiler_params=_params(("arbitrary", "arbitrary")),
    )(d_main, d_if, w_t, wg_t, x, pre_w, dx_tail)


def _matmul_tn(a, b, name):
    S, M = a.shape
    N = b.shape[1]
    tmm = 1152 if M % 1152 == 0 else min(M, 1024)
    tk = min(S, 1024)
    nk = S // tk

    def body(a_ref, b_ref, o_ref):
        part = _dot_tn(a_ref[...].astype(BF16), b_ref[...].astype(BF16))

        @pl.when(pl.program_id(1) == 0)
        def _():
            o_ref[...] = part

        @pl.when(pl.program_id(1) > 0)
        def _():
            o_ref[...] += part

    return pl.pallas_call(
        body, name=name, grid=(M // tmm, nk),
        in_specs=[pl.BlockSpec((tk, tmm), lambda i, k: (k, i)),
                  pl.BlockSpec((tk, N), lambda i, k: (k, 0))],
        out_specs=pl.BlockSpec((tmm, N), lambda i, k: (i, 0)),
        out_shape=jax.ShapeDtypeStruct((M, N), F32),
        compiler_params=_params(("arbitrary", "arbitrary")),
    )(a, b)


def _half_mean(v, low):
    s_lo = jnp.sum(jnp.where(low, v, 0.0), axis=1, keepdims=True)
    s_hi = jnp.sum(jnp.where(low, 0.0, v), axis=1, keepdims=True)
    return jnp.where(low, s_lo, s_hi) * (1.0 / 64.0)


def _silu_grad(z, s):
    return s * (1.0 + z * (1.0 - s))


def _tail(y_sb, h_ml, proj, x, p, target, sb_nw, ml_nw, w_out, post_w, w_gate, b_gate, w_up):
    S, D = x.shape
    tm = 256

    def body(ysb_ref, hml_ref, sbz_ref, mlo_ref, mlz_ref, x_ref, p_ref, tg_ref, sbw_ref, mlw_ref, wo_ref, pw_ref,
             wg_ref, bg_ref, wu_ref,
             dx_ref, dysb_ref, dhml_ref, dsbz_ref, dmlo_ref, dmlz_ref, mix_ref, dy_ref, h1_ref, dgp_ref, dpu_ref,
             small_ref):
        lane = lax.broadcasted_iota(jnp.int32, (tm, 128), 1)
        low = lane < 64
        sb_saved, ml_saved, mixed = [], [], []
        for s in range(4):
            sl = slice(128 * s, 128 * s + 128)
            y = ysb_ref[:, sl]
            rs = lax.rsqrt(_half_mean(y * y, low) + EPS)
            n = y * rs
            z = sbz_ref[:, sl]
            sg = jax.nn.sigmoid(z)
            w = sbw_ref[:, sl]
            mixed.append((n * w) * (z * sg))
            sb_saved.append((rs, n, z, sg, w))
        for s in range(4):
            sl = slice(128 * s, 128 * s + 128)
            og = jax.nn.sigmoid(mlo_ref[:, sl])
            hh = hml_ref[:, sl]
            t = og * hh
            rs = lax.rsqrt(jnp.mean(t * t, axis=1, keepdims=True) + EPS)
            n = t * rs
            z = mlz_ref[:, sl]
            sg = jax.nn.sigmoid(z)
            w = mlw_ref[:, sl]
            mixed.append((n * w) * (z * sg))
            ml_saved.append((rs, n, z, sg, w, og, hh))
        mix = jnp.concatenate(mixed, axis=1).astype(BF16)
        mix_ref[...] = mix
        y = _dot(mix, wo_ref[...])
        rs_y = lax.rsqrt(jnp.mean(y * y, axis=1, keepdims=True) + EPS)
        yn = y * rs_y
        pw = pw_ref[...]
        h1 = x_ref[...] + yn * pw
        h1b = h1.astype(BF16)
        h1_ref[...] = h1b
        gate = jax.nn.sigmoid(_dot(h1b, wg_ref[...]) + bg_ref[...])
        pu = _dot(p_ref[...].astype(BF16), wu_ref[...])
        err = (h1 + gate * pu) - tg_ref[...]
        loss = 0.5 * jnp.sum(jnp.sum(err * err, axis=1, keepdims=True) * (1.0 / D))
        d_out = err * (1.0 / D)
        dpu_ref[...] = (d_out * gate).astype(BF16)
        dgp = (d_out * pu) * (gate * (1.0 - gate))
        dgpb = dgp.astype(BF16)
        dgp_ref[...] = dgpb
        d_h1 = d_out + _dot_nt(dgpb, wg_ref[...])
        dx_ref[...] = d_h1
        gwy = d_h1 * pw
        d_y = rs_y * (gwy - yn * jnp.mean(gwy * yn, axis=1, keepdims=True))
        d_yb = d_y.astype(BF16)
        dy_ref[...] = d_yb
        d_mix = _dot_nt(d_yb, wo_ref[...])
        g_nw = []
        for s in range(4):
            sl = slice(128 * s, 128 * s + 128)
            rs, n, z, sg, w = sb_saved[s]
            da = d_mix[:, sl]
            act = z * sg
            dsbz_ref[:, sl] = (da * (n * w) * _silu_grad(z, sg)).astype(BF16)
            dn = da * w * act
            g_nw.append(jnp.sum(da * act * n, axis=0, keepdims=True))
            dysb_ref[:, sl] = rs * (dn - n * _half_mean(dn * n, low))
        for s in range(4):
            sl = slice(128 * s, 128 * s + 128)
            rs, n, z, sg, w, og, hh = ml_saved[s]
            da = d_mix[:, 512 + 128 * s:512 + 128 * s + 128]
            act = z * sg
            dmlz_ref[:, sl] = (da * (n * w) * _silu_grad(z, sg)).astype(BF16)
            dn = da * w * act
            g_nw.append(jnp.sum(da * act * n, axis=0, keepdims=True))
            dt = rs * (dn - n * jnp.mean(dn * n, axis=1, keepdims=True))
            dmlo_ref[:, sl] = (dt * hh * (og * (1.0 - og))).astype(BF16)
            dhml_ref[:, sl] = dt * og
        upd = jnp.concatenate([
            jnp.sum(d_h1 * yn, axis=0, keepdims=True),
            jnp.sum(dgp, axis=0, keepdims=True),
            jnp.concatenate(g_nw, axis=1),
            jnp.full((1, D), loss, F32),
            jnp.zeros((4, D), F32)], axis=0)

        @pl.when(pl.program_id(0) == 0)
        def _():
            small_ref[...] = upd

        @pl.when(pl.program_id(0) > 0)
        def _():
            small_ref[...] += upd

    def rows(width, col=0):
        return pl.BlockSpec((tm, width), lambda i: (i, col))

    def whole(a):
        return pl.BlockSpec(a.shape, lambda i: (0, 0))

    return pl.pallas_call(
        body, name="tail", grid=(S // tm,),
        in_specs=[rows(512), rows(512), rows(512, 3), rows(512, 7), rows(512, 8), rows(D), rows(256), rows(D),
                  whole(sb_nw), whole(ml_nw), whole(w_out), whole(post_w), whole(w_gate), whole(b_gate), whole(w_up)],
        out_specs=[rows(D), rows(512), rows(512), rows(512), rows(512), rows(512), rows(D), rows(D), rows(D), rows(D),
                   rows(D), pl.BlockSpec((8, D), lambda i: (0, 0))],
        out_shape=[jax.ShapeDtypeStruct((S, D), F32), jax.ShapeDtypeStruct((S, 512), F32),
                   jax.ShapeDtypeStruct((S, 512), F32)] + [jax.ShapeDtypeStruct((S, 512), BF16)] * 3
        + [jax.ShapeDtypeStruct((S, D), BF16)] * 5 + [jax.ShapeDtypeStruct((8, D), F32)],
        compiler_params=_params(("arbitrary",)),
    )(y_sb, h_ml, proj, proj, proj, x, p, target, sb_nw, ml_nw, w_out, post_w, w_gate, b_gate, w_up)


def _local_step(x, p, target, pre_w, w_t, wg_t, conv_w, conv_b, gbias, sb_nw, ml_nw, w_out, post_w, w_gate, b_gate,
                w_up):
    proj, proj_g, u = _inproj_fwd(x, pre_w, w_t, wg_t)
    y_sb, tot = _sb_fwd(proj)
    qk = _ml_prep(proj, conv_w, conv_b)
    ig, lf = _gate_prep(proj_g, gbias)
    h_ml, cst, nm = _ml_fwd(qk, proj, ig, lf)
    dx_tail, d_ysb, d_hml, d_sbz, d_mlo, d_mlz, mix, d_y, h1, dgp, dpu, small = _tail(
        y_sb, h_ml, proj, x, p, target, sb_nw, ml_nw, w_out, post_w, w_gate, b_gate, w_up)
    dq, dk, dv = _sb_bwd(proj, tot, d_ysb)
    dqc, dks, dmlv, dif, gif = _ml_bwd(qk, proj, ig, lf, cst, nm, d_hml)
    dmlqk, g_cw, g_cb = _ml_prep_bwd(proj, conv_w, conv_b, dqc, dks)
    d_main = jnp.concatenate([dq.astype(BF16), dk.astype(BF16), dv.astype(BF16), d_sbz, dmlqk.astype(BF16),
                              dmlv.astype(BF16), d_mlo, d_mlz], axis=1)
    d_if = dif.astype(BF16)
    dx, g_pre = _inproj_bwd(d_main, d_if, w_t, wg_t, x, pre_w, dx_tail)
    grads = dict(
        w_t=_matmul_tn(d_main, u, "gw_in"), wg_t=_matmul_tn(d_if, u, "gw_in_gates"),
        w_out=_matmul_tn(mix, d_y, "gw_out"), w_gate=_matmul_tn(h1, dgp, "gw_gate"), w_up=_matmul_tn(p, dpu, "gw_up"),
        conv_w=g_cw, conv_b=g_cb, pre_w=g_pre, gif=gif)
    return dx, grads, small


MESH = pl.DeviceIdType.MESH
ANY = pl.BlockSpec(memory_space=pl.ANY)
N_DEV = 8


def _place():
    return lax.axis_index("x"), lax.axis_index("y"), lax.axis_index("c")


def _block_of(px, py, pc):
    return 4 * px + 2 * py + pc


def _all_gather(a, b):
    def body(a_ref, b_ref, oa_ref, ob_ref, send_sems, recv_sems, local_sems):
        x, y, c = _place()
        me, sibling = (x, y, c), (x, y, 1 - c)
        chips = [(1 - x, y), (x, 1 - y), (1 - x, 1 - y)]
        pairs = ((a_ref, oa_ref), (b_ref, ob_ref))

        def copies(k, block, to, from_input=False):
            slot = _block_of(*block)
            return [pltpu.make_async_remote_copy(
                src_ref=src if from_input else out.at[slot], dst_ref=out.at[slot],
                send_sem=send_sems.at[t, k], recv_sem=recv_sems.at[t, k], device_id=to, device_id_type=MESH)
                for t, (src, out) in enumerate(pairs)]

        mine = [pltpu.make_async_copy(src, out.at[_block_of(*me)], local_sems.at[t])
                for t, (src, out) in enumerate(pairs)]
        for cp in mine:
            cp.start()
        first = copies(0, me, sibling, True)
        for j, chip in enumerate(chips):
            first += copies(1 + j, me, (*chip, c), True)
        for cp in first:
            cp.start()
        passed = []
        for j, chip in enumerate(chips):
            for cp in copies(1 + j, (*chip, c), me):
                cp.wait_recv()
            fwd = copies(4 + j, (*chip, c), sibling)
            for cp in fwd:
                cp.start()
            passed += fwd
        for cp in copies(0, sibling, me):
            cp.wait_recv()
        for j, chip in enumerate(chips):
            for cp in copies(4 + j, (*chip, 1 - c), me):
                cp.wait_recv()
        for cp in first + passed:
            cp.wait_send()
        for cp in mine:
            cp.wait()

    return pl.pallas_call(
        body, name="all_gather",
        in_specs=[ANY, ANY], out_specs=[ANY, ANY],
        out_shape=[jax.ShapeDtypeStruct((N_DEV,) + a.shape, a.dtype), jax.ShapeDtypeStruct((N_DEV,) + b.shape, b.dtype)],
        scratch_shapes=[pltpu.SemaphoreType.DMA((2, 7)), pltpu.SemaphoreType.DMA((2, 7)), pltpu.SemaphoreType.DMA((2,))],
    )(a, b)


def _exchange_pair(g, s):
    def body(g_ref, s_ref, og_ref, os_ref, send_sems, recv_sems, local_sem):
        x, y, c = _place()
        mine = _block_of(x, y, c)
        local = pltpu.make_async_copy(s_ref, os_ref.at[mine], local_sem)
        local.start()
        sent = []
        for k in range(4):
            sent.append(pltpu.make_async_remote_copy(
                src_ref=g_ref.at[k, 1 - c], dst_ref=og_ref.at[k], send_sem=send_sems.at[0, k],
                recv_sem=recv_sems.at[0, k], device_id=(x, y, 1 - c), device_id_type=MESH))
        for k in range(1, N_DEV):
            peer = (x ^ (k >> 2), y ^ ((k >> 1) & 1), c ^ (k & 1))
            sent.append(pltpu.make_async_remote_copy(
                src_ref=s_ref, dst_ref=os_ref.at[mine],
                send_sem=send_sems.at[1, k - 1], recv_sem=recv_sems.at[1, k - 1], device_id=peer, device_id_type=MESH))
        for cp in sent:
            cp.start()
        for cp in sent:
            cp.wait()
        local.wait()

    return pl.pallas_call(
        body, name="exchange_pair",
        in_specs=[ANY, ANY], out_specs=[ANY, ANY],
        out_shape=[jax.ShapeDtypeStruct((4,) + g.shape[2:], g.dtype), jax.ShapeDtypeStruct((N_DEV,) + s.shape, s.dtype)],
        scratch_shapes=[pltpu.SemaphoreType.DMA((2, 7)), pltpu.SemaphoreType.DMA((2, 7)), pltpu.SemaphoreType.DMA],
    )(g, s)


def _pair_sum(g, r, core, tr):
    _, _, R, D = g.shape

    def body(c_ref, g_ref, r_ref, o_ref):
        o_ref[...] = (g_ref[...] + r_ref[...]).astype(BF16)

    return pl.pallas_call(
        body, name="pair_sum",
        grid_spec=pltpu.PrefetchScalarGridSpec(
            num_scalar_prefetch=1, grid=(4, R // tr),
            in_specs=[pl.BlockSpec((None, None, tr, D), lambda k, i, c: (k, c[0], i, 0)),
                      pl.BlockSpec((None, tr, D), lambda k, i, c: (k, i, 0))],
            out_specs=pl.BlockSpec((None, tr, D), lambda k, i, c: (k, i, 0))),
        out_shape=jax.ShapeDtypeStruct((4, R, D), BF16),
        compiler_params=_params(("arbitrary", "arbitrary")),
    )(core, g, r)


def _exchange_chips(p):
    def body(p_ref, o_ref, send_sems, recv_sems, local_sem):
        x, y, c = _place()
        mine = 2 * x + y
        local = pltpu.make_async_copy(p_ref.at[mine], o_ref.at[mine], local_sem)
        local.start()
        sent = []
        for k in range(1, 4):
            px, py = x ^ (k >> 1), y ^ (k & 1)
            sent.append(pltpu.make_async_remote_copy(
                src_ref=p_ref.at[2 * px + py], dst_ref=o_ref.at[mine], send_sem=send_sems.at[k - 1],
                recv_sem=recv_sems.at[k - 1], device_id=(px, py, c), device_id_type=MESH))
        for cp in sent:
            cp.start()
        for cp in sent:
            cp.wait()
        local.wait()

    return pl.pallas_call(
        body, name="exchange_chips",
        in_specs=[ANY], out_specs=ANY, out_shape=jax.ShapeDtypeStruct(p.shape, p.dtype),
        scratch_shapes=[pltpu.SemaphoreType.DMA((3,)), pltpu.SemaphoreType.DMA((3,)), pltpu.SemaphoreType.DMA],
    )(p)


ADAM_LR, ADAM_B1, ADAM_B2, ADAM_EPS, ADAM_WD, ADAM_STEP = 0.001, 0.9, 0.999, 1e-08, 0.01, 10


def _adamw(w, g, m, v):
    m = ADAM_B1 * m + (1.0 - ADAM_B1) * g
    v = ADAM_B2 * v + (1.0 - ADAM_B2) * (g * g)
    m_hat = m / (1.0 - ADAM_B1 ** ADAM_STEP)
    v_hat = v / (1.0 - ADAM_B2 ** ADAM_STEP)
    return -ADAM_LR * (m_hat / (jnp.sqrt(v_hat) + ADAM_EPS) + ADAM_WD * w), m, v


def _adam(parts, w, m, v, tr, name):
    R, D = w.shape
    n = parts.shape[0]

    def body(p_ref, w_ref, m_ref, v_ref, g_ref, d_ref, nm_ref, nv_ref):
        g = p_ref[0].astype(F32)
        for k in range(1, n):
            g = g + p_ref[k].astype(F32)
        g_ref[...] = g
        d_ref[...], nm_ref[...], nv_ref[...] = _adamw(w_ref[...], g, m_ref[...], v_ref[...])

    blk = pl.BlockSpec((tr, D), lambda i: (i, 0))
    return pl.pallas_call(
        body, name=name, grid=(R // tr,),
        in_specs=[pl.BlockSpec((n, tr, D), lambda i: (0, i, 0)), blk, blk, blk],
        out_specs=[blk] * 4, out_shape=[jax.ShapeDtypeStruct((R, D), F32)] * 4,
        compiler_params=_params(("arbitrary",)),
    )(parts, w, m, v)


ROWS_IN = 592
ROWS_BF16 = ROWS_IN + 128 + 128 + 32
ROWS_CONV = 16
ROWS_ALL = ROWS_BF16 + ROWS_CONV
ROW_TILE = 224


def _pad_rows(a, rows):
    return jnp.pad(a, ((0, rows - a.shape[0]), (0, 0)))


def _pack_shards(w_in, w_out, w_gate, w_up, conv_w):
    return jnp.concatenate([
        _pad_rows(w_in[0].T, ROWS_IN), w_out[0], w_gate[0], w_up[0].reshape(32, D_MODEL),
        _pad_rows(jnp.pad(conv_w[0].reshape(1, 512), ((0, 0), (0, 512))), ROWS_CONV)], axis=0)


def _unpack_shards(a):
    return (a[:SHARD_IN].T[None], a[ROWS_IN:ROWS_IN + 128][None], a[ROWS_IN + 128:ROWS_IN + 256][None],
            a[ROWS_IN + 256:ROWS_BF16].reshape(1, 256, 128), a[ROWS_BF16, :512].reshape(1, 4, 128))


def _pack_small(pre_w, conv_b, i_bias, f_bias, sb_nw, ml_nw, post_w, b_gate):
    gates = jnp.pad(jnp.concatenate([i_bias, f_bias], axis=1), ((0, 0), (0, D_MODEL - 8)))
    return jnp.concatenate([post_w, b_gate, jnp.concatenate([sb_nw, ml_nw], axis=1), jnp.zeros((1, D_MODEL), F32),
                            pre_w, conv_b, gates, jnp.zeros((1, D_MODEL), F32)], axis=0)


def _unpack_small(a):
    return a[4:5], a[5:6], a[6:7, 0:4], a[6:7, 4:8], a[2:3, :512], a[2:3, 512:], a[0:1], a[1:2]


def kernel(x, p, pre_norm_w, w_in, ml_conv_w, ml_conv_b, ml_i_bias, ml_f_bias, sb_norm_w, ml_norm_w, w_out, post_norm_w, ple_w_up, ple_w_gate, ple_b_gate, loss_target, m_pre_norm_w, m_w_in, m_ml_conv_w, m_ml_conv_b, m_ml_i_bias, m_ml_f_bias, m_sb_norm_w, m_ml_norm_w, m_w_out, m_post_norm_w, m_ple_w_up, m_ple_w_gate, m_ple_b_gate, v_pre_norm_w, v_w_in, v_ml_conv_w, v_ml_conv_b, v_ml_i_bias, v_ml_f_bias, v_sb_norm_w, v_ml_norm_w, v_w_out, v_post_norm_w, v_ple_w_up, v_ple_w_gate, v_ple_b_gate):
    D = D_MODEL
    w_pk = _pack_shards(w_in, w_out, ple_w_gate, ple_w_up, ml_conv_w)
    m_pk = _pack_shards(m_w_in, m_w_out, m_ple_w_gate, m_ple_w_up, m_ml_conv_w)
    v_pk = _pack_shards(v_w_in, v_w_out, v_ple_w_gate, v_ple_w_up, v_ml_conv_w)
    w_sm = _pack_small(pre_norm_w, ml_conv_b, ml_i_bias, ml_f_bias, sb_norm_w, ml_norm_w, post_norm_w, ple_b_gate)
    m_sm = _pack_small(m_pre_norm_w, m_ml_conv_b, m_ml_i_bias, m_ml_f_bias, m_sb_norm_w, m_ml_norm_w, m_post_norm_w, m_ple_b_gate)
    v_sm = _pack_small(v_pre_norm_w, v_ml_conv_b, v_ml_i_bias, v_ml_f_bias, v_sb_norm_w, v_ml_norm_w, v_post_norm_w, v_ple_b_gate)

    ga, gb = _all_gather(w_pk[:ROWS_BF16].astype(BF16), w_pk[ROWS_BF16:])
    w_in_t = ga[:, :SHARD_IN].reshape(N_IN, D)
    w_t = w_in_t[:N_MAIN]
    wg_t = _pad_rows(w_in_t[N_MAIN:], 128)
    w_out_f = ga[:, ROWS_IN:ROWS_IN + 128].reshape(D, D)
    w_gate_f = ga[:, ROWS_IN + 128:ROWS_IN + 256].reshape(D, D)
    w_up_f = ga[:, ROWS_IN + 256:].reshape(N_DEV, 256, 128).transpose(1, 0, 2).reshape(256, D)
    conv_w_f = gb[:, 0, :512].reshape(N_DEV, 4, 128).transpose(1, 0, 2).reshape(4, D)
    gbias = jnp.pad(jnp.concatenate([ml_i_bias, ml_f_bias], axis=1), ((0, 0), (0, 120)))

    dx, g, small = _local_step(x[0], p[0, 0], loss_target[0], pre_norm_w, w_t, wg_t, conv_w_f, ml_conv_b, gbias,
                               sb_norm_w, ml_norm_w, w_out_f, post_norm_w, w_gate_f, ple_b_gate, w_up_f)

    g_in = jnp.concatenate([g["w_t"], g["wg_t"][:8]], axis=0).reshape(N_DEV, SHARD_IN, D)
    g_blocks = jnp.concatenate([
        jnp.pad(g_in, ((0, 0), (0, ROWS_IN - SHARD_IN), (0, 0))),
        g["w_out"].reshape(N_DEV, 128, D), g["w_gate"].reshape(N_DEV, 128, D),
        g["w_up"].reshape(256, N_DEV, 128).transpose(1, 0, 2).reshape(N_DEV, 32, D),
        jnp.pad(g["conv_w"].reshape(4, N_DEV, 128).transpose(1, 0, 2).reshape(N_DEV, 1, 512),
                ((0, 0), (0, ROWS_CONV - 1), (0, 512))),
    ], axis=1).reshape(4, 2, ROWS_ALL, D)
    g_small = jnp.concatenate([small[0:4], g["pre_w"], g["conv_b"], jnp.pad(g["gif"][0:1], ((0, 0), (0, D - 128))),
                               jnp.zeros((1, D), F32)], axis=0)
    from_pair, parts_sm = _exchange_pair(g_blocks, g_small)
    core = lax.axis_index("c").astype(jnp.int32).reshape(1)
    parts = _exchange_chips(_pair_sum(g_blocks, from_pair, core, ROW_TILE))

    grad_pk, delta_pk, nm_pk, nv_pk = _adam(parts, w_pk, m_pk, v_pk, ROW_TILE, "adam")
    grad_sm, delta_sm, nm_sm, nv_sm = _adam(parts_sm, w_sm, m_sm, v_sm, 8, "adam_small")
    loss = grad_sm[3, 0]

    def ordered(pk, sm):
        win, wout, wgate, wup, convw = _unpack_shards(pk)
        pre_w, conv_b, i_b, f_b, sb_nw, ml_nw, post_w, b_gate = _unpack_small(sm)
        return [pre_w, win, convw, conv_b, i_b, f_b, sb_nw, ml_nw, wout, post_w, wup, wgate, b_gate]

    return (loss, dx[None], *ordered(grad_pk, grad_sm), *ordered(delta_pk, delta_sm), *ordered(nm_pk, nm_sm),
            *ordered(nv_pk, nv_sm))
```

```python
import functools

import jax
import jax.numpy as jnp
from jax import lax
from jax.experimental import pallas as pl
from jax.experimental.pallas import tpu as pltpu

F32 = jnp.float32
BF16 = jnp.bfloat16
EPS = 1e-6
D_MODEL = 1024
SB_W = 512
ML_W = 512
N_MAIN = 4608
N_IN = 4616
SHARD_IN = 577
SHARD_IN_PAD = 584
TQ = 1024
TK = 256
ND = TQ // TK
LCH = 128
VMEM_LIMIT = 56 * 1024 * 1024


def _dot(a, b):
    return jnp.dot(a, b, preferred_element_type=F32)


def _dot_nt(a, b):
    return lax.dot_general(a, b, (((1,), (1,)), ((), ())), preferred_element_type=F32)


def _dot_tn(a, b):
    return lax.dot_general(a, b, (((0,), (0,)), ((), ())), preferred_element_type=F32)


def _split2(x):
    hi = x.astype(BF16)
    lo = (x - hi.astype(F32)).astype(BF16)
    return hi, lo


def _split3(x):
    hi = x.astype(BF16)
    r = x - hi.astype(F32)
    mid = r.astype(BF16)
    lo = (r - mid.astype(F32)).astype(BF16)
    return hi, mid, lo


def _params(sem):
    return pltpu.CompilerParams(dimension_semantics=sem, vmem_limit_bytes=VMEM_LIMIT)


def _log_sigmoid_parts(z):
    e = jnp.exp(-jnp.abs(z))
    return jnp.minimum(z, 0.0) - jnp.log(1.0 + e)


def _sb_fwd(proj):
    S = proj.shape[0]
    nq = S // TQ

    def body(q_ref, k_ref, v_ref, y_ref, t_ref, acc_ref, car_ref):
        i = pl.program_id(1)
        low = lax.broadcasted_iota(jnp.int32, (TQ, 128), 1) < 64
        row = lax.broadcasted_iota(jnp.int32, (TK, TK), 0)
        col = lax.broadcasted_iota(jnp.int32, (TK, TK), 1)
        uo = (row > col).astype(BF16)
        uo = jnp.concatenate([uo, uo], axis=0)
        q = q_ref[...] * 0.125
        qh = (jnp.where(low, q, 0.0).astype(BF16), jnp.where(low, 0.0, q).astype(BF16))
        acc_ref[...] = jnp.zeros_like(acc_ref)
        car_ref[...] = jnp.zeros_like(car_ref)

        def block(j, r0):
            diag = r0 is not None
            r0 = r0 or 0
            if diag:
                strict = (lax.broadcasted_iota(jnp.int32, (TQ - r0, TK), 1)
                          < lax.broadcasted_iota(jnp.int32, (TQ - r0, TK), 0))
            rows = pl.ds(pl.multiple_of(j * TK, TK), TK)
            kb = k_ref[rows, :].astype(BF16)
            vb = v_ref[rows, :].astype(BF16)
            for h in range(2):
                z = _dot_nt(qh[h][r0:], kb)
                lb = _log_sigmoid_parts(z)
                lk = lb - z
                if diag:
                    lk = jnp.where(strict, lk, 0.0)
                hi, lo = _split2(lk)
                rr = _dot(jnp.concatenate([hi, lo], axis=1), uo)
                car = car_ref[h, r0:, :]
                a = jnp.exp(lb + jnp.concatenate([car, car], axis=1) + rr)
                if diag:
                    a = jnp.where(strict, a, 0.0)
                acc_ref[h, r0:, :] += _dot(a.astype(BF16), vb)
                car_ref[h, r0:, :] = car + jnp.broadcast_to(rr[:, 0:1] + lk[:, 0:1], car.shape)

        for d in reversed(range(ND)):
            block(ND * i + d, TK * d)

        def loop(n, c):
            block(ND * i - 1 - n, None)
            return c

        lax.fori_loop(0, ND * i, loop, 0)
        y_ref[...] = jnp.where(low, acc_ref[0], acc_ref[1])
        t_ref[...] = jnp.where(low, car_ref[0], car_ref[1])

    return pl.pallas_call(
        body, name="sb_fwd", grid=(4, nq),
        in_specs=[pl.BlockSpec((TQ, 128), lambda p, i: (i, p)),
                  pl.BlockSpec((S, 128), lambda p, i: (0, 4 + p)),
                  pl.BlockSpec((S, 128), lambda p, i: (0, 8 + p))],
        out_specs=[pl.BlockSpec((TQ, 128), lambda p, i: (i, p)),
                   pl.BlockSpec((TQ, 128), lambda p, i: (i, p))],
        out_shape=[jax.ShapeDtypeStruct((S, SB_W), F32), jax.ShapeDtypeStruct((S, SB_W), F32)],
        scratch_shapes=[pltpu.VMEM((2, TQ, 128), F32), pltpu.VMEM((2, TQ, 128), F32)],
        compiler_params=_params(("arbitrary", "arbitrary")),
    )(proj, proj, proj)


def _sb_bwd(proj, tot, dy):
    S = proj.shape[0]
    nq = S // TQ

    def body(q_ref, k_ref, v_ref, t_ref, dy_ref, dq_ref, dk_ref, dv_ref, dqa_ref, cp_ref, cg_ref):
        i = pl.program_id(1)
        low = lax.broadcasted_iota(jnp.int32, (TQ, 128), 1) < 64
        row = lax.broadcasted_iota(jnp.int32, (TK, TK), 0)
        col = lax.broadcasted_iota(jnp.int32, (TK, TK), 1)
        u_inc = (row <= col).astype(BF16)
        u_inc = jnp.concatenate([u_inc, u_inc], axis=0)
        u_exc = (row < col).astype(BF16)
        q = q_ref[...] * 0.125
        qh = (jnp.where(low, q, 0.0).astype(BF16), jnp.where(low, 0.0, q).astype(BF16))
        dy_ = dy_ref[...]
        dyh = (jnp.where(low, dy_, 0.0).astype(BF16), jnp.where(low, 0.0, dy_).astype(BF16))
        t_ = t_ref[...]
        t_sw = pltpu.roll(t_, 64, 1)
        th = (jnp.where(low, t_, t_sw), jnp.where(low, t_sw, t_))
        dqa_ref[...] = jnp.zeros_like(dqa_ref)
        cp_ref[...] = jnp.zeros_like(cp_ref)
        cg_ref[...] = jnp.zeros_like(cg_ref)

        @pl.when(i == 0)
        def _():
            dk_ref[...] = jnp.zeros_like(dk_ref)
            dv_ref[...] = jnp.zeros_like(dv_ref)

        def block(j, r0):
            diag = r0 is not None
            r0 = r0 or 0
            if diag:
                strict = (lax.broadcasted_iota(jnp.int32, (TQ - r0, TK), 1)
                          < lax.broadcasted_iota(jnp.int32, (TQ - r0, TK), 0))
            rows = pl.ds(pl.multiple_of(j * TK, TK), TK)
            kb = k_ref[rows, :].astype(BF16)
            vb = v_ref[rows, :].astype(BF16)
            dk_acc = jnp.zeros((TK, 128), F32)
            dv_acc = jnp.zeros((TK, 128), F32)
            for h in range(2):
                qr, dyr = qh[h][r0:], dyh[h][r0:]
                z = _dot_nt(qr, kb)
                lb = _log_sigmoid_parts(z)
                lk = lb - z
                if diag:
                    lk = jnp.where(strict, lk, 0.0)
                hi, lo = _split2(lk)
                pp = _dot(jnp.concatenate([hi, lo], axis=1), u_inc)
                cp, cg = cp_ref[h, r0:, :], cg_ref[h, r0:, :]
                rest = th[h][r0:] - cp
                a = jnp.exp(lb + (jnp.concatenate([rest, rest], axis=1) - pp))
                if diag:
                    a = jnp.where(strict, a, 0.0)
                g = _dot_nt(dyr, vb) * a
                gg = _dot(g.astype(BF16), u_exc)
                beta = jnp.exp(lb)
                dz = g - beta * (g + (jnp.concatenate([cg, cg], axis=1) + gg))
                if diag:
                    dz = jnp.where(strict, dz, 0.0)
                dzb = dz.astype(BF16)
                dqa_ref[h, r0:, :] += _dot(dzb, kb)
                dk_acc += _dot_tn(dzb, qr)
                dv_acc += _dot_tn(a.astype(BF16), dyr)
                cp_ref[h, r0:, :] = cp + jnp.broadcast_to(pp[:, TK - 1:TK], cp.shape)
                cg_ref[h, r0:, :] = cg + jnp.broadcast_to(gg[:, TK - 1:TK] + g[:, TK - 1:TK], cg.shape)
            dk_ref[rows, :] += dk_acc
            dv_ref[rows, :] += dv_acc

        def loop(j, c):
            block(j, None)
            return c

        lax.fori_loop(0, ND * i, loop, 0)
        for d in range(ND):
            block(ND * i + d, TK * d)
        dq_ref[...] = jnp.where(low, dqa_ref[0], dqa_ref[1]) * 0.125

    return pl.pallas_call(
        body, name="sb_bwd", grid=(4, nq),
        in_specs=[pl.BlockSpec((TQ, 128), lambda p, i: (i, p)),
                  pl.BlockSpec((S, 128), lambda p, i: (0, 4 + p)),
                  pl.BlockSpec((S, 128), lambda p, i: (0, 8 + p)),
                  pl.BlockSpec((TQ, 128), lambda p, i: (i, p)),
                  pl.BlockSpec((TQ, 128), lambda p, i: (i, p))],
        out_specs=[pl.BlockSpec((TQ, 128), lambda p, i: (i, p)),
                   pl.BlockSpec((S, 128), lambda p, i: (0, p)),
                   pl.BlockSpec((S, 128), lambda p, i: (0, p))],
        out_shape=[jax.ShapeDtypeStruct((S, SB_W), F32)] * 3,
        scratch_shapes=[pltpu.VMEM((2, TQ, 128), F32)] * 3,
        compiler_params=_params(("arbitrary", "arbitrary")),
    )(proj, proj, proj, tot, dy)


ML_SCALE = 128 ** -0.5
RC = 256


def _conv_taps(cur, prev8, w):
    n = cur.shape[0]
    win = jnp.concatenate([prev8, cur], axis=0)
    out = w[3:4, :] * cur
    for j in range(3):
        out = out + w[j:j + 1, :] * pltpu.roll(win, 3 - j, 0)[8:8 + n]
    return out


def _ml_prep(proj, conv_w, conv_b):
    S = proj.shape[0]

    def body(x_ref, w_ref, b_ref, o_ref):
        c = pl.program_id(0)
        scale = jnp.where(c < 4, 1.0, ML_SCALE).astype(F32)
        w = w_ref[...]
        b = b_ref[...]
        for n in range(S // RC):
            cur = x_ref[n * RC:(n + 1) * RC, :]
            prev8 = x_ref[n * RC - 8:n * RC, :] if n else jnp.zeros((8, 128), F32)
            pre = b + _conv_taps(cur, prev8, w)
            o_ref[n * RC:(n + 1) * RC, :] = (pre * jax.nn.sigmoid(pre) * scale).astype(BF16)

    return pl.pallas_call(
        body, name="ml_prep", grid=(8,),
        in_specs=[pl.BlockSpec((S, 128), lambda c: (0, 16 + c)),
                  pl.BlockSpec((4, 128), lambda c: (0, c)),
                  pl.BlockSpec((1, 128), lambda c: (0, c))],
        out_specs=pl.BlockSpec((S, 128), lambda c: (0, c)),
        out_shape=jax.ShapeDtypeStruct((S, 1024), BF16),
        compiler_params=_params(("arbitrary",)),
    )(proj, conv_w, conv_b)


def _ml_prep_bwd(proj, conv_w, conv_b, dq, dk):
    S = proj.shape[0]

    def body(x_ref, w_ref, b_ref, dq_ref, dk_ref, dx_ref, gw_ref, gb_ref, dp_ref):
        c = pl.program_id(0)
        w = w_ref[...]
        b = b_ref[...]
        gw = [jnp.zeros((1, 128), F32) for _ in range(4)]
        gb = jnp.zeros((1, 128), F32)
        for n in range(S // RC):
            rows = slice(n * RC, (n + 1) * RC)
            cur = x_ref[rows, :]
            prev8 = x_ref[n * RC - 8:n * RC, :] if n else jnp.zeros((8, 128), F32)
            pre = b + _conv_taps(cur, prev8, w)
            s = jax.nn.sigmoid(pre)
            dpost = jnp.where(c < 4, dq_ref[rows, :], dk_ref[rows, :] * ML_SCALE)
            dpre = dpost * (s * (1.0 + pre * (1.0 - s)))
            dp_ref[rows, :] = dpre
            win = jnp.concatenate([prev8, cur], axis=0)
            gb = gb + jnp.sum(dpre, axis=0, keepdims=True)
            gw[3] = gw[3] + jnp.sum(dpre * cur, axis=0, keepdims=True)
            for j in range(3):
                gw[j] = gw[j] + jnp.sum(dpre * pltpu.roll(win, 3 - j, 0)[8:8 + RC], axis=0, keepdims=True)
        dp_ref[S:S + 8, :] = jnp.zeros((8, 128), F32)
        for n in range(S // RC):
            win = dp_ref[n * RC:(n + 1) * RC + 8, :]
            dx = w[3:4, :] * win[:RC]
            for j in range(3):
                dx = dx + w[j:j + 1, :] * pltpu.roll(win, RC + 8 - (3 - j), 0)[:RC]
            dx_ref[n * RC:(n + 1) * RC, :] = dx
        gw_ref[...] = jnp.concatenate(gw, axis=0)
        gb_ref[...] = gb

    return pl.pallas_call(
        body, name="ml_prep_bwd", grid=(8,),
        in_specs=[pl.BlockSpec((S, 128), lambda c: (0, 16 + c)),
                  pl.BlockSpec((4, 128), lambda c: (0, c)),
                  pl.BlockSpec((1, 128), lambda c: (0, c)),
                  pl.BlockSpec((S, 128), lambda c: (0, jnp.minimum(c, 3))),
                  pl.BlockSpec((S, 128), lambda c: (0, jnp.maximum(c - 4, 0)))],
        out_specs=[pl.BlockSpec((S, 128), lambda c: (0, c)),
                   pl.BlockSpec((4, 128), lambda c: (0, c)),
                   pl.BlockSpec((1, 128), lambda c: (0, c))],
        out_shape=[jax.ShapeDtypeStruct((S, 1024), F32), jax.ShapeDtypeStruct((4, 1024), F32),
                   jax.ShapeDtypeStruct((1, 1024), F32)],
        scratch_shapes=[pltpu.VMEM((S + 8, 128), F32)],
        compiler_params=_params(("arbitrary",)),
    )(proj, conv_w, conv_b, dq, dk)


def _gate_prep(proj_g, gbias):
    S = proj_g.shape[0]
    tm = 512

    def body(g_ref, b_ref, ig_ref, lf_ref):
        g = g_ref[...] + b_ref[...]
        for h in range(4):
            ig_ref[h] = jnp.broadcast_to(g[:, h:h + 1], (tm, 128))
            lf_ref[h] = jnp.broadcast_to(_log_sigmoid_parts(g[:, 4 + h:5 + h]), (tm, 128))

    return pl.pallas_call(
        body, name="gate_prep", grid=(S // tm,),
        in_specs=[pl.BlockSpec((tm, 128), lambda i: (i, 0)), pl.BlockSpec((1, 128), lambda i: (0, 0))],
        out_specs=[pl.BlockSpec((4, tm, 128), lambda i: (0, i, 0))] * 2,
        out_shape=[jax.ShapeDtypeStruct((4, S, 128), F32)] * 2,
        compiler_params=_params(("arbitrary",)),
    )(proj_g, gbias)


def _ml_chunk_fwd(q, k, v, ig, lf, ct, n_st, m_st, tri, causal):
    vf = v.astype(F32)
    vb = v.astype(BF16)
    b = sum(_dot(tri, part) for part in _split3(lf))
    b_last = b[LCH - 1:LCH, :]
    g = b_last - b + ig
    m_loc = jnp.max(g, axis=0, keepdims=True)
    w = jnp.exp(g - m_loc)
    vwf = vf * w
    vw = vwf.astype(BF16)
    ct_loc = _dot_tn(k, vw)
    kf = k.astype(F32)
    n_loc = jnp.sum(w * kf, axis=0, keepdims=True)
    r = jnp.transpose(ig - b)
    d_log = jnp.where(causal, b + r, -jnp.inf)
    m_t = jnp.maximum(b + m_st, jnp.max(d_log, axis=1, keepdims=True))
    w_in = jnp.exp(d_log - m_t)
    qk = _dot_nt(q, k)
    scores = qk * w_in
    cs = jnp.exp(b + m_st - m_t)
    ctb = ct.astype(BF16)
    qc = _dot(q, ctb)
    qf = q.astype(F32)
    qn = jnp.sum(qf * n_st, axis=1, keepdims=True)
    num = _dot(scores.astype(BF16), vb) + cs * qc
    den = jnp.sum(scores, axis=1, keepdims=True) + cs * qn
    em = jnp.exp(-m_t)
    dd = jnp.maximum(jnp.abs(den), em)
    h = num / dd
    m_new = jnp.maximum(b_last + m_st, m_loc)
    a = jnp.exp(b_last + m_st - m_new)
    gg = jnp.exp(m_loc - m_new)
    return dict(vf=vf, vb=vb, b=b, w=w, vwf=vwf, vw=vw, kf=kf, qf=qf, ct_loc=ct_loc, n_loc=n_loc, w_in=w_in, qk=qk,
                scores=scores, cs=cs, ctb=ctb, qc=qc, qn=qn, den=den, em=em, dd=dd, h=h, m_new=m_new, a=a, gg=gg)


def _ml_consts():
    row = lax.broadcasted_iota(jnp.int32, (LCH, LCH), 0)
    col = lax.broadcasted_iota(jnp.int32, (LCH, LCH), 1)
    return row, (col <= row), (col <= row).astype(BF16)


def _ml_fwd(qk, proj, ig, lf):
    S = proj.shape[0]
    nc = S // LCH

    def body(q_ref, k_ref, v_ref, ig_ref, lf_ref, h_ref, cst_ref, nm_ref, ct_ref, n_ref, m_ref):
        _, causal, tri = _ml_consts()
        ct_ref[...] = jnp.zeros_like(ct_ref)
        n_ref[...] = jnp.zeros_like(n_ref)
        m_ref[...] = jnp.zeros_like(m_ref)

        def chunk(c, carry):
            rows = pl.ds(pl.multiple_of(c * LCH, LCH), LCH)
            ct, n_st, m_st = ct_ref[...], n_ref[0:1, :], m_ref[0:1, :]
            cst_ref[c] = ct
            nm_ref[c, 0:8, :] = n_ref[...]
            nm_ref[c, 8:16, :] = m_ref[...]
            f = _ml_chunk_fwd(q_ref[rows, :], k_ref[rows, :], v_ref[rows, :], ig_ref[rows, :], lf_ref[rows, :],
                              ct, n_st, m_st, tri, causal)
            h_ref[rows, :] = f["h"]
            ct_ref[...] = f["a"] * ct + f["gg"] * f["ct_loc"]
            n_ref[...] = jnp.broadcast_to(f["a"] * n_st + f["gg"] * f["n_loc"], (8, 128))
            m_ref[...] = jnp.broadcast_to(f["m_new"], (8, 128))
            return carry

        lax.fori_loop(0, nc, chunk, 0)

    return pl.pallas_call(
        body, name="ml_fwd", grid=(4,),
        in_specs=[pl.BlockSpec((S, 128), lambda h: (0, h)),
                  pl.BlockSpec((S, 128), lambda h: (0, 4 + h)),
                  pl.BlockSpec((S, 128), lambda h: (0, 24 + h)),
                  pl.BlockSpec((None, S, 128), lambda h: (h, 0, 0)),
                  pl.BlockSpec((None, S, 128), lambda h: (h, 0, 0))],
        out_specs=[pl.BlockSpec((S, 128), lambda h: (0, h)),
                   pl.BlockSpec((None, nc, 128, 128), lambda h: (h, 0, 0, 0)),
                   pl.BlockSpec((None, nc, 16, 128), lambda h: (h, 0, 0, 0))],
        out_shape=[jax.ShapeDtypeStruct((S, ML_W), F32), jax.ShapeDtypeStruct((4, nc, 128, 128), F32),
                   jax.ShapeDtypeStruct((4, nc, 16, 128), F32)],
        scratch_shapes=[pltpu.VMEM((128, 128), F32), pltpu.VMEM((8, 128), F32), pltpu.VMEM((8, 128), F32)],
        compiler_params=_params(("arbitrary",)),
    )(qk, qk, proj, ig, lf)


def _ml_bwd(qk, proj, ig, lf, cst, nm, dh):
    S = proj.shape[0]
    nc = S // LCH

    def body(q_ref, k_ref, v_ref, ig_ref, lf_ref, cst_ref, nm_ref, dh_ref,
             dq_ref, dk_ref, dv_ref, dif_ref, gsum_ref, dct_ref, dn_ref, gi_ref):
        hd = pl.program_id(0)
        row, causal, tri = _ml_consts()
        lane = lax.broadcasted_iota(jnp.int32, (LCH, 128), 1)
        last_row = row == LCH - 1
        dct_ref[...] = jnp.zeros_like(dct_ref)
        dn_ref[...] = jnp.zeros_like(dn_ref)
        gi_ref[...] = jnp.zeros_like(gi_ref)

        @pl.when(hd == 0)
        def _():
            dif_ref[...] = jnp.zeros_like(dif_ref)

        def chunk(t, carry):
            c = nc - 1 - t
            rows = pl.ds(pl.multiple_of(c * LCH, LCH), LCH)
            q, k = q_ref[rows, :], k_ref[rows, :]
            ig_, lf_ = ig_ref[rows, :], lf_ref[rows, :]
            ct, n_st, m_st = cst_ref[c], nm_ref[c, 0:1, :], nm_ref[c, 8:9, :]
            f = _ml_chunk_fwd(q, k, v_ref[rows, :], ig_, lf_, ct, n_st, m_st, tri, causal)
            dh_ = dh_ref[rows, :]
            dct_new, dn_new = dct_ref[...], dn_ref[0:1, :]
            e_num = dh_ / f["dd"]
            hdh = jnp.sum(f["h"] * dh_, axis=1, keepdims=True)
            e_den = jnp.where(jnp.abs(f["den"]) > f["em"], -hdh / f["dd"] * jnp.sign(f["den"]), 0.0)
            e_num_b = e_num.astype(BF16)
            ds_ = _dot_nt(e_num_b, f["vb"]) + e_den
            dqk = ds_ * f["w_in"]
            gam = dqk * f["qk"]
            dqk_b = dqk.astype(BF16)
            cse = f["cs"] * e_den
            dq = _dot(dqk_b, k) + f["cs"] * _dot_nt(e_num_b, f["ctb"]) + cse * n_st
            dk = _dot_tn(dqk_b, q)
            dv = _dot_tn(f["scores"].astype(BF16), e_num_b)
            dcl = (f["gg"] * dct_new).astype(BF16)
            dnl = f["gg"] * dn_new
            kd = _dot(k, dcl)
            dv = dv + f["w"] * kd
            dk = dk + _dot_nt(f["vw"], dcl) + f["w"] * dnl
            gam_s = jnp.sum(kd * f["vwf"], axis=1, keepdims=True) + f["w"] * jnp.sum(f["kf"] * dnl, axis=1, keepdims=True)
            col_g = jnp.sum(jnp.transpose(gam), axis=1, keepdims=True) + gam_s
            db = (jnp.sum(gam, axis=1, keepdims=True) + jnp.sum(e_num * (f["cs"] * f["qc"]), axis=1, keepdims=True)
                  + cse * f["qn"] - col_g)
            db_last = jnp.sum(gam_s[:, 0:1]) + jnp.sum(f["a"][0:1, 0:1]) * (jnp.sum(dct_new * ct) + jnp.sum(dn_new * n_st))
            db = jnp.where(last_row, db + db_last, db)
            dlf = sum(_dot_tn(tri, part) for part in _split3(db))
            df = dlf * (1.0 - jnp.exp(lf_))
            dq_ref[rows, :] = dq
            dk_ref[rows, :] = dk
            dv_ref[rows, :] = dv
            dif_ref[rows, :] += jnp.where(lane == hd, col_g, 0.0) + jnp.where(lane == hd + 4, df, 0.0)
            clamped = jnp.where(jnp.abs(f["den"]) > f["em"], 0.0, hdh)
            gi_ref[...] += jnp.broadcast_to(jnp.sum(clamped, axis=0, keepdims=True), (8, 128))
            dct_ref[...] = f["a"] * dct_new + _dot_tn(q, (f["cs"] * e_num).astype(BF16))
            dn_ref[...] = jnp.broadcast_to(f["a"] * dn_new + jnp.sum(cse * f["qf"], axis=0, keepdims=True), (8, 128))
            return carry

        lax.fori_loop(0, nc, chunk, 0)
        lane8 = lax.broadcasted_iota(jnp.int32, (8, 128), 1)
        part = jnp.where(lane8 == hd, gi_ref[...], 0.0)

        @pl.when(hd == 0)
        def _():
            gsum_ref[...] = part

        @pl.when(hd > 0)
        def _():
            gsum_ref[...] += part

        @pl.when(hd == 3)
        def _():
            gsum_ref[...] += jnp.where(lane8 >= 4, jnp.sum(dif_ref[...], axis=0, keepdims=True), 0.0)

    return pl.pallas_call(
        body, name="ml_bwd", grid=(4,),
        in_specs=[pl.BlockSpec((S, 128), lambda h: (0, h)),
                  pl.BlockSpec((S, 128), lambda h: (0, 4 + h)),
                  pl.BlockSpec((S, 128), lambda h: (0, 24 + h)),
                  pl.BlockSpec((None, S, 128), lambda h: (h, 0, 0)),
                  pl.BlockSpec((None, S, 128), lambda h: (h, 0, 0)),
                  pl.BlockSpec((None, nc, 128, 128), lambda h: (h, 0, 0, 0)),
                  pl.BlockSpec((None, nc, 16, 128), lambda h: (h, 0, 0, 0)),
                  pl.BlockSpec((S, 128), lambda h: (0, h))],
        out_specs=[pl.BlockSpec((S, 128), lambda h: (0, h))] * 3 + [pl.BlockSpec((S, 128), lambda h: (0, 0)),
                                                                       pl.BlockSpec((8, 128), lambda h: (0, 0))],
        out_shape=[jax.ShapeDtypeStruct((S, ML_W), F32)] * 3 + [jax.ShapeDtypeStruct((S, 128), F32),
                                                                 jax.ShapeDtypeStruct((8, 128), F32)],
        scratch_shapes=[pltpu.VMEM((128, 128), F32), pltpu.VMEM((8, 128), F32), pltpu.VMEM((8, 128), F32)],
        compiler_params=_params(("arbitrary",)),
    )(qk, qk, proj, ig, lf, cst, nm, dh)


def _inproj_fwd(x, pre_w, w_t, wg_t):
    S, D = x.shape
    tm, tn = min(S, 1024), 1152

    def body(x_ref, pw_ref, w_ref, wg_ref, proj_ref, g_ref, u_ref):
        @pl.when(pl.program_id(1) == 0)
        def _():
            xf = x_ref[...]
            r = lax.rsqrt(jnp.mean(xf * xf, axis=-1, keepdims=True) + EPS)
            u = (xf * r * pw_ref[...]).astype(BF16)
            u_ref[...] = u
            g_ref[...] = _dot_nt(u, wg_ref[...])

        proj_ref[...] = _dot_nt(u_ref[...], w_ref[...])

    return pl.pallas_call(
        body, name="inproj_fwd", grid=(S // tm, N_MAIN // tn),
        in_specs=[pl.BlockSpec((tm, D), lambda i, j: (i, 0)),
                  pl.BlockSpec((1, D), lambda i, j: (0, 0)),
                  pl.BlockSpec((tn, D), lambda i, j: (j, 0)),
                  pl.BlockSpec((128, D), lambda i, j: (0, 0))],
        out_specs=[pl.BlockSpec((tm, tn), lambda i, j: (i, j)),
                   pl.BlockSpec((tm, 128), lambda i, j: (i, 0)),
                   pl.BlockSpec((tm, D), lambda i, j: (i, 0))],
        out_shape=[jax.ShapeDtypeStruct((S, N_MAIN), F32), jax.ShapeDtypeStruct((S, 128), F32),
                   jax.ShapeDtypeStruct((S, D), BF16)],
        compiler_params=_params(("arbitrary", "arbitrary")),
    )(x, pre_w, w_t, wg_t)


def _inproj_bwd(d_main, d_if, w_t, wg_t, x, pre_w, dx_tail):
    S, D = x.shape
    tm, tk = min(S, 1024), 1152
    nk = N_MAIN // tk

    def body(d_ref, dg_ref, w_ref, wg_ref, x_ref, pw_ref, dt_ref, dx_ref, gpw_ref, acc_ref):
        i, k = pl.program_id(0), pl.program_id(1)

        @pl.when(k == 0)
        def _():
            acc_ref[...] = _dot(dg_ref[...], wg_ref[...])

        acc_ref[...] += _dot(d_ref[...], w_ref[...])

        @pl.when(k == nk - 1)
        def _():
            xf = x_ref[...]
            r = lax.rsqrt(jnp.mean(xf * xf, axis=-1, keepdims=True) + EPS)
            xn = xf * r
            du = acc_ref[...]
            gw = du * pw_ref[...]
            dx_ref[...] = dt_ref[...] + r * (gw - xn * jnp.mean(gw * xn, axis=-1, keepdims=True))
            part = jnp.sum(du * xn, axis=0, keepdims=True)

            @pl.when(i == 0)
            def _():
                gpw_ref[...] = part

            @pl.when(i > 0)
            def _():
                gpw_ref[...] += part

    return pl.pallas_call(
        body, name="inproj_bwd", grid=(S // tm, nk),
        in_specs=[pl.BlockSpec((tm, tk), lambda i, k: (i, k)),
                  pl.BlockSpec((tm, 128), lambda i, k: (i, 0)),
                  pl.BlockSpec((tk, D), lambda i, k: (k, 0)),
                  pl.BlockSpec((128, D), lambda i, k: (0, 0)),
                  pl.BlockSpec((tm, D), lambda i, k: (i, 0)),
                  pl.BlockSpec((1, D), lambda i, k: (0, 0)),
                  pl.BlockSpec((tm, D), lambda i, k: (i, 0))],
        out_specs=[pl.BlockSpec((tm, D), lambda i, k: (i, 0)),
                   pl.BlockSpec((1, D), lambda i, k: (0, 0))],
        out_shape=[jax.ShapeDtypeStruct((S, D), F32), jax.ShapeDtypeStruct((1, D), F32)],
        scratch_shapes=[pltpu.VMEM((tm, D), F32)],
        compiler_params=_params(("arbitrary", "arbitrary")),
    )(d_main, d_if, w_t, wg_t, x, pre_w, dx_tail)


def _matmul_tn(a, b, name):
    S, M = a.shape
    N = b.shape[1]
    tmm = 1152 if M % 1152 == 0 else min(M, 1024)
    tk = min(S, 1024)
    nk = S // tk

    def body(a_ref, b_ref, o_ref):
        part = _dot_tn(a_ref[...].astype(BF16), b_ref[...].astype(BF16))

        @pl.when(pl.program_id(1) == 0)
        def _():
            o_ref[...] = part

        @pl.when(pl.program_id(1) > 0)
        def _():
            o_ref[...] += part

    return pl.pallas_call(
        body, name=name, grid=(M // tmm, nk),
        in_specs=[pl.BlockSpec((tk, tmm), lambda i, k: (k, i)),
                  pl.BlockSpec((tk, N), lambda i, k: (k, 0))],
        out_specs=pl.BlockSpec((tmm, N), lambda i, k: (i, 0)),
        out_shape=jax.ShapeDtypeStruct((M, N), F32),
        compiler_params=_params(("arbitrary", "arbitrary")),
    )(a, b)


def _half_mean(v, low):
    s_lo = jnp.sum(jnp.where(low, v, 0.0), axis=1, keepdims=True)
    s_hi = jnp.sum(jnp.where(low, 0.0, v), axis=1, keepdims=True)
    return jnp.where(low, s_lo, s_hi) * (1.0 / 64.0)


def _silu_grad(z, s):
    return s * (1.0 + z * (1.0 - s))


def _tail(y_sb, h_ml, proj, x, p, target, sb_nw, ml_nw, w_out, post_w, w_gate, b_gate, w_up):
    S, D = x.shape
    tm = 256

    def body(ysb_ref, hml_ref, sbz_ref, mlo_ref, mlz_ref, x_ref, p_ref, tg_ref, sbw_ref, mlw_ref, wo_ref, pw_ref,
             wg_ref, bg_ref, wu_ref,
             dx_ref, dysb_ref, dhml_ref, dsbz_ref, dmlo_ref, dmlz_ref, mix_ref, dy_ref, h1_ref, dgp_ref, dpu_ref,
             small_ref):
        lane = lax.broadcasted_iota(jnp.int32, (tm, 128), 1)
        low = lane < 64
        sb_saved, ml_saved, mixed = [], [], []
        for s in range(4):
            sl = slice(128 * s, 128 * s + 128)
            y = ysb_ref[:, sl]
            rs = lax.rsqrt(_half_mean(y * y, low) + EPS)
            n = y * rs
            z = sbz_ref[:, sl]
            sg = jax.nn.sigmoid(z)
            w = sbw_ref[:, sl]
            mixed.append((n * w) * (z * sg))
            sb_saved.append((rs, n, z, sg, w))
        for s in range(4):
            sl = slice(128 * s, 128 * s + 128)
            og = jax.nn.sigmoid(mlo_ref[:, sl])
            hh = hml_ref[:, sl]
            t = og * hh
            rs = lax.rsqrt(jnp.mean(t * t, axis=1, keepdims=True) + EPS)
            n = t * rs
            z = mlz_ref[:, sl]
            sg = jax.nn.sigmoid(z)
            w = mlw_ref[:, sl]
            mixed.append((n * w) * (z * sg))
            ml_saved.append((rs, n, z, sg, w, og, hh))
        mix = jnp.concatenate(mixed, axis=1).astype(BF16)
        mix_ref[...] = mix
        y = _dot(mix, wo_ref[...])
        rs_y = lax.rsqrt(jnp.mean(y * y, axis=1, keepdims=True) + EPS)
        yn = y * rs_y
        pw = pw_ref[...]
        h1 = x_ref[...] + yn * pw
        h1b = h1.astype(BF16)
        h1_ref[...] = h1b
        gate = jax.nn.sigmoid(_dot(h1b, wg_ref[...]) + bg_ref[...])
        pu = _dot(p_ref[...].astype(BF16), wu_ref[...])
        err = (h1 + gate * pu) - tg_ref[...]
        loss = 0.5 * jnp.sum(jnp.sum(err * err, axis=1, keepdims=True) * (1.0 / D))
        d_out = err * (1.0 / D)
        dpu_ref[...] = (d_out * gate).astype(BF16)
        dgp = (d_out * pu) * (gate * (1.0 - gate))
        dgpb = dgp.astype(BF16)
        dgp_ref[...] = dgpb
        d_h1 = d_out + _dot_nt(dgpb, wg_ref[...])
        dx_ref[...] = d_h1
        gwy = d_h1 * pw
        d_y = rs_y * (gwy - yn * jnp.mean(gwy * yn, axis=1, keepdims=True))
        d_yb = d_y.astype(BF16)
        dy_ref[...] = d_yb
        d_mix = _dot_nt(d_yb, wo_ref[...])
        g_nw = []
        for s in range(4):
            sl = slice(128 * s, 128 * s + 128)
            rs, n, z, sg, w = sb_saved[s]
            da = d_mix[:, sl]
            act = z * sg
            dsbz_ref[:, sl] = (da * (n * w) * _silu_grad(z, sg)).astype(BF16)
            dn = da * w * act
            g_nw.append(jnp.sum(da * act * n, axis=0, keepdims=True))
            dysb_ref[:, sl] = rs * (dn - n * _half_mean(dn * n, low))
        for s in range(4):
            sl = slice(128 * s, 128 * s + 128)
            rs, n, z, sg, w, og, hh = ml_saved[s]
            da = d_mix[:, 512 + 128 * s:512 + 128 * s + 128]
            act = z * sg
            dmlz_ref[:, sl] = (da * (n * w) * _silu_grad(z, sg)).astype(BF16)
            dn = da * w * act
            g_nw.append(jnp.sum(da * act * n, axis=0, keepdims=True))
            dt = rs * (dn - n * jnp.mean(dn * n, axis=1, keepdims=True))
            dmlo_ref[:, sl] = (dt * hh * (og * (1.0 - og))).astype(BF16)
            dhml_ref[:, sl] = dt * og
        upd = jnp.concatenate([
            jnp.sum(d_h1 * yn, axis=0, keepdims=True),
            jnp.sum(dgp, axis=0, keepdims=True),
            jnp.concatenate(g_nw, axis=1),
            jnp.full((1, D), loss, F32),
            jnp.zeros((4, D), F32)], axis=0)

        @pl.when(pl.program_id(0) == 0)
        def _():
            small_ref[...] = upd

        @pl.when(pl.program_id(0) > 0)
        def _():
            small_ref[...] += upd

    def rows(width, col=0):
        return pl.BlockSpec((tm, width), lambda i: (i, col))

    def whole(a):
        return pl.BlockSpec(a.shape, lambda i: (0, 0))

    return pl.pallas_call(
        body, name="tail", grid=(S // tm,),
        in_specs=[rows(512), rows(512), rows(512, 3), rows(512, 7), rows(512, 8), rows(D), rows(256), rows(D),
                  whole(sb_nw), whole(ml_nw), whole(w_out), whole(post_w), whole(w_gate), whole(b_gate), whole(w_up)],
        out_specs=[rows(D), rows(512), rows(512), rows(512), rows(512), rows(512), rows(D), rows(D), rows(D), rows(D),
                   rows(D), pl.BlockSpec((8, D), lambda i: (0, 0))],
        out_shape=[jax.ShapeDtypeStruct((S, D), F32), jax.ShapeDtypeStruct((S, 512), F32),
                   jax.ShapeDtypeStruct((S, 512), F32)] + [jax.ShapeDtypeStruct((S, 512), BF16)] * 3
        + [jax.ShapeDtypeStruct((S, D), BF16)] * 5 + [jax.ShapeDtypeStruct((8, D), F32)],
        compiler_params=_params(("arbitrary",)),
    )(y_sb, h_ml, proj, proj, proj, x, p, target, sb_nw, ml_nw, w_out, post_w, w_gate, b_gate, w_up)


def _local_step(x, p, target, pre_w, w_t, wg_t, conv_w, conv_b, gbias, sb_nw, ml_nw, w_out, post_w, w_gate, b_gate,
                w_up):
    proj, proj_g, u = _inproj_fwd(x, pre_w, w_t, wg_t)
    y_sb, tot = _sb_fwd(proj)
    qk = _ml_prep(proj, conv_w, conv_b)
    ig, lf = _gate_prep(proj_g, gbias)
    h_ml, cst, nm = _ml_fwd(qk, proj, ig, lf)
    dx_tail, d_ysb, d_hml, d_sbz, d_mlo, d_mlz, mix, d_y, h1, dgp, dpu, small = _tail(
        y_sb, h_ml, proj, x, p, target, sb_nw, ml_nw, w_out, post_w, w_gate, b_gate, w_up)
    dq, dk, dv = _sb_bwd(proj, tot, d_ysb)
    dqc, dks, dmlv, dif, gif = _ml_bwd(qk, proj, ig, lf, cst, nm, d_hml)
    dmlqk, g_cw, g_cb = _ml_prep_bwd(proj, conv_w, conv_b, dqc, dks)
    d_main = jnp.concatenate([dq.astype(BF16), dk.astype(BF16), dv.astype(BF16), d_sbz, dmlqk.astype(BF16),
                              dmlv.astype(BF16), d_mlo, d_mlz], axis=1)
    d_if = dif.astype(BF16)
    dx, g_pre = _inproj_bwd(d_main, d_if, w_t, wg_t, x, pre_w, dx_tail)
    grads = dict(
        w_t=_matmul_tn(d_main, u, "gw_in"), wg_t=_matmul_tn(d_if, u, "gw_in_gates"),
        w_out=_matmul_tn(mix, d_y, "gw_out"), w_gate=_matmul_tn(h1, dgp, "gw_gate"), w_up=_matmul_tn(p, dpu, "gw_up"),
        conv_w=g_cw, conv_b=g_cb, pre_w=g_pre, gif=gif)
    return dx, grads, small


MESH = pl.DeviceIdType.MESH
ANY = pl.BlockSpec(memory_space=pl.ANY)
N_DEV = 8


def _place():
    return lax.axis_index("x"), lax.axis_index("y"), lax.axis_index("c")


def _block_of(px, py, pc):
    return 4 * px + 2 * py + pc


def _all_gather(a, b):
    def body(a_ref, b_ref, oa_ref, ob_ref, send_sems, recv_sems, local_sems):
        x, y, c = _place()
        me, sibling = (x, y, c), (x, y, 1 - c)
        chips = [(1 - x, y), (x, 1 - y), (1 - x, 1 - y)]
        pairs = ((a_ref, oa_ref), (b_ref, ob_ref))

        def copies(k, block, to, from_input=False):
            slot = _block_of(*block)
            return [pltpu.make_async_remote_copy(
                src_ref=src if from_input else out.at[slot], dst_ref=out.at[slot],
                send_sem=send_sems.at[t, k], recv_sem=recv_sems.at[t, k], device_id=to, device_id_type=MESH)
                for t, (src, out) in enumerate(pairs)]

        mine = [pltpu.make_async_copy(src, out.at[_block_of(*me)], local_sems.at[t])
                for t, (src, out) in enumerate(pairs)]
        for cp in mine:
            cp.start()
        first = copies(0, me, sibling, True)
        for j, chip in enumerate(chips):
            first += copies(1 + j, me, (*chip, c), True)
        for cp in first:
            cp.start()
        passed = []
        for j, chip in enumerate(chips):
            for cp in copies(1 + j, (*chip, c), me):
                cp.wait_recv()
            fwd = copies(4 + j, (*chip, c), sibling)
            for cp in fwd:
                cp.start()
            passed += fwd
        for cp in copies(0, sibling, me):
            cp.wait_recv()
        for j, chip in enumerate(chips):
            for cp in copies(4 + j, (*chip, 1 - c), me):
                cp.wait_recv()
        for cp in first + passed:
            cp.wait_send()
        for cp in mine:
            cp.wait()

    return pl.pallas_call(
        body, name="all_gather",
        in_specs=[ANY, ANY], out_specs=[ANY, ANY],
        out_shape=[jax.ShapeDtypeStruct((N_DEV,) + a.shape, a.dtype), jax.ShapeDtypeStruct((N_DEV,) + b.shape, b.dtype)],
        scratch_shapes=[pltpu.SemaphoreType.DMA((2, 7)), pltpu.SemaphoreType.DMA((2, 7)), pltpu.SemaphoreType.DMA((2,))],
    )(a, b)


def _exchange_pair(g, s):
    def body(g_ref, s_ref, og_ref, os_ref, send_sems, recv_sems, local_sem):
        x, y, c = _place()
        mine = _block_of(x, y, c)
        local = pltpu.make_async_copy(s_ref, os_ref.at[mine], local_sem)
        local.start()
        sent = []
        for k in range(4):
            sent.append(pltpu.make_async_remote_copy(
                src_ref=g_ref.at[k, 1 - c], dst_ref=og_ref.at[k], send_sem=send_sems.at[0, k],
                recv_sem=recv_sems.at[0, k], device_id=(x, y, 1 - c), device_id_type=MESH))
        for k in range(1, N_DEV):
            peer = (x ^ (k >> 2), y ^ ((k >> 1) & 1), c ^ (k & 1))
            sent.append(pltpu.make_async_remote_copy(
                src_ref=s_ref, dst_ref=os_ref.at[mine],
                send_sem=send_sems.at[1, k - 1], recv_sem=recv_sems.at[1, k - 1], device_id=peer, device_id_type=MESH))
        for cp in sent:
            cp.start()
        for cp in sent:
            cp.wait()
        local.wait()

    return pl.pallas_call(
        body, name="exchange_pair",
        in_specs=[ANY, ANY], out_specs=[ANY, ANY],
        out_shape=[jax.ShapeDtypeStruct((4,) + g.shape[2:], g.dtype), jax.ShapeDtypeStruct((N_DEV,) + s.shape, s.dtype)],
        scratch_shapes=[pltpu.SemaphoreType.DMA((2, 7)), pltpu.SemaphoreType.DMA((2, 7)), pltpu.SemaphoreType.DMA],
    )(g, s)


def _pair_sum(g, r, core, tr):
    _, _, R, D = g.shape

    def body(c_ref, g_ref, r_ref, o_ref):
        o_ref[...] = (g_ref[...] + r_ref[...]).astype(BF16)

    return pl.pallas_call(
        body, name="pair_sum",
        grid_spec=pltpu.PrefetchScalarGridSpec(
            num_scalar_prefetch=1, grid=(4, R // tr),
            in_specs=[pl.BlockSpec((None, None, tr, D), lambda k, i, c: (k, c[0], i, 0)),
                      pl.BlockSpec((None, tr, D), lambda k, i, c: (k, i, 0))],
            out_specs=pl.BlockSpec((None, tr, D), lambda k, i, c: (k, i, 0))),
        out_shape=jax.ShapeDtypeStruct((4, R, D), BF16),
        compiler_params=_params(("arbitrary", "arbitrary")),
    )(core, g, r)


def _exchange_chips(p):
    def body(p_ref, o_ref, send_sems, recv_sems, local_sem):
        x, y, c = _place()
        mine = 2 * x + y
        local = pltpu.make_async_copy(p_ref.at[mine], o_ref.at[mine], local_sem)
        local.start()
        sent = []
        for k in range(1, 4):
            px, py = x ^ (k >> 1), y ^ (k & 1)
            sent.append(pltpu.make_async_remote_copy(
                src_ref=p_ref.at[2 * px + py], dst_ref=o_ref.at[mine], send_sem=send_sems.at[k - 1],
                recv_sem=recv_sems.at[k - 1], device_id=(px, py, c), device_id_type=MESH))
        for cp in sent:
            cp.start()
        for cp in sent:
            cp.wait()
        local.wait()

    return pl.pallas_call(
        body, name="exchange_chips",
        in_specs=[ANY], out_specs=ANY, out_shape=jax.ShapeDtypeStruct(p.shape, p.dtype),
        scratch_shapes=[pltpu.SemaphoreType.DMA((3,)), pltpu.SemaphoreType.DMA((3,)), pltpu.SemaphoreType.DMA],
    )(p)


ADAM_LR, ADAM_B1, ADAM_B2, ADAM_EPS, ADAM_WD, ADAM_STEP = 0.001, 0.9, 0.999, 1e-08, 0.01, 10


def _adamw(w, g, m, v):
    m = ADAM_B1 * m + (1.0 - ADAM_B1) * g
    v = ADAM_B2 * v + (1.0 - ADAM_B2) * (g * g)
    m_hat = m / (1.0 - ADAM_B1 ** ADAM_STEP)
    v_hat = v / (1.0 - ADAM_B2 ** ADAM_STEP)
    return -ADAM_LR * (m_hat / (jnp.sqrt(v_hat) + ADAM_EPS) + ADAM_WD * w), m, v


def _adam(parts, w, m, v, tr, name):
    R, D = w.shape
    n = parts.shape[0]

    def body(p_ref, w_ref, m_ref, v_ref, g_ref, d_ref, nm_ref, nv_ref):
        g = p_ref[0].astype(F32)
        for k in range(1, n):
            g = g + p_ref[k].astype(F32)
        g_ref[...] = g
        d_ref[...], nm_ref[...], nv_ref[...] = _adamw(w_ref[...], g, m_ref[...], v_ref[...])

    blk = pl.BlockSpec((tr, D), lambda i: (i, 0))
    return pl.pallas_call(
        body, name=name, grid=(R // tr,),
        in_specs=[pl.BlockSpec((n, tr, D), lambda i: (0, i, 0)), blk, blk, blk],
        out_specs=[blk] * 4, out_shape=[jax.ShapeDtypeStruct((R, D), F32)] * 4,
        compiler_params=_params(("arbitrary",)),
    )(parts, w, m, v)


ROWS_IN = 592
ROWS_BF16 = ROWS_IN + 128 + 128 + 32
ROWS_CONV = 16
ROWS_ALL = ROWS_BF16 + ROWS_CONV
ROW_TILE = 224


def _pad_rows(a, rows):
    return jnp.pad(a, ((0, rows - a.shape[0]), (0, 0)))


def _pack_shards(w_in, w_out, w_gate, w_up, conv_w):
    return jnp.concatenate([
        _pad_rows(w_in[0].T, ROWS_IN), w_out[0], w_gate[0], w_up[0].reshape(32, D_MODEL),
        _pad_rows(jnp.pad(conv_w[0].reshape(1, 512), ((0, 0), (0, 512))), ROWS_CONV)], axis=0)


def _unpack_shards(a):
    return (a[:SHARD_IN].T[None], a[ROWS_IN:ROWS_IN + 128][None], a[ROWS_IN + 128:ROWS_IN + 256][None],
            a[ROWS_IN + 256:ROWS_BF16].reshape(1, 256, 128), a[ROWS_BF16, :512].reshape(1, 4, 128))


def _pack_small(pre_w, conv_b, i_bias, f_bias, sb_nw, ml_nw, post_w, b_gate):
    gates = jnp.pad(jnp.concatenate([i_bias, f_bias], axis=1), ((0, 0), (0, D_MODEL - 8)))
    return jnp.concatenate([post_w, b_gate, jnp.concatenate([sb_nw, ml_nw], axis=1), jnp.zeros((1, D_MODEL), F32),
                            pre_w, conv_b, gates, jnp.zeros((1, D_MODEL), F32)], axis=0)


def _unpack_small(a):
    return a[4:5], a[5:6], a[6:7, 0:4], a[6:7, 4:8], a[2:3, :512], a[2:3, 512:], a[0:1], a[1:2]


def kernel(x, p, pre_norm_w, w_in, ml_conv_w, ml_conv_b, ml_i_bias, ml_f_bias, sb_norm_w, ml_norm_w, w_out, post_norm_w, ple_w_up, ple_w_gate, ple_b_gate, loss_target, m_pre_norm_w, m_w_in, m_ml_conv_w, m_ml_conv_b, m_ml_i_bias, m_ml_f_bias, m_sb_norm_w, m_ml_norm_w, m_w_out, m_post_norm_w, m_ple_w_up, m_ple_w_gate, m_ple_b_gate, v_pre_norm_w, v_w_in, v_ml_conv_w, v_ml_conv_b, v_ml_i_bias, v_ml_f_bias, v_sb_norm_w, v_ml_norm_w, v_w_out, v_post_norm_w, v_ple_w_up, v_ple_w_gate, v_ple_b_gate):
    D = D_MODEL
    w_pk = _pack_shards(w_in, w_out, ple_w_gate, ple_w_up, ml_conv_w)
    m_pk = _pack_shards(m_w_in, m_w_out, m_ple_w_gate, m_ple_w_up, m_ml_conv_w)
    v_pk = _pack_shards(v_w_in, v_w_out, v_ple_w_gate, v_ple_w_up, v_ml_conv_w)
    w_sm = _pack_small(pre_norm_w, ml_conv_b, ml_i_bias, ml_f_bias, sb_norm_w, ml_norm_w, post_norm_w, ple_b_gate)
    m_sm = _pack_small(m_pre_norm_w, m_ml_conv_b, m_ml_i_bias, m_ml_f_bias, m_sb_norm_w, m_ml_norm_w, m_post_norm_w, m_ple_b_gate)
    v_sm = _pack_small(v_pre_norm_w, v_ml_conv_b, v_ml_i_bias, v_ml_f_bias, v_sb_norm_w, v_ml_norm_w, v_post_norm_w, v_ple_b_gate)

    ga, gb = _all_gather(w_pk[:ROWS_BF16].astype(BF16), w_pk[ROWS_BF16:])
    w_in_t = ga[:, :SHARD_IN].reshape(N_IN, D)
    w_t = w_in_t
    wg_t = _pad_rows(w_in_t[N_MAIN:], 128)
    w_out_f = ga[:, ROWS_IN:ROWS_IN + 128].reshape(D, D)
    w_gate_f = ga[:, ROWS_IN + 128:ROWS_IN + 256].reshape(D, D)
    w_up_f = ga[:, ROWS_IN + 256:].reshape(N_DEV, 256, 128).transpose(1, 0, 2).reshape(256, D)
    conv_w_f = gb[:, 0, :512].reshape(N_DEV, 4, 128).transpose(1, 0, 2).reshape(4, D)
    gbias = jnp.pad(jnp.concatenate([ml_i_bias, ml_f_bias], axis=1), ((0, 0), (0, 120)))

    dx, g, small = _local_step(x[0], p[0, 0], loss_target[0], pre_norm_w, w_t, wg_t, conv_w_f, ml_conv_b, gbias,
                               sb_norm_w, ml_norm_w, w_out_f, post_norm_w, w_gate_f, ple_b_gate, w_up_f)

    g_in = jnp.concatenate([g["w_t"], g["wg_t"][:8]], axis=0).reshape(N_DEV, SHARD_IN, D)
    g_blocks = jnp.concatenate([
        jnp.pad(g_in, ((0, 0), (0, ROWS_IN - SHARD_IN), (0, 0))),
        g["w_out"].reshape(N_DEV, 128, D), g["w_gate"].reshape(N_DEV, 128, D),
        g["w_up"].reshape(256, N_DEV, 128).transpose(1, 0, 2).reshape(N_DEV, 32, D),
        jnp.pad(g["conv_w"].reshape(4, N_DEV, 128).transpose(1, 0, 2).reshape(N_DEV, 1, 512),
                ((0, 0), (0, ROWS_CONV - 1), (0, 512))),
    ], axis=1).reshape(4, 2, ROWS_ALL, D)
    g_small = jnp.concatenate([small[0:4], g["pre_w"], g["conv_b"], jnp.pad(g["gif"][0:1], ((0, 0), (0, D - 128))),
                               jnp.zeros((1, D), F32)], axis=0)
    from_pair, parts_sm = _exchange_pair(g_blocks, g_small)
    core = lax.axis_index("c").astype(jnp.int32).reshape(1)
    parts = _exchange_chips(_pair_sum(g_blocks, from_pair, core, ROW_TILE))

    grad_pk, delta_pk, nm_pk, nv_pk = _adam(parts, w_pk, m_pk, v_pk, ROW_TILE, "adam")
    grad_sm, delta_sm, nm_sm, nv_sm = _adam(parts_sm, w_sm, m_sm, v_sm, 8, "adam_small")
    loss = grad_sm[3, 0]

    def ordered(pk, sm):
        win, wout, wgate, wup, convw = _unpack_shards(pk)
        pre_w, conv_b, i_b, f_b, sb_nw, ml_nw, post_w, b_gate = _unpack_small(sm)
        return [pre_w, win, convw, conv_b, i_b, f_b, sb_nw, ml_nw, wout, post_w, wup, wgate, b_gate]

    return (loss, dx[None], *ordered(grad_pk, grad_sm), *ordered(delta_pk, delta_sm), *ordered(nm_pk, nm_sm),
            *ordered(nv_pk, nv_sm))
```

```python
import functools

import jax
import jax.numpy as jnp
from jax import lax
from jax.experimental import pallas as pl
from jax.experimental.pallas import tpu as pltpu

F32 = jnp.float32
BF16 = jnp.bfloat16
EPS = 1e-6
D_MODEL = 1024
SB_W = 512
ML_W = 512
N_MAIN = 4608
N_IN = 4616
SHARD_IN = 577
SHARD_IN_PAD = 584
TQ = 1024
TK = 256
ND = TQ // TK
LCH = 128
VMEM_LIMIT = 56 * 1024 * 1024


def _dot(a, b):
    return jnp.dot(a, b, preferred_element_type=F32)


def _dot_nt(a, b):
    return lax.dot_general(a, b, (((1,), (1,)), ((), ())), preferred_element_type=F32)


def _dot_tn(a, b):
    return lax.dot_general(a, b, (((0,), (0,)), ((), ())), preferred_element_type=F32)


def _split2(x):
    hi = x.astype(BF16)
    lo = (x - hi.astype(F32)).astype(BF16)
    return hi, lo


def _split3(x):
    hi = x.astype(BF16)
    r = x - hi.astype(F32)
    mid = r.astype(BF16)
    lo = (r - mid.astype(F32)).astype(BF16)
    return hi, mid, lo


def _params(sem):
    return pltpu.CompilerParams(dimension_semantics=sem, vmem_limit_bytes=VMEM_LIMIT)


def _log_sigmoid_parts(z):
    e = jnp.exp(-jnp.abs(z))
    return jnp.minimum(z, 0.0) - jnp.log(1.0 + e)


def _sb_fwd(proj):
    S = proj.shape[0]
    nq = S // TQ

    def body(q_ref, k_ref, v_ref, y_ref, t_ref, acc_ref, car_ref):
        i = pl.program_id(1)
        low = lax.broadcasted_iota(jnp.int32, (TQ, 128), 1) < 64
        row = lax.broadcasted_iota(jnp.int32, (TK, TK), 0)
        col = lax.broadcasted_iota(jnp.int32, (TK, TK), 1)
        uo = (row > col).astype(BF16)
        uo = jnp.concatenate([uo, uo], axis=0)
        q = q_ref[...] * 0.125
        qh = (jnp.where(low, q, 0.0).astype(BF16), jnp.where(low, 0.0, q).astype(BF16))
        acc_ref[...] = jnp.zeros_like(acc_ref)
        car_ref[...] = jnp.zeros_like(car_ref)

        def block(j, r0):
            diag = r0 is not None
            r0 = r0 or 0
            if diag:
                strict = (lax.broadcasted_iota(jnp.int32, (TQ - r0, TK), 1)
                          < lax.broadcasted_iota(jnp.int32, (TQ - r0, TK), 0))
            rows = pl.ds(pl.multiple_of(j * TK, TK), TK)
            kb = k_ref[rows, :].astype(BF16)
            vb = v_ref[rows, :].astype(BF16)
            for h in range(2):
                z = _dot_nt(qh[h][r0:], kb)
                lb = _log_sigmoid_parts(z)
                lk = lb - z
                if diag:
                    lk = jnp.where(strict, lk, 0.0)
                hi, lo = _split2(lk)
                rr = _dot(jnp.concatenate([hi, lo], axis=1), uo)
                car = car_ref[h, r0:, :]
                a = jnp.exp(lb + jnp.concatenate([car, car], axis=1) + rr)
                if diag:
                    a = jnp.where(strict, a, 0.0)
                acc_ref[h, r0:, :] += _dot(a.astype(BF16), vb)
                car_ref[h, r0:, :] = car + jnp.broadcast_to(rr[:, 0:1] + lk[:, 0:1], car.shape)

        for d in reversed(range(ND)):
            block(ND * i + d, TK * d)

        def loop(n, c):
            block(ND * i - 1 - n, None)
            return c

        lax.fori_loop(0, ND * i, loop, 0)
        y_ref[...] = jnp.where(low, acc_ref[0], acc_ref[1])
        t_ref[...] = jnp.where(low, car_ref[0], car_ref[1])

    return pl.pallas_call(
        body, name="sb_fwd", grid=(4, nq),
        in_specs=[pl.BlockSpec((TQ, 128), lambda p, i: (i, p)),
                  pl.BlockSpec((S, 128), lambda p, i: (0, 4 + p)),
                  pl.BlockSpec((S, 128), lambda p, i: (0, 8 + p))],
        out_specs=[pl.BlockSpec((TQ, 128), lambda p, i: (i, p)),
                   pl.BlockSpec((TQ, 128), lambda p, i: (i, p))],
        out_shape=[jax.ShapeDtypeStruct((S, SB_W), F32), jax.ShapeDtypeStruct((S, SB_W), F32)],
        scratch_shapes=[pltpu.VMEM((2, TQ, 128), F32), pltpu.VMEM((2, TQ, 128), F32)],
        compiler_params=_params(("arbitrary", "arbitrary")),
    )(proj, proj, proj)


def _sb_bwd(proj, tot, dy):
    S = proj.shape[0]
    nq = S // TQ

    def body(q_ref, k_ref, v_ref, t_ref, dy_ref, dq_ref, dk_ref, dv_ref, dqa_ref, cp_ref, cg_ref):
        i = pl.program_id(1)
        low = lax.broadcasted_iota(jnp.int32, (TQ, 128), 1) < 64
        row = lax.broadcasted_iota(jnp.int32, (TK, TK), 0)
        col = lax.broadcasted_iota(jnp.int32, (TK, TK), 1)
        u_inc = (row <= col).astype(BF16)
        u_inc = jnp.concatenate([u_inc, u_inc], axis=0)
        u_exc = (row < col).astype(BF16)
        q = q_ref[...] * 0.125
        qh = (jnp.where(low, q, 0.0).astype(BF16), jnp.where(low, 0.0, q).astype(BF16))
        dy_ = dy_ref[...]
        dyh = (jnp.where(low, dy_, 0.0).astype(BF16), jnp.where(low, 0.0, dy_).astype(BF16))
        t_ = t_ref[...]
        t_sw = pltpu.roll(t_, 64, 1)
        th = (jnp.where(low, t_, t_sw), jnp.where(low, t_sw, t_))
        dqa_ref[...] = jnp.zeros_like(dqa_ref)
        cp_ref[...] = jnp.zeros_like(cp_ref)
        cg_ref[...] = jnp.zeros_like(cg_ref)

        @pl.when(i == 0)
        def _():
            dk_ref[...] = jnp.zeros_like(dk_ref)
            dv_ref[...] = jnp.zeros_like(dv_ref)

        def block(j, r0):
            diag = r0 is not None
            r0 = r0 or 0
            if diag:
                strict = (lax.broadcasted_iota(jnp.int32, (TQ - r0, TK), 1)
                          < lax.broadcasted_iota(jnp.int32, (TQ - r0, TK), 0))
            rows = pl.ds(pl.multiple_of(j * TK, TK), TK)
            kb = k_ref[rows, :].astype(BF16)
            vb = v_ref[rows, :].astype(BF16)
            dk_acc = jnp.zeros((TK, 128), F32)
            dv_acc = jnp.zeros((TK, 128), F32)
            for h in range(2):
                qr, dyr = qh[h][r0:], dyh[h][r0:]
                z = _dot_nt(qr, kb)
                lb = _log_sigmoid_parts(z)
                lk = lb - z
                if diag:
                    lk = jnp.where(strict, lk, 0.0)
                hi, lo = _split2(lk)
                pp = _dot(jnp.concatenate([hi, lo], axis=1), u_inc)
                cp, cg = cp_ref[h, r0:, :], cg_ref[h, r0:, :]
                rest = th[h][r0:] - cp
                a = jnp.exp(lb + (jnp.concatenate([rest, rest], axis=1) - pp))
                if diag:
                    a = jnp.where(strict, a, 0.0)
                g = _dot_nt(dyr, vb) * a
                gg = _dot(g.astype(BF16), u_exc)
                beta = jnp.exp(lb)
                dz = g - beta * (g + (jnp.concatenate([cg, cg], axis=1) + gg))
                if diag:
                    dz = jnp.where(strict, dz, 0.0)
                dzb = dz.astype(BF16)
                dqa_ref[h, r0:, :] += _dot(dzb, kb)
                dk_acc += _dot_tn(dzb, qr)
                dv_acc += _dot_tn(a.astype(BF16), dyr)
                cp_ref[h, r0:, :] = cp + jnp.broadcast_to(pp[:, TK - 1:TK], cp.shape)
                cg_ref[h, r0:, :] = cg + jnp.broadcast_to(gg[:, TK - 1:TK] + g[:, TK - 1:TK], cg.shape)
            dk_ref[rows, :] += dk_acc
            dv_ref[rows, :] += dv_acc

        def loop(j, c):
            block(j, None)
            return c

        lax.fori_loop(0, ND * i, loop, 0)
        for d in range(ND):
            block(ND * i + d, TK * d)
        dq_ref[...] = jnp.where(low, dqa_ref[0], dqa_ref[1]) * 0.125

    return pl.pallas_call(
        body, name="sb_bwd", grid=(4, nq),
        in_specs=[pl.BlockSpec((TQ, 128), lambda p, i: (i, p)),
                  pl.BlockSpec((S, 128), lambda p, i: (0, 4 + p)),
                  pl.BlockSpec((S, 128), lambda p, i: (0, 8 + p)),
                  pl.BlockSpec((TQ, 128), lambda p, i: (i, p)),
                  pl.BlockSpec((TQ, 128), lambda p, i: (i, p))],
        out_specs=[pl.BlockSpec((TQ, 128), lambda p, i: (i, p)),
                   pl.BlockSpec((S, 128), lambda p, i: (0, p)),
                   pl.BlockSpec((S, 128), lambda p, i: (0, p))],
        out_shape=[jax.ShapeDtypeStruct((S, SB_W), F32)] * 3,
        scratch_shapes=[pltpu.VMEM((2, TQ, 128), F32)] * 3,
        compiler_params=_params(("arbitrary", "arbitrary")),
    )(proj, proj, proj, tot, dy)


ML_SCALE = 128 ** -0.5
RC = 256


def _conv_taps(cur, prev8, w):
    n = cur.shape[0]
    win = jnp.concatenate([prev8, cur], axis=0)
    out = w[3:4, :] * cur
    for j in range(3):
        out = out + w[j:j + 1, :] * pltpu.roll(win, 3 - j, 0)[8:8 + n]
    return out


def _ml_prep(proj, conv_w, conv_b):
    S = proj.shape[0]

    def body(x_ref, w_ref, b_ref, o_ref):
        c = pl.program_id(0)
        scale = jnp.where(c < 4, 1.0, ML_SCALE).astype(F32)
        w = w_ref[...]
        b = b_ref[...]
        for n in range(S // RC):
            cur = x_ref[n * RC:(n + 1) * RC, :]
            prev8 = x_ref[n * RC - 8:n * RC, :] if n else jnp.zeros((8, 128), F32)
            pre = b + _conv_taps(cur, prev8, w)
            o_ref[n * RC:(n + 1) * RC, :] = (pre * jax.nn.sigmoid(pre) * scale).astype(BF16)

    return pl.pallas_call(
        body, name="ml_prep", grid=(8,),
        in_specs=[pl.BlockSpec((S, 128), lambda c: (0, 16 + c)),
                  pl.BlockSpec((4, 128), lambda c: (0, c)),
                  pl.BlockSpec((1, 128), lambda c: (0, c))],
        out_specs=pl.BlockSpec((S, 128), lambda c: (0, c)),
        out_shape=jax.ShapeDtypeStruct((S, 1024), BF16),
        compiler_params=_params(("arbitrary",)),
    )(proj, conv_w, conv_b)


def _ml_prep_bwd(proj, conv_w, conv_b, dq, dk):
    S = proj.shape[0]

    def body(x_ref, w_ref, b_ref, dq_ref, dk_ref, dx_ref, gw_ref, gb_ref, dp_ref):
        c = pl.program_id(0)
        w = w_ref[...]
        b = b_ref[...]
        gw = [jnp.zeros((1, 128), F32) for _ in range(4)]
        gb = jnp.zeros((1, 128), F32)
        for n in range(S // RC):
            rows = slice(n * RC, (n + 1) * RC)
            cur = x_ref[rows, :]
            prev8 = x_ref[n * RC - 8:n * RC, :] if n else jnp.zeros((8, 128), F32)
            pre = b + _conv_taps(cur, prev8, w)
            s = jax.nn.sigmoid(pre)
            dpost = jnp.where(c < 4, dq_ref[rows, :], dk_ref[rows, :] * ML_SCALE)
            dpre = dpost * (s * (1.0 + pre * (1.0 - s)))
            dp_ref[rows, :] = dpre
            win = jnp.concatenate([prev8, cur], axis=0)
            gb = gb + jnp.sum(dpre, axis=0, keepdims=True)
            gw[3] = gw[3] + jnp.sum(dpre * cur, axis=0, keepdims=True)
            for j in range(3):
                gw[j] = gw[j] + jnp.sum(dpre * pltpu.roll(win, 3 - j, 0)[8:8 + RC], axis=0, keepdims=True)
        dp_ref[S:S + 8, :] = jnp.zeros((8, 128), F32)
        for n in range(S // RC):
            win = dp_ref[n * RC:(n + 1) * RC + 8, :]
            dx = w[3:4, :] * win[:RC]
            for j in range(3):
                dx = dx + w[j:j + 1, :] * pltpu.roll(win, RC + 8 - (3 - j), 0)[:RC]
            dx_ref[n * RC:(n + 1) * RC, :] = dx
        gw_ref[...] = jnp.concatenate(gw, axis=0)
        gb_ref[...] = gb

    return pl.pallas_call(
        body, name="ml_prep_bwd", grid=(8,),
        in_specs=[pl.BlockSpec((S, 128), lambda c: (0, 16 + c)),
                  pl.BlockSpec((4, 128), lambda c: (0, c)),
                  pl.BlockSpec((1, 128), lambda c: (0, c)),
                  pl.BlockSpec((S, 128), lambda c: (0, jnp.minimum(c, 3))),
                  pl.BlockSpec((S, 128), lambda c: (0, jnp.maximum(c - 4, 0)))],
        out_specs=[pl.BlockSpec((S, 128), lambda c: (0, c)),
                   pl.BlockSpec((4, 128), lambda c: (0, c)),
                   pl.BlockSpec((1, 128), lambda c: (0, c))],
        out_shape=[jax.ShapeDtypeStruct((S, 1024), F32), jax.ShapeDtypeStruct((4, 1024), F32),
                   jax.ShapeDtypeStruct((1, 1024), F32)],
        scratch_shapes=[pltpu.VMEM((S + 8, 128), F32)],
        compiler_params=_params(("arbitrary",)),
    )(proj, conv_w, conv_b, dq, dk)


def _gate_prep(proj_g, gbias):
    S = proj_g.shape[0]
    tm = 512

    def body(g_ref, b_ref, ig_ref, lf_ref):
        g = g_ref[...] + b_ref[...]
        for h in range(4):
            ig_ref[h] = jnp.broadcast_to(g[:, h:h + 1], (tm, 128))
            lf_ref[h] = jnp.broadcast_to(_log_sigmoid_parts(g[:, 4 + h:5 + h]), (tm, 128))

    return pl.pallas_call(
        body, name="gate_prep", grid=(S // tm,),
        in_specs=[pl.BlockSpec((tm, 128), lambda i: (i, 0)), pl.BlockSpec((1, 128), lambda i: (0, 0))],
        out_specs=[pl.BlockSpec((4, tm, 128), lambda i: (0, i, 0))] * 2,
        out_shape=[jax.ShapeDtypeStruct((4, S, 128), F32)] * 2,
        compiler_params=_params(("arbitrary",)),
    )(proj_g, gbias)


def _ml_chunk_fwd(q, k, v, ig, lf, ct, n_st, m_st, tri, causal):
    vf = v.astype(F32)
    vb = v.astype(BF16)
    b = sum(_dot(tri, part) for part in _split3(lf))
    b_last = b[LCH - 1:LCH, :]
    g = b_last - b + ig
    m_loc = jnp.max(g, axis=0, keepdims=True)
    w = jnp.exp(g - m_loc)
    vwf = vf * w
    vw = vwf.astype(BF16)
    ct_loc = _dot_tn(k, vw)
    kf = k.astype(F32)
    n_loc = jnp.sum(w * kf, axis=0, keepdims=True)
    r = jnp.transpose(ig - b)
    d_log = jnp.where(causal, b + r, -jnp.inf)
    m_t = jnp.maximum(b + m_st, jnp.max(d_log, axis=1, keepdims=True))
    w_in = jnp.exp(d_log - m_t)
    qk = _dot_nt(q, k)
    scores = qk * w_in
    cs = jnp.exp(b + m_st - m_t)
    ctb = ct.astype(BF16)
    qc = _dot(q, ctb)
    qf = q.astype(F32)
    qn = jnp.sum(qf * n_st, axis=1, keepdims=True)
    num = _dot(scores.astype(BF16), vb) + cs * qc
    den = jnp.sum(scores, axis=1, keepdims=True) + cs * qn
    em = jnp.exp(-m_t)
    dd = jnp.maximum(jnp.abs(den), em)
    h = num / dd
    m_new = jnp.maximum(b_last + m_st, m_loc)
    a = jnp.exp(b_last + m_st - m_new)
    gg = jnp.exp(m_loc - m_new)
    return dict(vf=vf, vb=vb, b=b, w=w, vwf=vwf, vw=vw, kf=kf, qf=qf, ct_loc=ct_loc, n_loc=n_loc, w_in=w_in, qk=qk,
                scores=scores, cs=cs, ctb=ctb, qc=qc, qn=qn, den=den, em=em, dd=dd, h=h, m_new=m_new, a=a, gg=gg)


def _ml_consts():
    row = lax.broadcasted_iota(jnp.int32, (LCH, LCH), 0)
    col = lax.broadcasted_iota(jnp.int32, (LCH, LCH), 1)
    return row, (col <= row), (col <= row).astype(BF16)


def _ml_fwd(qk, proj, ig, lf):
    S = proj.shape[0]
    nc = S // LCH

    def body(q_ref, k_ref, v_ref, ig_ref, lf_ref, h_ref, cst_ref, nm_ref, ct_ref, n_ref, m_ref):
        _, causal, tri = _ml_consts()
        ct_ref[...] = jnp.zeros_like(ct_ref)
        n_ref[...] = jnp.zeros_like(n_ref)
        m_ref[...] = jnp.zeros_like(m_ref)

        def chunk(c, carry):
            rows = pl.ds(pl.multiple_of(c * LCH, LCH), LCH)
            ct, n_st, m_st = ct_ref[...], n_ref[0:1, :], m_ref[0:1, :]
            cst_ref[c] = ct
            nm_ref[c, 0:8, :] = n_ref[...]
            nm_ref[c, 8:16, :] = m_ref[...]
            f = _ml_chunk_fwd(q_ref[rows, :], k_ref[rows, :], v_ref[rows, :], ig_ref[rows, :], lf_ref[rows, :],
                              ct, n_st, m_st, tri, causal)
            h_ref[rows, :] = f["h"]
            ct_ref[...] = f["a"] * ct + f["gg"] * f["ct_loc"]
            n_ref[...] = jnp.broadcast_to(f["a"] * n_st + f["gg"] * f["n_loc"], (8, 128))
            m_ref[...] = jnp.broadcast_to(f["m_new"], (8, 128))
            return carry

        lax.fori_loop(0, nc, chunk, 0)

    return pl.pallas_call(
        body, name="ml_fwd", grid=(4,),
        in_specs=[pl.BlockSpec((S, 128), lambda h: (0, h)),
                  pl.BlockSpec((S, 128), lambda h: (0, 4 + h)),
                  pl.BlockSpec((S, 128), lambda h: (0, 24 + h)),
                  pl.BlockSpec((None, S, 128), lambda h: (h, 0, 0)),
                  pl.BlockSpec((None, S, 128), lambda h: (h, 0, 0))],
        out_specs=[pl.BlockSpec((S, 128), lambda h: (0, h)),
                   pl.BlockSpec((None, nc, 128, 128), lambda h: (h, 0, 0, 0)),
                   pl.BlockSpec((None, nc, 16, 128), lambda h: (h, 0, 0, 0))],
        out_shape=[jax.ShapeDtypeStruct((S, ML_W), F32), jax.ShapeDtypeStruct((4, nc, 128, 128), F32),
                   jax.ShapeDtypeStruct((4, nc, 16, 128), F32)],
        scratch_shapes=[pltpu.VMEM((128, 128), F32), pltpu.VMEM((8, 128), F32), pltpu.VMEM((8, 128), F32)],
        compiler_params=_params(("arbitrary",)),
    )(qk, qk, proj, ig, lf)


def _ml_bwd(qk, proj, ig, lf, cst, nm, dh):
    S = proj.shape[0]
    nc = S // LCH

    def body(q_ref, k_ref, v_ref, ig_ref, lf_ref, cst_ref, nm_ref, dh_ref,
             dq_ref, dk_ref, dv_ref, dif_ref, gsum_ref, dct_ref, dn_ref, gi_ref):
        hd = pl.program_id(0)
        row, causal, tri = _ml_consts()
        lane = lax.broadcasted_iota(jnp.int32, (LCH, 128), 1)
        last_row = row == LCH - 1
        dct_ref[...] = jnp.zeros_like(dct_ref)
        dn_ref[...] = jnp.zeros_like(dn_ref)
        gi_ref[...] = jnp.zeros_like(gi_ref)

        @pl.when(hd == 0)
        def _():
            dif_ref[...] = jnp.zeros_like(dif_ref)

        def chunk(t, carry):
            c = nc - 1 - t
            rows = pl.ds(pl.multiple_of(c * LCH, LCH), LCH)
            q, k = q_ref[rows, :], k_ref[rows, :]
            ig_, lf_ = ig_ref[rows, :], lf_ref[rows, :]
            ct, n_st, m_st = cst_ref[c], nm_ref[c, 0:1, :], nm_ref[c, 8:9, :]
            f = _ml_chunk_fwd(q, k, v_ref[rows, :], ig_, lf_, ct, n_st, m_st, tri, causal)
            dh_ = dh_ref[rows, :]
            dct_new, dn_new = dct_ref[...], dn_ref[0:1, :]
            e_num = dh_ / f["dd"]
            hdh = jnp.sum(f["h"] * dh_, axis=1, keepdims=True)
            e_den = jnp.where(jnp.abs(f["den"]) > f["em"], -hdh / f["dd"] * jnp.sign(f["den"]), 0.0)
            e_num_b = e_num.astype(BF16)
            ds_ = _dot_nt(e_num_b, f["vb"]) + e_den
            dqk = ds_ * f["w_in"]
            gam = dqk * f["qk"]
            dqk_b = dqk.astype(BF16)
            cse = f["cs"] * e_den
            dq = _dot(dqk_b, k) + f["cs"] * _dot_nt(e_num_b, f["ctb"]) + cse * n_st
            dk = _dot_tn(dqk_b, q)
            dv = _dot_tn(f["scores"].astype(BF16), e_num_b)
            dcl = (f["gg"] * dct_new).astype(BF16)
            dnl = f["gg"] * dn_new
            kd = _dot(k, dcl)
            dv = dv + f["w"] * kd
            dk = dk + _dot_nt(f["vw"], dcl) + f["w"] * dnl
            gam_s = jnp.sum(kd * f["vwf"], axis=1, keepdims=True) + f["w"] * jnp.sum(f["kf"] * dnl, axis=1, keepdims=True)
            col_g = jnp.sum(jnp.transpose(gam), axis=1, keepdims=True) + gam_s
            db = (jnp.sum(gam, axis=1, keepdims=True) + jnp.sum(e_num * (f["cs"] * f["qc"]), axis=1, keepdims=True)
                  + cse * f["qn"] - col_g)
            db_last = jnp.sum(gam_s[:, 0:1]) + jnp.sum(f["a"][0:1, 0:1]) * (jnp.sum(dct_new * ct) + jnp.sum(dn_new * n_st))
            db = jnp.where(last_row, db + db_last, db)
            dlf = sum(_dot_tn(tri, part) for part in _split3(db))
            df = dlf * (1.0 - jnp.exp(lf_))
            dq_ref[rows, :] = dq
            dk_ref[rows, :] = dk
            dv_ref[rows, :] = dv
            dif_ref[rows, :] += jnp.where(lane == hd, col_g, 0.0) + jnp.where(lane == hd + 4, df, 0.0)
            clamped = jnp.where(jnp.abs(f["den"]) > f["em"], 0.0, hdh)
            gi_ref[...] += jnp.broadcast_to(jnp.sum(clamped, axis=0, keepdims=True), (8, 128))
            dct_ref[...] = f["a"] * dct_new + _dot_tn(q, (f["cs"] * e_num).astype(BF16))
            dn_ref[...] = jnp.broadcast_to(f["a"] * dn_new + jnp.sum(cse * f["qf"], axis=0, keepdims=True), (8, 128))
            return carry

        lax.fori_loop(0, nc, chunk, 0)
        lane8 = lax.broadcasted_iota(jnp.int32, (8, 128), 1)
        part = jnp.where(lane8 == hd, gi_ref[...], 0.0)

        @pl.when(hd == 0)
        def _():
            gsum_ref[...] = part

        @pl.when(hd > 0)
        def _():
            gsum_ref[...] += part

        @pl.when(hd == 3)
        def _():
            gsum_ref[...] += jnp.where(lane8 >= 4, jnp.sum(dif_ref[...], axis=0, keepdims=True), 0.0)

    return pl.pallas_call(
        body, name="ml_bwd", grid=(4,),
        in_specs=[pl.BlockSpec((S, 128), lambda h: (0, h)),
                  pl.BlockSpec((S, 128), lambda h: (0, 4 + h)),
                  pl.BlockSpec((S, 128), lambda h: (0, 24 + h)),
                  pl.BlockSpec((None, S, 128), lambda h: (h, 0, 0)),
                  pl.BlockSpec((None, S, 128), lambda h: (h, 0, 0)),
                  pl.BlockSpec((None, nc, 128, 128), lambda h: (h, 0, 0, 0)),
                  pl.BlockSpec((None, nc, 16, 128), lambda h: (h, 0, 0, 0)),
                  pl.BlockSpec((S, 128), lambda h: (0, h))],
        out_specs=[pl.BlockSpec((S, 128), lambda h: (0, h))] * 3 + [pl.BlockSpec((S, 128), lambda h: (0, 0)),
                                                                       pl.BlockSpec((8, 128), lambda h: (0, 0))],
        out_shape=[jax.ShapeDtypeStruct((S, ML_W), F32)] * 3 + [jax.ShapeDtypeStruct((S, 128), F32),
                                                                 jax.ShapeDtypeStruct((8, 128), F32)],
        scratch_shapes=[pltpu.VMEM((128, 128), F32), pltpu.VMEM((8, 128), F32), pltpu.VMEM((8, 128), F32)],
        compiler_params=_params(("arbitrary",)),
    )(qk, qk, proj, ig, lf, cst, nm, dh)


MESH = pl.DeviceIdType.MESH
ANY = pl.BlockSpec(memory_space=pl.ANY)
N_DEV = 8


def _place():
    return lax.axis_index("x"), lax.axis_index("y"), lax.axis_index("c")


def _block_of(px, py, pc):
    return 4 * px + 2 * py + pc


def _copies_to_all(b_ref, o_ref, send_sems, recv_sems, local_sem):
    x, y, c = _place()
    mine = _block_of(x, y, c)
    copies = [pltpu.make_async_copy(b_ref, o_ref.at[mine], local_sem)]
    for k in range(1, N_DEV):
        peer = (x ^ (k >> 2), y ^ ((k >> 1) & 1), c ^ (k & 1))
        copies.append(pltpu.make_async_remote_copy(
            src_ref=b_ref, dst_ref=o_ref.at[mine], send_sem=send_sems.at[k - 1], recv_sem=recv_sems.at[k - 1],
            device_id=peer, device_id_type=MESH))
    return copies


def _copies_between_chips(p_ref, o_ref, send_sems, recv_sems, local_sem):
    x, y, c = _place()
    mine = 2 * x + y
    copies = [pltpu.make_async_copy(p_ref.at[mine], o_ref.at[mine], local_sem)]
    for k in range(1, 4):
        px, py = x ^ (k >> 1), y ^ (k & 1)
        copies.append(pltpu.make_async_remote_copy(
            src_ref=p_ref.at[2 * px + py], dst_ref=o_ref.at[mine], send_sem=send_sems.at[k - 1],
            recv_sem=recv_sems.at[k - 1], device_id=(px, py, c), device_id_type=MESH))
    return copies


def _around_grid(copies, first, last):
    @pl.when(first)
    def _():
        for cp in copies():
            cp.start()

    @pl.when(last)
    def _():
        for cp in copies():
            cp.wait()


def _inproj_fwd(x, pre_w, w_t, wg_t, blk=None):
    S, D = x.shape
    tm, tn = min(S, 1024), 1152
    ni, nj = S // tm, N_MAIN // tn

    def body(x_ref, pw_ref, w_ref, wg_ref, *rest):
        if blk is None:
            proj_ref, g_ref, u_ref = rest
        else:
            b_ref, proj_ref, g_ref, u_ref, o_ref, send_sems, recv_sems, local_sem = rest
            i, j = pl.program_id(0), pl.program_id(1)
            _around_grid(lambda: _copies_to_all(b_ref, o_ref, send_sems, recv_sems, local_sem),
                         (i == 0) & (j == 0), (i == ni - 1) & (j == nj - 1))

        @pl.when(pl.program_id(1) == 0)
        def _():
            xf = x_ref[...]
            r = lax.rsqrt(jnp.mean(xf * xf, axis=-1, keepdims=True) + EPS)
            u = (xf * r * pw_ref[...]).astype(BF16)
            u_ref[...] = u
            g_ref[...] = _dot_nt(u, wg_ref[...])

        proj_ref[...] = _dot_nt(u_ref[...], w_ref[...])

    in_specs = [pl.BlockSpec((tm, D), lambda i, j: (i, 0)),
                pl.BlockSpec((1, D), lambda i, j: (0, 0)),
                pl.BlockSpec((tn, D), lambda i, j: (j, 0)),
                pl.BlockSpec((128, D), lambda i, j: (0, 0))]
    out_specs = [pl.BlockSpec((tm, tn), lambda i, j: (i, j)),
                 pl.BlockSpec((tm, 128), lambda i, j: (i, 0)),
                 pl.BlockSpec((tm, D), lambda i, j: (i, 0))]
    out_shape = [jax.ShapeDtypeStruct((S, N_MAIN), F32), jax.ShapeDtypeStruct((S, 128), F32),
                 jax.ShapeDtypeStruct((S, D), BF16)]
    operands, scratch = (x, pre_w, w_t, wg_t), []
    if blk is not None:
        in_specs, out_specs, operands = in_specs + [ANY], out_specs + [ANY], operands + (blk,)
        out_shape = out_shape + [jax.ShapeDtypeStruct((N_DEV,) + blk.shape, blk.dtype)]
        scratch = [pltpu.SemaphoreType.DMA((7,)), pltpu.SemaphoreType.DMA((7,)), pltpu.SemaphoreType.DMA]
    return pl.pallas_call(
        body, name="inproj_fwd", grid=(ni, nj), in_specs=in_specs, out_specs=out_specs, out_shape=out_shape,
        scratch_shapes=scratch, compiler_params=_params(("arbitrary", "arbitrary")),
    )(*operands)


def _inproj_bwd(d_main, d_if, w_t, wg_t, x, pre_w, dx_tail, parts=None):
    S, D = x.shape
    tm, tk = min(S, 1024), 1152
    ni, nk = S // tm, N_MAIN // tk

    def body(d_ref, dg_ref, w_ref, wg_ref, x_ref, pw_ref, dt_ref, *rest):
        i, k = pl.program_id(0), pl.program_id(1)
        if parts is None:
            dx_ref, gpw_ref, acc_ref = rest
        else:
            p_ref, dx_ref, gpw_ref, o_ref, acc_ref, send_sems, recv_sems, local_sem = rest
            _around_grid(lambda: _copies_between_chips(p_ref, o_ref, send_sems, recv_sems, local_sem),
                         (i == 0) & (k == 0), (i == ni - 1) & (k == nk - 1))

        @pl.when(k == 0)
        def _():
            acc_ref[...] = _dot(dg_ref[...], wg_ref[...])

        acc_ref[...] += _dot(d_ref[...], w_ref[...])

        @pl.when(k == nk - 1)
        def _():
            xf = x_ref[...]
            r = lax.rsqrt(jnp.mean(xf * xf, axis=-1, keepdims=True) + EPS)
            xn = xf * r
            du = acc_ref[...]
            gw = du * pw_ref[...]
            dx_ref[...] = dt_ref[...] + r * (gw - xn * jnp.mean(gw * xn, axis=-1, keepdims=True))
            part = jnp.sum(du * xn, axis=0, keepdims=True)

            @pl.when(i == 0)
            def _():
                gpw_ref[...] = part

            @pl.when(i > 0)
            def _():
                gpw_ref[...] += part

    in_specs = [pl.BlockSpec((tm, tk), lambda i, k: (i, k)),
                pl.BlockSpec((tm, 128), lambda i, k: (i, 0)),
                pl.BlockSpec((tk, D), lambda i, k: (k, 0)),
                pl.BlockSpec((128, D), lambda i, k: (0, 0)),
                pl.BlockSpec((tm, D), lambda i, k: (i, 0)),
                pl.BlockSpec((1, D), lambda i, k: (0, 0)),
                pl.BlockSpec((tm, D), lambda i, k: (i, 0))]
    out_specs = [pl.BlockSpec((tm, D), lambda i, k: (i, 0)), pl.BlockSpec((1, D), lambda i, k: (0, 0))]
    out_shape = [jax.ShapeDtypeStruct((S, D), F32), jax.ShapeDtypeStruct((1, D), F32)]
    operands, scratch = (d_main, d_if, w_t, wg_t, x, pre_w, dx_tail), [pltpu.VMEM((tm, D), F32)]
    if parts is not None:
        in_specs, out_specs, operands = in_specs + [ANY], out_specs + [ANY], operands + (parts,)
        out_shape = out_shape + [jax.ShapeDtypeStruct(parts.shape, parts.dtype)]
        scratch = scratch + [pltpu.SemaphoreType.DMA((3,)), pltpu.SemaphoreType.DMA((3,)), pltpu.SemaphoreType.DMA]
    return pl.pallas_call(
        body, name="inproj_bwd", grid=(ni, nk), in_specs=in_specs, out_specs=out_specs, out_shape=out_shape,
        scratch_shapes=scratch, compiler_params=_params(("arbitrary", "arbitrary")),
    )(*operands)


def _matmul_tn(a, b, name):
    S, M = a.shape
    N = b.shape[1]
    tmm = 1152 if M % 1152 == 0 else min(M, 1024)
    tk = min(S, 1024)
    nk = S // tk

    def body(a_ref, b_ref, o_ref):
        part = _dot_tn(a_ref[...].astype(BF16), b_ref[...].astype(BF16))

        @pl.when(pl.program_id(1) == 0)
        def _():
            o_ref[...] = part

        @pl.when(pl.program_id(1) > 0)
        def _():
            o_ref[...] += part

    return pl.pallas_call(
        body, name=name, grid=(M // tmm, nk),
        in_specs=[pl.BlockSpec((tk, tmm), lambda i, k: (k, i)),
                  pl.BlockSpec((tk, N), lambda i, k: (k, 0))],
        out_specs=pl.BlockSpec((tmm, N), lambda i, k: (i, 0)),
        out_shape=jax.ShapeDtypeStruct((M, N), F32),
        compiler_params=_params(("arbitrary", "arbitrary")),
    )(a, b)


def _half_mean(v, low):
    s_lo = jnp.sum(jnp.where(low, v, 0.0), axis=1, keepdims=True)
    s_hi = jnp.sum(jnp.where(low, 0.0, v), axis=1, keepdims=True)
    return jnp.where(low, s_lo, s_hi) * (1.0 / 64.0)


def _silu_grad(z, s):
    return s * (1.0 + z * (1.0 - s))


def _tail(y_sb, h_ml, proj, x, p, target, sb_nw, ml_nw, w_out, post_w, w_gate, b_gate, w_up):
    S, D = x.shape
    tm = 256

    def body(ysb_ref, hml_ref, sbz_ref, mlo_ref, mlz_ref, x_ref, p_ref, tg_ref, sbw_ref, mlw_ref, wo_ref, pw_ref,
             wg_ref, bg_ref, wu_ref,
             dx_ref, dysb_ref, dhml_ref, dsbz_ref, dmlo_ref, dmlz_ref, mix_ref, dy_ref, h1_ref, dgp_ref, dpu_ref,
             small_ref):
        lane = lax.broadcasted_iota(jnp.int32, (tm, 128), 1)
        low = lane < 64
        sb_saved, ml_saved, mixed = [], [], []
        for s in range(4):
            sl = slice(128 * s, 128 * s + 128)
            y = ysb_ref[:, sl]
            rs = lax.rsqrt(_half_mean(y * y, low) + EPS)
            n = y * rs
            z = sbz_ref[:, sl]
            sg = jax.nn.sigmoid(z)
            w = sbw_ref[:, sl]
            mixed.append((n * w) * (z * sg))
            sb_saved.append((rs, n, z, sg, w))
        for s in range(4):
            sl = slice(128 * s, 128 * s + 128)
            og = jax.nn.sigmoid(mlo_ref[:, sl])
            hh = hml_ref[:, sl]
            t = og * hh
            rs = lax.rsqrt(jnp.mean(t * t, axis=1, keepdims=True) + EPS)
            n = t * rs
            z = mlz_ref[:, sl]
            sg = jax.nn.sigmoid(z)
            w = mlw_ref[:, sl]
            mixed.append((n * w) * (z * sg))
            ml_saved.append((rs, n, z, sg, w, og, hh))
        mix = jnp.concatenate(mixed, axis=1).astype(BF16)
        mix_ref[...] = mix
        y = _dot(mix, wo_ref[...])
        rs_y = lax.rsqrt(jnp.mean(y * y, axis=1, keepdims=True) + EPS)
        yn = y * rs_y
        pw = pw_ref[...]
        h1 = x_ref[...] + yn * pw
        h1b = h1.astype(BF16)
        h1_ref[...] = h1b
        gate = jax.nn.sigmoid(_dot(h1b, wg_ref[...]) + bg_ref[...])
        pu = _dot(p_ref[...].astype(BF16), wu_ref[...])
        err = (h1 + gate * pu) - tg_ref[...]
        loss = 0.5 * jnp.sum(jnp.sum(err * err, axis=1, keepdims=True) * (1.0 / D))
        d_out = err * (1.0 / D)
        dpu_ref[...] = (d_out * gate).astype(BF16)
        dgp = (d_out * pu) * (gate * (1.0 - gate))
        dgpb = dgp.astype(BF16)
        dgp_ref[...] = dgpb
        d_h1 = d_out + _dot_nt(dgpb, wg_ref[...])
        dx_ref[...] = d_h1
        gwy = d_h1 * pw
        d_y = rs_y * (gwy - yn * jnp.mean(gwy * yn, axis=1, keepdims=True))
        d_yb = d_y.astype(BF16)
        dy_ref[...] = d_yb
        d_mix = _dot_nt(d_yb, wo_ref[...])
        g_nw = []
        for s in range(4):
            sl = slice(128 * s, 128 * s + 128)
            rs, n, z, sg, w = sb_saved[s]
            da = d_mix[:, sl]
            act = z * sg
            dsbz_ref[:, sl] = (da * (n * w) * _silu_grad(z, sg)).astype(BF16)
            dn = da * w * act
            g_nw.append(jnp.sum(da * act * n, axis=0, keepdims=True))
            dysb_ref[:, sl] = rs * (dn - n * _half_mean(dn * n, low))
        for s in range(4):
            sl = slice(128 * s, 128 * s + 128)
            rs, n, z, sg, w, og, hh = ml_saved[s]
            da = d_mix[:, 512 + 128 * s:512 + 128 * s + 128]
            act = z * sg
            dmlz_ref[:, sl] = (da * (n * w) * _silu_grad(z, sg)).astype(BF16)
            dn = da * w * act
            g_nw.append(jnp.sum(da * act * n, axis=0, keepdims=True))
            dt = rs * (dn - n * jnp.mean(dn * n, axis=1, keepdims=True))
            dmlo_ref[:, sl] = (dt * hh * (og * (1.0 - og))).astype(BF16)
            dhml_ref[:, sl] = dt * og
        upd = jnp.concatenate([
            jnp.sum(d_h1 * yn, axis=0, keepdims=True),
            jnp.sum(dgp, axis=0, keepdims=True),
            jnp.concatenate(g_nw, axis=1),
            jnp.full((1, D), loss, F32),
            jnp.zeros((4, D), F32)], axis=0)

        @pl.when(pl.program_id(0) == 0)
        def _():
            small_ref[...] = upd

        @pl.when(pl.program_id(0) > 0)
        def _():
            small_ref[...] += upd

    def rows(width, col=0):
        return pl.BlockSpec((tm, width), lambda i: (i, col))

    def whole(a):
        return pl.BlockSpec(a.shape, lambda i: (0, 0))

    return pl.pallas_call(
        body, name="tail", grid=(S // tm,),
        in_specs=[rows(512), rows(512), rows(512, 3), rows(512, 7), rows(512, 8), rows(D), rows(256), rows(D),
                  whole(sb_nw), whole(ml_nw), whole(w_out), whole(post_w), whole(w_gate), whole(b_gate), whole(w_up)],
        out_specs=[rows(D), rows(512), rows(512), rows(512), rows(512), rows(512), rows(D), rows(D), rows(D), rows(D),
                   rows(D), pl.BlockSpec((8, D), lambda i: (0, 0))],
        out_shape=[jax.ShapeDtypeStruct((S, D), F32), jax.ShapeDtypeStruct((S, 512), F32),
                   jax.ShapeDtypeStruct((S, 512), F32)] + [jax.ShapeDtypeStruct((S, 512), BF16)] * 3
        + [jax.ShapeDtypeStruct((S, D), BF16)] * 5 + [jax.ShapeDtypeStruct((8, D), F32)],
        compiler_params=_params(("arbitrary",)),
    )(y_sb, h_ml, proj, proj, proj, x, p, target, sb_nw, ml_nw, w_out, post_w, w_gate, b_gate, w_up)


def _local_step(x, p, target, pre_w, w_t, wg_t, conv_w, conv_b, gbias, sb_nw, ml_nw, post_w, b_gate, late,
                exchange=None):
    if callable(late[1]):
        proj, proj_g, u, gathered = _inproj_fwd(x, pre_w, w_t, wg_t, late[0])
        w_out, w_gate, w_up = late[1](gathered)
    else:
        proj, proj_g, u = _inproj_fwd(x, pre_w, w_t, wg_t)
        w_out, w_gate, w_up = late
    y_sb, tot = _sb_fwd(proj)
    qk = _ml_prep(proj, conv_w, conv_b)
    ig, lf = _gate_prep(proj_g, gbias)
    h_ml, cst, nm = _ml_fwd(qk, proj, ig, lf)
    dx_tail, d_ysb, d_hml, d_sbz, d_mlo, d_mlz, mix, d_y, h1, dgp, dpu, small = _tail(
        y_sb, h_ml, proj, x, p, target, sb_nw, ml_nw, w_out, post_w, w_gate, b_gate, w_up)
    dq, dk, dv = _sb_bwd(proj, tot, d_ysb)
    dqc, dks, dmlv, dif, gif = _ml_bwd(qk, proj, ig, lf, cst, nm, d_hml)
    dmlqk, g_cw, g_cb = _ml_prep_bwd(proj, conv_w, conv_b, dqc, dks)
    d_main = jnp.concatenate([dq.astype(BF16), dk.astype(BF16), dv.astype(BF16), d_sbz, dmlqk.astype(BF16),
                              dmlv.astype(BF16), d_mlo, d_mlz], axis=1)
    d_if = dif.astype(BF16)
    grads = dict(
        w_t=_matmul_tn(d_main, u, "gw_in"), wg_t=_matmul_tn(d_if, u, "gw_in_gates"),
        w_out=_matmul_tn(mix, d_y, "gw_out"), w_gate=_matmul_tn(h1, dgp, "gw_gate"), w_up=_matmul_tn(p, dpu, "gw_up"),
        conv_w=g_cw, conv_b=g_cb, gif=gif)
    parts = exchange(grads) if exchange else None
    return _inproj_bwd(d_main, d_if, w_t, wg_t, x, pre_w, dx_tail, parts), grads, small


def _all_gather(a, b):
    def body(a_ref, b_ref, oa_ref, ob_ref, send_sems, recv_sems, local_sems):
        x, y, c = _place()
        me, sibling = (x, y, c), (x, y, 1 - c)
        chips = [(1 - x, y), (x, 1 - y), (1 - x, 1 - y)]
        pairs = ((a_ref, oa_ref), (b_ref, ob_ref))

        def copies(k, block, to, from_input=False):
            slot = _block_of(*block)
            return [pltpu.make_async_remote_copy(
                src_ref=src if from_input else out.at[slot], dst_ref=out.at[slot],
                send_sem=send_sems.at[t, k], recv_sem=recv_sems.at[t, k], device_id=to, device_id_type=MESH)
                for t, (src, out) in enumerate(pairs)]

        mine = [pltpu.make_async_copy(src, out.at[_block_of(*me)], local_sems.at[t])
                for t, (src, out) in enumerate(pairs)]
        for cp in mine:
            cp.start()
        first = copies(0, me, sibling, True)
        for j, chip in enumerate(chips):
            first += copies(1 + j, me, (*chip, c), True)
        for cp in first:
            cp.start()
        passed = []
        for j, chip in enumerate(chips):
            for cp in copies(1 + j, (*chip, c), me):
                cp.wait_recv()
            fwd = copies(4 + j, (*chip, c), sibling)
            for cp in fwd:
                cp.start()
            passed += fwd
        for cp in copies(0, sibling, me):
            cp.wait_recv()
        for j, chip in enumerate(chips):
            for cp in copies(4 + j, (*chip, 1 - c), me):
                cp.wait_recv()
        for cp in first + passed:
            cp.wait_send()
        for cp in mine:
            cp.wait()

    return pl.pallas_call(
        body, name="all_gather",
        in_specs=[ANY, ANY], out_specs=[ANY, ANY],
        out_shape=[jax.ShapeDtypeStruct((N_DEV,) + a.shape, a.dtype), jax.ShapeDtypeStruct((N_DEV,) + b.shape, b.dtype)],
        scratch_shapes=[pltpu.SemaphoreType.DMA((2, 7)), pltpu.SemaphoreType.DMA((2, 7)), pltpu.SemaphoreType.DMA((2,))],
    )(a, b)


def _exchange_pair(g):
    def body(g_ref, og_ref, send_sems, recv_sems):
        x, y, c = _place()
        sent = [pltpu.make_async_remote_copy(
            src_ref=g_ref.at[k, 1 - c], dst_ref=og_ref.at[k], send_sem=send_sems.at[k], recv_sem=recv_sems.at[k],
            device_id=(x, y, 1 - c), device_id_type=MESH) for k in range(4)]
        for cp in sent:
            cp.start()
        for cp in sent:
            cp.wait()

    return pl.pallas_call(
        body, name="exchange_pair", in_specs=[ANY], out_specs=ANY,
        out_shape=jax.ShapeDtypeStruct((4,) + g.shape[2:], g.dtype),
        scratch_shapes=[pltpu.SemaphoreType.DMA((4,)), pltpu.SemaphoreType.DMA((4,))],
    )(g)


def _pair_sum(g, r, core, tr):
    _, _, R, D = g.shape

    def body(c_ref, g_ref, r_ref, o_ref):
        o_ref[...] = (g_ref[...] + r_ref[...]).astype(BF16)

    return pl.pallas_call(
        body, name="pair_sum",
        grid_spec=pltpu.PrefetchScalarGridSpec(
            num_scalar_prefetch=1, grid=(4, R // tr),
            in_specs=[pl.BlockSpec((None, None, tr, D), lambda k, i, c: (k, c[0], i, 0)),
                      pl.BlockSpec((None, tr, D), lambda k, i, c: (k, i, 0))],
            out_specs=pl.BlockSpec((None, tr, D), lambda k, i, c: (k, i, 0))),
        out_shape=jax.ShapeDtypeStruct((4, R, D), BF16),
        compiler_params=_params(("arbitrary", "arbitrary")),
    )(core, g, r)


ADAM_LR, ADAM_B1, ADAM_B2, ADAM_EPS, ADAM_WD, ADAM_STEP = 0.001, 0.9, 0.999, 1e-08, 0.01, 10


def _adamw(w, g, m, v):
    m = ADAM_B1 * m + (1.0 - ADAM_B1) * g
    v = ADAM_B2 * v + (1.0 - ADAM_B2) * (g * g)
    m_hat = m / (1.0 - ADAM_B1 ** ADAM_STEP)
    v_hat = v / (1.0 - ADAM_B2 ** ADAM_STEP)
    return -ADAM_LR * (m_hat / (jnp.sqrt(v_hat) + ADAM_EPS) + ADAM_WD * w), m, v


def _adam(parts, w, m, v, tr, name, small=None):
    R, D = w.shape
    n = parts.shape[0]
    steps = R // tr

    def body(p_ref, w_ref, m_ref, v_ref, *rest):
        if small is None:
            g_ref, d_ref, nm_ref, nv_ref = rest
        else:
            s_ref, g_ref, d_ref, nm_ref, nv_ref, o_ref, send_sems, recv_sems, local_sem = rest
            i = pl.program_id(0)
            _around_grid(lambda: _copies_to_all(s_ref, o_ref, send_sems, recv_sems, local_sem), i == 0, i == steps - 1)
        g = p_ref[0].astype(F32)
        for k in range(1, n):
            g = g + p_ref[k].astype(F32)
        g_ref[...] = g
        d_ref[...], nm_ref[...], nv_ref[...] = _adamw(w_ref[...], g, m_ref[...], v_ref[...])

    blk = pl.BlockSpec((tr, D), lambda i: (i, 0))
    in_specs = [pl.BlockSpec((n, tr, D), lambda i: (0, i, 0)), blk, blk, blk]
    out_specs, out_shape = [blk] * 4, [jax.ShapeDtypeStruct((R, D), F32)] * 4
    operands, scratch = (parts, w, m, v), []
    if small is not None:
        in_specs, out_specs, operands = in_specs + [ANY], out_specs + [ANY], operands + (small,)
        out_shape = out_shape + [jax.ShapeDtypeStruct((N_DEV,) + small.shape, small.dtype)]
        scratch = [pltpu.SemaphoreType.DMA((7,)), pltpu.SemaphoreType.DMA((7,)), pltpu.SemaphoreType.DMA]
    return pl.pallas_call(
        body, name=name, grid=(steps,), in_specs=in_specs, out_specs=out_specs, out_shape=out_shape,
        scratch_shapes=scratch, compiler_params=_params(("arbitrary",)),
    )(*operands)


ROWS_IN = 592
ROWS_BF16 = ROWS_IN + 128 + 128 + 32
ROWS_CONV = 16
ROWS_ALL = ROWS_BF16 + ROWS_CONV
ROW_TILE = 224


def _pad_rows(a, rows):
    return jnp.pad(a, ((0, rows - a.shape[0]), (0, 0)))


def _pack_shards(w_in, w_out, w_gate, w_up, conv_w):
    return jnp.concatenate([
        _pad_rows(w_in[0].T, ROWS_IN), w_out[0], w_gate[0], w_up[0].reshape(32, D_MODEL),
        _pad_rows(jnp.pad(conv_w[0].reshape(1, 512), ((0, 0), (0, 512))), ROWS_CONV)], axis=0)


def _unpack_shards(a):
    return (a[:SHARD_IN].T[None], a[ROWS_IN:ROWS_IN + 128][None], a[ROWS_IN + 128:ROWS_IN + 256][None],
            a[ROWS_IN + 256:ROWS_BF16].reshape(1, 256, 128), a[ROWS_BF16, :512].reshape(1, 4, 128))


def _pack_small(pre_w, conv_b, i_bias, f_bias, sb_nw, ml_nw, post_w, b_gate):
    gates = jnp.pad(jnp.concatenate([i_bias, f_bias], axis=1), ((0, 0), (0, D_MODEL - 8)))
    return jnp.concatenate([post_w, b_gate, jnp.concatenate([sb_nw, ml_nw], axis=1), jnp.zeros((1, D_MODEL), F32),
                            pre_w, conv_b, gates, jnp.zeros((1, D_MODEL), F32)], axis=0)


def _unpack_small(a):
    return a[4:5], a[5:6], a[6:7, 0:4], a[6:7, 4:8], a[2:3, :512], a[2:3, 512:], a[0:1], a[1:2]


def kernel(x, p, pre_norm_w, w_in, ml_conv_w, ml_conv_b, ml_i_bias, ml_f_bias, sb_norm_w, ml_norm_w, w_out, post_norm_w, ple_w_up, ple_w_gate, ple_b_gate, loss_target, m_pre_norm_w, m_w_in, m_ml_conv_w, m_ml_conv_b, m_ml_i_bias, m_ml_f_bias, m_sb_norm_w, m_ml_norm_w, m_w_out, m_post_norm_w, m_ple_w_up, m_ple_w_gate, m_ple_b_gate, v_pre_norm_w, v_w_in, v_ml_conv_w, v_ml_conv_b, v_ml_i_bias, v_ml_f_bias, v_sb_norm_w, v_ml_norm_w, v_w_out, v_post_norm_w, v_ple_w_up, v_ple_w_gate, v_ple_b_gate):
    D = D_MODEL
    w_pk = _pack_shards(w_in, w_out, ple_w_gate, ple_w_up, ml_conv_w)
    m_pk = _pack_shards(m_w_in, m_w_out, m_ple_w_gate, m_ple_w_up, m_ml_conv_w)
    v_pk = _pack_shards(v_w_in, v_w_out, v_ple_w_gate, v_ple_w_up, v_ml_conv_w)
    w_sm = _pack_small(pre_norm_w, ml_conv_b, ml_i_bias, ml_f_bias, sb_norm_w, ml_norm_w, post_norm_w, ple_b_gate)
    m_sm = _pack_small(m_pre_norm_w, m_ml_conv_b, m_ml_i_bias, m_ml_f_bias, m_sb_norm_w, m_ml_norm_w, m_post_norm_w, m_ple_b_gate)
    v_sm = _pack_small(v_pre_norm_w, v_ml_conv_b, v_ml_i_bias, v_ml_f_bias, v_sb_norm_w, v_ml_norm_w, v_post_norm_w, v_ple_b_gate)

    w_bf = w_pk[:ROWS_BF16].astype(BF16)
    ga, gb = _all_gather(w_bf[:ROWS_IN], w_pk[ROWS_BF16:])
    w_in_t = ga[:, :SHARD_IN].reshape(N_IN, D)
    wg_t = _pad_rows(w_in_t[N_MAIN:], 128)
    conv_w_f = gb[:, 0, :512].reshape(N_DEV, 4, 128).transpose(1, 0, 2).reshape(4, D)
    gbias = jnp.pad(jnp.concatenate([ml_i_bias, ml_f_bias], axis=1), ((0, 0), (0, 120)))

    def unpack_late(gl):
        return (gl[:, :128].reshape(D, D), gl[:, 128:256].reshape(D, D),
                gl[:, 256:].reshape(N_DEV, 256, 128).transpose(1, 0, 2).reshape(256, D))

    def exchange(g):
        g_in = jnp.concatenate([g["w_t"], g["wg_t"][:8]], axis=0).reshape(N_DEV, SHARD_IN, D)
        g_blocks = jnp.concatenate([
            jnp.pad(g_in, ((0, 0), (0, ROWS_IN - SHARD_IN), (0, 0))),
            g["w_out"].reshape(N_DEV, 128, D), g["w_gate"].reshape(N_DEV, 128, D),
            g["w_up"].reshape(256, N_DEV, 128).transpose(1, 0, 2).reshape(N_DEV, 32, D),
            jnp.pad(g["conv_w"].reshape(4, N_DEV, 128).transpose(1, 0, 2).reshape(N_DEV, 1, 512),
                    ((0, 0), (0, ROWS_CONV - 1), (0, 512))),
        ], axis=1).reshape(4, 2, ROWS_ALL, D)
        core = lax.axis_index("c").astype(jnp.int32).reshape(1)
        return _pair_sum(g_blocks, _exchange_pair(g_blocks), core, ROW_TILE)

    (dx, g_pre, parts), g, small = _local_step(
        x[0], p[0, 0], loss_target[0], pre_norm_w, w_in_t, wg_t, conv_w_f, ml_conv_b, gbias, sb_norm_w, ml_norm_w,
        post_norm_w, ple_b_gate, (w_bf[ROWS_IN:], unpack_late), exchange)

    g_small = jnp.concatenate([small[0:4], g_pre, g["conv_b"], jnp.pad(g["gif"][0:1], ((0, 0), (0, D - 128))),
                               jnp.zeros((1, D), F32)], axis=0)
    grad_pk, delta_pk, nm_pk, nv_pk, parts_sm = _adam(parts, w_pk, m_pk, v_pk, ROW_TILE, "adam", g_small)
    grad_sm, delta_sm, nm_sm, nv_sm = _adam(parts_sm, w_sm, m_sm, v_sm, 8, "adam_small")
    loss = grad_sm[3, 0]

    def ordered(pk, sm):
        win, wout, wgate, wup, convw = _unpack_shards(pk)
        pre_w, conv_b, i_b, f_b, sb_nw, ml_nw, post_w, b_gate = _unpack_small(sm)
        return [pre_w, win, convw, conv_b, i_b, f_b, sb_nw, ml_nw, wout, post_w, wup, wgate, b_gate]

    return (loss, dx[None], *ordered(grad_pk, grad_sm), *ordered(delta_pk, delta_sm), *ordered(nm_pk, nm_sm),
            *ordered(nv_pk, nv_sm))
```

```python
import functools

import jax
import jax.numpy as jnp
from jax import lax
from jax.experimental import pallas as pl
from jax.experimental.pallas import tpu as pltpu

F32 = jnp.float32
BF16 = jnp.bfloat16
EPS = 1e-6
D_MODEL = 1024
SB_W = 512
ML_W = 512
N_MAIN = 4608
N_IN = 4616
SHARD_IN = 577
SHARD_IN_PAD = 584
TQ = 1024
TK = 256
ND = TQ // TK
LCH = 128
VMEM_LIMIT = 56 * 1024 * 1024


def _dot(a, b):
    return jnp.dot(a, b, preferred_element_type=F32)


def _dot_nt(a, b):
    return lax.dot_general(a, b, (((1,), (1,)), ((), ())), preferred_element_type=F32)


def _dot_tn(a, b):
    return lax.dot_general(a, b, (((0,), (0,)), ((), ())), preferred_element_type=F32)


def _split2(x):
    hi = x.astype(BF16)
    lo = (x - hi.astype(F32)).astype(BF16)
    return hi, lo


def _split3(x):
    hi = x.astype(BF16)
    r = x - hi.astype(F32)
    mid = r.astype(BF16)
    lo = (r - mid.astype(F32)).astype(BF16)
    return hi, mid, lo


def _params(sem):
    return pltpu.CompilerParams(dimension_semantics=sem, vmem_limit_bytes=VMEM_LIMIT)


def _log_sigmoid_parts(z):
    e = jnp.exp(-jnp.abs(z))
    return jnp.minimum(z, 0.0) - jnp.log(1.0 + e)


def _sb_fwd(proj):
    S = proj.shape[0]
    nq = S // TQ

    def body(q_ref, k_ref, v_ref, y_ref, t_ref, acc_ref, car_ref):
        i = pl.program_id(1)
        low = lax.broadcasted_iota(jnp.int32, (TQ, 128), 1) < 64
        row = lax.broadcasted_iota(jnp.int32, (TK, TK), 0)
        col = lax.broadcasted_iota(jnp.int32, (TK, TK), 1)
        uo = (row > col).astype(BF16)
        uo = jnp.concatenate([uo, uo], axis=0)
        q = q_ref[...] * 0.125
        qh = (jnp.where(low, q, 0.0).astype(BF16), jnp.where(low, 0.0, q).astype(BF16))
        acc_ref[...] = jnp.zeros_like(acc_ref)
        car_ref[...] = jnp.zeros_like(car_ref)

        def block(j, r0):
            diag = r0 is not None
            r0 = r0 or 0
            if diag:
                strict = (lax.broadcasted_iota(jnp.int32, (TQ - r0, TK), 1)
                          < lax.broadcasted_iota(jnp.int32, (TQ - r0, TK), 0))
            rows = pl.ds(pl.multiple_of(j * TK, TK), TK)
            kb = k_ref[rows, :].astype(BF16)
            vb = v_ref[rows, :].astype(BF16)
            for h in range(2):
                z = _dot_nt(qh[h][r0:], kb)
                lb = _log_sigmoid_parts(z)
                lk = lb - z
                if diag:
                    lk = jnp.where(strict, lk, 0.0)
                hi, lo = _split2(lk)
                rr = _dot(jnp.concatenate([hi, lo], axis=1), uo)
                car = car_ref[h, r0:, :]
                a = jnp.exp(lb + jnp.concatenate([car, car], axis=1) + rr)
                if diag:
                    a = jnp.where(strict, a, 0.0)
                acc_ref[h, r0:, :] += _dot(a.astype(BF16), vb)
                car_ref[h, r0:, :] = car + jnp.broadcast_to(rr[:, 0:1] + lk[:, 0:1], car.shape)

        for d in reversed(range(ND)):
            block(ND * i + d, TK * d)

        def loop(n, c):
            block(ND * i - 1 - n, None)
            return c

        lax.fori_loop(0, ND * i, loop, 0)
        y_ref[...] = jnp.where(low, acc_ref[0], acc_ref[1])
        t_ref[...] = jnp.where(low, car_ref[0], car_ref[1])

    return pl.pallas_call(
        body, name="sb_fwd", grid=(4, nq),
        in_specs=[pl.BlockSpec((TQ, 128), lambda p, i: (i, p)),
                  pl.BlockSpec((S, 128), lambda p, i: (0, 4 + p)),
                  pl.BlockSpec((S, 128), lambda p, i: (0, 8 + p))],
        out_specs=[pl.BlockSpec((TQ, 128), lambda p, i: (i, p)),
                   pl.BlockSpec((TQ, 128), lambda p, i: (i, p))],
        out_shape=[jax.ShapeDtypeStruct((S, SB_W), F32), jax.ShapeDtypeStruct((S, SB_W), F32)],
        scratch_shapes=[pltpu.VMEM((2, TQ, 128), F32), pltpu.VMEM((2, TQ, 128), F32)],
        compiler_params=_params(("arbitrary", "arbitrary")),
    )(proj, proj, proj)


def _sb_bwd(proj, tot, dy):
    S = proj.shape[0]
    nq = S // TQ

    def body(q_ref, k_ref, v_ref, t_ref, dy_ref, dq_ref, dk_ref, dv_ref, dqa_ref, cp_ref, cg_ref):
        i = pl.program_id(1)
        low = lax.broadcasted_iota(jnp.int32, (TQ, 128), 1) < 64
        row = lax.broadcasted_iota(jnp.int32, (TK, TK), 0)
        col = lax.broadcasted_iota(jnp.int32, (TK, TK), 1)
        u_inc = (row <= col).astype(BF16)
        u_inc = jnp.concatenate([u_inc, u_inc], axis=0)
        u_exc = (row < col).astype(BF16)
        q = q_ref[...] * 0.125
        qh = (jnp.where(low, q, 0.0).astype(BF16), jnp.where(low, 0.0, q).astype(BF16))
        dy_ = dy_ref[...]
        dyh = (jnp.where(low, dy_, 0.0).astype(BF16), jnp.where(low, 0.0, dy_).astype(BF16))
        t_ = t_ref[...]
        t_sw = pltpu.roll(t_, 64, 1)
        th = (jnp.where(low, t_, t_sw), jnp.where(low, t_sw, t_))
        dqa_ref[...] = jnp.zeros_like(dqa_ref)
        cp_ref[...] = jnp.zeros_like(cp_ref)
        cg_ref[...] = jnp.zeros_like(cg_ref)

        @pl.when(i == 0)
        def _():
            dk_ref[...] = jnp.zeros_like(dk_ref)
            dv_ref[...] = jnp.zeros_like(dv_ref)

        def block(j, r0):
            diag = r0 is not None
            r0 = r0 or 0
            if diag:
                strict = (lax.broadcasted_iota(jnp.int32, (TQ - r0, TK), 1)
                          < lax.broadcasted_iota(jnp.int32, (TQ - r0, TK), 0))
            rows = pl.ds(pl.multiple_of(j * TK, TK), TK)
            kb = k_ref[rows, :].astype(BF16)
            vb = v_ref[rows, :].astype(BF16)
            dk_acc = jnp.zeros((TK, 128), F32)
            dv_acc = jnp.zeros((TK, 128), F32)
            for h in range(2):
                qr, dyr = qh[h][r0:], dyh[h][r0:]
                z = _dot_nt(qr, kb)
                lb = _log_sigmoid_parts(z)
                lk = lb - z
                if diag:
                    lk = jnp.where(strict, lk, 0.0)
                hi, lo = _split2(lk)
                pp = _dot(jnp.concatenate([hi, lo], axis=1), u_inc)
                cp, cg = cp_ref[h, r0:, :], cg_ref[h, r0:, :]
                rest = th[h][r0:] - cp
                a = jnp.exp(lb + (jnp.concatenate([rest, rest], axis=1) - pp))
                if diag:
                    a = jnp.where(strict, a, 0.0)
                g = _dot_nt(dyr, vb) * a
                gg = _dot(g.astype(BF16), u_exc)
                beta = jnp.exp(lb)
                dz = g - beta * (g + (jnp.concatenate([cg, cg], axis=1) + gg))
                if diag:
                    dz = jnp.where(strict, dz, 0.0)
                dzb = dz.astype(BF16)
                dqa_ref[h, r0:, :] += _dot(dzb, kb)
                dk_acc += _dot_tn(dzb, qr)
                dv_acc += _dot_tn(a.astype(BF16), dyr)
                cp_ref[h, r0:, :] = cp + jnp.broadcast_to(pp[:, TK - 1:TK], cp.shape)
                cg_ref[h, r0:, :] = cg + jnp.broadcast_to(gg[:, TK - 1:TK] + g[:, TK - 1:TK], cg.shape)
            dk_ref[rows, :] += dk_acc
            dv_ref[rows, :] += dv_acc

        def loop(j, c):
            block(j, None)
            return c

        lax.fori_loop(0, ND * i, loop, 0)
        for d in range(ND):
            block(ND * i + d, TK * d)
        dq_ref[...] = jnp.where(low, dqa_ref[0], dqa_ref[1]) * 0.125

    return pl.pallas_call(
        body, name="sb_bwd", grid=(4, nq),
        in_specs=[pl.BlockSpec((TQ, 128), lambda p, i: (i, p)),
                  pl.BlockSpec((S, 128), lambda p, i: (0, 4 + p)),
                  pl.BlockSpec((S, 128), lambda p, i: (0, 8 + p)),
                  pl.BlockSpec((TQ, 128), lambda p, i: (i, p)),
                  pl.BlockSpec((TQ, 128), lambda p, i: (i, p))],
        out_specs=[pl.BlockSpec((TQ, 128), lambda p, i: (i, p)),
                   pl.BlockSpec((S, 128), lambda p, i: (0, p)),
                   pl.BlockSpec((S, 128), lambda p, i: (0, p))],
        out_shape=[jax.ShapeDtypeStruct((S, SB_W), F32)] * 3,
        scratch_shapes=[pltpu.VMEM((2, TQ, 128), F32)] * 3,
        compiler_params=_params(("arbitrary", "arbitrary")),
    )(proj, proj, proj, tot, dy)


ML_SCALE = 128 ** -0.5
RC = 256


def _conv_taps(cur, prev8, w):
    n = cur.shape[0]
    win = jnp.concatenate([prev8, cur], axis=0)
    out = w[3:4, :] * cur
    for j in range(3):
        out = out + w[j:j + 1, :] * pltpu.roll(win, 3 - j, 0)[8:8 + n]
    return out


def _ml_prep(proj, conv_w, conv_b):
    S = proj.shape[0]

    def body(x_ref, w_ref, b_ref, o_ref):
        c = pl.program_id(0)
        scale = jnp.where(c < 4, 1.0, ML_SCALE).astype(F32)
        w = w_ref[...]
        b = b_ref[...]

        @pl.when(c < 8)
        def _():
            for n in range(S // RC):
                cur = x_ref[n * RC:(n + 1) * RC, :]
                prev8 = x_ref[n * RC - 8:n * RC, :] if n else jnp.zeros((8, 128), F32)
                pre = b + _conv_taps(cur, prev8, w)
                o_ref[n * RC:(n + 1) * RC, :] = (pre * jax.nn.sigmoid(pre) * scale).astype(BF16)

        @pl.when(c >= 8)
        def _():
            o_ref[...] = x_ref[...].astype(BF16)

    return pl.pallas_call(
        body, name="ml_prep", grid=(12,),
        in_specs=[pl.BlockSpec((S, 128), lambda c: (0, 16 + c)),
                  pl.BlockSpec((4, 128), lambda c: (0, jnp.minimum(c, 7))),
                  pl.BlockSpec((1, 128), lambda c: (0, jnp.minimum(c, 7)))],
        out_specs=pl.BlockSpec((S, 128), lambda c: (0, c)),
        out_shape=jax.ShapeDtypeStruct((S, 1536), BF16),
        compiler_params=_params(("arbitrary",)),
    )(proj, conv_w, conv_b)


def _ml_prep_bwd(proj, conv_w, conv_b, dq, dk):
    S = proj.shape[0]

    def body(x_ref, w_ref, b_ref, dq_ref, dk_ref, dx_ref, gw_ref, gb_ref, dp_ref):
        c = pl.program_id(0)
        w = w_ref[...]
        b = b_ref[...]
        gw = [jnp.zeros((1, 128), F32) for _ in range(4)]
        gb = jnp.zeros((1, 128), F32)
        for n in range(S // RC):
            rows = slice(n * RC, (n + 1) * RC)
            cur = x_ref[rows, :]
            prev8 = x_ref[n * RC - 8:n * RC, :] if n else jnp.zeros((8, 128), F32)
            pre = b + _conv_taps(cur, prev8, w)
            s = jax.nn.sigmoid(pre)
            dpost = jnp.where(c < 4, dq_ref[rows, :].astype(F32), dk_ref[rows, :].astype(F32) * ML_SCALE)
            dpre = dpost * (s * (1.0 + pre * (1.0 - s)))
            dp_ref[rows, :] = dpre
            win = jnp.concatenate([prev8, cur], axis=0)
            gb = gb + jnp.sum(dpre, axis=0, keepdims=True)
            gw[3] = gw[3] + jnp.sum(dpre * cur, axis=0, keepdims=True)
            for j in range(3):
                gw[j] = gw[j] + jnp.sum(dpre * pltpu.roll(win, 3 - j, 0)[8:8 + RC], axis=0, keepdims=True)
        dp_ref[S:S + 8, :] = jnp.zeros((8, 128), F32)
        for n in range(S // RC):
            win = dp_ref[n * RC:(n + 1) * RC + 8, :]
            dx = w[3:4, :] * win[:RC]
            for j in range(3):
                dx = dx + w[j:j + 1, :] * pltpu.roll(win, RC + 8 - (3 - j), 0)[:RC]
            dx_ref[n * RC:(n + 1) * RC, :] = dx
        gw_ref[...] = jnp.concatenate(gw, axis=0)
        gb_ref[...] = gb

    return pl.pallas_call(
        body, name="ml_prep_bwd", grid=(8,),
        in_specs=[pl.BlockSpec((S, 128), lambda c: (0, 16 + c)),
                  pl.BlockSpec((4, 128), lambda c: (0, c)),
                  pl.BlockSpec((1, 128), lambda c: (0, c)),
                  pl.BlockSpec((S, 128), lambda c: (0, jnp.minimum(c, 3))),
                  pl.BlockSpec((S, 128), lambda c: (0, jnp.maximum(c - 4, 0)))],
        out_specs=[pl.BlockSpec((S, 128), lambda c: (0, c)),
                   pl.BlockSpec((4, 128), lambda c: (0, c)),
                   pl.BlockSpec((1, 128), lambda c: (0, c))],
        out_shape=[jax.ShapeDtypeStruct((S, 1024), F32), jax.ShapeDtypeStruct((4, 1024), F32),
                   jax.ShapeDtypeStruct((1, 1024), F32)],
        scratch_shapes=[pltpu.VMEM((S + 8, 128), F32)],
        compiler_params=_params(("arbitrary",)),
    )(proj, conv_w, conv_b, dq, dk)


NH = 4


def _heads(x):
    return jnp.stack([x[:, 128 * h:128 * (h + 1)] for h in range(NH)])


def _unheads(x):
    return jnp.concatenate([x[h] for h in range(NH)], axis=1)


def _per_head(f, *xs):
    return jnp.stack([f(*[x[h] for x in xs]) for h in range(NH)])


def _ml_gates(g):
    lg = _log_sigmoid_parts(g)
    ig = jnp.stack([jnp.broadcast_to(g[:, h:h + 1], (LCH, 128)) for h in range(NH)])
    lf = jnp.stack([jnp.broadcast_to(lg[:, 4 + h:5 + h], (LCH, 128)) for h in range(NH)])
    return ig, lf


def _tri_sum(tri, x, dot):
    return _per_head(lambda a, b, c: dot(tri, a) + dot(tri, b) + dot(tri, c), *_split3(x))


def _ml_chunk_fwd(q, k, v, ig, lf, ct, n_st, m_st, tri, causal):
    vf = v.astype(F32)
    b = _tri_sum(tri, lf, _dot)
    b_last = b[:, LCH - 1:LCH, :]
    g = b_last - b + ig
    m_loc = jnp.max(g, axis=1, keepdims=True)
    w = jnp.exp(g - m_loc)
    vwf = vf * w
    vw = vwf.astype(BF16)
    ct_loc = _per_head(_dot_tn, k, vw)
    kf = k.astype(F32)
    n_loc = jnp.sum(w * kf, axis=1, keepdims=True)
    r = _per_head(jnp.transpose, ig - b)
    d_log = jnp.where(causal, b + r, -jnp.inf)
    m_t = jnp.maximum(b + m_st, jnp.max(d_log, axis=2, keepdims=True))
    w_in = jnp.exp(d_log - m_t)
    qk = _per_head(_dot_nt, q, k)
    scores = qk * w_in
    cs = jnp.exp(b + m_st - m_t)
    ctb = ct.astype(BF16)
    qc = _per_head(_dot, q, ctb)
    qf = q.astype(F32)
    qn = jnp.sum(qf * n_st, axis=2, keepdims=True)
    num = _per_head(_dot, scores.astype(BF16), v) + cs * qc
    den = jnp.sum(scores, axis=2, keepdims=True) + cs * qn
    em = jnp.exp(-m_t)
    dd = jnp.maximum(jnp.abs(den), em)
    h = num / dd
    m_new = jnp.maximum(b_last + m_st, m_loc)
    a = jnp.exp(b_last + m_st - m_new)
    gg = jnp.exp(m_loc - m_new)
    return dict(vwf=vwf, vw=vw, kf=kf, qf=qf, ct_loc=ct_loc, n_loc=n_loc, w=w, w_in=w_in, qk=qk, scores=scores,
                cs=cs, ctb=ctb, qc=qc, qn=qn, den=den, em=em, dd=dd, h=h, m_new=m_new, a=a, gg=gg)


def _ml_consts():
    row = lax.broadcasted_iota(jnp.int32, (LCH, LCH), 0)
    col = lax.broadcasted_iota(jnp.int32, (LCH, LCH), 1)
    return row, (col <= row), (col <= row).astype(BF16)


def _ml_rows(qkv_ref, g_ref, gb_ref, c):
    rows = pl.ds(pl.multiple_of(c * LCH, LCH), LCH)
    q, k, v = (_heads(qkv_ref[rows, 512 * t:512 * (t + 1)]) for t in range(3))
    ig, lf = _ml_gates(g_ref[rows, :] + gb_ref[...])
    return rows, q, k, v, ig, lf


VMEM_SPEC = pl.BlockSpec(memory_space=pltpu.VMEM)


def _ml_fwd(qkv, proj_g, gbias):
    S = qkv.shape[0]
    nc = S // LCH

    def body(qkv_ref, g_ref, gb_ref, h_ref, cst_ref, nm_ref, ct_ref, n_ref, m_ref):
        _, causal, tri = _ml_consts()
        ct_ref[...] = jnp.zeros_like(ct_ref)
        n_ref[...] = jnp.zeros_like(n_ref)
        m_ref[...] = jnp.zeros_like(m_ref)

        def chunk(c, carry):
            rows, q, k, v, ig, lf = _ml_rows(qkv_ref, g_ref, gb_ref, c)
            ct, n_st, m_st = ct_ref[...], n_ref[:, 0:1, :], m_ref[:, 0:1, :]
            cst_ref[c] = ct
            nm_ref[c, :, 0:8, :] = n_ref[...]
            nm_ref[c, :, 8:16, :] = m_ref[...]
            f = _ml_chunk_fwd(q, k, v, ig, lf, ct, n_st, m_st, tri, causal)
            h_ref[rows, :] = _unheads(f["h"])
            ct_ref[...] = f["a"] * ct + f["gg"] * f["ct_loc"]
            n_ref[...] = jnp.broadcast_to(f["a"] * n_st + f["gg"] * f["n_loc"], (NH, 8, 128))
            m_ref[...] = jnp.broadcast_to(f["m_new"], (NH, 8, 128))
            return carry

        lax.fori_loop(0, nc, chunk, 0)

    return pl.pallas_call(
        body, name="ml_fwd", in_specs=[VMEM_SPEC] * 3, out_specs=[VMEM_SPEC] * 3,
        out_shape=[jax.ShapeDtypeStruct((S, ML_W), F32), jax.ShapeDtypeStruct((nc, NH, 128, 128), F32),
                   jax.ShapeDtypeStruct((nc, NH, 16, 128), F32)],
        scratch_shapes=[pltpu.VMEM((NH, 128, 128), F32), pltpu.VMEM((NH, 8, 128), F32), pltpu.VMEM((NH, 8, 128), F32)],
        compiler_params=pltpu.CompilerParams(vmem_limit_bytes=VMEM_LIMIT),
    )(qkv, proj_g, gbias)


def _ml_bwd(qkv, proj_g, gbias, cst, nm, dh):
    S = qkv.shape[0]
    nc = S // LCH

    def body(qkv_ref, g_ref, gb_ref, cst_ref, nm_ref, dh_ref, dq_ref, dk_ref, dv_ref, dif_ref, gsum_ref,
             dct_ref, dn_ref, gi_ref):
        row, causal, tri = _ml_consts()
        lane = lax.broadcasted_iota(jnp.int32, (LCH, 128), 1)
        last_row = row == LCH - 1
        dct_ref[...] = jnp.zeros_like(dct_ref)
        dn_ref[...] = jnp.zeros_like(dn_ref)
        gi_ref[...] = jnp.zeros_like(gi_ref)

        def chunk(t, carry):
            c = nc - 1 - t
            rows, q, k, v, ig, lf = _ml_rows(qkv_ref, g_ref, gb_ref, c)
            ct, n_st, m_st = cst_ref[c], nm_ref[c, :, 0:1, :], nm_ref[c, :, 8:9, :]
            f = _ml_chunk_fwd(q, k, v, ig, lf, ct, n_st, m_st, tri, causal)
            dh_ = _heads(dh_ref[rows, :])
            dct_new, dn_new = dct_ref[...], dn_ref[:, 0:1, :]
            e_num = dh_ / f["dd"]
            hdh = jnp.sum(f["h"] * dh_, axis=2, keepdims=True)
            free = jnp.abs(f["den"]) > f["em"]
            e_den = jnp.where(free, -hdh / f["dd"] * jnp.sign(f["den"]), 0.0)
            e_num_b = e_num.astype(BF16)
            ds_ = _per_head(_dot_nt, e_num_b, v) + e_den
            dqk = ds_ * f["w_in"]
            gam = dqk * f["qk"]
            dqk_b = dqk.astype(BF16)
            cse = f["cs"] * e_den
            dq = _per_head(_dot, dqk_b, k) + f["cs"] * _per_head(_dot_nt, e_num_b, f["ctb"]) + cse * n_st
            dk = _per_head(_dot_tn, dqk_b, q)
            dv = _per_head(_dot_tn, f["scores"].astype(BF16), e_num_b)
            dcl = (f["gg"] * dct_new).astype(BF16)
            dnl = f["gg"] * dn_new
            kd = _per_head(_dot, k, dcl)
            dv = dv + f["w"] * kd
            dk = dk + _per_head(_dot_nt, f["vw"], dcl) + f["w"] * dnl
            gam_s = (jnp.sum(kd * f["vwf"], axis=2, keepdims=True)
                     + f["w"] * jnp.sum(f["kf"] * dnl, axis=2, keepdims=True))
            col_g = jnp.sum(_per_head(jnp.transpose, gam), axis=2, keepdims=True) + gam_s
            db = (jnp.sum(gam, axis=2, keepdims=True) + jnp.sum(e_num * (f["cs"] * f["qc"]), axis=2, keepdims=True)
                  + cse * f["qn"] - col_g)
            state = jnp.sum(jnp.sum(dct_new * ct, axis=2, keepdims=True), axis=1, keepdims=True)
            state = state + jnp.sum(dn_new * n_st, axis=2, keepdims=True)
            db_last = jnp.sum(gam_s[:, :, 0:1], axis=1, keepdims=True) + f["a"][:, :, 0:1] * state
            db = jnp.where(last_row, db + db_last, db)
            dlf = _tri_sum(tri, db, _dot_tn)
            df = dlf * (1.0 - jnp.exp(lf))
            dq_ref[rows, :] = _unheads(dq).astype(BF16)
            dk_ref[rows, :] = _unheads(dk).astype(BF16)
            dv_ref[rows, :] = _unheads(dv).astype(BF16)
            dif = jnp.zeros((LCH, 128), F32)
            for h in range(NH):
                dif = dif + jnp.where(lane == h, col_g[h], 0.0) + jnp.where(lane == h + 4, df[h], 0.0)
            dif_ref[rows, :] = dif
            clamped = jnp.where(free, 0.0, hdh)
            gi_ref[...] += jnp.broadcast_to(jnp.sum(clamped, axis=1, keepdims=True), (NH, 8, 128))
            dct_ref[...] = f["a"] * dct_new + _per_head(_dot_tn, q, (f["cs"] * e_num).astype(BF16))
            dn_ref[...] = jnp.broadcast_to(f["a"] * dn_new + jnp.sum(cse * f["qf"], axis=1, keepdims=True), (NH, 8, 128))
            return carry

        lax.fori_loop(0, nc, chunk, 0)
        lane8 = lax.broadcasted_iota(jnp.int32, (8, 128), 1)
        gsum = jnp.where(lane8 >= 4, jnp.sum(dif_ref[...], axis=0, keepdims=True), 0.0)
        for h in range(NH):
            gsum = gsum + jnp.where(lane8 == h, gi_ref[h], 0.0)
        gsum_ref[...] = gsum

    return pl.pallas_call(
        body, name="ml_bwd", in_specs=[VMEM_SPEC] * 6, out_specs=[VMEM_SPEC] * 5,
        out_shape=[jax.ShapeDtypeStruct((S, ML_W), BF16)] * 3 + [jax.ShapeDtypeStruct((S, 128), F32),
                                                                  jax.ShapeDtypeStruct((8, 128), F32)],
        scratch_shapes=[pltpu.VMEM((NH, 128, 128), F32), pltpu.VMEM((NH, 8, 128), F32), pltpu.VMEM((NH, 8, 128), F32)],
        compiler_params=pltpu.CompilerParams(vmem_limit_bytes=VMEM_LIMIT),
    )(qkv, proj_g, gbias, cst, nm, dh)


MESH = pl.DeviceIdType.MESH
ANY = pl.BlockSpec(memory_space=pl.ANY)
N_DEV = 8


def _place():
    return lax.axis_index("x"), lax.axis_index("y"), lax.axis_index("c")


def _block_of(px, py, pc):
    return 4 * px + 2 * py + pc


def _copies_to_all(b_ref, o_ref, send_sems, recv_sems, local_sem):
    x, y, c = _place()
    mine = _block_of(x, y, c)
    copies = [pltpu.make_async_copy(b_ref, o_ref.at[mine], local_sem)]
    for k in range(1, N_DEV):
        peer = (x ^ (k >> 2), y ^ ((k >> 1) & 1), c ^ (k & 1))
        copies.append(pltpu.make_async_remote_copy(
            src_ref=b_ref, dst_ref=o_ref.at[mine], send_sem=send_sems.at[k - 1], recv_sem=recv_sems.at[k - 1],
            device_id=peer, device_id_type=MESH))
    return copies


def _copies_between_chips(p_ref, o_ref, send_sems, recv_sems, local_sem):
    x, y, c = _place()
    mine = 2 * x + y
    copies = [pltpu.make_async_copy(p_ref.at[mine], o_ref.at[mine], local_sem)]
    for k in range(1, 4):
        px, py = x ^ (k >> 1), y ^ (k & 1)
        copies.append(pltpu.make_async_remote_copy(
            src_ref=p_ref.at[2 * px + py], dst_ref=o_ref.at[mine], send_sem=send_sems.at[k - 1],
            recv_sem=recv_sems.at[k - 1], device_id=(px, py, c), device_id_type=MESH))
    return copies


def _around_grid(copies, first, last):
    @pl.when(first)
    def _():
        for cp in copies():
            cp.start()

    @pl.when(last)
    def _():
        for cp in copies():
            cp.wait()


def _inproj_fwd(x, pre_w, w_t, wg_t, blk=None):
    S, D = x.shape
    tm, tn = min(S, 1024), 1152
    ni, nj = S // tm, N_MAIN // tn

    def body(x_ref, pw_ref, w_ref, wg_ref, *rest):
        if blk is None:
            proj_ref, g_ref, u_ref = rest
        else:
            b_ref, proj_ref, g_ref, u_ref, o_ref, send_sems, recv_sems, local_sem = rest
            i, j = pl.program_id(0), pl.program_id(1)
            _around_grid(lambda: _copies_to_all(b_ref, o_ref, send_sems, recv_sems, local_sem),
                         (i == 0) & (j == 0), (i == ni - 1) & (j == nj - 1))

        @pl.when(pl.program_id(1) == 0)
        def _():
            xf = x_ref[...]
            r = lax.rsqrt(jnp.mean(xf * xf, axis=-1, keepdims=True) + EPS)
            u = (xf * r * pw_ref[...]).astype(BF16)
            u_ref[...] = u
            g_ref[...] = _dot_nt(u, wg_ref[...])

        proj_ref[...] = _dot_nt(u_ref[...], w_ref[...])

    in_specs = [pl.BlockSpec((tm, D), lambda i, j: (i, 0)),
                pl.BlockSpec((1, D), lambda i, j: (0, 0)),
                pl.BlockSpec((tn, D), lambda i, j: (j, 0)),
                pl.BlockSpec((128, D), lambda i, j: (0, 0))]
    out_specs = [pl.BlockSpec((tm, tn), lambda i, j: (i, j)),
                 pl.BlockSpec((tm, 128), lambda i, j: (i, 0)),
                 pl.BlockSpec((tm, D), lambda i, j: (i, 0))]
    out_shape = [jax.ShapeDtypeStruct((S, N_MAIN), F32), jax.ShapeDtypeStruct((S, 128), F32),
                 jax.ShapeDtypeStruct((S, D), BF16)]
    operands, scratch = (x, pre_w, w_t, wg_t), []
    if blk is not None:
        in_specs, out_specs, operands = in_specs + [ANY], out_specs + [ANY], operands + (blk,)
        out_shape = out_shape + [jax.ShapeDtypeStruct((N_DEV,) + blk.shape, blk.dtype)]
        scratch = [pltpu.SemaphoreType.DMA((7,)), pltpu.SemaphoreType.DMA((7,)), pltpu.SemaphoreType.DMA]
    return pl.pallas_call(
        body, name="inproj_fwd", grid=(ni, nj), in_specs=in_specs, out_specs=out_specs, out_shape=out_shape,
        scratch_shapes=scratch, compiler_params=_params(("arbitrary", "arbitrary")),
    )(*operands)


def _inproj_bwd(d_main, d_if, w_t, wg_t, x, pre_w, dx_tail, parts=None):
    S, D = x.shape
    tm, tk = min(S, 1024), 1152
    ni, nk = S // tm, N_MAIN // tk

    def body(d_ref, dg_ref, w_ref, wg_ref, x_ref, pw_ref, dt_ref, *rest):
        i, k = pl.program_id(0), pl.program_id(1)
        if parts is None:
            dx_ref, gpw_ref, acc_ref = rest
        else:
            p_ref, dx_ref, gpw_ref, o_ref, acc_ref, send_sems, recv_sems, local_sem = rest
            _around_grid(lambda: _copies_between_chips(p_ref, o_ref, send_sems, recv_sems, local_sem),
                         (i == 0) & (k == 0), (i == ni - 1) & (k == nk - 1))

        @pl.when(k == 0)
        def _():
            acc_ref[...] = _dot(dg_ref[...], wg_ref[...])

        acc_ref[...] += _dot(d_ref[...], w_ref[...])

        @pl.when(k == nk - 1)
        def _():
            xf = x_ref[...]
            r = lax.rsqrt(jnp.mean(xf * xf, axis=-1, keepdims=True) + EPS)
            xn = xf * r
            du = acc_ref[...]
            gw = du * pw_ref[...]
            dx_ref[...] = dt_ref[...] + r * (gw - xn * jnp.mean(gw * xn, axis=-1, keepdims=True))
            part = jnp.sum(du * xn, axis=0, keepdims=True)

            @pl.when(i == 0)
            def _():
                gpw_ref[...] = part

            @pl.when(i > 0)
            def _():
                gpw_ref[...] += part

    in_specs = [pl.BlockSpec((tm, tk), lambda i, k: (i, k)),
                pl.BlockSpec((tm, 128), lambda i, k: (i, 0)),
                pl.BlockSpec((tk, D), lambda i, k: (k, 0)),
                pl.BlockSpec((128, D), lambda i, k: (0, 0)),
                pl.BlockSpec((tm, D), lambda i, k: (i, 0)),
                pl.BlockSpec((1, D), lambda i, k: (0, 0)),
                pl.BlockSpec((tm, D), lambda i, k: (i, 0))]
    out_specs = [pl.BlockSpec((tm, D), lambda i, k: (i, 0)), pl.BlockSpec((1, D), lambda i, k: (0, 0))]
    out_shape = [jax.ShapeDtypeStruct((S, D), F32), jax.ShapeDtypeStruct((1, D), F32)]
    operands, scratch = (d_main, d_if, w_t, wg_t, x, pre_w, dx_tail), [pltpu.VMEM((tm, D), F32)]
    if parts is not None:
        in_specs, out_specs, operands = in_specs + [ANY], out_specs + [ANY], operands + (parts,)
        out_shape = out_shape + [jax.ShapeDtypeStruct(parts.shape, parts.dtype)]
        scratch = scratch + [pltpu.SemaphoreType.DMA((3,)), pltpu.SemaphoreType.DMA((3,)), pltpu.SemaphoreType.DMA]
    return pl.pallas_call(
        body, name="inproj_bwd", grid=(ni, nk), in_specs=in_specs, out_specs=out_specs, out_shape=out_shape,
        scratch_shapes=scratch, compiler_params=_params(("arbitrary", "arbitrary")),
    )(*operands)


def _matmul_tn(a, b, name):
    S, M = a.shape
    N = b.shape[1]
    tmm = 1152 if M % 1152 == 0 else min(M, 1024)
    tk = min(S, 1024)
    nk = S // tk

    def body(a_ref, b_ref, o_ref):
        part = _dot_tn(a_ref[...].astype(BF16), b_ref[...].astype(BF16))

        @pl.when(pl.program_id(1) == 0)
        def _():
            o_ref[...] = part

        @pl.when(pl.program_id(1) > 0)
        def _():
            o_ref[...] += part

    return pl.pallas_call(
        body, name=name, grid=(M // tmm, nk),
        in_specs=[pl.BlockSpec((tk, tmm), lambda i, k: (k, i)),
                  pl.BlockSpec((tk, N), lambda i, k: (k, 0))],
        out_specs=pl.BlockSpec((tmm, N), lambda i, k: (i, 0)),
        out_shape=jax.ShapeDtypeStruct((M, N), F32),
        compiler_params=_params(("arbitrary", "arbitrary")),
    )(a, b)


def _half_mean(v, low):
    s_lo = jnp.sum(jnp.where(low, v, 0.0), axis=1, keepdims=True)
    s_hi = jnp.sum(jnp.where(low, 0.0, v), axis=1, keepdims=True)
    return jnp.where(low, s_lo, s_hi) * (1.0 / 64.0)


def _silu_grad(z, s):
    return s * (1.0 + z * (1.0 - s))


def _tail(y_sb, h_ml, proj, x, p, target, sb_nw, ml_nw, w_out, post_w, w_gate, b_gate, w_up):
    S, D = x.shape
    tm = 256

    def body(ysb_ref, hml_ref, sbz_ref, mlo_ref, mlz_ref, x_ref, p_ref, tg_ref, sbw_ref, mlw_ref, wo_ref, pw_ref,
             wg_ref, bg_ref, wu_ref,
             dx_ref, dysb_ref, dhml_ref, dsbz_ref, dmlo_ref, dmlz_ref, mix_ref, dy_ref, h1_ref, dgp_ref, dpu_ref,
             small_ref):
        lane = lax.broadcasted_iota(jnp.int32, (tm, 128), 1)
        low = lane < 64
        sb_saved, ml_saved, mixed = [], [], []
        for s in range(4):
            sl = slice(128 * s, 128 * s + 128)
            y = ysb_ref[:, sl]
            rs = lax.rsqrt(_half_mean(y * y, low) + EPS)
            n = y * rs
            z = sbz_ref[:, sl]
            sg = jax.nn.sigmoid(z)
            w = sbw_ref[:, sl]
            mixed.append((n * w) * (z * sg))
            sb_saved.append((rs, n, z, sg, w))
        for s in range(4):
            sl = slice(128 * s, 128 * s + 128)
            og = jax.nn.sigmoid(mlo_ref[:, sl])
            hh = hml_ref[:, sl]
            t = og * hh
            rs = lax.rsqrt(jnp.mean(t * t, axis=1, keepdims=True) + EPS)
            n = t * rs
            z = mlz_ref[:, sl]
            sg = jax.nn.sigmoid(z)
            w = mlw_ref[:, sl]
            mixed.append((n * w) * (z * sg))
            ml_saved.append((rs, n, z, sg, w, og, hh))
        mix = jnp.concatenate(mixed, axis=1).astype(BF16)
        mix_ref[...] = mix
        y = _dot(mix, wo_ref[...])
        rs_y = lax.rsqrt(jnp.mean(y * y, axis=1, keepdims=True) + EPS)
        yn = y * rs_y
        pw = pw_ref[...]
        h1 = x_ref[...] + yn * pw
        h1b = h1.astype(BF16)
        h1_ref[...] = h1b
        gate = jax.nn.sigmoid(_dot(h1b, wg_ref[...]) + bg_ref[...])
        pu = _dot(p_ref[...].astype(BF16), wu_ref[...])
        err = (h1 + gate * pu) - tg_ref[...]
        loss = 0.5 * jnp.sum(jnp.sum(err * err, axis=1, keepdims=True) * (1.0 / D))
        d_out = err * (1.0 / D)
        dpu_ref[...] = (d_out * gate).astype(BF16)
        dgp = (d_out * pu) * (gate * (1.0 - gate))
        dgpb = dgp.astype(BF16)
        dgp_ref[...] = dgpb
        d_h1 = d_out + _dot_nt(dgpb, wg_ref[...])
        dx_ref[...] = d_h1
        gwy = d_h1 * pw
        d_y = rs_y * (gwy - yn * jnp.mean(gwy * yn, axis=1, keepdims=True))
        d_yb = d_y.astype(BF16)
        dy_ref[...] = d_yb
        d_mix = _dot_nt(d_yb, wo_ref[...])
        g_nw = []
        for s in range(4):
            sl = slice(128 * s, 128 * s + 128)
            rs, n, z, sg, w = sb_saved[s]
            da = d_mix[:, sl]
            act = z * sg
            dsbz_ref[:, sl] = (da * (n * w) * _silu_grad(z, sg)).astype(BF16)
            dn = da * w * act
            g_nw.append(jnp.sum(da * act * n, axis=0, keepdims=True))
            dysb_ref[:, sl] = rs * (dn - n * _half_mean(dn * n, low))
        for s in range(4):
            sl = slice(128 * s, 128 * s + 128)
            rs, n, z, sg, w, og, hh = ml_saved[s]
            da = d_mix[:, 512 + 128 * s:512 + 128 * s + 128]
            act = z * sg
            dmlz_ref[:, sl] = (da * (n * w) * _silu_grad(z, sg)).astype(BF16)
            dn = da * w * act
            g_nw.append(jnp.sum(da * act * n, axis=0, keepdims=True))
            dt = rs * (dn - n * jnp.mean(dn * n, axis=1, keepdims=True))
            dmlo_ref[:, sl] = (dt * hh * (og * (1.0 - og))).astype(BF16)
            dhml_ref[:, sl] = dt * og
        upd = jnp.concatenate([
            jnp.sum(d_h1 * yn, axis=0, keepdims=True),
            jnp.sum(dgp, axis=0, keepdims=True),
            jnp.concatenate(g_nw, axis=1),
            jnp.full((1, D), loss, F32),
            jnp.zeros((4, D), F32)], axis=0)

        @pl.when(pl.program_id(0) == 0)
        def _():
            small_ref[...] = upd

        @pl.when(pl.program_id(0) > 0)
        def _():
            small_ref[...] += upd

    def rows(width, col=0):
        return pl.BlockSpec((tm, width), lambda i: (i, col))

    def whole(a):
        return pl.BlockSpec(a.shape, lambda i: (0, 0))

    return pl.pallas_call(
        body, name="tail", grid=(S // tm,),
        in_specs=[rows(512), rows(512), rows(512, 3), rows(512, 7), rows(512, 8), rows(D), rows(256), rows(D),
                  whole(sb_nw), whole(ml_nw), whole(w_out), whole(post_w), whole(w_gate), whole(b_gate), whole(w_up)],
        out_specs=[rows(D), rows(512), rows(512), rows(512), rows(512), rows(512), rows(D), rows(D), rows(D), rows(D),
                   rows(D), pl.BlockSpec((8, D), lambda i: (0, 0))],
        out_shape=[jax.ShapeDtypeStruct((S, D), F32), jax.ShapeDtypeStruct((S, 512), F32),
                   jax.ShapeDtypeStruct((S, 512), F32)] + [jax.ShapeDtypeStruct((S, 512), BF16)] * 3
        + [jax.ShapeDtypeStruct((S, D), BF16)] * 5 + [jax.ShapeDtypeStruct((8, D), F32)],
        compiler_params=_params(("arbitrary",)),
    )(y_sb, h_ml, proj, proj, proj, x, p, target, sb_nw, ml_nw, w_out, post_w, w_gate, b_gate, w_up)


def _local_step(x, p, target, pre_w, w_t, wg_t, conv_w, conv_b, gbias, sb_nw, ml_nw, post_w, b_gate, late,
                exchange=None):
    if callable(late[1]):
        proj, proj_g, u, gathered = _inproj_fwd(x, pre_w, w_t, wg_t, late[0])
        w_out, w_gate, w_up = late[1](gathered)
    else:
        proj, proj_g, u = _inproj_fwd(x, pre_w, w_t, wg_t)
        w_out, w_gate, w_up = late
    y_sb, tot = _sb_fwd(proj)
    qkv = _ml_prep(proj, conv_w, conv_b)
    h_ml, cst, nm = _ml_fwd(qkv, proj_g, gbias)
    dx_tail, d_ysb, d_hml, d_sbz, d_mlo, d_mlz, mix, d_y, h1, dgp, dpu, small = _tail(
        y_sb, h_ml, proj, x, p, target, sb_nw, ml_nw, w_out, post_w, w_gate, b_gate, w_up)
    dq, dk, dv = _sb_bwd(proj, tot, d_ysb)
    dqc, dks, dmlv, dif, gif = _ml_bwd(qkv, proj_g, gbias, cst, nm, d_hml)
    dmlqk, g_cw, g_cb = _ml_prep_bwd(proj, conv_w, conv_b, dqc, dks)
    d_main = jnp.concatenate([dq.astype(BF16), dk.astype(BF16), dv.astype(BF16), d_sbz, dmlqk.astype(BF16),
                              dmlv, d_mlo, d_mlz], axis=1)
    d_if = dif.astype(BF16)
    grads = dict(
        w_t=_matmul_tn(d_main, u, "gw_in"), wg_t=_matmul_tn(d_if, u, "gw_in_gates"),
        w_out=_matmul_tn(mix, d_y, "gw_out"), w_gate=_matmul_tn(h1, dgp, "gw_gate"), w_up=_matmul_tn(p, dpu, "gw_up"),
        conv_w=g_cw, conv_b=g_cb, gif=gif)
    parts = exchange(grads) if exchange else None
    return _inproj_bwd(d_main, d_if, w_t, wg_t, x, pre_w, dx_tail, parts), grads, small


def _all_gather(a, b):
    def body(a_ref, b_ref, oa_ref, ob_ref, send_sems, recv_sems, local_sems):
        x, y, c = _place()
        me, sibling = (x, y, c), (x, y, 1 - c)
        chips = [(1 - x, y), (x, 1 - y), (1 - x, 1 - y)]
        pairs = ((a_ref, oa_ref), (b_ref, ob_ref))

        def copies(k, block, to, from_input=False):
            slot = _block_of(*block)
            return [pltpu.make_async_remote_copy(
                src_ref=src if from_input else out.at[slot], dst_ref=out.at[slot],
                send_sem=send_sems.at[t, k], recv_sem=recv_sems.at[t, k], device_id=to, device_id_type=MESH)
                for t, (src, out) in enumerate(pairs)]

        mine = [pltpu.make_async_copy(src, out.at[_block_of(*me)], local_sems.at[t])
                for t, (src, out) in enumerate(pairs)]
        for cp in mine:
            cp.start()
        first = copies(0, me, sibling, True)
        for j, chip in enumerate(chips):
            first += copies(1 + j, me, (*chip, c), True)
        for cp in first:
            cp.start()
        passed = []
        for j, chip in enumerate(chips):
            for cp in copies(1 + j, (*chip, c), me):
                cp.wait_recv()
            fwd = copies(4 + j, (*chip, c), sibling)
            for cp in fwd:
                cp.start()
            passed += fwd
        for cp in copies(0, sibling, me):
            cp.wait_recv()
        for j, chip in enumerate(chips):
            for cp in copies(4 + j, (*chip, 1 - c), me):
                cp.wait_recv()
        for cp in first + passed:
            cp.wait_send()
        for cp in mine:
            cp.wait()

    return pl.pallas_call(
        body, name="all_gather",
        in_specs=[ANY, ANY], out_specs=[ANY, ANY],
        out_shape=[jax.ShapeDtypeStruct((N_DEV,) + a.shape, a.dtype), jax.ShapeDtypeStruct((N_DEV,) + b.shape, b.dtype)],
        scratch_shapes=[pltpu.SemaphoreType.DMA((2, 7)), pltpu.SemaphoreType.DMA((2, 7)), pltpu.SemaphoreType.DMA((2,))],
    )(a, b)


def _exchange_pair(g):
    def body(g_ref, og_ref, send_sems, recv_sems):
        x, y, c = _place()
        sent = [pltpu.make_async_remote_copy(
            src_ref=g_ref.at[k, 1 - c], dst_ref=og_ref.at[k], send_sem=send_sems.at[k], recv_sem=recv_sems.at[k],
            device_id=(x, y, 1 - c), device_id_type=MESH) for k in range(4)]
        for cp in sent:
            cp.start()
        for cp in sent:
            cp.wait()

    return pl.pallas_call(
        body, name="exchange_pair", in_specs=[ANY], out_specs=ANY,
        out_shape=jax.ShapeDtypeStruct((4,) + g.shape[2:], g.dtype),
        scratch_shapes=[pltpu.SemaphoreType.DMA((4,)), pltpu.SemaphoreType.DMA((4,))],
    )(g)


def _pair_sum(g, r, core, tr):
    _, _, R, D = g.shape

    def body(c_ref, g_ref, r_ref, o_ref):
        o_ref[...] = (g_ref[...] + r_ref[...]).astype(BF16)

    return pl.pallas_call(
        body, name="pair_sum",
        grid_spec=pltpu.PrefetchScalarGridSpec(
            num_scalar_prefetch=1, grid=(4, R // tr),
            in_specs=[pl.BlockSpec((None, None, tr, D), lambda k, i, c: (k, c[0], i, 0)),
                      pl.BlockSpec((None, tr, D), lambda k, i, c: (k, i, 0))],
            out_specs=pl.BlockSpec((None, tr, D), lambda k, i, c: (k, i, 0))),
        out_shape=jax.ShapeDtypeStruct((4, R, D), BF16),
        compiler_params=_params(("arbitrary", "arbitrary")),
    )(core, g, r)


ADAM_LR, ADAM_B1, ADAM_B2, ADAM_EPS, ADAM_WD, ADAM_STEP = 0.001, 0.9, 0.999, 1e-08, 0.01, 10


def _adamw(w, g, m, v):
    m = ADAM_B1 * m + (1.0 - ADAM_B1) * g
    v = ADAM_B2 * v + (1.0 - ADAM_B2) * (g * g)
    m_hat = m / (1.0 - ADAM_B1 ** ADAM_STEP)
    v_hat = v / (1.0 - ADAM_B2 ** ADAM_STEP)
    return -ADAM_LR * (m_hat / (jnp.sqrt(v_hat) + ADAM_EPS) + ADAM_WD * w), m, v


def _adam(parts, w, m, v, tr, name, small=None):
    R, D = w.shape
    n = parts.shape[0]
    steps = R // tr

    def body(p_ref, w_ref, m_ref, v_ref, *rest):
        if small is None:
            g_ref, d_ref, nm_ref, nv_ref = rest
        else:
            s_ref, g_ref, d_ref, nm_ref, nv_ref, o_ref, send_sems, recv_sems, local_sem = rest
            i = pl.program_id(0)
            _around_grid(lambda: _copies_to_all(s_ref, o_ref, send_sems, recv_sems, local_sem), i == 0, i == steps - 1)
        g = p_ref[0].astype(F32)
        for k in range(1, n):
            g = g + p_ref[k].astype(F32)
        g_ref[...] = g
        d_ref[...], nm_ref[...], nv_ref[...] = _adamw(w_ref[...], g, m_ref[...], v_ref[...])

    blk = pl.BlockSpec((tr, D), lambda i: (i, 0))
    in_specs = [pl.BlockSpec((n, tr, D), lambda i: (0, i, 0)), blk, blk, blk]
    out_specs, out_shape = [blk] * 4, [jax.ShapeDtypeStruct((R, D), F32)] * 4
    operands, scratch = (parts, w, m, v), []
    if small is not None:
        in_specs, out_specs, operands = in_specs + [ANY], out_specs + [ANY], operands + (small,)
        out_shape = out_shape + [jax.ShapeDtypeStruct((N_DEV,) + small.shape, small.dtype)]
        scratch = [pltpu.SemaphoreType.DMA((7,)), pltpu.SemaphoreType.DMA((7,)), pltpu.SemaphoreType.DMA]
    return pl.pallas_call(
        body, name=name, grid=(steps,), in_specs=in_specs, out_specs=out_specs, out_shape=out_shape,
        scratch_shapes=scratch, compiler_params=_params(("arbitrary",)),
    )(*operands)


ROWS_IN = 592
ROWS_BF16 = ROWS_IN + 128 + 128 + 32
ROWS_CONV = 16
ROWS_ALL = ROWS_BF16 + ROWS_CONV
ROW_TILE = 224


def _pad_rows(a, rows):
    return jnp.pad(a, ((0, rows - a.shape[0]), (0, 0)))


def _pack_shards(w_in, w_out, w_gate, w_up, conv_w):
    return jnp.concatenate([
        _pad_rows(w_in[0].T, ROWS_IN), w_out[0], w_gate[0], w_up[0].reshape(32, D_MODEL),
        _pad_rows(jnp.pad(conv_w[0].reshape(1, 512), ((0, 0), (0, 512))), ROWS_CONV)], axis=0)


def _unpack_shards(a):
    return (a[:SHARD_IN].T[None], a[ROWS_IN:ROWS_IN + 128][None], a[ROWS_IN + 128:ROWS_IN + 256][None],
            a[ROWS_IN + 256:ROWS_BF16].reshape(1, 256, 128), a[ROWS_BF16, :512].reshape(1, 4, 128))


def _pack_small(pre_w, conv_b, i_bias, f_bias, sb_nw, ml_nw, post_w, b_gate):
    gates = jnp.pad(jnp.concatenate([i_bias, f_bias], axis=1), ((0, 0), (0, D_MODEL - 8)))
    return jnp.concatenate([post_w, b_gate, jnp.concatenate([sb_nw, ml_nw], axis=1), jnp.zeros((1, D_MODEL), F32),
                            pre_w, conv_b, gates, jnp.zeros((1, D_MODEL), F32)], axis=0)


def _unpack_small(a):
    return a[4:5], a[5:6], a[6:7, 0:4], a[6:7, 4:8], a[2:3, :512], a[2:3, 512:], a[0:1], a[1:2]


def kernel(x, p, pre_norm_w, w_in, ml_conv_w, ml_conv_b, ml_i_bias, ml_f_bias, sb_norm_w, ml_norm_w, w_out, post_norm_w, ple_w_up, ple_w_gate, ple_b_gate, loss_target, m_pre_norm_w, m_w_in, m_ml_conv_w, m_ml_conv_b, m_ml_i_bias, m_ml_f_bias, m_sb_norm_w, m_ml_norm_w, m_w_out, m_post_norm_w, m_ple_w_up, m_ple_w_gate, m_ple_b_gate, v_pre_norm_w, v_w_in, v_ml_conv_w, v_ml_conv_b, v_ml_i_bias, v_ml_f_bias, v_sb_norm_w, v_ml_norm_w, v_w_out, v_post_norm_w, v_ple_w_up, v_ple_w_gate, v_ple_b_gate):
    D = D_MODEL
    w_pk = _pack_shards(w_in, w_out, ple_w_gate, ple_w_up, ml_conv_w)
    m_pk = _pack_shards(m_w_in, m_w_out, m_ple_w_gate, m_ple_w_up, m_ml_conv_w)
    v_pk = _pack_shards(v_w_in, v_w_out, v_ple_w_gate, v_ple_w_up, v_ml_conv_w)
    w_sm = _pack_small(pre_norm_w, ml_conv_b, ml_i_bias, ml_f_bias, sb_norm_w, ml_norm_w, post_norm_w, ple_b_gate)
    m_sm = _pack_small(m_pre_norm_w, m_ml_conv_b, m_ml_i_bias, m_ml_f_bias, m_sb_norm_w, m_ml_norm_w, m_post_norm_w, m_ple_b_gate)
    v_sm = _pack_small(v_pre_norm_w, v_ml_conv_b, v_ml_i_bias, v_ml_f_bias, v_sb_norm_w, v_ml_norm_w, v_post_norm_w, v_ple_b_gate)

    w_bf = w_pk[:ROWS_BF16].astype(BF16)
    ga, gb = _all_gather(w_bf[:ROWS_IN], w_pk[ROWS_BF16:])
    w_in_t = ga[:, :SHARD_IN].reshape(N_IN, D)
    wg_t = _pad_rows(w_in_t[N_MAIN:], 128)
    conv_w_f = gb[:, 0, :512].reshape(N_DEV, 4, 128).transpose(1, 0, 2).reshape(4, D)
    gbias = jnp.pad(jnp.concatenate([ml_i_bias, ml_f_bias], axis=1), ((0, 0), (0, 120)))

    def unpack_late(gl):
        return (gl[:, :128].reshape(D, D), gl[:, 128:256].reshape(D, D),
                gl[:, 256:].reshape(N_DEV, 256, 128).transpose(1, 0, 2).reshape(256, D))

    def exchange(g):
        g_in = jnp.concatenate([g["w_t"], g["wg_t"][:8]], axis=0).reshape(N_DEV, SHARD_IN, D)
        g_blocks = jnp.concatenate([
            jnp.pad(g_in, ((0, 0), (0, ROWS_IN - SHARD_IN), (0, 0))),
            g["w_out"].reshape(N_DEV, 128, D), g["w_gate"].reshape(N_DEV, 128, D),
            g["w_up"].reshape(256, N_DEV, 128).transpose(1, 0, 2).reshape(N_DEV, 32, D),
            jnp.pad(g["conv_w"].reshape(4, N_DEV, 128).transpose(1, 0, 2).reshape(N_DEV, 1, 512),
                    ((0, 0), (0, ROWS_CONV - 1), (0, 512))),
        ], axis=1).reshape(4, 2, ROWS_ALL, D)
        core = lax.axis_index("c").astype(jnp.int32).reshape(1)
        return _pair_sum(g_blocks, _exchange_pair(g_blocks), core, ROW_TILE)

    (dx, g_pre, parts), g, small = _local_step(
        x[0], p[0, 0], loss_target[0], pre_norm_w, w_in_t, wg_t, conv_w_f, ml_conv_b, gbias, sb_norm_w, ml_norm_w,
        post_norm_w, ple_b_gate, (w_bf[ROWS_IN:], unpack_late), exchange)

    g_small = jnp.concatenate([small[0:4], g_pre, g["conv_b"], jnp.pad(g["gif"][0:1], ((0, 0), (0, D - 128))),
                               jnp.zeros((1, D), F32)], axis=0)
    grad_pk, delta_pk, nm_pk, nv_pk, parts_sm = _adam(parts, w_pk, m_pk, v_pk, ROW_TILE, "adam", g_small)
    grad_sm, delta_sm, nm_sm, nv_sm = _adam(parts_sm, w_sm, m_sm, v_sm, 8, "adam_small")
    loss = grad_sm[3, 0]

    def ordered(pk, sm):
        win, wout, wgate, wup, convw = _unpack_shards(pk)
        pre_w, conv_b, i_b, f_b, sb_nw, ml_nw, post_w, b_gate = _unpack_small(sm)
        return [pre_w, win, convw, conv_b, i_b, f_b, sb_nw, ml_nw, wout, post_w, wup, wgate, b_gate]

    return (loss, dx[None], *ordered(grad_pk, grad_sm), *ordered(delta_pk, delta_sm), *ordered(nm_pk, nm_sm),
            *ordered(nv_pk, nv_sm))
```

```python
import functools

import jax
import jax.numpy as jnp
from jax import lax
from jax.experimental import pallas as pl
from jax.experimental.pallas import tpu as pltpu

F32 = jnp.float32
BF16 = jnp.bfloat16
EPS = 1e-6
D_MODEL = 1024
SB_W = 512
ML_W = 512
N_MAIN = 4608
N_IN = 4616
SHARD_IN = 577
SHARD_IN_PAD = 584
TQ = 1024
TK = 256
ND = TQ // TK
LCH = 128
VMEM_LIMIT = 56 * 1024 * 1024


def _dot(a, b):
    return jnp.dot(a, b, preferred_element_type=F32)


def _dot_nt(a, b):
    return lax.dot_general(a, b, (((1,), (1,)), ((), ())), preferred_element_type=F32)


def _dot_tn(a, b):
    return lax.dot_general(a, b, (((0,), (0,)), ((), ())), preferred_element_type=F32)


def _split2(x):
    hi = x.astype(BF16)
    lo = (x - hi.astype(F32)).astype(BF16)
    return hi, lo


def _split3(x):
    hi = x.astype(BF16)
    r = x - hi.astype(F32)
    mid = r.astype(BF16)
    lo = (r - mid.astype(F32)).astype(BF16)
    return hi, mid, lo


def _params(sem):
    return pltpu.CompilerParams(dimension_semantics=sem, vmem_limit_bytes=VMEM_LIMIT)


def _log_sigmoid_parts(z):
    e = jnp.exp(-jnp.abs(z))
    return jnp.minimum(z, 0.0) - jnp.log(1.0 + e)


def _neg_log_sigmoid(nz):
    nz = jnp.minimum(nz, 80.0)
    sp = jnp.log(1.0 + jnp.exp(nz))
    return sp, nz - sp


def _sb_fwd(proj):
    S = proj.shape[0]
    nq = S // TQ

    def body(q_ref, k_ref, v_ref, y_ref, t_ref, acc_ref, car_ref):
        i = pl.program_id(1)
        low = lax.broadcasted_iota(jnp.int32, (TQ, 128), 1) < 64
        row = lax.broadcasted_iota(jnp.int32, (TK, TK), 0)
        col = lax.broadcasted_iota(jnp.int32, (TK, TK), 1)
        uo = (row > col).astype(BF16)
        uo = jnp.concatenate([uo, uo], axis=0)
        q = q_ref[...] * 0.125
        qh = (jnp.where(low, -q, 0.0).astype(BF16), jnp.where(low, 0.0, -q).astype(BF16))
        acc_ref[...] = jnp.zeros_like(acc_ref)
        car_ref[...] = jnp.zeros_like(car_ref)

        def block(j, r0):
            diag = r0 is not None
            r0 = r0 or 0
            if diag:
                strict = (lax.broadcasted_iota(jnp.int32, (TQ - r0, TK), 1)
                          < lax.broadcasted_iota(jnp.int32, (TQ - r0, TK), 0))
            rows = pl.ds(pl.multiple_of(j * TK, TK), TK)
            kb = k_ref[rows, :].astype(BF16)
            vb = v_ref[rows, :].astype(BF16)
            for h in range(2):
                sp, lk = _neg_log_sigmoid(_dot_nt(qh[h][r0:], kb))
                if diag:
                    lk = jnp.where(strict, lk, 0.0)
                hi, lo = _split2(lk)
                rr = _dot(jnp.concatenate([hi, lo], axis=1), uo)
                car = car_ref[h, r0:, :]
                a = jnp.exp((jnp.concatenate([car, car], axis=1) + rr) - sp)
                if diag:
                    a = jnp.where(strict, a, 0.0)
                acc_ref[h, r0:, :] += _dot(a.astype(BF16), vb)
                car_ref[h, r0:, :] = car + jnp.broadcast_to(rr[:, 0:1] + lk[:, 0:1], car.shape)

        for d in reversed(range(ND)):
            block(ND * i + d, TK * d)

        def loop(n, c):
            block(ND * i - 1 - n, None)
            return c

        lax.fori_loop(0, ND * i, loop, 0)
        y_ref[...] = jnp.where(low, acc_ref[0], acc_ref[1])
        t_ref[...] = jnp.where(low, car_ref[0], car_ref[1])

    return pl.pallas_call(
        body, name="sb_fwd", grid=(4, nq),
        in_specs=[pl.BlockSpec((TQ, 128), lambda p, i: (i, p)),
                  pl.BlockSpec((S, 128), lambda p, i: (0, 4 + p)),
                  pl.BlockSpec((S, 128), lambda p, i: (0, 8 + p))],
        out_specs=[pl.BlockSpec((TQ, 128), lambda p, i: (i, p)),
                   pl.BlockSpec((TQ, 128), lambda p, i: (i, p))],
        out_shape=[jax.ShapeDtypeStruct((S, SB_W), F32), jax.ShapeDtypeStruct((S, SB_W), F32)],
        scratch_shapes=[pltpu.VMEM((2, TQ, 128), F32), pltpu.VMEM((2, TQ, 128), F32)],
        compiler_params=_params(("arbitrary", "arbitrary")),
    )(proj, proj, proj)


def _sb_bwd(proj, tot, dy):
    S = proj.shape[0]
    nq = S // TQ

    def body(q_ref, k_ref, v_ref, t_ref, dy_ref, dq_ref, dk_ref, dv_ref, dqa_ref, cp_ref, cg_ref, dkt_ref, dvt_ref):
        i = pl.program_id(1)
        low = lax.broadcasted_iota(jnp.int32, (TQ, 128), 1) < 64
        row = lax.broadcasted_iota(jnp.int32, (TK, TK), 0)
        col = lax.broadcasted_iota(jnp.int32, (TK, TK), 1)
        u_inc = (row <= col).astype(BF16)
        u_inc = jnp.concatenate([u_inc, u_inc], axis=0)
        u_exc = (row < col).astype(BF16)
        q = q_ref[...] * 0.125
        qh = (jnp.where(low, -q, 0.0).astype(BF16), jnp.where(low, 0.0, -q).astype(BF16))
        dy_ = dy_ref[...]
        dyh = (jnp.where(low, dy_, 0.0).astype(BF16), jnp.where(low, 0.0, dy_).astype(BF16))
        qt = tuple(jnp.transpose(x) for x in qh)
        dyt = tuple(jnp.transpose(x) for x in dyh)
        t_ = t_ref[...]
        t_sw = pltpu.roll(t_, 64, 1)
        th = (jnp.where(low, t_, t_sw), jnp.where(low, t_sw, t_))
        dqa_ref[...] = jnp.zeros_like(dqa_ref)
        cp_ref[...] = jnp.zeros_like(cp_ref)
        cg_ref[...] = jnp.zeros_like(cg_ref)

        @pl.when(i == 0)
        def _():
            dkt_ref[...] = jnp.zeros_like(dkt_ref)
            dvt_ref[...] = jnp.zeros_like(dvt_ref)

        def block(j, r0):
            diag = r0 is not None
            r0 = r0 or 0
            if diag:
                strict = (lax.broadcasted_iota(jnp.int32, (TQ - r0, TK), 1)
                          < lax.broadcasted_iota(jnp.int32, (TQ - r0, TK), 0))
            rows = pl.ds(pl.multiple_of(j * TK, TK), TK)
            kb = k_ref[rows, :].astype(BF16)
            vb = v_ref[rows, :].astype(BF16)
            dk_acc = jnp.zeros((128, TK), F32)
            dv_acc = jnp.zeros((128, TK), F32)
            for h in range(2):
                qr, dyr = qh[h][r0:], dyh[h][r0:]
                sp, lk = _neg_log_sigmoid(_dot_nt(qr, kb))
                if diag:
                    lk = jnp.where(strict, lk, 0.0)
                hi, lo = _split2(lk)
                pp = _dot(jnp.concatenate([hi, lo], axis=1), u_inc)
                cp, cg = cp_ref[h, r0:, :], cg_ref[h, r0:, :]
                rest = th[h][r0:] - cp
                a = jnp.exp((jnp.concatenate([rest, rest], axis=1) - pp) - sp)
                if diag:
                    a = jnp.where(strict, a, 0.0)
                g = _dot_nt(dyr, vb) * a
                gg = _dot(g.astype(BF16), u_exc)
                beta = jnp.exp(-sp)
                dz = g - beta * (g + (jnp.concatenate([cg, cg], axis=1) + gg))
                if diag:
                    dz = jnp.where(strict, dz, 0.0)
                dzb = dz.astype(BF16)
                dqa_ref[h, r0:, :] += _dot(dzb, kb)
                dk_acc += _dot(qt[h][:, r0:], dzb)
                dv_acc += _dot(dyt[h][:, r0:], a.astype(BF16))
                cp_ref[h, r0:, :] = cp + jnp.broadcast_to(pp[:, TK - 1:TK], cp.shape)
                cg_ref[h, r0:, :] = cg + jnp.broadcast_to(gg[:, TK - 1:TK] + g[:, TK - 1:TK], cg.shape)
            dkt_ref[:, rows] -= dk_acc
            dvt_ref[:, rows] += dv_acc

        def loop(j, c):
            block(j, None)
            return c

        lax.fori_loop(0, ND * i, loop, 0)
        for d in range(ND):
            block(ND * i + d, TK * d)
        dq_ref[...] = jnp.where(low, dqa_ref[0], dqa_ref[1]) * 0.125

        @pl.when(i == nq - 1)
        def _():
            dk_ref[...] = jnp.transpose(dkt_ref[...])
            dv_ref[...] = jnp.transpose(dvt_ref[...])

    return pl.pallas_call(
        body, name="sb_bwd", grid=(4, nq),
        in_specs=[pl.BlockSpec((TQ, 128), lambda p, i: (i, p)),
                  pl.BlockSpec((S, 128), lambda p, i: (0, 4 + p)),
                  pl.BlockSpec((S, 128), lambda p, i: (0, 8 + p)),
                  pl.BlockSpec((TQ, 128), lambda p, i: (i, p)),
                  pl.BlockSpec((TQ, 128), lambda p, i: (i, p))],
        out_specs=[pl.BlockSpec((TQ, 128), lambda p, i: (i, p)),
                   pl.BlockSpec((S, 128), lambda p, i: (0, p)),
                   pl.BlockSpec((S, 128), lambda p, i: (0, p))],
        out_shape=[jax.ShapeDtypeStruct((S, SB_W), F32)] * 3,
        scratch_shapes=[pltpu.VMEM((2, TQ, 128), F32)] * 3 + [pltpu.VMEM((128, S), F32)] * 2,
        compiler_params=_params(("arbitrary", "arbitrary")),
    )(proj, proj, proj, tot, dy)


ML_SCALE = 128 ** -0.5
RC = 256


def _conv_taps(cur, prev8, w):
    n = cur.shape[0]
    win = jnp.concatenate([prev8, cur], axis=0)
    out = w[3:4, :] * cur
    for j in range(3):
        out = out + w[j:j + 1, :] * pltpu.roll(win, 3 - j, 0)[8:8 + n]
    return out


def _ml_prep(proj, conv_w, conv_b):
    S = proj.shape[0]

    def body(x_ref, w_ref, b_ref, o_ref):
        c = pl.program_id(0)
        scale = jnp.where(c < 4, 1.0, ML_SCALE).astype(F32)
        w = w_ref[...]
        b = b_ref[...]

        @pl.when(c < 8)
        def _():
            for n in range(S // RC):
                cur = x_ref[n * RC:(n + 1) * RC, :]
                prev8 = x_ref[n * RC - 8:n * RC, :] if n else jnp.zeros((8, 128), F32)
                pre = b + _conv_taps(cur, prev8, w)
                o_ref[n * RC:(n + 1) * RC, :] = (pre * jax.nn.sigmoid(pre) * scale).astype(BF16)

        @pl.when(c >= 8)
        def _():
            o_ref[...] = x_ref[...].astype(BF16)

    return pl.pallas_call(
        body, name="ml_prep", grid=(12,),
        in_specs=[pl.BlockSpec((S, 128), lambda c: (0, 16 + c)),
                  pl.BlockSpec((4, 128), lambda c: (0, jnp.minimum(c, 7))),
                  pl.BlockSpec((1, 128), lambda c: (0, jnp.minimum(c, 7)))],
        out_specs=pl.BlockSpec((S, 128), lambda c: (0, c)),
        out_shape=jax.ShapeDtypeStruct((S, 1536), BF16),
        compiler_params=_params(("arbitrary",)),
    )(proj, conv_w, conv_b)


def _ml_prep_bwd(proj, conv_w, conv_b, dq, dk):
    S = proj.shape[0]

    def body(x_ref, w_ref, b_ref, dq_ref, dk_ref, dx_ref, gw_ref, gb_ref, dp_ref):
        c = pl.program_id(0)
        w = w_ref[...]
        b = b_ref[...]
        gw = [jnp.zeros((1, 128), F32) for _ in range(4)]
        gb = jnp.zeros((1, 128), F32)
        for n in range(S // RC):
            rows = slice(n * RC, (n + 1) * RC)
            cur = x_ref[rows, :]
            prev8 = x_ref[n * RC - 8:n * RC, :] if n else jnp.zeros((8, 128), F32)
            pre = b + _conv_taps(cur, prev8, w)
            s = jax.nn.sigmoid(pre)
            dpost = jnp.where(c < 4, dq_ref[rows, :].astype(F32), dk_ref[rows, :].astype(F32) * ML_SCALE)
            dpre = dpost * (s * (1.0 + pre * (1.0 - s)))
            dp_ref[rows, :] = dpre
            win = jnp.concatenate([prev8, cur], axis=0)
            gb = gb + jnp.sum(dpre, axis=0, keepdims=True)
            gw[3] = gw[3] + jnp.sum(dpre * cur, axis=0, keepdims=True)
            for j in range(3):
                gw[j] = gw[j] + jnp.sum(dpre * pltpu.roll(win, 3 - j, 0)[8:8 + RC], axis=0, keepdims=True)
        dp_ref[S:S + 8, :] = jnp.zeros((8, 128), F32)
        for n in range(S // RC):
            win = dp_ref[n * RC:(n + 1) * RC + 8, :]
            dx = w[3:4, :] * win[:RC]
            for j in range(3):
                dx = dx + w[j:j + 1, :] * pltpu.roll(win, RC + 8 - (3 - j), 0)[:RC]
            dx_ref[n * RC:(n + 1) * RC, :] = dx
        gw_ref[...] = jnp.concatenate(gw, axis=0)
        gb_ref[...] = gb

    return pl.pallas_call(
        body, name="ml_prep_bwd", grid=(8,),
        in_specs=[pl.BlockSpec((S, 128), lambda c: (0, 16 + c)),
                  pl.BlockSpec((4, 128), lambda c: (0, c)),
                  pl.BlockSpec((1, 128), lambda c: (0, c)),
                  pl.BlockSpec((S, 128), lambda c: (0, jnp.minimum(c, 3))),
                  pl.BlockSpec((S, 128), lambda c: (0, jnp.maximum(c - 4, 0)))],
        out_specs=[pl.BlockSpec((S, 128), lambda c: (0, c)),
                   pl.BlockSpec((4, 128), lambda c: (0, c)),
                   pl.BlockSpec((1, 128), lambda c: (0, c))],
        out_shape=[jax.ShapeDtypeStruct((S, 1024), F32), jax.ShapeDtypeStruct((4, 1024), F32),
                   jax.ShapeDtypeStruct((1, 1024), F32)],
        scratch_shapes=[pltpu.VMEM((S + 8, 128), F32)],
        compiler_params=_params(("arbitrary",)),
    )(proj, conv_w, conv_b, dq, dk)


NH = 4


def _heads(x):
    return jnp.stack([x[:, 128 * h:128 * (h + 1)] for h in range(NH)])


def _unheads(x):
    return jnp.concatenate([x[h] for h in range(NH)], axis=1)


def _per_head(f, *xs):
    return jnp.stack([f(*[x[h] for x in xs]) for h in range(NH)])


def _ml_gates(g):
    lg = _log_sigmoid_parts(g)
    ig = jnp.stack([jnp.broadcast_to(g[:, h:h + 1], (LCH, 128)) for h in range(NH)])
    lf = jnp.stack([jnp.broadcast_to(lg[:, 4 + h:5 + h], (LCH, 128)) for h in range(NH)])
    return ig, lf


def _tri_sum(tri, x, dot):
    return _per_head(lambda a, b, c: dot(tri, a) + dot(tri, b) + dot(tri, c), *_split3(x))


def _ml_chunk_fwd(q, k, v, ig, lf, ct, n_st, m_st, tri, causal):
    vf = v.astype(F32)
    b = _tri_sum(tri, lf, _dot)
    b_last = b[:, LCH - 1:LCH, :]
    g = b_last - b + ig
    m_loc = jnp.max(g, axis=1, keepdims=True)
    w = jnp.exp(g - m_loc)
    vwf = vf * w
    vw = vwf.astype(BF16)
    ct_loc = _per_head(_dot_tn, k, vw)
    kf = k.astype(F32)
    n_loc = jnp.sum(w * kf, axis=1, keepdims=True)
    r = _per_head(jnp.transpose, ig - b)
    d_log = jnp.where(causal, b + r, -jnp.inf)
    m_t = jnp.maximum(b + m_st, jnp.max(d_log, axis=2, keepdims=True))
    w_in = jnp.exp(d_log - m_t)
    qk = _per_head(_dot_nt, q, k)
    scores = qk * w_in
    cs = jnp.exp(b + m_st - m_t)
    ctb = ct.astype(BF16)
    qc = _per_head(_dot, q, ctb)
    qf = q.astype(F32)
    qn = jnp.sum(qf * n_st, axis=2, keepdims=True)
    num = _per_head(_dot, scores.astype(BF16), v) + cs * qc
    den = jnp.sum(scores, axis=2, keepdims=True) + cs * qn
    em = jnp.exp(-m_t)
    dd = jnp.maximum(jnp.abs(den), em)
    h = num / dd
    m_new = jnp.maximum(b_last + m_st, m_loc)
    a = jnp.exp(b_last + m_st - m_new)
    gg = jnp.exp(m_loc - m_new)
    return dict(vwf=vwf, vw=vw, kf=kf, qf=qf, ct_loc=ct_loc, n_loc=n_loc, w=w, w_in=w_in, qk=qk, scores=scores,
                cs=cs, ctb=ctb, qc=qc, qn=qn, den=den, em=em, dd=dd, h=h, m_new=m_new, a=a, gg=gg)


def _ml_consts():
    row = lax.broadcasted_iota(jnp.int32, (LCH, LCH), 0)
    col = lax.broadcasted_iota(jnp.int32, (LCH, LCH), 1)
    return row, (col <= row), (col <= row).astype(BF16)


def _ml_rows(qkv_ref, g_ref, gb_ref, c):
    rows = pl.ds(pl.multiple_of(c * LCH, LCH), LCH)
    q, k, v = (_heads(qkv_ref[rows, 512 * t:512 * (t + 1)]) for t in range(3))
    ig, lf = _ml_gates(g_ref[rows, :] + gb_ref[...])
    return rows, q, k, v, ig, lf


VMEM_SPEC = pl.BlockSpec(memory_space=pltpu.VMEM)


def _ml_fwd(qkv, proj_g, gbias):
    S = qkv.shape[0]
    nc = S // LCH

    def body(qkv_ref, g_ref, gb_ref, h_ref, cst_ref, nm_ref, ct_ref, n_ref, m_ref):
        _, causal, tri = _ml_consts()
        ct_ref[...] = jnp.zeros_like(ct_ref)
        n_ref[...] = jnp.zeros_like(n_ref)
        m_ref[...] = jnp.zeros_like(m_ref)

        def chunk(c, carry):
            rows, q, k, v, ig, lf = _ml_rows(qkv_ref, g_ref, gb_ref, c)
            ct, n_st, m_st = ct_ref[...], n_ref[:, 0:1, :], m_ref[:, 0:1, :]
            cst_ref[c] = ct
            nm_ref[c, :, 0:8, :] = n_ref[...]
            nm_ref[c, :, 8:16, :] = m_ref[...]
            f = _ml_chunk_fwd(q, k, v, ig, lf, ct, n_st, m_st, tri, causal)
            h_ref[rows, :] = _unheads(f["h"])
            ct_ref[...] = f["a"] * ct + f["gg"] * f["ct_loc"]
            n_ref[...] = jnp.broadcast_to(f["a"] * n_st + f["gg"] * f["n_loc"], (NH, 8, 128))
            m_ref[...] = jnp.broadcast_to(f["m_new"], (NH, 8, 128))
            return carry

        lax.fori_loop(0, nc, chunk, 0)

    return pl.pallas_call(
        body, name="ml_fwd", in_specs=[VMEM_SPEC] * 3, out_specs=[VMEM_SPEC] * 3,
        out_shape=[jax.ShapeDtypeStruct((S, ML_W), F32), jax.ShapeDtypeStruct((nc, NH, 128, 128), F32),
                   jax.ShapeDtypeStruct((nc, NH, 16, 128), F32)],
        scratch_shapes=[pltpu.VMEM((NH, 128, 128), F32), pltpu.VMEM((NH, 8, 128), F32), pltpu.VMEM((NH, 8, 128), F32)],
        compiler_params=pltpu.CompilerParams(vmem_limit_bytes=VMEM_LIMIT),
    )(qkv, proj_g, gbias)


def _ml_bwd(qkv, proj_g, gbias, cst, nm, dh):
    S = qkv.shape[0]
    nc = S // LCH

    def body(qkv_ref, g_ref, gb_ref, cst_ref, nm_ref, dh_ref, dq_ref, dk_ref, dv_ref, dif_ref, gsum_ref,
             dct_ref, dn_ref, gi_ref):
        row, causal, tri = _ml_consts()
        lane = lax.broadcasted_iota(jnp.int32, (LCH, 128), 1)
        last_row = row == LCH - 1
        dct_ref[...] = jnp.zeros_like(dct_ref)
        dn_ref[...] = jnp.zeros_like(dn_ref)
        gi_ref[...] = jnp.zeros_like(gi_ref)

        def chunk(t, carry):
            c = nc - 1 - t
            rows, q, k, v, ig, lf = _ml_rows(qkv_ref, g_ref, gb_ref, c)
            ct, n_st, m_st = cst_ref[c], nm_ref[c, :, 0:1, :], nm_ref[c, :, 8:9, :]
            f = _ml_chunk_fwd(q, k, v, ig, lf, ct, n_st, m_st, tri, causal)
            dh_ = _heads(dh_ref[rows, :])
            dct_new, dn_new = dct_ref[...], dn_ref[:, 0:1, :]
            e_num = dh_ / f["dd"]
            hdh = jnp.sum(f["h"] * dh_, axis=2, keepdims=True)
            free = jnp.abs(f["den"]) > f["em"]
            e_den = jnp.where(free, -hdh / f["dd"] * jnp.sign(f["den"]), 0.0)
            e_num_b = e_num.astype(BF16)
            ds_ = _per_head(_dot_nt, e_num_b, v) + e_den
            dqk = ds_ * f["w_in"]
            gam = dqk * f["qk"]
            dqk_b = dqk.astype(BF16)
            cse = f["cs"] * e_den
            dq = _per_head(_dot, dqk_b, k) + f["cs"] * _per_head(_dot_nt, e_num_b, f["ctb"]) + cse * n_st
            dk = _per_head(_dot_tn, dqk_b, q)
            dv = _per_head(_dot_tn, f["scores"].astype(BF16), e_num_b)
            dcl = (f["gg"] * dct_new).astype(BF16)
            dnl = f["gg"] * dn_new
            kd = _per_head(_dot, k, dcl)
            dv = dv + f["w"] * kd
            dk = dk + _per_head(_dot_nt, f["vw"], dcl) + f["w"] * dnl
            gam_s = (jnp.sum(kd * f["vwf"], axis=2, keepdims=True)
                     + f["w"] * jnp.sum(f["kf"] * dnl, axis=2, keepdims=True))
            col_g = jnp.sum(_per_head(jnp.transpose, gam), axis=2, keepdims=True) + gam_s
            db = (jnp.sum(gam, axis=2, keepdims=True) + jnp.sum(e_num * (f["cs"] * f["qc"]), axis=2, keepdims=True)
                  + cse * f["qn"] - col_g)
            state = jnp.sum(jnp.sum(dct_new * ct, axis=2, keepdims=True), axis=1, keepdims=True)
            state = state + jnp.sum(dn_new * n_st, axis=2, keepdims=True)
            db_last = jnp.sum(gam_s[:, :, 0:1], axis=1, keepdims=True) + f["a"][:, :, 0:1] * state
            db = jnp.where(last_row, db + db_last, db)
            dlf = _tri_sum(tri, db, _dot_tn)
            df = dlf * (1.0 - jnp.exp(lf))
            dq_ref[rows, :] = _unheads(dq).astype(BF16)
            dk_ref[rows, :] = _unheads(dk).astype(BF16)
            dv_ref[rows, :] = _unheads(dv).astype(BF16)
            dif = jnp.zeros((LCH, 128), F32)
            for h in range(NH):
                dif = dif + jnp.where(lane == h, col_g[h], 0.0) + jnp.where(lane == h + 4, df[h], 0.0)
            dif_ref[rows, :] = dif
            clamped = jnp.where(free, 0.0, hdh)
            gi_ref[...] += jnp.broadcast_to(jnp.sum(clamped, axis=1, keepdims=True), (NH, 8, 128))
            dct_ref[...] = f["a"] * dct_new + _per_head(_dot_tn, q, (f["cs"] * e_num).astype(BF16))
            dn_ref[...] = jnp.broadcast_to(f["a"] * dn_new + jnp.sum(cse * f["qf"], axis=1, keepdims=True), (NH, 8, 128))
            return carry

        lax.fori_loop(0, nc, chunk, 0)
        lane8 = lax.broadcasted_iota(jnp.int32, (8, 128), 1)
        gsum = jnp.where(lane8 >= 4, jnp.sum(dif_ref[...], axis=0, keepdims=True), 0.0)
        for h in range(NH):
            gsum = gsum + jnp.where(lane8 == h, gi_ref[h], 0.0)
        gsum_ref[...] = gsum

    return pl.pallas_call(
        body, name="ml_bwd", in_specs=[VMEM_SPEC] * 6, out_specs=[VMEM_SPEC] * 5,
        out_shape=[jax.ShapeDtypeStruct((S, ML_W), BF16)] * 3 + [jax.ShapeDtypeStruct((S, 128), F32),
                                                                  jax.ShapeDtypeStruct((8, 128), F32)],
        scratch_shapes=[pltpu.VMEM((NH, 128, 128), F32), pltpu.VMEM((NH, 8, 128), F32), pltpu.VMEM((NH, 8, 128), F32)],
        compiler_params=pltpu.CompilerParams(vmem_limit_bytes=VMEM_LIMIT),
    )(qkv, proj_g, gbias, cst, nm, dh)


MESH = pl.DeviceIdType.MESH
ANY = pl.BlockSpec(memory_space=pl.ANY)
N_DEV = 8


def _place():
    return lax.axis_index("x"), lax.axis_index("y"), lax.axis_index("c")


def _block_of(px, py, pc):
    return 4 * px + 2 * py + pc


def _copies_to_all(b_ref, o_ref, send_sems, recv_sems, local_sem):
    x, y, c = _place()
    mine = _block_of(x, y, c)
    copies = [pltpu.make_async_copy(b_ref, o_ref.at[mine], local_sem)]
    for k in range(1, N_DEV):
        peer = (x ^ (k >> 2), y ^ ((k >> 1) & 1), c ^ (k & 1))
        copies.append(pltpu.make_async_remote_copy(
            src_ref=b_ref, dst_ref=o_ref.at[mine], send_sem=send_sems.at[k - 1], recv_sem=recv_sems.at[k - 1],
            device_id=peer, device_id_type=MESH))
    return copies


def _copies_between_chips(p_ref, o_ref, send_sems, recv_sems, local_sem):
    x, y, c = _place()
    mine = 2 * x + y
    copies = [pltpu.make_async_copy(p_ref.at[mine], o_ref.at[mine], local_sem)]
    for k in range(1, 4):
        px, py = x ^ (k >> 1), y ^ (k & 1)
        copies.append(pltpu.make_async_remote_copy(
            src_ref=p_ref.at[2 * px + py], dst_ref=o_ref.at[mine], send_sem=send_sems.at[k - 1],
            recv_sem=recv_sems.at[k - 1], device_id=(px, py, c), device_id_type=MESH))
    return copies


def _around_grid(copies, first, last):
    @pl.when(first)
    def _():
        for cp in copies():
            cp.start()

    @pl.when(last)
    def _():
        for cp in copies():
            cp.wait()


def _inproj_fwd(x, pre_w, w_t, wg_t, blk=None):
    S, D = x.shape
    tm, tn = min(S, 1024), 1152
    ni, nj = S // tm, N_MAIN // tn

    def body(x_ref, pw_ref, w_ref, wg_ref, *rest):
        if blk is None:
            proj_ref, g_ref, u_ref = rest
        else:
            b_ref, proj_ref, g_ref, u_ref, o_ref, send_sems, recv_sems, local_sem = rest
            i, j = pl.program_id(0), pl.program_id(1)
            _around_grid(lambda: _copies_to_all(b_ref, o_ref, send_sems, recv_sems, local_sem),
                         (i == 0) & (j == 0), (i == ni - 1) & (j == nj - 1))

        @pl.when(pl.program_id(1) == 0)
        def _():
            xf = x_ref[...]
            r = lax.rsqrt(jnp.mean(xf * xf, axis=-1, keepdims=True) + EPS)
            u = (xf * r * pw_ref[...]).astype(BF16)
            u_ref[...] = u
            g_ref[...] = _dot_nt(u, wg_ref[...])

        proj_ref[...] = _dot_nt(u_ref[...], w_ref[...])

    in_specs = [pl.BlockSpec((tm, D), lambda i, j: (i, 0)),
                pl.BlockSpec((1, D), lambda i, j: (0, 0)),
                pl.BlockSpec((tn, D), lambda i, j: (j, 0)),
                pl.BlockSpec((128, D), lambda i, j: (0, 0))]
    out_specs = [pl.BlockSpec((tm, tn), lambda i, j: (i, j)),
                 pl.BlockSpec((tm, 128), lambda i, j: (i, 0)),
                 pl.BlockSpec((tm, D), lambda i, j: (i, 0))]
    out_shape = [jax.ShapeDtypeStruct((S, N_MAIN), F32), jax.ShapeDtypeStruct((S, 128), F32),
                 jax.ShapeDtypeStruct((S, D), BF16)]
    operands, scratch = (x, pre_w, w_t, wg_t), []
    if blk is not None:
        in_specs, out_specs, operands = in_specs + [ANY], out_specs + [ANY], operands + (blk,)
        out_shape = out_shape + [jax.ShapeDtypeStruct((N_DEV,) + blk.shape, blk.dtype)]
        scratch = [pltpu.SemaphoreType.DMA((7,)), pltpu.SemaphoreType.DMA((7,)), pltpu.SemaphoreType.DMA]
    return pl.pallas_call(
        body, name="inproj_fwd", grid=(ni, nj), in_specs=in_specs, out_specs=out_specs, out_shape=out_shape,
        scratch_shapes=scratch, compiler_params=_params(("arbitrary", "arbitrary")),
    )(*operands)


def _inproj_bwd(d_main, d_if, w_t, wg_t, x, pre_w, dx_tail, parts=None):
    S, D = x.shape
    tm, tk = min(S, 1024), 1152
    ni, nk = S // tm, N_MAIN // tk

    def body(d_ref, dg_ref, w_ref, wg_ref, x_ref, pw_ref, dt_ref, *rest):
        i, k = pl.program_id(0), pl.program_id(1)
        if parts is None:
            dx_ref, gpw_ref, acc_ref = rest
        else:
            p_ref, dx_ref, gpw_ref, o_ref, acc_ref, send_sems, recv_sems, local_sem = rest
            _around_grid(lambda: _copies_between_chips(p_ref, o_ref, send_sems, recv_sems, local_sem),
                         (i == 0) & (k == 0), (i == ni - 1) & (k == nk - 1))

        @pl.when(k == 0)
        def _():
            acc_ref[...] = _dot(dg_ref[...], wg_ref[...])

        acc_ref[...] += _dot(d_ref[...], w_ref[...])

        @pl.when(k == nk - 1)
        def _():
            xf = x_ref[...]
            r = lax.rsqrt(jnp.mean(xf * xf, axis=-1, keepdims=True) + EPS)
            xn = xf * r
            du = acc_ref[...]
            gw = du * pw_ref[...]
            dx_ref[...] = dt_ref[...] + r * (gw - xn * jnp.mean(gw * xn, axis=-1, keepdims=True))
            part = jnp.sum(du * xn, axis=0, keepdims=True)

            @pl.when(i == 0)
            def _():
                gpw_ref[...] = part

            @pl.when(i > 0)
            def _():
                gpw_ref[...] += part

    in_specs = [pl.BlockSpec((tm, tk), lambda i, k: (i, k)),
                pl.BlockSpec((tm, 128), lambda i, k: (i, 0)),
                pl.BlockSpec((tk, D), lambda i, k: (k, 0)),
                pl.BlockSpec((128, D), lambda i, k: (0, 0)),
                pl.BlockSpec((tm, D), lambda i, k: (i, 0)),
                pl.BlockSpec((1, D), lambda i, k: (0, 0)),
                pl.BlockSpec((tm, D), lambda i, k: (i, 0))]
    out_specs = [pl.BlockSpec((tm, D), lambda i, k: (i, 0)), pl.BlockSpec((1, D), lambda i, k: (0, 0))]
    out_shape = [jax.ShapeDtypeStruct((S, D), F32), jax.ShapeDtypeStruct((1, D), F32)]
    operands, scratch = (d_main, d_if, w_t, wg_t, x, pre_w, dx_tail), [pltpu.VMEM((tm, D), F32)]
    if parts is not None:
        in_specs, out_specs, operands = in_specs + [ANY], out_specs + [ANY], operands + (parts,)
        out_shape = out_shape + [jax.ShapeDtypeStruct(parts.shape, parts.dtype)]
        scratch = scratch + [pltpu.SemaphoreType.DMA((3,)), pltpu.SemaphoreType.DMA((3,)), pltpu.SemaphoreType.DMA]
    return pl.pallas_call(
        body, name="inproj_bwd", grid=(ni, nk), in_specs=in_specs, out_specs=out_specs, out_shape=out_shape,
        scratch_shapes=scratch, compiler_params=_params(("arbitrary", "arbitrary")),
    )(*operands)


def _matmul_tn(a, b, name):
    S, M = a.shape
    N = b.shape[1]
    tmm = 1152 if M % 1152 == 0 else min(M, 1024)
    tk = min(S, 1024)
    nk = S // tk

    def body(a_ref, b_ref, o_ref):
        part = _dot_tn(a_ref[...].astype(BF16), b_ref[...].astype(BF16))

        @pl.when(pl.program_id(1) == 0)
        def _():
            o_ref[...] = part

        @pl.when(pl.program_id(1) > 0)
        def _():
            o_ref[...] += part

    return pl.pallas_call(
        body, name=name, grid=(M // tmm, nk),
        in_specs=[pl.BlockSpec((tk, tmm), lambda i, k: (k, i)),
                  pl.BlockSpec((tk, N), lambda i, k: (k, 0))],
        out_specs=pl.BlockSpec((tmm, N), lambda i, k: (i, 0)),
        out_shape=jax.ShapeDtypeStruct((M, N), F32),
        compiler_params=_params(("arbitrary", "arbitrary")),
    )(a, b)


def _half_mean(v, low):
    s_lo = jnp.sum(jnp.where(low, v, 0.0), axis=1, keepdims=True)
    s_hi = jnp.sum(jnp.where(low, 0.0, v), axis=1, keepdims=True)
    return jnp.where(low, s_lo, s_hi) * (1.0 / 64.0)


def _silu_grad(z, s):
    return s * (1.0 + z * (1.0 - s))


def _tail(y_sb, h_ml, proj, x, p, target, sb_nw, ml_nw, w_out, post_w, w_gate, b_gate, w_up):
    S, D = x.shape
    tm = 256

    def body(ysb_ref, hml_ref, sbz_ref, mlo_ref, mlz_ref, x_ref, p_ref, tg_ref, sbw_ref, mlw_ref, wo_ref, pw_ref,
             wg_ref, bg_ref, wu_ref,
             dx_ref, dysb_ref, dhml_ref, dsbz_ref, dmlo_ref, dmlz_ref, mix_ref, dy_ref, h1_ref, dgp_ref, dpu_ref,
             small_ref):
        lane = lax.broadcasted_iota(jnp.int32, (tm, 128), 1)
        low = lane < 64
        sb_saved, ml_saved, mixed = [], [], []
        for s in range(4):
            sl = slice(128 * s, 128 * s + 128)
            y = ysb_ref[:, sl]
            rs = lax.rsqrt(_half_mean(y * y, low) + EPS)
            n = y * rs
            z = sbz_ref[:, sl]
            sg = jax.nn.sigmoid(z)
            w = sbw_ref[:, sl]
            mixed.append((n * w) * (z * sg))
            sb_saved.append((rs, n, z, sg, w))
        for s in range(4):
            sl = slice(128 * s, 128 * s + 128)
            og = jax.nn.sigmoid(mlo_ref[:, sl])
            hh = hml_ref[:, sl]
            t = og * hh
            rs = lax.rsqrt(jnp.mean(t * t, axis=1, keepdims=True) + EPS)
            n = t * rs
            z = mlz_ref[:, sl]
            sg = jax.nn.sigmoid(z)
            w = mlw_ref[:, sl]
            mixed.append((n * w) * (z * sg))
            ml_saved.append((rs, n, z, sg, w, og, hh))
        mix = jnp.concatenate(mixed, axis=1).astype(BF16)
        mix_ref[...] = mix
        y = _dot(mix, wo_ref[...])
        rs_y = lax.rsqrt(jnp.mean(y * y, axis=1, keepdims=True) + EPS)
        yn = y * rs_y
        pw = pw_ref[...]
        h1 = x_ref[...] + yn * pw
        h1b = h1.astype(BF16)
        h1_ref[...] = h1b
        gate = jax.nn.sigmoid(_dot(h1b, wg_ref[...]) + bg_ref[...])
        pu = _dot(p_ref[...].astype(BF16), wu_ref[...])
        err = (h1 + gate * pu) - tg_ref[...]
        loss = 0.5 * jnp.sum(jnp.sum(err * err, axis=1, keepdims=True) * (1.0 / D))
        d_out = err * (1.0 / D)
        dpu_ref[...] = (d_out * gate).astype(BF16)
        dgp = (d_out * pu) * (gate * (1.0 - gate))
        dgpb = dgp.astype(BF16)
        dgp_ref[...] = dgpb
        d_h1 = d_out + _dot_nt(dgpb, wg_ref[...])
        dx_ref[...] = d_h1
        gwy = d_h1 * pw
        d_y = rs_y * (gwy - yn * jnp.mean(gwy * yn, axis=1, keepdims=True))
        d_yb = d_y.astype(BF16)
        dy_ref[...] = d_yb
        d_mix = _dot_nt(d_yb, wo_ref[...])
        g_nw = []
        for s in range(4):
            sl = slice(128 * s, 128 * s + 128)
            rs, n, z, sg, w = sb_saved[s]
            da = d_mix[:, sl]
            act = z * sg
            dsbz_ref[:, sl] = (da * (n * w) * _silu_grad(z, sg)).astype(BF16)
            dn = da * w * act
            g_nw.append(jnp.sum(da * act * n, axis=0, keepdims=True))
            dysb_ref[:, sl] = rs * (dn - n * _half_mean(dn * n, low))
        for s in range(4):
            sl = slice(128 * s, 128 * s + 128)
            rs, n, z, sg, w, og, hh = ml_saved[s]
            da = d_mix[:, 512 + 128 * s:512 + 128 * s + 128]
            act = z * sg
            dmlz_ref[:, sl] = (da * (n * w) * _silu_grad(z, sg)).astype(BF16)
            dn = da * w * act
            g_nw.append(jnp.sum(da * act * n, axis=0, keepdims=True))
            dt = rs * (dn - n * jnp.mean(dn * n, axis=1, keepdims=True))
            dmlo_ref[:, sl] = (dt * hh * (og * (1.0 - og))).astype(BF16)
            dhml_ref[:, sl] = dt * og
        upd = jnp.concatenate([
            jnp.sum(d_h1 * yn, axis=0, keepdims=True),
            jnp.sum(dgp, axis=0, keepdims=True),
            jnp.concatenate(g_nw, axis=1),
            jnp.full((1, D), loss, F32),
            jnp.zeros((4, D), F32)], axis=0)

        @pl.when(pl.program_id(0) == 0)
        def _():
            small_ref[...] = upd

        @pl.when(pl.program_id(0) > 0)
        def _():
            small_ref[...] += upd

    def rows(width, col=0):
        return pl.BlockSpec((tm, width), lambda i: (i, col))

    def whole(a):
        return pl.BlockSpec(a.shape, lambda i: (0, 0))

    return pl.pallas_call(
        body, name="tail", grid=(S // tm,),
        in_specs=[rows(512), rows(512), rows(512, 3), rows(512, 7), rows(512, 8), rows(D), rows(256), rows(D),
                  whole(sb_nw), whole(ml_nw), whole(w_out), whole(post_w), whole(w_gate), whole(b_gate), whole(w_up)],
        out_specs=[rows(D), rows(512), rows(512), rows(512), rows(512), rows(512), rows(D), rows(D), rows(D), rows(D),
                   rows(D), pl.BlockSpec((8, D), lambda i: (0, 0))],
        out_shape=[jax.ShapeDtypeStruct((S, D), F32), jax.ShapeDtypeStruct((S, 512), F32),
                   jax.ShapeDtypeStruct((S, 512), F32)] + [jax.ShapeDtypeStruct((S, 512), BF16)] * 3
        + [jax.ShapeDtypeStruct((S, D), BF16)] * 5 + [jax.ShapeDtypeStruct((8, D), F32)],
        compiler_params=_params(("arbitrary",)),
    )(y_sb, h_ml, proj, proj, proj, x, p, target, sb_nw, ml_nw, w_out, post_w, w_gate, b_gate, w_up)


def _local_step(x, p, target, pre_w, w_t, wg_t, conv_w, conv_b, gbias, sb_nw, ml_nw, post_w, b_gate, late,
                exchange=None):
    if callable(late[1]):
        proj, proj_g, u, gathered = _inproj_fwd(x, pre_w, w_t, wg_t, late[0])
        w_out, w_gate, w_up = late[1](gathered)
    else:
        proj, proj_g, u = _inproj_fwd(x, pre_w, w_t, wg_t)
        w_out, w_gate, w_up = late
    y_sb, tot = _sb_fwd(proj)
    qkv = _ml_prep(proj, conv_w, conv_b)
    h_ml, cst, nm = _ml_fwd(qkv, proj_g, gbias)
    dx_tail, d_ysb, d_hml, d_sbz, d_mlo, d_mlz, mix, d_y, h1, dgp, dpu, small = _tail(
        y_sb, h_ml, proj, x, p, target, sb_nw, ml_nw, w_out, post_w, w_gate, b_gate, w_up)
    dq, dk, dv = _sb_bwd(proj, tot, d_ysb)
    dqc, dks, dmlv, dif, gif = _ml_bwd(qkv, proj_g, gbias, cst, nm, d_hml)
    dmlqk, g_cw, g_cb = _ml_prep_bwd(proj, conv_w, conv_b, dqc, dks)
    d_main = jnp.concatenate([dq.astype(BF16), dk.astype(BF16), dv.astype(BF16), d_sbz, dmlqk.astype(BF16),
                              dmlv, d_mlo, d_mlz], axis=1)
    d_if = dif.astype(BF16)
    grads = dict(
        w_t=_matmul_tn(d_main, u, "gw_in"), wg_t=_matmul_tn(d_if, u, "gw_in_gates"),
        w_out=_matmul_tn(mix, d_y, "gw_out"), w_gate=_matmul_tn(h1, dgp, "gw_gate"), w_up=_matmul_tn(p, dpu, "gw_up"),
        conv_w=g_cw, conv_b=g_cb, gif=gif)
    parts = exchange(grads) if exchange else None
    return _inproj_bwd(d_main, d_if, w_t, wg_t, x, pre_w, dx_tail, parts), grads, small


def _all_gather(a, b):
    def body(a_ref, b_ref, oa_ref, ob_ref, send_sems, recv_sems, local_sems):
        x, y, c = _place()
        me, sibling = (x, y, c), (x, y, 1 - c)
        chips = [(1 - x, y), (x, 1 - y), (1 - x, 1 - y)]
        pairs = ((a_ref, oa_ref), (b_ref, ob_ref))

        def copies(k, block, to, from_input=False):
            slot = _block_of(*block)
            return [pltpu.make_async_remote_copy(
                src_ref=src if from_input else out.at[slot], dst_ref=out.at[slot],
                send_sem=send_sems.at[t, k], recv_sem=recv_sems.at[t, k], device_id=to, device_id_type=MESH)
                for t, (src, out) in enumerate(pairs)]

        mine = [pltpu.make_async_copy(src, out.at[_block_of(*me)], local_sems.at[t])
                for t, (src, out) in enumerate(pairs)]
        for cp in mine:
            cp.start()
        first = copies(0, me, sibling, True)
        for j, chip in enumerate(chips):
            first += copies(1 + j, me, (*chip, c), True)
        for cp in first:
            cp.start()
        passed = []
        for j, chip in enumerate(chips):
            for cp in copies(1 + j, (*chip, c), me):
                cp.wait_recv()
            fwd = copies(4 + j, (*chip, c), sibling)
            for cp in fwd:
                cp.start()
            passed += fwd
        for cp in copies(0, sibling, me):
            cp.wait_recv()
        for j, chip in enumerate(chips):
            for cp in copies(4 + j, (*chip, 1 - c), me):
                cp.wait_recv()
        for cp in first + passed:
            cp.wait_send()
        for cp in mine:
            cp.wait()

    return pl.pallas_call(
        body, name="all_gather",
        in_specs=[ANY, ANY], out_specs=[ANY, ANY],
        out_shape=[jax.ShapeDtypeStruct((N_DEV,) + a.shape, a.dtype), jax.ShapeDtypeStruct((N_DEV,) + b.shape, b.dtype)],
        scratch_shapes=[pltpu.SemaphoreType.DMA((2, 7)), pltpu.SemaphoreType.DMA((2, 7)), pltpu.SemaphoreType.DMA((2,))],
    )(a, b)


def _exchange_pair(g):
    def body(g_ref, og_ref, send_sems, recv_sems):
        x, y, c = _place()
        sent = [pltpu.make_async_remote_copy(
            src_ref=g_ref.at[k, 1 - c], dst_ref=og_ref.at[k], send_sem=send_sems.at[k], recv_sem=recv_sems.at[k],
            device_id=(x, y, 1 - c), device_id_type=MESH) for k in range(4)]
        for cp in sent:
            cp.start()
        for cp in sent:
            cp.wait()

    return pl.pallas_call(
        body, name="exchange_pair", in_specs=[ANY], out_specs=ANY,
        out_shape=jax.ShapeDtypeStruct((4,) + g.shape[2:], g.dtype),
        scratch_shapes=[pltpu.SemaphoreType.DMA((4,)), pltpu.SemaphoreType.DMA((4,))],
    )(g)


def _pair_sum(g, r, core, tr):
    _, _, R, D = g.shape

    def body(c_ref, g_ref, r_ref, o_ref):
        o_ref[...] = (g_ref[...] + r_ref[...]).astype(BF16)

    return pl.pallas_call(
        body, name="pair_sum",
        grid_spec=pltpu.PrefetchScalarGridSpec(
            num_scalar_prefetch=1, grid=(4, R // tr),
            in_specs=[pl.BlockSpec((None, None, tr, D), lambda k, i, c: (k, c[0], i, 0)),
                      pl.BlockSpec((None, tr, D), lambda k, i, c: (k, i, 0))],
            out_specs=pl.BlockSpec((None, tr, D), lambda k, i, c: (k, i, 0))),
        out_shape=jax.ShapeDtypeStruct((4, R, D), BF16),
        compiler_params=_params(("arbitrary", "arbitrary")),
    )(core, g, r)


ADAM_LR, ADAM_B1, ADAM_B2, ADAM_EPS, ADAM_WD, ADAM_STEP = 0.001, 0.9, 0.999, 1e-08, 0.01, 10


def _adamw(w, g, m, v):
    m = ADAM_B1 * m + (1.0 - ADAM_B1) * g
    v = ADAM_B2 * v + (1.0 - ADAM_B2) * (g * g)
    m_hat = m / (1.0 - ADAM_B1 ** ADAM_STEP)
    v_hat = v / (1.0 - ADAM_B2 ** ADAM_STEP)
    return -ADAM_LR * (m_hat / (jnp.sqrt(v_hat) + ADAM_EPS) + ADAM_WD * w), m, v


def _adam(parts, w, m, v, tr, name, small=None):
    R, D = w.shape
    n = parts.shape[0]
    steps = R // tr

    def body(p_ref, w_ref, m_ref, v_ref, *rest):
        if small is None:
            g_ref, d_ref, nm_ref, nv_ref = rest
        else:
            s_ref, g_ref, d_ref, nm_ref, nv_ref, o_ref, send_sems, recv_sems, local_sem = rest
            i = pl.program_id(0)
            _around_grid(lambda: _copies_to_all(s_ref, o_ref, send_sems, recv_sems, local_sem), i == 0, i == steps - 1)
        g = p_ref[0].astype(F32)
        for k in range(1, n):
            g = g + p_ref[k].astype(F32)
        g_ref[...] = g
        d_ref[...], nm_ref[...], nv_ref[...] = _adamw(w_ref[...], g, m_ref[...], v_ref[...])

    blk = pl.BlockSpec((tr, D), lambda i: (i, 0))
    in_specs = [pl.BlockSpec((n, tr, D), lambda i: (0, i, 0)), blk, blk, blk]
    out_specs, out_shape = [blk] * 4, [jax.ShapeDtypeStruct((R, D), F32)] * 4
    operands, scratch = (parts, w, m, v), []
    if small is not None:
        in_specs, out_specs, operands = in_specs + [ANY], out_specs + [ANY], operands + (small,)
        out_shape = out_shape + [jax.ShapeDtypeStruct((N_DEV,) + small.shape, small.dtype)]
        scratch = [pltpu.SemaphoreType.DMA((7,)), pltpu.SemaphoreType.DMA((7,)), pltpu.SemaphoreType.DMA]
    return pl.pallas_call(
        body, name=name, grid=(steps,), in_specs=in_specs, out_specs=out_specs, out_shape=out_shape,
        scratch_shapes=scratch, compiler_params=_params(("arbitrary",)),
    )(*operands)


ROWS_IN = 592
ROWS_BF16 = ROWS_IN + 128 + 128 + 32
ROWS_CONV = 16
ROWS_ALL = ROWS_BF16 + ROWS_CONV
ROW_TILE = 224


def _pad_rows(a, rows):
    return jnp.pad(a, ((0, rows - a.shape[0]), (0, 0)))


def _pack_shards(w_in, w_out, w_gate, w_up, conv_w):
    return jnp.concatenate([
        _pad_rows(w_in[0].T, ROWS_IN), w_out[0], w_gate[0], w_up[0].reshape(32, D_MODEL),
        _pad_rows(jnp.pad(conv_w[0].reshape(1, 512), ((0, 0), (0, 512))), ROWS_CONV)], axis=0)


def _unpack_shards(a):
    return (a[:SHARD_IN].T[None], a[ROWS_IN:ROWS_IN + 128][None], a[ROWS_IN + 128:ROWS_IN + 256][None],
            a[ROWS_IN + 256:ROWS_BF16].reshape(1, 256, 128), a[ROWS_BF16, :512].reshape(1, 4, 128))


def _pack_small(pre_w, conv_b, i_bias, f_bias, sb_nw, ml_nw, post_w, b_gate):
    gates = jnp.pad(jnp.concatenate([i_bias, f_bias], axis=1), ((0, 0), (0, D_MODEL - 8)))
    return jnp.concatenate([post_w, b_gate, jnp.concatenate([sb_nw, ml_nw], axis=1), jnp.zeros((1, D_MODEL), F32),
                            pre_w, conv_b, gates, jnp.zeros((1, D_MODEL), F32)], axis=0)


def _unpack_small(a):
    return a[4:5], a[5:6], a[6:7, 0:4], a[6:7, 4:8], a[2:3, :512], a[2:3, 512:], a[0:1], a[1:2]


def kernel(x, p, pre_norm_w, w_in, ml_conv_w, ml_conv_b, ml_i_bias, ml_f_bias, sb_norm_w, ml_norm_w, w_out, post_norm_w, ple_w_up, ple_w_gate, ple_b_gate, loss_target, m_pre_norm_w, m_w_in, m_ml_conv_w, m_ml_conv_b, m_ml_i_bias, m_ml_f_bias, m_sb_norm_w, m_ml_norm_w, m_w_out, m_post_norm_w, m_ple_w_up, m_ple_w_gate, m_ple_b_gate, v_pre_norm_w, v_w_in, v_ml_conv_w, v_ml_conv_b, v_ml_i_bias, v_ml_f_bias, v_sb_norm_w, v_ml_norm_w, v_w_out, v_post_norm_w, v_ple_w_up, v_ple_w_gate, v_ple_b_gate):
    D = D_MODEL
    w_pk = _pack_shards(w_in, w_out, ple_w_gate, ple_w_up, ml_conv_w)
    m_pk = _pack_shards(m_w_in, m_w_out, m_ple_w_gate, m_ple_w_up, m_ml_conv_w)
    v_pk = _pack_shards(v_w_in, v_w_out, v_ple_w_gate, v_ple_w_up, v_ml_conv_w)
    w_sm = _pack_small(pre_norm_w, ml_conv_b, ml_i_bias, ml_f_bias, sb_norm_w, ml_norm_w, post_norm_w, ple_b_gate)
    m_sm = _pack_small(m_pre_norm_w, m_ml_conv_b, m_ml_i_bias, m_ml_f_bias, m_sb_norm_w, m_ml_norm_w, m_post_norm_w, m_ple_b_gate)
    v_sm = _pack_small(v_pre_norm_w, v_ml_conv_b, v_ml_i_bias, v_ml_f_bias, v_sb_norm_w, v_ml_norm_w, v_post_norm_w, v_ple_b_gate)

    w_bf = w_pk[:ROWS_BF16].astype(BF16)
    ga, gb = _all_gather(w_bf[:ROWS_IN], w_pk[ROWS_BF16:])
    w_in_t = ga[:, :SHARD_IN].reshape(N_IN, D)
    wg_t = _pad_rows(w_in_t[N_MAIN:], 128)
    conv_w_f = gb[:, 0, :512].reshape(N_DEV, 4, 128).transpose(1, 0, 2).reshape(4, D)
    gbias = jnp.pad(jnp.concatenate([ml_i_bias, ml_f_bias], axis=1), ((0, 0), (0, 120)))

    def unpack_late(gl):
        return (gl[:, :128].reshape(D, D), gl[:, 128:256].reshape(D, D),
                gl[:, 256:].reshape(N_DEV, 256, 128).transpose(1, 0, 2).reshape(256, D))

    def exchange(g):
        g_in = jnp.concatenate([g["w_t"], g["wg_t"][:8]], axis=0).reshape(N_DEV, SHARD_IN, D)
        g_blocks = jnp.concatenate([
            jnp.pad(g_in, ((0, 0), (0, ROWS_IN - SHARD_IN), (0, 0))),
            g["w_out"].reshape(N_DEV, 128, D), g["w_gate"].reshape(N_DEV, 128, D),
            g["w_up"].reshape(256, N_DEV, 128).transpose(1, 0, 2).reshape(N_DEV, 32, D),
            jnp.pad(g["conv_w"].reshape(4, N_DEV, 128).transpose(1, 0, 2).reshape(N_DEV, 1, 512),
                    ((0, 0), (0, ROWS_CONV - 1), (0, 512))),
        ], axis=1).reshape(4, 2, ROWS_ALL, D)
        core = lax.axis_index("c").astype(jnp.int32).reshape(1)
        return _pair_sum(g_blocks, _exchange_pair(g_blocks), core, ROW_TILE)

    (dx, g_pre, parts), g, small = _local_step(
        x[0], p[0, 0], loss_target[0], pre_norm_w, w_in_t, wg_t, conv_w_f, ml_conv_b, gbias, sb_norm_w, ml_norm_w,
        post_norm_w, ple_b_gate, (w_bf[ROWS_IN:], unpack_late), exchange)

    g_small = jnp.concatenate([small[0:4], g_pre, g["conv_b"], jnp.pad(g["gif"][0:1], ((0, 0), (0, D - 128))),
                               jnp.zeros((1, D), F32)], axis=0)
    grad_pk, delta_pk, nm_pk, nv_pk, parts_sm = _adam(parts, w_pk, m_pk, v_pk, ROW_TILE, "adam", g_small)
    grad_sm, delta_sm, nm_sm, nv_sm = _adam(parts_sm, w_sm, m_sm, v_sm, 8, "adam_small")
    loss = grad_sm[3, 0]

    def ordered(pk, sm):
        win, wout, wgate, wup, convw = _unpack_shards(pk)
        pre_w, conv_b, i_b, f_b, sb_nw, ml_nw, post_w, b_gate = _unpack_small(sm)
        return [pre_w, win, convw, conv_b, i_b, f_b, sb_nw, ml_nw, wout, post_w, wup, wgate, b_gate]

    return (loss, dx[None], *ordered(grad_pk, grad_sm), *ordered(delta_pk, delta_sm), *ordered(nm_pk, nm_sm),
            *ordered(nv_pk, nv_sm))
```

```python
import functools

import jax
import jax.numpy as jnp
from jax import lax
from jax.experimental import pallas as pl
from jax.experimental.pallas import tpu as pltpu

F32 = jnp.float32
BF16 = jnp.bfloat16
EPS = 1e-6
D_MODEL = 1024
SB_W = 512
ML_W = 512
N_MAIN = 4608
N_IN = 4616
SHARD_IN = 577
SHARD_IN_PAD = 584
TQ = 1024
TK = 256
ND = TQ // TK
LCH = 128
VMEM_LIMIT = 56 * 1024 * 1024


def _dot(a, b):
    return jnp.dot(a, b, preferred_element_type=F32)


def _dot_nt(a, b):
    return lax.dot_general(a, b, (((1,), (1,)), ((), ())), preferred_element_type=F32)


def _dot_tn(a, b):
    return lax.dot_general(a, b, (((0,), (0,)), ((), ())), preferred_element_type=F32)


def _split2(x):
    hi = x.astype(BF16)
    lo = (x - hi.astype(F32)).astype(BF16)
    return hi, lo


def _split3(x):
    hi = x.astype(BF16)
    r = x - hi.astype(F32)
    mid = r.astype(BF16)
    lo = (r - mid.astype(F32)).astype(BF16)
    return hi, mid, lo


def _params(sem):
    return pltpu.CompilerParams(dimension_semantics=sem, vmem_limit_bytes=VMEM_LIMIT)


def _log_sigmoid_parts(z):
    e = jnp.exp(-jnp.abs(z))
    return jnp.minimum(z, 0.0) - jnp.log(1.0 + e)


def _neg_log_sigmoid(nz):
    nz = jnp.minimum(nz, 80.0)
    sp = jnp.log(1.0 + jnp.exp(nz))
    return sp, nz - sp


def _sb_fwd(proj):
    S = proj.shape[0]
    nq = S // TQ

    def body(q_ref, k_ref, v_ref, y_ref, t_ref, acc_ref, car_ref, zs_ref):
        i = pl.program_id(1)
        low = lax.broadcasted_iota(jnp.int32, (TQ, 128), 1) < 64
        row = lax.broadcasted_iota(jnp.int32, (TK, TK), 0)
        col = lax.broadcasted_iota(jnp.int32, (TK, TK), 1)
        uo = (row > col).astype(BF16)
        uo = jnp.concatenate([uo, uo], axis=0)
        q = q_ref[...] * 0.125
        qh = (jnp.where(low, -q, 0.0).astype(BF16), jnp.where(low, 0.0, -q).astype(BF16))
        acc_ref[...] = jnp.zeros_like(acc_ref)
        car_ref[...] = jnp.zeros_like(car_ref)

        def block(j, r0):
            diag = r0 is not None
            r0 = r0 or 0
            if diag:
                strict = (lax.broadcasted_iota(jnp.int32, (TQ - r0, TK), 1)
                          < lax.broadcasted_iota(jnp.int32, (TQ - r0, TK), 0))
            rows = pl.ds(pl.multiple_of(j * TK, TK), TK)
            kb = k_ref[rows, :].astype(BF16)
            vb = v_ref[rows, :].astype(BF16)
            if not diag:
                kn = k_ref[pl.ds(pl.multiple_of(jnp.maximum(j - 1, 0) * TK, TK), TK), :].astype(BF16)

            def first(h):
                if diag:
                    nz = _dot_nt(qh[h][r0:], kb)
                else:
                    nz = zs_ref[h]
                    zs_ref[h] = _dot_nt(qh[h], kn)
                sp, lk = _neg_log_sigmoid(nz)
                if diag:
                    lk = jnp.where(strict, lk, 0.0)
                hi, lo = _split2(lk)
                return sp, lk[:, 0:1], _dot(jnp.concatenate([hi, lo], axis=1), uo)

            def second(h, sp, lk0, rr):
                car = car_ref[h, r0:, :]
                a = jnp.exp((jnp.concatenate([car, car], axis=1) + rr) - sp)
                if diag:
                    a = jnp.where(strict, a, 0.0)
                acc_ref[h, r0:, :] += _dot(a.astype(BF16), vb)
                car_ref[h, r0:, :] = car + jnp.broadcast_to(rr[:, 0:1] + lk0, car.shape)

            if diag:
                halves = [first(h) for h in range(2)]
                for h in range(2):
                    second(h, *halves[h])
            else:
                for h in range(2):
                    second(h, *first(h))

        for d in reversed(range(ND)):
            block(ND * i + d, TK * d)

        @pl.when(i > 0)
        def _():
            k0 = k_ref[pl.ds(pl.multiple_of((ND * i - 1) * TK, TK), TK), :].astype(BF16)
            for h in range(2):
                zs_ref[h] = _dot_nt(qh[h], k0)

        def loop(n, c):
            block(ND * i - 1 - n, None)
            return c

        lax.fori_loop(0, ND * i, loop, 0)
        y_ref[...] = jnp.where(low, acc_ref[0], acc_ref[1])
        t_ref[...] = jnp.where(low, car_ref[0], car_ref[1])

    return pl.pallas_call(
        body, name="sb_fwd", grid=(4, nq),
        in_specs=[pl.BlockSpec((TQ, 128), lambda p, i: (i, p)),
                  pl.BlockSpec((S, 128), lambda p, i: (0, 4 + p)),
                  pl.BlockSpec((S, 128), lambda p, i: (0, 8 + p))],
        out_specs=[pl.BlockSpec((TQ, 128), lambda p, i: (i, p)),
                   pl.BlockSpec((TQ, 128), lambda p, i: (i, p))],
        out_shape=[jax.ShapeDtypeStruct((S, SB_W), F32), jax.ShapeDtypeStruct((S, SB_W), F32)],
        scratch_shapes=[pltpu.VMEM((2, TQ, 128), F32), pltpu.VMEM((2, TQ, 128), F32), pltpu.VMEM((2, TQ, TK), F32)],
        compiler_params=_params(("arbitrary", "arbitrary")),
    )(proj, proj, proj)


def _sb_bwd(proj, tot, dy):
    S = proj.shape[0]
    nq = S // TQ

    def body(q_ref, k_ref, v_ref, t_ref, dy_ref, dq_ref, dk_ref, dv_ref, dqa_ref, cp_ref, cg_ref, dkt_ref, dvt_ref):
        i = pl.program_id(1)
        low = lax.broadcasted_iota(jnp.int32, (TQ, 128), 1) < 64
        row = lax.broadcasted_iota(jnp.int32, (TK, TK), 0)
        col = lax.broadcasted_iota(jnp.int32, (TK, TK), 1)
        u_inc = (row <= col).astype(BF16)
        u_inc = jnp.concatenate([u_inc, u_inc], axis=0)
        u_exc = (row < col).astype(BF16)
        q = q_ref[...] * 0.125
        qh = (jnp.where(low, -q, 0.0).astype(BF16), jnp.where(low, 0.0, -q).astype(BF16))
        dy_ = dy_ref[...]
        dyh = (jnp.where(low, dy_, 0.0).astype(BF16), jnp.where(low, 0.0, dy_).astype(BF16))
        qt = tuple(jnp.transpose(x) for x in qh)
        dyt = tuple(jnp.transpose(x) for x in dyh)
        t_ = t_ref[...]
        t_sw = pltpu.roll(t_, 64, 1)
        th = (jnp.where(low, t_, t_sw), jnp.where(low, t_sw, t_))
        dqa_ref[...] = jnp.zeros_like(dqa_ref)
        cp_ref[...] = jnp.zeros_like(cp_ref)
        cg_ref[...] = jnp.zeros_like(cg_ref)

        @pl.when(i == 0)
        def _():
            dkt_ref[...] = jnp.zeros_like(dkt_ref)
            dvt_ref[...] = jnp.zeros_like(dvt_ref)

        def block(j, r0):
            diag = r0 is not None
            r0 = r0 or 0
            if diag:
                strict = (lax.broadcasted_iota(jnp.int32, (TQ - r0, TK), 1)
                          < lax.broadcasted_iota(jnp.int32, (TQ - r0, TK), 0))
            rows = pl.ds(pl.multiple_of(j * TK, TK), TK)
            kb = k_ref[rows, :].astype(BF16)
            vb = v_ref[rows, :].astype(BF16)
            dk_acc = jnp.zeros((128, TK), F32)
            dv_acc = jnp.zeros((128, TK), F32)
            for h in range(2):
                qr, dyr = qh[h][r0:], dyh[h][r0:]
                sp, lk = _neg_log_sigmoid(_dot_nt(qr, kb))
                if diag:
                    lk = jnp.where(strict, lk, 0.0)
                hi, lo = _split2(lk)
                pp = _dot(jnp.concatenate([hi, lo], axis=1), u_inc)
                cp, cg = cp_ref[h, r0:, :], cg_ref[h, r0:, :]
                rest = th[h][r0:] - cp
                a = jnp.exp((jnp.concatenate([rest, rest], axis=1) - pp) - sp)
                if diag:
                    a = jnp.where(strict, a, 0.0)
                g = _dot_nt(dyr, vb) * a
                gg = _dot(g.astype(BF16), u_exc)
                beta = jnp.exp(-sp)
                dz = g - beta * (g + (jnp.concatenate([cg, cg], axis=1) + gg))
                if diag:
                    dz = jnp.where(strict, dz, 0.0)
                dzb = dz.astype(BF16)
                dqa_ref[h, r0:, :] += _dot(dzb, kb)
                dk_acc += _dot(qt[h][:, r0:], dzb)
                dv_acc += _dot(dyt[h][:, r0:], a.astype(BF16))
                cp_ref[h, r0:, :] = cp + jnp.broadcast_to(pp[:, TK - 1:TK], cp.shape)
                cg_ref[h, r0:, :] = cg + jnp.broadcast_to(gg[:, TK - 1:TK] + g[:, TK - 1:TK], cg.shape)
            dkt_ref[:, rows] -= dk_acc
            dvt_ref[:, rows] += dv_acc

        def loop(j, c):
            block(j, None)
            return c

        lax.fori_loop(0, ND * i, loop, 0)
        for d in range(ND):
            block(ND * i + d, TK * d)
        dq_ref[...] = jnp.where(low, dqa_ref[0], dqa_ref[1]) * 0.125

        @pl.when(i == nq - 1)
        def _():
            dk_ref[...] = jnp.transpose(dkt_ref[...])
            dv_ref[...] = jnp.transpose(dvt_ref[...])

    return pl.pallas_call(
        body, name="sb_bwd", grid=(4, nq),
        in_specs=[pl.BlockSpec((TQ, 128), lambda p, i: (i, p)),
                  pl.BlockSpec((S, 128), lambda p, i: (0, 4 + p)),
                  pl.BlockSpec((S, 128), lambda p, i: (0, 8 + p)),
                  pl.BlockSpec((TQ, 128), lambda p, i: (i, p)),
                  pl.BlockSpec((TQ, 128), lambda p, i: (i, p))],
        out_specs=[pl.BlockSpec((TQ, 128), lambda p, i: (i, p)),
                   pl.BlockSpec((S, 128), lambda p, i: (0, p)),
                   pl.BlockSpec((S, 128), lambda p, i: (0, p))],
        out_shape=[jax.ShapeDtypeStruct((S, SB_W), F32)] * 3,
        scratch_shapes=[pltpu.VMEM((2, TQ, 128), F32)] * 3 + [pltpu.VMEM((128, S), F32)] * 2,
        compiler_params=_params(("arbitrary", "arbitrary")),
    )(proj, proj, proj, tot, dy)


ML_SCALE = 128 ** -0.5
RC = 256


def _conv_taps(cur, prev8, w):
    n = cur.shape[0]
    win = jnp.concatenate([prev8, cur], axis=0)
    out = w[3:4, :] * cur
    for j in range(3):
        out = out + w[j:j + 1, :] * pltpu.roll(win, 3 - j, 0)[8:8 + n]
    return out


def _ml_prep(proj, conv_w, conv_b):
    S = proj.shape[0]

    def body(x_ref, w_ref, b_ref, o_ref):
        c = pl.program_id(0)
        scale = jnp.where(c < 4, 1.0, ML_SCALE).astype(F32)
        w = w_ref[...]
        b = b_ref[...]

        @pl.when(c < 8)
        def _():
            for n in range(S // RC):
                cur = x_ref[n * RC:(n + 1) * RC, :]
                prev8 = x_ref[n * RC - 8:n * RC, :] if n else jnp.zeros((8, 128), F32)
                pre = b + _conv_taps(cur, prev8, w)
                o_ref[n * RC:(n + 1) * RC, :] = (pre * jax.nn.sigmoid(pre) * scale).astype(BF16)

        @pl.when(c >= 8)
        def _():
            o_ref[...] = x_ref[...].astype(BF16)

    return pl.pallas_call(
        body, name="ml_prep", grid=(12,),
        in_specs=[pl.BlockSpec((S, 128), lambda c: (0, 16 + c)),
                  pl.BlockSpec((4, 128), lambda c: (0, jnp.minimum(c, 7))),
                  pl.BlockSpec((1, 128), lambda c: (0, jnp.minimum(c, 7)))],
        out_specs=pl.BlockSpec((S, 128), lambda c: (0, c)),
        out_shape=jax.ShapeDtypeStruct((S, 1536), BF16),
        compiler_params=_params(("arbitrary",)),
    )(proj, conv_w, conv_b)


def _ml_prep_bwd(proj, conv_w, conv_b, dq, dk):
    S = proj.shape[0]

    def body(x_ref, w_ref, b_ref, dq_ref, dk_ref, dx_ref, gw_ref, gb_ref, dp_ref):
        c = pl.program_id(0)
        w = w_ref[...]
        b = b_ref[...]
        gw = [jnp.zeros((1, 128), F32) for _ in range(4)]
        gb = jnp.zeros((1, 128), F32)
        for n in range(S // RC):
            rows = slice(n * RC, (n + 1) * RC)
            cur = x_ref[rows, :]
            prev8 = x_ref[n * RC - 8:n * RC, :] if n else jnp.zeros((8, 128), F32)
            pre = b + _conv_taps(cur, prev8, w)
            s = jax.nn.sigmoid(pre)
            dpost = jnp.where(c < 4, dq_ref[rows, :].astype(F32), dk_ref[rows, :].astype(F32) * ML_SCALE)
            dpre = dpost * (s * (1.0 + pre * (1.0 - s)))
            dp_ref[rows, :] = dpre
            win = jnp.concatenate([prev8, cur], axis=0)
            gb = gb + jnp.sum(dpre, axis=0, keepdims=True)
            gw[3] = gw[3] + jnp.sum(dpre * cur, axis=0, keepdims=True)
            for j in range(3):
                gw[j] = gw[j] + jnp.sum(dpre * pltpu.roll(win, 3 - j, 0)[8:8 + RC], axis=0, keepdims=True)
        dp_ref[S:S + 8, :] = jnp.zeros((8, 128), F32)
        for n in range(S // RC):
            win = dp_ref[n * RC:(n + 1) * RC + 8, :]
            dx = w[3:4, :] * win[:RC]
            for j in range(3):
                dx = dx + w[j:j + 1, :] * pltpu.roll(win, RC + 8 - (3 - j), 0)[:RC]
            dx_ref[n * RC:(n + 1) * RC, :] = dx
        gw_ref[...] = jnp.concatenate(gw, axis=0)
        gb_ref[...] = gb

    return pl.pallas_call(
        body, name="ml_prep_bwd", grid=(8,),
        in_specs=[pl.BlockSpec((S, 128), lambda c: (0, 16 + c)),
                  pl.BlockSpec((4, 128), lambda c: (0, c)),
                  pl.BlockSpec((1, 128), lambda c: (0, c)),
                  pl.BlockSpec((S, 128), lambda c: (0, jnp.minimum(c, 3))),
                  pl.BlockSpec((S, 128), lambda c: (0, jnp.maximum(c - 4, 0)))],
        out_specs=[pl.BlockSpec((S, 128), lambda c: (0, c)),
                   pl.BlockSpec((4, 128), lambda c: (0, c)),
                   pl.BlockSpec((1, 128), lambda c: (0, c))],
        out_shape=[jax.ShapeDtypeStruct((S, 1024), F32), jax.ShapeDtypeStruct((4, 1024), F32),
                   jax.ShapeDtypeStruct((1, 1024), F32)],
        scratch_shapes=[pltpu.VMEM((S + 8, 128), F32)],
        compiler_params=_params(("arbitrary",)),
    )(proj, conv_w, conv_b, dq, dk)


NH = 4


def _heads(x):
    return jnp.stack([x[:, 128 * h:128 * (h + 1)] for h in range(NH)])


def _unheads(x):
    return jnp.concatenate([x[h] for h in range(NH)], axis=1)


def _per_head(f, *xs):
    return jnp.stack([f(*[x[h] for x in xs]) for h in range(NH)])


def _ml_gates(g):
    lg = _log_sigmoid_parts(g)
    ig = jnp.stack([jnp.broadcast_to(g[:, h:h + 1], (LCH, 128)) for h in range(NH)])
    lf = jnp.stack([jnp.broadcast_to(lg[:, 4 + h:5 + h], (LCH, 128)) for h in range(NH)])
    return ig, lf


def _tri_sum(tri, x, dot):
    return _per_head(lambda a, b, c: dot(tri, a) + dot(tri, b) + dot(tri, c), *_split3(x))


def _ml_chunk_fwd(q, k, v, ig, lf, ct, n_st, m_st, tri, causal):
    vf = v.astype(F32)
    b = _tri_sum(tri, lf, _dot)
    b_last = b[:, LCH - 1:LCH, :]
    g = b_last - b + ig
    m_loc = jnp.max(g, axis=1, keepdims=True)
    w = jnp.exp(g - m_loc)
    vwf = vf * w
    vw = vwf.astype(BF16)
    ct_loc = _per_head(_dot_tn, k, vw)
    kf = k.astype(F32)
    n_loc = jnp.sum(w * kf, axis=1, keepdims=True)
    r = _per_head(jnp.transpose, ig - b)
    d_log = jnp.where(causal, b + r, -jnp.inf)
    m_t = jnp.maximum(b + m_st, jnp.max(d_log, axis=2, keepdims=True))
    w_in = jnp.exp(d_log - m_t)
    qk = _per_head(_dot_nt, q, k)
    scores = qk * w_in
    cs = jnp.exp(b + m_st - m_t)
    ctb = ct.astype(BF16)
    qc = _per_head(_dot, q, ctb)
    qf = q.astype(F32)
    qn = jnp.sum(qf * n_st, axis=2, keepdims=True)
    num = _per_head(_dot, scores.astype(BF16), v) + cs * qc
    den = jnp.sum(scores, axis=2, keepdims=True) + cs * qn
    em = jnp.exp(-m_t)
    dd = jnp.maximum(jnp.abs(den), em)
    h = num / dd
    m_new = jnp.maximum(b_last + m_st, m_loc)
    a = jnp.exp(b_last + m_st - m_new)
    gg = jnp.exp(m_loc - m_new)
    return dict(vwf=vwf, vw=vw, kf=kf, qf=qf, ct_loc=ct_loc, n_loc=n_loc, w=w, w_in=w_in, qk=qk, scores=scores,
                cs=cs, ctb=ctb, qc=qc, qn=qn, den=den, em=em, dd=dd, h=h, m_new=m_new, a=a, gg=gg)


def _ml_consts():
    row = lax.broadcasted_iota(jnp.int32, (LCH, LCH), 0)
    col = lax.broadcasted_iota(jnp.int32, (LCH, LCH), 1)
    return row, (col <= row), (col <= row).astype(BF16)


def _ml_rows(qkv_ref, g_ref, gb_ref, c):
    rows = pl.ds(pl.multiple_of(c * LCH, LCH), LCH)
    q, k, v = (_heads(qkv_ref[rows, 512 * t:512 * (t + 1)]) for t in range(3))
    ig, lf = _ml_gates(g_ref[rows, :] + gb_ref[...])
    return rows, q, k, v, ig, lf


VMEM_SPEC = pl.BlockSpec(memory_space=pltpu.VMEM)


def _ml_fwd(qkv, proj_g, gbias):
    S = qkv.shape[0]
    nc = S // LCH

    def body(qkv_ref, g_ref, gb_ref, h_ref, cst_ref, nm_ref, ct_ref, n_ref, m_ref):
        _, causal, tri = _ml_consts()
        ct_ref[...] = jnp.zeros_like(ct_ref)
        n_ref[...] = jnp.zeros_like(n_ref)
        m_ref[...] = jnp.zeros_like(m_ref)

        def chunk(c, carry):
            rows, q, k, v, ig, lf = _ml_rows(qkv_ref, g_ref, gb_ref, c)
            ct, n_st, m_st = ct_ref[...], n_ref[:, 0:1, :], m_ref[:, 0:1, :]
            cst_ref[c] = ct
            nm_ref[c, :, 0:8, :] = n_ref[...]
            nm_ref[c, :, 8:16, :] = m_ref[...]
            f = _ml_chunk_fwd(q, k, v, ig, lf, ct, n_st, m_st, tri, causal)
            h_ref[rows, :] = _unheads(f["h"])
            ct_ref[...] = f["a"] * ct + f["gg"] * f["ct_loc"]
            n_ref[...] = jnp.broadcast_to(f["a"] * n_st + f["gg"] * f["n_loc"], (NH, 8, 128))
            m_ref[...] = jnp.broadcast_to(f["m_new"], (NH, 8, 128))
            return carry

        lax.fori_loop(0, nc, chunk, 0)

    return pl.pallas_call(
        body, name="ml_fwd", in_specs=[VMEM_SPEC] * 3, out_specs=[VMEM_SPEC] * 3,
        out_shape=[jax.ShapeDtypeStruct((S, ML_W), F32), jax.ShapeDtypeStruct((nc, NH, 128, 128), F32),
                   jax.ShapeDtypeStruct((nc, NH, 16, 128), F32)],
        scratch_shapes=[pltpu.VMEM((NH, 128, 128), F32), pltpu.VMEM((NH, 8, 128), F32), pltpu.VMEM((NH, 8, 128), F32)],
        compiler_params=pltpu.CompilerParams(vmem_limit_bytes=VMEM_LIMIT),
    )(qkv, proj_g, gbias)


def _ml_bwd(qkv, proj_g, gbias, cst, nm, dh):
    S = qkv.shape[0]
    nc = S // LCH

    def body(qkv_ref, g_ref, gb_ref, cst_ref, nm_ref, dh_ref, dq_ref, dk_ref, dv_ref, dif_ref, gsum_ref,
             dct_ref, dn_ref, gi_ref):
        row, causal, tri = _ml_consts()
        lane = lax.broadcasted_iota(jnp.int32, (LCH, 128), 1)
        last_row = row == LCH - 1
        dct_ref[...] = jnp.zeros_like(dct_ref)
        dn_ref[...] = jnp.zeros_like(dn_ref)
        gi_ref[...] = jnp.zeros_like(gi_ref)

        def chunk(t, carry):
            c = nc - 1 - t
            rows, q, k, v, ig, lf = _ml_rows(qkv_ref, g_ref, gb_ref, c)
            ct, n_st, m_st = cst_ref[c], nm_ref[c, :, 0:1, :], nm_ref[c, :, 8:9, :]
            f = _ml_chunk_fwd(q, k, v, ig, lf, ct, n_st, m_st, tri, causal)
            dh_ = _heads(dh_ref[rows, :])
            dct_new, dn_new = dct_ref[...], dn_ref[:, 0:1, :]
            e_num = dh_ / f["dd"]
            hdh = jnp.sum(f["h"] * dh_, axis=2, keepdims=True)
            free = jnp.abs(f["den"]) > f["em"]
            e_den = jnp.where(free, -hdh / f["dd"] * jnp.sign(f["den"]), 0.0)
            e_num_b = e_num.astype(BF16)
            ds_ = _per_head(_dot_nt, e_num_b, v) + e_den
            dqk = ds_ * f["w_in"]
            gam = dqk * f["qk"]
            dqk_b = dqk.astype(BF16)
            cse = f["cs"] * e_den
            dq = _per_head(_dot, dqk_b, k) + f["cs"] * _per_head(_dot_nt, e_num_b, f["ctb"]) + cse * n_st
            dk = _per_head(_dot_tn, dqk_b, q)
            dv = _per_head(_dot_tn, f["scores"].astype(BF16), e_num_b)
            dcl = (f["gg"] * dct_new).astype(BF16)
            dnl = f["gg"] * dn_new
            kd = _per_head(_dot, k, dcl)
            dv = dv + f["w"] * kd
            dk = dk + _per_head(_dot_nt, f["vw"], dcl) + f["w"] * dnl
            gam_s = (jnp.sum(kd * f["vwf"], axis=2, keepdims=True)
                     + f["w"] * jnp.sum(f["kf"] * dnl, axis=2, keepdims=True))
            col_g = jnp.sum(_per_head(jnp.transpose, gam), axis=2, keepdims=True) + gam_s
            db = (jnp.sum(gam, axis=2, keepdims=True) + jnp.sum(e_num * (f["cs"] * f["qc"]), axis=2, keepdims=True)
                  + cse * f["qn"] - col_g)
            state = jnp.sum(jnp.sum(dct_new * ct, axis=2, keepdims=True), axis=1, keepdims=True)
            state = state + jnp.sum(dn_new * n_st, axis=2, keepdims=True)
            db_last = jnp.sum(gam_s[:, :, 0:1], axis=1, keepdims=True) + f["a"][:, :, 0:1] * state
            db = jnp.where(last_row, db + db_last, db)
            dlf = _tri_sum(tri, db, _dot_tn)
            df = dlf * (1.0 - jnp.exp(lf))
            dq_ref[rows, :] = _unheads(dq).astype(BF16)
            dk_ref[rows, :] = _unheads(dk).astype(BF16)
            dv_ref[rows, :] = _unheads(dv).astype(BF16)
            dif = jnp.zeros((LCH, 128), F32)
            for h in range(NH):
                dif = dif + jnp.where(lane == h, col_g[h], 0.0) + jnp.where(lane == h + 4, df[h], 0.0)
            dif_ref[rows, :] = dif
            clamped = jnp.where(free, 0.0, hdh)
            gi_ref[...] += jnp.broadcast_to(jnp.sum(clamped, axis=1, keepdims=True), (NH, 8, 128))
            dct_ref[...] = f["a"] * dct_new + _per_head(_dot_tn, q, (f["cs"] * e_num).astype(BF16))
            dn_ref[...] = jnp.broadcast_to(f["a"] * dn_new + jnp.sum(cse * f["qf"], axis=1, keepdims=True), (NH, 8, 128))
            return carry

        lax.fori_loop(0, nc, chunk, 0)
        lane8 = lax.broadcasted_iota(jnp.int32, (8, 128), 1)
        gsum = jnp.where(lane8 >= 4, jnp.sum(dif_ref[...], axis=0, keepdims=True), 0.0)
        for h in range(NH):
            gsum = gsum + jnp.where(lane8 == h, gi_ref[h], 0.0)
        gsum_ref[...] = gsum

    return pl.pallas_call(
        body, name="ml_bwd", in_specs=[VMEM_SPEC] * 6, out_specs=[VMEM_SPEC] * 5,
        out_shape=[jax.ShapeDtypeStruct((S, ML_W), BF16)] * 3 + [jax.ShapeDtypeStruct((S, 128), F32),
                                                                  jax.ShapeDtypeStruct((8, 128), F32)],
        scratch_shapes=[pltpu.VMEM((NH, 128, 128), F32), pltpu.VMEM((NH, 8, 128), F32), pltpu.VMEM((NH, 8, 128), F32)],
        compiler_params=pltpu.CompilerParams(vmem_limit_bytes=VMEM_LIMIT),
    )(qkv, proj_g, gbias, cst, nm, dh)


MESH = pl.DeviceIdType.MESH
ANY = pl.BlockSpec(memory_space=pl.ANY)
N_DEV = 8


def _place():
    return lax.axis_index("x"), lax.axis_index("y"), lax.axis_index("c")


def _block_of(px, py, pc):
    return 4 * px + 2 * py + pc


def _copies_to_all(b_ref, o_ref, send_sems, recv_sems, local_sem):
    x, y, c = _place()
    mine = _block_of(x, y, c)
    copies = [pltpu.make_async_copy(b_ref, o_ref.at[mine], local_sem)]
    for k in range(1, N_DEV):
        peer = (x ^ (k >> 2), y ^ ((k >> 1) & 1), c ^ (k & 1))
        copies.append(pltpu.make_async_remote_copy(
            src_ref=b_ref, dst_ref=o_ref.at[mine], send_sem=send_sems.at[k - 1], recv_sem=recv_sems.at[k - 1],
            device_id=peer, device_id_type=MESH))
    return copies


def _copies_between_chips(p_ref, o_ref, send_sems, recv_sems, local_sem):
    x, y, c = _place()
    mine = 2 * x + y
    copies = [pltpu.make_async_copy(p_ref.at[mine], o_ref.at[mine], local_sem)]
    for k in range(1, 4):
        px, py = x ^ (k >> 1), y ^ (k & 1)
        copies.append(pltpu.make_async_remote_copy(
            src_ref=p_ref.at[2 * px + py], dst_ref=o_ref.at[mine], send_sem=send_sems.at[k - 1],
            recv_sem=recv_sems.at[k - 1], device_id=(px, py, c), device_id_type=MESH))
    return copies


def _around_grid(copies, first, last):
    @pl.when(first)
    def _():
        for cp in copies():
            cp.start()

    @pl.when(last)
    def _():
        for cp in copies():
            cp.wait()


def _inproj_fwd(x, pre_w, w_t, wg_t, blk=None):
    S, D = x.shape
    tm, tn = min(S, 1024), 1152
    ni, nj = S // tm, N_MAIN // tn

    def body(x_ref, pw_ref, w_ref, wg_ref, *rest):
        if blk is None:
            proj_ref, g_ref, u_ref = rest
        else:
            b_ref, proj_ref, g_ref, u_ref, o_ref, send_sems, recv_sems, local_sem = rest
            i, j = pl.program_id(0), pl.program_id(1)
            _around_grid(lambda: _copies_to_all(b_ref, o_ref, send_sems, recv_sems, local_sem),
                         (i == 0) & (j == 0), (i == ni - 1) & (j == nj - 1))

        @pl.when(pl.program_id(1) == 0)
        def _():
            xf = x_ref[...]
            r = lax.rsqrt(jnp.mean(xf * xf, axis=-1, keepdims=True) + EPS)
            u = (xf * r * pw_ref[...]).astype(BF16)
            u_ref[...] = u
            g_ref[...] = _dot_nt(u, wg_ref[...])

        proj_ref[...] = _dot_nt(u_ref[...], w_ref[...])

    in_specs = [pl.BlockSpec((tm, D), lambda i, j: (i, 0)),
                pl.BlockSpec((1, D), lambda i, j: (0, 0)),
                pl.BlockSpec((tn, D), lambda i, j: (j, 0)),
                pl.BlockSpec((128, D), lambda i, j: (0, 0))]
    out_specs = [pl.BlockSpec((tm, tn), lambda i, j: (i, j)),
                 pl.BlockSpec((tm, 128), lambda i, j: (i, 0)),
                 pl.BlockSpec((tm, D), lambda i, j: (i, 0))]
    out_shape = [jax.ShapeDtypeStruct((S, N_MAIN), F32), jax.ShapeDtypeStruct((S, 128), F32),
                 jax.ShapeDtypeStruct((S, D), BF16)]
    operands, scratch = (x, pre_w, w_t, wg_t), []
    if blk is not None:
        in_specs, out_specs, operands = in_specs + [ANY], out_specs + [ANY], operands + (blk,)
        out_shape = out_shape + [jax.ShapeDtypeStruct((N_DEV,) + blk.shape, blk.dtype)]
        scratch = [pltpu.SemaphoreType.DMA((7,)), pltpu.SemaphoreType.DMA((7,)), pltpu.SemaphoreType.DMA]
    return pl.pallas_call(
        body, name="inproj_fwd", grid=(ni, nj), in_specs=in_specs, out_specs=out_specs, out_shape=out_shape,
        scratch_shapes=scratch, compiler_params=_params(("arbitrary", "arbitrary")),
    )(*operands)


def _inproj_bwd(d_main, d_if, w_t, wg_t, x, pre_w, dx_tail, parts=None):
    S, D = x.shape
    tm, tk = min(S, 1024), 1152
    ni, nk = S // tm, N_MAIN // tk

    def body(d_ref, dg_ref, w_ref, wg_ref, x_ref, pw_ref, dt_ref, *rest):
        i, k = pl.program_id(0), pl.program_id(1)
        if parts is None:
            dx_ref, gpw_ref, acc_ref = rest
        else:
            p_ref, dx_ref, gpw_ref, o_ref, acc_ref, send_sems, recv_sems, local_sem = rest
            _around_grid(lambda: _copies_between_chips(p_ref, o_ref, send_sems, recv_sems, local_sem),
                         (i == 0) & (k == 0), (i == ni - 1) & (k == nk - 1))

        @pl.when(k == 0)
        def _():
            acc_ref[...] = _dot(dg_ref[...], wg_ref[...])

        acc_ref[...] += _dot(d_ref[...], w_ref[...])

        @pl.when(k == nk - 1)
        def _():
            xf = x_ref[...]
            r = lax.rsqrt(jnp.mean(xf * xf, axis=-1, keepdims=True) + EPS)
            xn = xf * r
            du = acc_ref[...]
            gw = du * pw_ref[...]
            dx_ref[...] = dt_ref[...] + r * (gw - xn * jnp.mean(gw * xn, axis=-1, keepdims=True))
            part = jnp.sum(du * xn, axis=0, keepdims=True)

            @pl.when(i == 0)
            def _():
                gpw_ref[...] = part

            @pl.when(i > 0)
            def _():
                gpw_ref[...] += part

    in_specs = [pl.BlockSpec((tm, tk), lambda i, k: (i, k)),
                pl.BlockSpec((tm, 128), lambda i, k: (i, 0)),
                pl.BlockSpec((tk, D), lambda i, k: (k, 0)),
                pl.BlockSpec((128, D), lambda i, k: (0, 0)),
                pl.BlockSpec((tm, D), lambda i, k: (i, 0)),
                pl.BlockSpec((1, D), lambda i, k: (0, 0)),
                pl.BlockSpec((tm, D), lambda i, k: (i, 0))]
    out_specs = [pl.BlockSpec((tm, D), lambda i, k: (i, 0)), pl.BlockSpec((1, D), lambda i, k: (0, 0))]
    out_shape = [jax.ShapeDtypeStruct((S, D), F32), jax.ShapeDtypeStruct((1, D), F32)]
    operands, scratch = (d_main, d_if, w_t, wg_t, x, pre_w, dx_tail), [pltpu.VMEM((tm, D), F32)]
    if parts is not None:
        in_specs, out_specs, operands = in_specs + [ANY], out_specs + [ANY], operands + (parts,)
        out_shape = out_shape + [jax.ShapeDtypeStruct(parts.shape, parts.dtype)]
        scratch = scratch + [pltpu.SemaphoreType.DMA((3,)), pltpu.SemaphoreType.DMA((3,)), pltpu.SemaphoreType.DMA]
    return pl.pallas_call(
        body, name="inproj_bwd", grid=(ni, nk), in_specs=in_specs, out_specs=out_specs, out_shape=out_shape,
        scratch_shapes=scratch, compiler_params=_params(("arbitrary", "arbitrary")),
    )(*operands)


def _matmul_tn(a, b, name):
    S, M = a.shape
    N = b.shape[1]
    tmm = 1152 if M % 1152 == 0 else min(M, 1024)
    tk = min(S, 1024)
    nk = S // tk

    def body(a_ref, b_ref, o_ref):
        part = _dot_tn(a_ref[...].astype(BF16), b_ref[...].astype(BF16))

        @pl.when(pl.program_id(1) == 0)
        def _():
            o_ref[...] = part

        @pl.when(pl.program_id(1) > 0)
        def _():
            o_ref[...] += part

    return pl.pallas_call(
        body, name=name, grid=(M // tmm, nk),
        in_specs=[pl.BlockSpec((tk, tmm), lambda i, k: (k, i)),
                  pl.BlockSpec((tk, N), lambda i, k: (k, 0))],
        out_specs=pl.BlockSpec((tmm, N), lambda i, k: (i, 0)),
        out_shape=jax.ShapeDtypeStruct((M, N), F32),
        compiler_params=_params(("arbitrary", "arbitrary")),
    )(a, b)


def _half_mean(v, low):
    s_lo = jnp.sum(jnp.where(low, v, 0.0), axis=1, keepdims=True)
    s_hi = jnp.sum(jnp.where(low, 0.0, v), axis=1, keepdims=True)
    return jnp.where(low, s_lo, s_hi) * (1.0 / 64.0)


def _silu_grad(z, s):
    return s * (1.0 + z * (1.0 - s))


def _tail(y_sb, h_ml, proj, x, p, target, sb_nw, ml_nw, w_out, post_w, w_gate, b_gate, w_up):
    S, D = x.shape
    tm = 256

    def body(ysb_ref, hml_ref, sbz_ref, mlo_ref, mlz_ref, x_ref, p_ref, tg_ref, sbw_ref, mlw_ref, wo_ref, pw_ref,
             wg_ref, bg_ref, wu_ref,
             dx_ref, dysb_ref, dhml_ref, dsbz_ref, dmlo_ref, dmlz_ref, mix_ref, dy_ref, h1_ref, dgp_ref, dpu_ref,
             small_ref):
        lane = lax.broadcasted_iota(jnp.int32, (tm, 128), 1)
        low = lane < 64
        sb_saved, ml_saved, mixed = [], [], []
        for s in range(4):
            sl = slice(128 * s, 128 * s + 128)
            y = ysb_ref[:, sl]
            rs = lax.rsqrt(_half_mean(y * y, low) + EPS)
            n = y * rs
            z = sbz_ref[:, sl]
            sg = jax.nn.sigmoid(z)
            w = sbw_ref[:, sl]
            mixed.append((n * w) * (z * sg))
            sb_saved.append((rs, n, z, sg, w))
        for s in range(4):
            sl = slice(128 * s, 128 * s + 128)
            og = jax.nn.sigmoid(mlo_ref[:, sl])
            hh = hml_ref[:, sl]
            t = og * hh
            rs = lax.rsqrt(jnp.mean(t * t, axis=1, keepdims=True) + EPS)
            n = t * rs
            z = mlz_ref[:, sl]
            sg = jax.nn.sigmoid(z)
            w = mlw_ref[:, sl]
            mixed.append((n * w) * (z * sg))
            ml_saved.append((rs, n, z, sg, w, og, hh))
        mix = jnp.concatenate(mixed, axis=1).astype(BF16)
        mix_ref[...] = mix
        y = _dot(mix, wo_ref[...])
        rs_y = lax.rsqrt(jnp.mean(y * y, axis=1, keepdims=True) + EPS)
        yn = y * rs_y
        pw = pw_ref[...]
        h1 = x_ref[...] + yn * pw
        h1b = h1.astype(BF16)
        h1_ref[...] = h1b
        gate = jax.nn.sigmoid(_dot(h1b, wg_ref[...]) + bg_ref[...])
        pu = _dot(p_ref[...].astype(BF16), wu_ref[...])
        err = (h1 + gate * pu) - tg_ref[...]
        loss = 0.5 * jnp.sum(jnp.sum(err * err, axis=1, keepdims=True) * (1.0 / D))
        d_out = err * (1.0 / D)
        dpu_ref[...] = (d_out * gate).astype(BF16)
        dgp = (d_out * pu) * (gate * (1.0 - gate))
        dgpb = dgp.astype(BF16)
        dgp_ref[...] = dgpb
        d_h1 = d_out + _dot_nt(dgpb, wg_ref[...])
        dx_ref[...] = d_h1
        gwy = d_h1 * pw
        d_y = rs_y * (gwy - yn * jnp.mean(gwy * yn, axis=1, keepdims=True))
        d_yb = d_y.astype(BF16)
        dy_ref[...] = d_yb
        d_mix = _dot_nt(d_yb, wo_ref[...])
        g_nw = []
        for s in range(4):
            sl = slice(128 * s, 128 * s + 128)
            rs, n, z, sg, w = sb_saved[s]
            da = d_mix[:, sl]
            act = z * sg
            dsbz_ref[:, sl] = (da * (n * w) * _silu_grad(z, sg)).astype(BF16)
            dn = da * w * act
            g_nw.append(jnp.sum(da * act * n, axis=0, keepdims=True))
            dysb_ref[:, sl] = rs * (dn - n * _half_mean(dn * n, low))
        for s in range(4):
            sl = slice(128 * s, 128 * s + 128)
            rs, n, z, sg, w, og, hh = ml_saved[s]
            da = d_mix[:, 512 + 128 * s:512 + 128 * s + 128]
            act = z * sg
            dmlz_ref[:, sl] = (da * (n * w) * _silu_grad(z, sg)).astype(BF16)
            dn = da * w * act
            g_nw.append(jnp.sum(da * act * n, axis=0, keepdims=True))
            dt = rs * (dn - n * jnp.mean(dn * n, axis=1, keepdims=True))
            dmlo_ref[:, sl] = (dt * hh * (og * (1.0 - og))).astype(BF16)
            dhml_ref[:, sl] = dt * og
        upd = jnp.concatenate([
            jnp.sum(d_h1 * yn, axis=0, keepdims=True),
            jnp.sum(dgp, axis=0, keepdims=True),
            jnp.concatenate(g_nw, axis=1),
            jnp.full((1, D), loss, F32),
            jnp.zeros((4, D), F32)], axis=0)

        @pl.when(pl.program_id(0) == 0)
        def _():
            small_ref[...] = upd

        @pl.when(pl.program_id(0) > 0)
        def _():
            small_ref[...] += upd

    def rows(width, col=0):
        return pl.BlockSpec((tm, width), lambda i: (i, col))

    def whole(a):
        return pl.BlockSpec(a.shape, lambda i: (0, 0))

    return pl.pallas_call(
        body, name="tail", grid=(S // tm,),
        in_specs=[rows(512), rows(512), rows(512, 3), rows(512, 7), rows(512, 8), rows(D), rows(256), rows(D),
                  whole(sb_nw), whole(ml_nw), whole(w_out), whole(post_w), whole(w_gate), whole(b_gate), whole(w_up)],
        out_specs=[rows(D), rows(512), rows(512), rows(512), rows(512), rows(512), rows(D), rows(D), rows(D), rows(D),
                   rows(D), pl.BlockSpec((8, D), lambda i: (0, 0))],
        out_shape=[jax.ShapeDtypeStruct((S, D), F32), jax.ShapeDtypeStruct((S, 512), F32),
                   jax.ShapeDtypeStruct((S, 512), F32)] + [jax.ShapeDtypeStruct((S, 512), BF16)] * 3
        + [jax.ShapeDtypeStruct((S, D), BF16)] * 5 + [jax.ShapeDtypeStruct((8, D), F32)],
        compiler_params=_params(("arbitrary",)),
    )(y_sb, h_ml, proj, proj, proj, x, p, target, sb_nw, ml_nw, w_out, post_w, w_gate, b_gate, w_up)


def _local_step(x, p, target, pre_w, w_t, wg_t, conv_w, conv_b, gbias, sb_nw, ml_nw, post_w, b_gate, late,
                exchange=None):
    if callable(late[1]):
        proj, proj_g, u, gathered = _inproj_fwd(x, pre_w, w_t, wg_t, late[0])
        w_out, w_gate, w_up = late[1](gathered)
    else:
        proj, proj_g, u = _inproj_fwd(x, pre_w, w_t, wg_t)
        w_out, w_gate, w_up = late
    y_sb, tot = _sb_fwd(proj)
    qkv = _ml_prep(proj, conv_w, conv_b)
    h_ml, cst, nm = _ml_fwd(qkv, proj_g, gbias)
    dx_tail, d_ysb, d_hml, d_sbz, d_mlo, d_mlz, mix, d_y, h1, dgp, dpu, small = _tail(
        y_sb, h_ml, proj, x, p, target, sb_nw, ml_nw, w_out, post_w, w_gate, b_gate, w_up)
    dq, dk, dv = _sb_bwd(proj, tot, d_ysb)
    dqc, dks, dmlv, dif, gif = _ml_bwd(qkv, proj_g, gbias, cst, nm, d_hml)
    dmlqk, g_cw, g_cb = _ml_prep_bwd(proj, conv_w, conv_b, dqc, dks)
    d_main = jnp.concatenate([dq.astype(BF16), dk.astype(BF16), dv.astype(BF16), d_sbz, dmlqk.astype(BF16),
                              dmlv, d_mlo, d_mlz], axis=1)
    d_if = dif.astype(BF16)
    grads = dict(
        w_t=_matmul_tn(d_main, u, "gw_in"), wg_t=_matmul_tn(d_if, u, "gw_in_gates"),
        w_out=_matmul_tn(mix, d_y, "gw_out"), w_gate=_matmul_tn(h1, dgp, "gw_gate"), w_up=_matmul_tn(p, dpu, "gw_up"),
        conv_w=g_cw, conv_b=g_cb, gif=gif)
    parts = exchange(grads) if exchange else None
    return _inproj_bwd(d_main, d_if, w_t, wg_t, x, pre_w, dx_tail, parts), grads, small


def _all_gather(a, b):
    def body(a_ref, b_ref, oa_ref, ob_ref, send_sems, recv_sems, local_sems):
        x, y, c = _place()
        me, sibling = (x, y, c), (x, y, 1 - c)
        chips = [(1 - x, y), (x, 1 - y), (1 - x, 1 - y)]
        pairs = ((a_ref, oa_ref), (b_ref, ob_ref))

        def copies(k, block, to, from_input=False):
            slot = _block_of(*block)
            return [pltpu.make_async_remote_copy(
                src_ref=src if from_input else out.at[slot], dst_ref=out.at[slot],
                send_sem=send_sems.at[t, k], recv_sem=recv_sems.at[t, k], device_id=to, device_id_type=MESH)
                for t, (src, out) in enumerate(pairs)]

        mine = [pltpu.make_async_copy(src, out.at[_block_of(*me)], local_sems.at[t])
                for t, (src, out) in enumerate(pairs)]
        for cp in mine:
            cp.start()
        first = copies(0, me, sibling, True)
        for j, chip in enumerate(chips):
            first += copies(1 + j, me, (*chip, c), True)
        for cp in first:
            cp.start()
        passed = []
        for j, chip in enumerate(chips):
            for cp in copies(1 + j, (*chip, c), me):
                cp.wait_recv()
            fwd = copies(4 + j, (*chip, c), sibling)
            for cp in fwd:
                cp.start()
            passed += fwd
        for cp in copies(0, sibling, me):
            cp.wait_recv()
        for j, chip in enumerate(chips):
            for cp in copies(4 + j, (*chip, 1 - c), me):
                cp.wait_recv()
        for cp in first + passed:
            cp.wait_send()
        for cp in mine:
            cp.wait()

    return pl.pallas_call(
        body, name="all_gather",
        in_specs=[ANY, ANY], out_specs=[ANY, ANY],
        out_shape=[jax.ShapeDtypeStruct((N_DEV,) + a.shape, a.dtype), jax.ShapeDtypeStruct((N_DEV,) + b.shape, b.dtype)],
        scratch_shapes=[pltpu.SemaphoreType.DMA((2, 7)), pltpu.SemaphoreType.DMA((2, 7)), pltpu.SemaphoreType.DMA((2,))],
    )(a, b)


def _exchange_pair(g):
    def body(g_ref, og_ref, send_sems, recv_sems):
        x, y, c = _place()
        sent = [pltpu.make_async_remote_copy(
            src_ref=g_ref.at[k, 1 - c], dst_ref=og_ref.at[k], send_sem=send_sems.at[k], recv_sem=recv_sems.at[k],
            device_id=(x, y, 1 - c), device_id_type=MESH) for k in range(4)]
        for cp in sent:
            cp.start()
        for cp in sent:
            cp.wait()

    return pl.pallas_call(
        body, name="exchange_pair", in_specs=[ANY], out_specs=ANY,
        out_shape=jax.ShapeDtypeStruct((4,) + g.shape[2:], g.dtype),
        scratch_shapes=[pltpu.SemaphoreType.DMA((4,)), pltpu.SemaphoreType.DMA((4,))],
    )(g)


def _pair_sum(g, r, core, tr):
    _, _, R, D = g.shape

    def body(c_ref, g_ref, r_ref, o_ref):
        o_ref[...] = (g_ref[...] + r_ref[...]).astype(BF16)

    return pl.pallas_call(
        body, name="pair_sum",
        grid_spec=pltpu.PrefetchScalarGridSpec(
            num_scalar_prefetch=1, grid=(4, R // tr),
            in_specs=[pl.BlockSpec((None, None, tr, D), lambda k, i, c: (k, c[0], i, 0)),
                      pl.BlockSpec((None, tr, D), lambda k, i, c: (k, i, 0))],
            out_specs=pl.BlockSpec((None, tr, D), lambda k, i, c: (k, i, 0))),
        out_shape=jax.ShapeDtypeStruct((4, R, D), BF16),
        compiler_params=_params(("arbitrary", "arbitrary")),
    )(core, g, r)


ADAM_LR, ADAM_B1, ADAM_B2, ADAM_EPS, ADAM_WD, ADAM_STEP = 0.001, 0.9, 0.999, 1e-08, 0.01, 10


def _adamw(w, g, m, v):
    m = ADAM_B1 * m + (1.0 - ADAM_B1) * g
    v = ADAM_B2 * v + (1.0 - ADAM_B2) * (g * g)
    m_hat = m / (1.0 - ADAM_B1 ** ADAM_STEP)
    v_hat = v / (1.0 - ADAM_B2 ** ADAM_STEP)
    return -ADAM_LR * (m_hat / (jnp.sqrt(v_hat) + ADAM_EPS) + ADAM_WD * w), m, v


def _adam(parts, w, m, v, tr, name, small=None):
    R, D = w.shape
    n = parts.shape[0]
    steps = R // tr

    def body(p_ref, w_ref, m_ref, v_ref, *rest):
        if small is None:
            g_ref, d_ref, nm_ref, nv_ref = rest
        else:
            s_ref, g_ref, d_ref, nm_ref, nv_ref, o_ref, send_sems, recv_sems, local_sem = rest
            i = pl.program_id(0)
            _around_grid(lambda: _copies_to_all(s_ref, o_ref, send_sems, recv_sems, local_sem), i == 0, i == steps - 1)
        g = p_ref[0].astype(F32)
        for k in range(1, n):
            g = g + p_ref[k].astype(F32)
        g_ref[...] = g
        d_ref[...], nm_ref[...], nv_ref[...] = _adamw(w_ref[...], g, m_ref[...], v_ref[...])

    blk = pl.BlockSpec((tr, D), lambda i: (i, 0))
    in_specs = [pl.BlockSpec((n, tr, D), lambda i: (0, i, 0)), blk, blk, blk]
    out_specs, out_shape = [blk] * 4, [jax.ShapeDtypeStruct((R, D), F32)] * 4
    operands, scratch = (parts, w, m, v), []
    if small is not None:
        in_specs, out_specs, operands = in_specs + [ANY], out_specs + [ANY], operands + (small,)
        out_shape = out_shape + [jax.ShapeDtypeStruct((N_DEV,) + small.shape, small.dtype)]
        scratch = [pltpu.SemaphoreType.DMA((7,)), pltpu.SemaphoreType.DMA((7,)), pltpu.SemaphoreType.DMA]
    return pl.pallas_call(
        body, name=name, grid=(steps,), in_specs=in_specs, out_specs=out_specs, out_shape=out_shape,
        scratch_shapes=scratch, compiler_params=_params(("arbitrary",)),
    )(*operands)


ROWS_IN = 592
ROWS_BF16 = ROWS_IN + 128 + 128 + 32
ROWS_CONV = 16
ROWS_ALL = ROWS_BF16 + ROWS_CONV
ROW_TILE = 224


def _pad_rows(a, rows):
    return jnp.pad(a, ((0, rows - a.shape[0]), (0, 0)))


def _pack_shards(w_in, w_out, w_gate, w_up, conv_w):
    return jnp.concatenate([
        _pad_rows(w_in[0].T, ROWS_IN), w_out[0], w_gate[0], w_up[0].reshape(32, D_MODEL),
        _pad_rows(jnp.pad(conv_w[0].reshape(1, 512), ((0, 0), (0, 512))), ROWS_CONV)], axis=0)


def _unpack_shards(a):
    return (a[:SHARD_IN].T[None], a[ROWS_IN:ROWS_IN + 128][None], a[ROWS_IN + 128:ROWS_IN + 256][None],
            a[ROWS_IN + 256:ROWS_BF16].reshape(1, 256, 128), a[ROWS_BF16, :512].reshape(1, 4, 128))


def _pack_small(pre_w, conv_b, i_bias, f_bias, sb_nw, ml_nw, post_w, b_gate):
    gates = jnp.pad(jnp.concatenate([i_bias, f_bias], axis=1), ((0, 0), (0, D_MODEL - 8)))
    return jnp.concatenate([post_w, b_gate, jnp.concatenate([sb_nw, ml_nw], axis=1), jnp.zeros((1, D_MODEL), F32),
                            pre_w, conv_b, gates, jnp.zeros((1, D_MODEL), F32)], axis=0)


def _unpack_small(a):
    return a[4:5], a[5:6], a[6:7, 0:4], a[6:7, 4:8], a[2:3, :512], a[2:3, 512:], a[0:1], a[1:2]


def kernel(x, p, pre_norm_w, w_in, ml_conv_w, ml_conv_b, ml_i_bias, ml_f_bias, sb_norm_w, ml_norm_w, w_out, post_norm_w, ple_w_up, ple_w_gate, ple_b_gate, loss_target, m_pre_norm_w, m_w_in, m_ml_conv_w, m_ml_conv_b, m_ml_i_bias, m_ml_f_bias, m_sb_norm_w, m_ml_norm_w, m_w_out, m_post_norm_w, m_ple_w_up, m_ple_w_gate, m_ple_b_gate, v_pre_norm_w, v_w_in, v_ml_conv_w, v_ml_conv_b, v_ml_i_bias, v_ml_f_bias, v_sb_norm_w, v_ml_norm_w, v_w_out, v_post_norm_w, v_ple_w_up, v_ple_w_gate, v_ple_b_gate):
    D = D_MODEL
    w_pk = _pack_shards(w_in, w_out, ple_w_gate, ple_w_up, ml_conv_w)
    m_pk = _pack_shards(m_w_in, m_w_out, m_ple_w_gate, m_ple_w_up, m_ml_conv_w)
    v_pk = _pack_shards(v_w_in, v_w_out, v_ple_w_gate, v_ple_w_up, v_ml_conv_w)
    w_sm = _pack_small(pre_norm_w, ml_conv_b, ml_i_bias, ml_f_bias, sb_norm_w, ml_norm_w, post_norm_w, ple_b_gate)
    m_sm = _pack_small(m_pre_norm_w, m_ml_conv_b, m_ml_i_bias, m_ml_f_bias, m_sb_norm_w, m_ml_norm_w, m_post_norm_w, m_ple_b_gate)
    v_sm = _pack_small(v_pre_norm_w, v_ml_conv_b, v_ml_i_bias, v_ml_f_bias, v_sb_norm_w, v_ml_norm_w, v_post_norm_w, v_ple_b_gate)

    w_bf = w_pk[:ROWS_BF16].astype(BF16)
    ga, gb = _all_gather(w_bf[:ROWS_IN], w_pk[ROWS_BF16:])
    w_in_t = ga[:, :SHARD_IN].reshape(N_IN, D)
    wg_t = _pad_rows(w_in_t[N_MAIN:], 128)
    conv_w_f = gb[:, 0, :512].reshape(N_DEV, 4, 128).transpose(1, 0, 2).reshape(4, D)
    gbias = jnp.pad(jnp.concatenate([ml_i_bias, ml_f_bias], axis=1), ((0, 0), (0, 120)))

    def unpack_late(gl):
        return (gl[:, :128].reshape(D, D), gl[:, 128:256].reshape(D, D),
                gl[:, 256:].reshape(N_DEV, 256, 128).transpose(1, 0, 2).reshape(256, D))

    def exchange(g):
        g_in = jnp.concatenate([g["w_t"], g["wg_t"][:8]], axis=0).reshape(N_DEV, SHARD_IN, D)
        g_blocks = jnp.concatenate([
            jnp.pad(g_in, ((0, 0), (0, ROWS_IN - SHARD_IN), (0, 0))),
            g["w_out"].reshape(N_DEV, 128, D), g["w_gate"].reshape(N_DEV, 128, D),
            g["w_up"].reshape(256, N_DEV, 128).transpose(1, 0, 2).reshape(N_DEV, 32, D),
            jnp.pad(g["conv_w"].reshape(4, N_DEV, 128).transpose(1, 0, 2).reshape(N_DEV, 1, 512),
                    ((0, 0), (0, ROWS_CONV - 1), (0, 512))),
        ], axis=1).reshape(4, 2, ROWS_ALL, D)
        core = lax.axis_index("c").astype(jnp.int32).reshape(1)
        return _pair_sum(g_blocks, _exchange_pair(g_blocks), core, ROW_TILE)

    (dx, g_pre, parts), g, small = _local_step(
        x[0], p[0, 0], loss_target[0], pre_norm_w, w_in_t, wg_t, conv_w_f, ml_conv_b, gbias, sb_norm_w, ml_norm_w,
        post_norm_w, ple_b_gate, (w_bf[ROWS_IN:], unpack_late), exchange)

    g_small = jnp.concatenate([small[0:4], g_pre, g["conv_b"], jnp.pad(g["gif"][0:1], ((0, 0), (0, D - 128))),
                               jnp.zeros((1, D), F32)], axis=0)
    grad_pk, delta_pk, nm_pk, nv_pk, parts_sm = _adam(parts, w_pk, m_pk, v_pk, ROW_TILE, "adam", g_small)
    grad_sm, delta_sm, nm_sm, nv_sm = _adam(parts_sm, w_sm, m_sm, v_sm, 8, "adam_small")
    loss = grad_sm[3, 0]

    def ordered(pk, sm):
        win, wout, wgate, wup, convw = _unpack_shards(pk)
        pre_w, conv_b, i_b, f_b, sb_nw, ml_nw, post_w, b_gate = _unpack_small(sm)
        return [pre_w, win, convw, conv_b, i_b, f_b, sb_nw, ml_nw, wout, post_w, wup, wgate, b_gate]

    return (loss, dx[None], *ordered(grad_pk, grad_sm), *ordered(delta_pk, delta_sm), *ordered(nm_pk, nm_sm),
            *ordered(nv_pk, nv_sm))
```

```python
import functools

import jax
import jax.numpy as jnp
from jax import lax
from jax.experimental import pallas as pl
from jax.experimental.pallas import tpu as pltpu

F32 = jnp.float32
BF16 = jnp.bfloat16
EPS = 1e-6
D_MODEL = 1024
SB_W = 512
ML_W = 512
N_MAIN = 4608
N_IN = 4616
SHARD_IN = 577
SHARD_IN_PAD = 584
TQ = 1024
TK = 256
ND = TQ // TK
LCH = 128
VMEM_LIMIT = 56 * 1024 * 1024


def _dot(a, b):
    return jnp.dot(a, b, preferred_element_type=F32)


def _dot_nt(a, b):
    return lax.dot_general(a, b, (((1,), (1,)), ((), ())), preferred_element_type=F32)


def _dot_tn(a, b):
    return lax.dot_general(a, b, (((0,), (0,)), ((), ())), preferred_element_type=F32)


def _split2(x):
    hi = x.astype(BF16)
    lo = (x - hi.astype(F32)).astype(BF16)
    return hi, lo


def _split3(x):
    hi = x.astype(BF16)
    r = x - hi.astype(F32)
    mid = r.astype(BF16)
    lo = (r - mid.astype(F32)).astype(BF16)
    return hi, mid, lo


def _params(sem):
    return pltpu.CompilerParams(dimension_semantics=sem, vmem_limit_bytes=VMEM_LIMIT)


def _log_sigmoid_parts(z):
    e = jnp.exp(-jnp.abs(z))
    return jnp.minimum(z, 0.0) - jnp.log(1.0 + e)


def _neg_log_sigmoid(nz):
    nz = jnp.minimum(nz, 80.0)
    sp = jnp.log(1.0 + jnp.exp(nz))
    return sp, nz - sp


def _sb_fwd(proj):
    S = proj.shape[0]
    nq = S // TQ

    def body(q_ref, k_ref, v_ref, y_ref, t_ref, acc_ref, car_ref, zs_ref):
        i = pl.program_id(1)
        low = lax.broadcasted_iota(jnp.int32, (TQ, 128), 1) < 64
        row = lax.broadcasted_iota(jnp.int32, (TK, TK), 0)
        col = lax.broadcasted_iota(jnp.int32, (TK, TK), 1)
        uo = (row > col).astype(BF16)
        uo = jnp.concatenate([uo, uo], axis=0)
        q = q_ref[...] * 0.125
        qh = (jnp.where(low, -q, 0.0).astype(BF16), jnp.where(low, 0.0, -q).astype(BF16))
        acc_ref[...] = jnp.zeros_like(acc_ref)
        car_ref[...] = jnp.zeros_like(car_ref)

        def block(j, r0):
            diag = r0 is not None
            r0 = r0 or 0
            if diag:
                strict = (lax.broadcasted_iota(jnp.int32, (TQ - r0, TK), 1)
                          < lax.broadcasted_iota(jnp.int32, (TQ - r0, TK), 0))
            rows = pl.ds(pl.multiple_of(j * TK, TK), TK)
            kb = k_ref[rows, :].astype(BF16)
            vb = v_ref[rows, :].astype(BF16)
            if not diag:
                kn = k_ref[pl.ds(pl.multiple_of(jnp.maximum(j - 1, 0) * TK, TK), TK), :].astype(BF16)

            def first(h):
                if diag:
                    nz = _dot_nt(qh[h][r0:], kb)
                else:
                    nz = zs_ref[h]
                    zs_ref[h] = _dot_nt(qh[h], kn)
                sp, lk = _neg_log_sigmoid(nz)
                if diag:
                    lk = jnp.where(strict, lk, 0.0)
                hi, lo = _split2(lk)
                return sp, lk[:, 0:1], _dot(jnp.concatenate([hi, lo], axis=1), uo)

            def second(h, sp, lk0, rr):
                car = car_ref[h, r0:, :]
                a = jnp.exp((jnp.concatenate([car, car], axis=1) + rr) - sp)
                if diag:
                    a = jnp.where(strict, a, 0.0)
                acc_ref[h, r0:, :] += _dot(a.astype(BF16), vb)
                car_ref[h, r0:, :] = car + jnp.broadcast_to(rr[:, 0:1] + lk0, car.shape)

            if diag:
                halves = [first(h) for h in range(2)]
                for h in range(2):
                    second(h, *halves[h])
            else:
                for h in range(2):
                    second(h, *first(h))

        for d in reversed(range(ND)):
            block(ND * i + d, TK * d)

        @pl.when(i > 0)
        def _():
            k0 = k_ref[pl.ds(pl.multiple_of((ND * i - 1) * TK, TK), TK), :].astype(BF16)
            for h in range(2):
                zs_ref[h] = _dot_nt(qh[h], k0)

        def loop(n, c):
            block(ND * i - 1 - n, None)
            return c

        lax.fori_loop(0, ND * i, loop, 0)
        y_ref[...] = jnp.where(low, acc_ref[0], acc_ref[1])
        t_ref[...] = jnp.where(low, car_ref[0], car_ref[1])

    return pl.pallas_call(
        body, name="sb_fwd", grid=(4, nq),
        in_specs=[pl.BlockSpec((TQ, 128), lambda p, i: (i, p)),
                  pl.BlockSpec((S, 128), lambda p, i: (0, 4 + p)),
                  pl.BlockSpec((S, 128), lambda p, i: (0, 8 + p))],
        out_specs=[pl.BlockSpec((TQ, 128), lambda p, i: (i, p)),
                   pl.BlockSpec((TQ, 128), lambda p, i: (i, p))],
        out_shape=[jax.ShapeDtypeStruct((S, SB_W), F32), jax.ShapeDtypeStruct((S, SB_W), F32)],
        scratch_shapes=[pltpu.VMEM((2, TQ, 128), F32), pltpu.VMEM((2, TQ, 128), F32), pltpu.VMEM((2, TQ, TK), F32)],
        compiler_params=_params(("arbitrary", "arbitrary")),
    )(proj, proj, proj)


def _sb_bwd(proj, tot, dy):
    S = proj.shape[0]
    nq = S // TQ

    def body(q_ref, k_ref, v_ref, t_ref, dy_ref, dq_ref, dk_ref, dv_ref, dqa_ref, cp_ref, cg_ref, dkt_ref, dvt_ref):
        i = pl.program_id(1)
        low = lax.broadcasted_iota(jnp.int32, (TQ, 128), 1) < 64
        row = lax.broadcasted_iota(jnp.int32, (TK, TK), 0)
        col = lax.broadcasted_iota(jnp.int32, (TK, TK), 1)
        u_inc = (row <= col).astype(BF16)
        u_inc = jnp.concatenate([u_inc, u_inc], axis=0)
        u_exc = (row < col).astype(BF16)
        q = q_ref[...] * 0.125
        qh = (jnp.where(low, -q, 0.0).astype(BF16), jnp.where(low, 0.0, -q).astype(BF16))
        dy_ = dy_ref[...]
        dyh = (jnp.where(low, dy_, 0.0).astype(BF16), jnp.where(low, 0.0, dy_).astype(BF16))
        qt = tuple(jnp.transpose(x) for x in qh)
        dyt = tuple(jnp.transpose(x) for x in dyh)
        t_ = t_ref[...]
        t_sw = pltpu.roll(t_, 64, 1)
        th = (jnp.where(low, t_, t_sw), jnp.where(low, t_sw, t_))
        dqa_ref[...] = jnp.zeros_like(dqa_ref)
        cp_ref[...] = jnp.zeros_like(cp_ref)
        cg_ref[...] = jnp.zeros_like(cg_ref)

        @pl.when(i == 0)
        def _():
            dkt_ref[...] = jnp.zeros_like(dkt_ref)
            dvt_ref[...] = jnp.zeros_like(dvt_ref)

        def block(j, r0):
            diag = r0 is not None
            r0 = r0 or 0
            if diag:
                strict = (lax.broadcasted_iota(jnp.int32, (TQ - r0, TK), 1)
                          < lax.broadcasted_iota(jnp.int32, (TQ - r0, TK), 0))
            rows = pl.ds(pl.multiple_of(j * TK, TK), TK)
            kb = k_ref[rows, :].astype(BF16)
            vb = v_ref[rows, :].astype(BF16)
            dk_acc = jnp.zeros((128, TK), F32)
            dv_acc = jnp.zeros((128, TK), F32)
            for h in range(2):
                qr, dyr = qh[h][r0:], dyh[h][r0:]
                sp, lk = _neg_log_sigmoid(_dot_nt(qr, kb))
                if diag:
                    lk = jnp.where(strict, lk, 0.0)
                hi, lo = _split2(lk)
                pp = _dot(jnp.concatenate([hi, lo], axis=1), u_inc)
                cp, cg = cp_ref[h, r0:, :], cg_ref[h, r0:, :]
                rest = th[h][r0:] - cp
                a = jnp.exp((jnp.concatenate([rest, rest], axis=1) - pp) - sp)
                if diag:
                    a = jnp.where(strict, a, 0.0)
                g = _dot_nt(dyr, vb) * a
                gg = _dot(g.astype(BF16), u_exc)
                beta = jnp.exp(-sp)
                dz = g - beta * (g + (jnp.concatenate([cg, cg], axis=1) + gg))
                if diag:
                    dz = jnp.where(strict, dz, 0.0)
                dzb = dz.astype(BF16)
                dqa_ref[h, r0:, :] += _dot(dzb, kb)
                dk_acc += _dot(qt[h][:, r0:], dzb)
                dv_acc += _dot(dyt[h][:, r0:], a.astype(BF16))
                cp_ref[h, r0:, :] = cp + jnp.broadcast_to(pp[:, TK - 1:TK], cp.shape)
                cg_ref[h, r0:, :] = cg + jnp.broadcast_to(gg[:, TK - 1:TK] + g[:, TK - 1:TK], cg.shape)
            dkt_ref[:, rows] -= dk_acc
            dvt_ref[:, rows] += dv_acc

        def loop(j, c):
            block(j, None)
            return c

        lax.fori_loop(0, ND * i, loop, 0)
        for d in range(ND):
            block(ND * i + d, TK * d)
        dq_ref[...] = jnp.where(low, dqa_ref[0], dqa_ref[1]) * 0.125

        @pl.when(i == nq - 1)
        def _():
            dk_ref[...] = jnp.transpose(dkt_ref[...])
            dv_ref[...] = jnp.transpose(dvt_ref[...])

    return pl.pallas_call(
        body, name="sb_bwd", grid=(4, nq),
        in_specs=[pl.BlockSpec((TQ, 128), lambda p, i: (i, p)),
                  pl.BlockSpec((S, 128), lambda p, i: (0, 4 + p)),
                  pl.BlockSpec((S, 128), lambda p, i: (0, 8 + p)),
                  pl.BlockSpec((TQ, 128), lambda p, i: (i, p)),
                  pl.BlockSpec((TQ, 128), lambda p, i: (i, p))],
        out_specs=[pl.BlockSpec((TQ, 128), lambda p, i: (i, p)),
                   pl.BlockSpec((S, 128), lambda p, i: (0, p)),
                   pl.BlockSpec((S, 128), lambda p, i: (0, p))],
        out_shape=[jax.ShapeDtypeStruct((S, SB_W), F32)] * 3,
        scratch_shapes=[pltpu.VMEM((2, TQ, 128), F32)] * 3 + [pltpu.VMEM((128, S), F32)] * 2,
        compiler_params=_params(("arbitrary", "arbitrary")),
    )(proj, proj, proj, tot, dy)


ML_SCALE = 128 ** -0.5
RC = 256


def _conv_taps(cur, prev8, w):
    n = cur.shape[0]
    win = jnp.concatenate([prev8, cur], axis=0)
    out = w[3:4, :] * cur
    for j in range(3):
        out = out + w[j:j + 1, :] * pltpu.roll(win, 3 - j, 0)[8:8 + n]
    return out


def _ml_prep(proj, conv_w, conv_b):
    S = proj.shape[0]

    def body(x_ref, w_ref, b_ref, o_ref):
        c = pl.program_id(0)
        scale = jnp.where(c < 4, 1.0, ML_SCALE).astype(F32)
        w = w_ref[...]
        b = b_ref[...]

        @pl.when(c < 8)
        def _():
            for n in range(S // RC):
                cur = x_ref[n * RC:(n + 1) * RC, :]
                prev8 = x_ref[n * RC - 8:n * RC, :] if n else jnp.zeros((8, 128), F32)
                pre = b + _conv_taps(cur, prev8, w)
                o_ref[n * RC:(n + 1) * RC, :] = (pre * jax.nn.sigmoid(pre) * scale).astype(BF16)

        @pl.when(c >= 8)
        def _():
            o_ref[...] = x_ref[...].astype(BF16)

    return pl.pallas_call(
        body, name="ml_prep", grid=(12,),
        in_specs=[pl.BlockSpec((S, 128), lambda c: (0, 16 + c)),
                  pl.BlockSpec((4, 128), lambda c: (0, jnp.minimum(c, 7))),
                  pl.BlockSpec((1, 128), lambda c: (0, jnp.minimum(c, 7)))],
        out_specs=pl.BlockSpec((S, 128), lambda c: (0, c)),
        out_shape=jax.ShapeDtypeStruct((S, 1536), BF16),
        compiler_params=_params(("arbitrary",)),
    )(proj, conv_w, conv_b)


def _ml_prep_bwd(proj, conv_w, conv_b, dq, dk):
    S = proj.shape[0]

    def body(x_ref, w_ref, b_ref, dq_ref, dk_ref, dx_ref, gw_ref, gb_ref, dp_ref):
        c = pl.program_id(0)
        w = w_ref[...]
        b = b_ref[...]
        gw = [jnp.zeros((1, 128), F32) for _ in range(4)]
        gb = jnp.zeros((1, 128), F32)
        for n in range(S // RC):
            rows = slice(n * RC, (n + 1) * RC)
            cur = x_ref[rows, :]
            prev8 = x_ref[n * RC - 8:n * RC, :] if n else jnp.zeros((8, 128), F32)
            pre = b + _conv_taps(cur, prev8, w)
            s = jax.nn.sigmoid(pre)
            dpost = jnp.where(c < 4, dq_ref[rows, :].astype(F32), dk_ref[rows, :].astype(F32) * ML_SCALE)
            dpre = dpost * (s * (1.0 + pre * (1.0 - s)))
            dp_ref[rows, :] = dpre
            win = jnp.concatenate([prev8, cur], axis=0)
            gb = gb + jnp.sum(dpre, axis=0, keepdims=True)
            gw[3] = gw[3] + jnp.sum(dpre * cur, axis=0, keepdims=True)
            for j in range(3):
                gw[j] = gw[j] + jnp.sum(dpre * pltpu.roll(win, 3 - j, 0)[8:8 + RC], axis=0, keepdims=True)
        dp_ref[S:S + 8, :] = jnp.zeros((8, 128), F32)
        for n in range(S // RC):
            win = dp_ref[n * RC:(n + 1) * RC + 8, :]
            dx = w[3:4, :] * win[:RC]
            for j in range(3):
                dx = dx + w[j:j + 1, :] * pltpu.roll(win, RC + 8 - (3 - j), 0)[:RC]
            dx_ref[n * RC:(n + 1) * RC, :] = dx
        gw_ref[...] = jnp.concatenate(gw, axis=0)
        gb_ref[...] = gb

    return pl.pallas_call(
        body, name="ml_prep_bwd", grid=(8,),
        in_specs=[pl.BlockSpec((S, 128), lambda c: (0, 16 + c)),
                  pl.BlockSpec((4, 128), lambda c: (0, c)),
                  pl.BlockSpec((1, 128), lambda c: (0, c)),
                  pl.BlockSpec((S, 128), lambda c: (0, jnp.minimum(c, 3))),
                  pl.BlockSpec((S, 128), lambda c: (0, jnp.maximum(c - 4, 0)))],
        out_specs=[pl.BlockSpec((S, 128), lambda c: (0, c)),
                   pl.BlockSpec((4, 128), lambda c: (0, c)),
                   pl.BlockSpec((1, 128), lambda c: (0, c))],
        out_shape=[jax.ShapeDtypeStruct((S, 1024), F32), jax.ShapeDtypeStruct((4, 1024), F32),
                   jax.ShapeDtypeStruct((1, 1024), F32)],
        scratch_shapes=[pltpu.VMEM((S + 8, 128), F32)],
        compiler_params=_params(("arbitrary",)),
    )(proj, conv_w, conv_b, dq, dk)


NH = 4


def _heads(x):
    return jnp.stack([x[:, 128 * h:128 * (h + 1)] for h in range(NH)])


def _unheads(x):
    return jnp.concatenate([x[h] for h in range(NH)], axis=1)


def _per_head(f, *xs):
    return jnp.stack([f(*[x[h] for x in xs]) for h in range(NH)])


def _ml_gates(g):
    lg = _log_sigmoid_parts(g)
    ig = jnp.stack([jnp.broadcast_to(g[:, h:h + 1], (LCH, 128)) for h in range(NH)])
    lf = jnp.stack([jnp.broadcast_to(lg[:, 4 + h:5 + h], (LCH, 128)) for h in range(NH)])
    return ig, lf


def _tri_sum(tri, x, dot):
    return _per_head(lambda a, b, c: dot(tri, a) + dot(tri, b) + dot(tri, c), *_split3(x))


def _ml_chunk_fwd(q, k, v, ig, lf, ct, n_st, m_st, tri, causal):
    vf = v.astype(F32)
    b = _tri_sum(tri, lf, _dot)
    b_last = b[:, LCH - 1:LCH, :]
    g = b_last - b + ig
    m_loc = jnp.max(g, axis=1, keepdims=True)
    w = jnp.exp(g - m_loc)
    vwf = vf * w
    vw = vwf.astype(BF16)
    ct_loc = _per_head(_dot_tn, k, vw)
    kf = k.astype(F32)
    n_loc = jnp.sum(w * kf, axis=1, keepdims=True)
    r = _per_head(jnp.transpose, ig - b)
    d_log = jnp.where(causal, b + r, -jnp.inf)
    m_t = jnp.maximum(b + m_st, jnp.max(d_log, axis=2, keepdims=True))
    w_in = jnp.exp(d_log - m_t)
    qk = _per_head(_dot_nt, q, k)
    scores = qk * w_in
    cs = jnp.exp(b + m_st - m_t)
    ctb = ct.astype(BF16)
    qc = _per_head(_dot, q, ctb)
    qf = q.astype(F32)
    qn = jnp.sum(qf * n_st, axis=2, keepdims=True)
    num = _per_head(_dot, scores.astype(BF16), v) + cs * qc
    den = jnp.sum(scores, axis=2, keepdims=True) + cs * qn
    em = jnp.exp(-m_t)
    dd = jnp.maximum(jnp.abs(den), em)
    h = num / dd
    m_new = jnp.maximum(b_last + m_st, m_loc)
    a = jnp.exp(b_last + m_st - m_new)
    gg = jnp.exp(m_loc - m_new)
    return dict(vwf=vwf, vw=vw, kf=kf, qf=qf, ct_loc=ct_loc, n_loc=n_loc, w=w, w_in=w_in, qk=qk, scores=scores,
                cs=cs, ctb=ctb, qc=qc, qn=qn, den=den, em=em, dd=dd, h=h, m_new=m_new, a=a, gg=gg)


def _ml_consts():
    row = lax.broadcasted_iota(jnp.int32, (LCH, LCH), 0)
    col = lax.broadcasted_iota(jnp.int32, (LCH, LCH), 1)
    return row, (col <= row), (col <= row).astype(BF16)


def _ml_rows(qkv_ref, g_ref, gb_ref, c):
    rows = pl.ds(pl.multiple_of(c * LCH, LCH), LCH)
    q, k, v = (_heads(qkv_ref[rows, 512 * t:512 * (t + 1)]) for t in range(3))
    ig, lf = _ml_gates(g_ref[rows, :] + gb_ref[...])
    return rows, q, k, v, ig, lf


VMEM_SPEC = pl.BlockSpec(memory_space=pltpu.VMEM)


def _ml_fwd(qkv, proj_g, gbias):
    S = qkv.shape[0]
    nc = S // LCH

    def body(qkv_ref, g_ref, gb_ref, h_ref, cst_ref, nm_ref, ct_ref, n_ref, m_ref):
        _, causal, tri = _ml_consts()
        ct_ref[...] = jnp.zeros_like(ct_ref)
        n_ref[...] = jnp.zeros_like(n_ref)
        m_ref[...] = jnp.zeros_like(m_ref)

        def chunk(c, carry):
            rows, q, k, v, ig, lf = _ml_rows(qkv_ref, g_ref, gb_ref, c)
            ct, n_st, m_st = ct_ref[...], n_ref[:, 0:1, :], m_ref[:, 0:1, :]
            cst_ref[c] = ct
            nm_ref[c, :, 0:8, :] = n_ref[...]
            nm_ref[c, :, 8:16, :] = m_ref[...]
            f = _ml_chunk_fwd(q, k, v, ig, lf, ct, n_st, m_st, tri, causal)
            h_ref[rows, :] = _unheads(f["h"])
            ct_ref[...] = f["a"] * ct + f["gg"] * f["ct_loc"]
            n_ref[...] = jnp.broadcast_to(f["a"] * n_st + f["gg"] * f["n_loc"], (NH, 8, 128))
            m_ref[...] = jnp.broadcast_to(f["m_new"], (NH, 8, 128))
            return carry

        lax.fori_loop(0, nc, chunk, 0)

    return pl.pallas_call(
        body, name="ml_fwd", in_specs=[VMEM_SPEC] * 3, out_specs=[VMEM_SPEC] * 3,
        out_shape=[jax.ShapeDtypeStruct((S, ML_W), F32), jax.ShapeDtypeStruct((nc, NH, 128, 128), F32),
                   jax.ShapeDtypeStruct((nc, NH, 16, 128), F32)],
        scratch_shapes=[pltpu.VMEM((NH, 128, 128), F32), pltpu.VMEM((NH, 8, 128), F32), pltpu.VMEM((NH, 8, 128), F32)],
        compiler_params=pltpu.CompilerParams(vmem_limit_bytes=VMEM_LIMIT),
    )(qkv, proj_g, gbias)


def _ml_bwd(qkv, proj_g, gbias, cst, nm, dh):
    S = qkv.shape[0]
    nc = S // LCH

    def body(qkv_ref, g_ref, gb_ref, cst_ref, nm_ref, dh_ref, dq_ref, dk_ref, dv_ref, dif_ref, gsum_ref,
             dct_ref, dn_ref, gi_ref):
        row, causal, tri = _ml_consts()
        lane = lax.broadcasted_iota(jnp.int32, (LCH, 128), 1)
        last_row = row == LCH - 1
        dct_ref[...] = jnp.zeros_like(dct_ref)
        dn_ref[...] = jnp.zeros_like(dn_ref)
        gi_ref[...] = jnp.zeros_like(gi_ref)

        def chunk(t, carry):
            c = nc - 1 - t
            rows, q, k, v, ig, lf = _ml_rows(qkv_ref, g_ref, gb_ref, c)
            ct, n_st, m_st = cst_ref[c], nm_ref[c, :, 0:1, :], nm_ref[c, :, 8:9, :]
            f = _ml_chunk_fwd(q, k, v, ig, lf, ct, n_st, m_st, tri, causal)
            dh_ = _heads(dh_ref[rows, :])
            dct_new, dn_new = dct_ref[...], dn_ref[:, 0:1, :]
            e_num = dh_ / f["dd"]
            hdh = jnp.sum(f["h"] * dh_, axis=2, keepdims=True)
            free = jnp.abs(f["den"]) > f["em"]
            e_den = jnp.where(free, -hdh / f["dd"] * jnp.sign(f["den"]), 0.0)
            e_num_b = e_num.astype(BF16)
            ds_ = _per_head(_dot_nt, e_num_b, v) + e_den
            dqk = ds_ * f["w_in"]
            gam = dqk * f["qk"]
            dqk_b = dqk.astype(BF16)
            cse = f["cs"] * e_den
            dq = _per_head(_dot, dqk_b, k) + f["cs"] * _per_head(_dot_nt, e_num_b, f["ctb"]) + cse * n_st
            dk = _per_head(_dot_tn, dqk_b, q)
            dv = _per_head(_dot_tn, f["scores"].astype(BF16), e_num_b)
            dcl = (f["gg"] * dct_new).astype(BF16)
            dnl = f["gg"] * dn_new
            kd = _per_head(_dot, k, dcl)
            dv = dv + f["w"] * kd
            dk = dk + _per_head(_dot_nt, f["vw"], dcl) + f["w"] * dnl
            gam_s = (jnp.sum(kd * f["vwf"], axis=2, keepdims=True)
                     + f["w"] * jnp.sum(f["kf"] * dnl, axis=2, keepdims=True))
            col_g = jnp.sum(_per_head(jnp.transpose, gam), axis=2, keepdims=True) + gam_s
            db = (jnp.sum(gam, axis=2, keepdims=True) + jnp.sum(e_num * (f["cs"] * f["qc"]), axis=2, keepdims=True)
                  + cse * f["qn"] - col_g)
            state = jnp.sum(jnp.sum(dct_new * ct, axis=2, keepdims=True), axis=1, keepdims=True)
            state = state + jnp.sum(dn_new * n_st, axis=2, keepdims=True)
            db_last = jnp.sum(gam_s[:, :, 0:1], axis=1, keepdims=True) + f["a"][:, :, 0:1] * state
            db = jnp.where(last_row, db + db_last, db)
            dlf = _tri_sum(tri, db, _dot_tn)
            df = dlf * (1.0 - jnp.exp(lf))
            dq_ref[rows, :] = _unheads(dq).astype(BF16)
            dk_ref[rows, :] = _unheads(dk).astype(BF16)
            dv_ref[rows, :] = _unheads(dv).astype(BF16)
            dif = jnp.zeros((LCH, 128), F32)
            for h in range(NH):
                dif = dif + jnp.where(lane == h, col_g[h], 0.0) + jnp.where(lane == h + 4, df[h], 0.0)
            dif_ref[rows, :] = dif
            clamped = jnp.where(free, 0.0, hdh)
            gi_ref[...] += jnp.broadcast_to(jnp.sum(clamped, axis=1, keepdims=True), (NH, 8, 128))
            dct_ref[...] = f["a"] * dct_new + _per_head(_dot_tn, q, (f["cs"] * e_num).astype(BF16))
            dn_ref[...] = jnp.broadcast_to(f["a"] * dn_new + jnp.sum(cse * f["qf"], axis=1, keepdims=True), (NH, 8, 128))
            return carry

        lax.fori_loop(0, nc, chunk, 0)
        lane8 = lax.broadcasted_iota(jnp.int32, (8, 128), 1)
        gsum = jnp.where(lane8 >= 4, jnp.sum(dif_ref[...], axis=0, keepdims=True), 0.0)
        for h in range(NH):
            gsum = gsum + jnp.where(lane8 == h, gi_ref[h], 0.0)
        gsum_ref[...] = gsum

    return pl.pallas_call(
        body, name="ml_bwd", in_specs=[VMEM_SPEC] * 6, out_specs=[VMEM_SPEC] * 5,
        out_shape=[jax.ShapeDtypeStruct((S, ML_W), BF16)] * 3 + [jax.ShapeDtypeStruct((S, 128), F32),
                                                                  jax.ShapeDtypeStruct((8, 128), F32)],
        scratch_shapes=[pltpu.VMEM((NH, 128, 128), F32), pltpu.VMEM((NH, 8, 128), F32), pltpu.VMEM((NH, 8, 128), F32)],
        compiler_params=pltpu.CompilerParams(vmem_limit_bytes=VMEM_LIMIT),
    )(qkv, proj_g, gbias, cst, nm, dh)


MESH = pl.DeviceIdType.MESH
ANY = pl.BlockSpec(memory_space=pl.ANY)
N_DEV = 8


def _place():
    return lax.axis_index("x"), lax.axis_index("y"), lax.axis_index("c")


def _block_of(px, py, pc):
    return 4 * px + 2 * py + pc


def _copies_to_all(b_ref, o_ref, send_sems, recv_sems, local_sem):
    x, y, c = _place()
    mine = _block_of(x, y, c)
    copies = [pltpu.make_async_copy(b_ref, o_ref.at[mine], local_sem)]
    for k in range(1, N_DEV):
        peer = (x ^ (k >> 2), y ^ ((k >> 1) & 1), c ^ (k & 1))
        copies.append(pltpu.make_async_remote_copy(
            src_ref=b_ref, dst_ref=o_ref.at[mine], send_sem=send_sems.at[k - 1], recv_sem=recv_sems.at[k - 1],
            device_id=peer, device_id_type=MESH))
    return copies


def _copies_between_chips(p_ref, o_ref, send_sems, recv_sems, local_sem):
    x, y, c = _place()
    mine = 2 * x + y
    copies = [pltpu.make_async_copy(p_ref.at[mine], o_ref.at[mine], local_sem)]
    for k in range(1, 4):
        px, py = x ^ (k >> 1), y ^ (k & 1)
        copies.append(pltpu.make_async_remote_copy(
            src_ref=p_ref.at[2 * px + py], dst_ref=o_ref.at[mine], send_sem=send_sems.at[k - 1],
            recv_sem=recv_sems.at[k - 1], device_id=(px, py, c), device_id_type=MESH))
    return copies


def _around_grid(copies, first, last):
    @pl.when(first)
    def _():
        for cp in copies():
            cp.start()

    @pl.when(last)
    def _():
        for cp in copies():
            cp.wait()


def _inproj_fwd(x, pre_w, w_t, wg_t, blk=None):
    S, D = x.shape
    tm, tn = min(S, 1024), 1152
    ni, nj = S // tm, N_MAIN // tn

    def body(x_ref, pw_ref, w_ref, wg_ref, *rest):
        if blk is None:
            proj_ref, g_ref, u_ref = rest
        else:
            b_ref, proj_ref, g_ref, u_ref, o_ref, send_sems, recv_sems, local_sem = rest
            i, j = pl.program_id(0), pl.program_id(1)
            _around_grid(lambda: _copies_to_all(b_ref, o_ref, send_sems, recv_sems, local_sem),
                         (i == 0) & (j == 0), (i == ni - 1) & (j == nj - 1))

        @pl.when(pl.program_id(1) == 0)
        def _():
            xf = x_ref[...]
            r = lax.rsqrt(jnp.mean(xf * xf, axis=-1, keepdims=True) + EPS)
            u = (xf * r * pw_ref[...]).astype(BF16)
            u_ref[...] = u
            g_ref[...] = _dot_nt(u, wg_ref[...])

        proj_ref[...] = _dot_nt(u_ref[...], w_ref[...])

    in_specs = [pl.BlockSpec((tm, D), lambda i, j: (i, 0)),
                pl.BlockSpec((1, D), lambda i, j: (0, 0)),
                pl.BlockSpec((tn, D), lambda i, j: (j, 0)),
                pl.BlockSpec((128, D), lambda i, j: (0, 0))]
    out_specs = [pl.BlockSpec((tm, tn), lambda i, j: (i, j)),
                 pl.BlockSpec((tm, 128), lambda i, j: (i, 0)),
                 pl.BlockSpec((tm, D), lambda i, j: (i, 0))]
    out_shape = [jax.ShapeDtypeStruct((S, N_MAIN), F32), jax.ShapeDtypeStruct((S, 128), F32),
                 jax.ShapeDtypeStruct((S, D), BF16)]
    operands, scratch = (x, pre_w, w_t, wg_t), []
    if blk is not None:
        in_specs, out_specs, operands = in_specs + [ANY], out_specs + [ANY], operands + (blk,)
        out_shape = out_shape + [jax.ShapeDtypeStruct((N_DEV,) + blk.shape, blk.dtype)]
        scratch = [pltpu.SemaphoreType.DMA((7,)), pltpu.SemaphoreType.DMA((7,)), pltpu.SemaphoreType.DMA]
    return pl.pallas_call(
        body, name="inproj_fwd", grid=(ni, nj), in_specs=in_specs, out_specs=out_specs, out_shape=out_shape,
        scratch_shapes=scratch, compiler_params=_params(("arbitrary", "arbitrary")),
    )(*operands)


def _inproj_bwd(d_main, d_if, w_t, wg_t, x, pre_w, dx_tail, parts=None):
    S, D = x.shape
    tm, tk = min(S, 1024), 1152
    ni, nk = S // tm, N_MAIN // tk

    def body(d_ref, dg_ref, w_ref, wg_ref, x_ref, pw_ref, dt_ref, *rest):
        i, k = pl.program_id(0), pl.program_id(1)
        if parts is None:
            dx_ref, gpw_ref, acc_ref = rest
        else:
            p_ref, dx_ref, gpw_ref, o_ref, acc_ref, send_sems, recv_sems, local_sem = rest
            _around_grid(lambda: _copies_between_chips(p_ref, o_ref, send_sems, recv_sems, local_sem),
                         (i == 0) & (k == 0), (i == ni - 1) & (k == nk - 1))

        @pl.when(k == 0)
        def _():
            acc_ref[...] = _dot(dg_ref[...], wg_ref[...])

        acc_ref[...] += _dot(d_ref[...], w_ref[...])

        @pl.when(k == nk - 1)
        def _():
            xf = x_ref[...]
            r = lax.rsqrt(jnp.mean(xf * xf, axis=-1, keepdims=True) + EPS)
            xn = xf * r
            du = acc_ref[...]
            gw = du * pw_ref[...]
            dx_ref[...] = dt_ref[...] + r * (gw - xn * jnp.mean(gw * xn, axis=-1, keepdims=True))
            part = jnp.sum(du * xn, axis=0, keepdims=True)

            @pl.when(i == 0)
            def _():
                gpw_ref[...] = part

            @pl.when(i > 0)
            def _():
                gpw_ref[...] += part

    in_specs = [pl.BlockSpec((tm, tk), lambda i, k: (i, k)),
                pl.BlockSpec((tm, 128), lambda i, k: (i, 0)),
                pl.BlockSpec((tk, D), lambda i, k: (k, 0)),
                pl.BlockSpec((128, D), lambda i, k: (0, 0)),
                pl.BlockSpec((tm, D), lambda i, k: (i, 0)),
                pl.BlockSpec((1, D), lambda i, k: (0, 0)),
                pl.BlockSpec((tm, D), lambda i, k: (i, 0))]
    out_specs = [pl.BlockSpec((tm, D), lambda i, k: (i, 0)), pl.BlockSpec((1, D), lambda i, k: (0, 0))]
    out_shape = [jax.ShapeDtypeStruct((S, D), F32), jax.ShapeDtypeStruct((1, D), F32)]
    operands, scratch = (d_main, d_if, w_t, wg_t, x, pre_w, dx_tail), [pltpu.VMEM((tm, D), F32)]
    if parts is not None:
        in_specs, out_specs, operands = in_specs + [ANY], out_specs + [ANY], operands + (parts,)
        out_shape = out_shape + [jax.ShapeDtypeStruct(parts.shape, parts.dtype)]
        scratch = scratch + [pltpu.SemaphoreType.DMA((3,)), pltpu.SemaphoreType.DMA((3,)), pltpu.SemaphoreType.DMA]
    return pl.pallas_call(
        body, name="inproj_bwd", grid=(ni, nk), in_specs=in_specs, out_specs=out_specs, out_shape=out_shape,
        scratch_shapes=scratch, compiler_params=_params(("arbitrary", "arbitrary")),
    )(*operands)


def _matmul_tn(a, b, name, out_rows=None):
    S, M = a.shape
    N = b.shape[1]
    tmm = 1152 if M % 1152 == 0 else min(M, 1024)
    tk = min(S, 1024)
    nk = S // tk

    def body(a_ref, b_ref, o_ref):
        part = _dot_tn(a_ref[...].astype(BF16), b_ref[...].astype(BF16))

        @pl.when(pl.program_id(1) == 0)
        def _():
            o_ref[...] = part

        @pl.when(pl.program_id(1) > 0)
        def _():
            o_ref[...] += part

    return pl.pallas_call(
        body, name=name, grid=(M // tmm, nk),
        in_specs=[pl.BlockSpec((tk, tmm), lambda i, k: (k, i)),
                  pl.BlockSpec((tk, N), lambda i, k: (k, 0))],
        out_specs=pl.BlockSpec((tmm, N), lambda i, k: (i, 0)),
        out_shape=jax.ShapeDtypeStruct((out_rows or M, N), F32),
        compiler_params=_params(("arbitrary", "arbitrary")),
    )(a, b)


def _gate_rows_tn(d_if, u, g):
    S, D = u.shape
    tk = min(S, 1024)
    nk = S // tk

    def body(a_ref, b_ref, g_ref, o_ref, acc_ref):
        k = pl.program_id(0)
        part = _dot_tn(a_ref[...], b_ref[...])

        @pl.when(k == 0)
        def _():
            acc_ref[...] = part

        @pl.when(k > 0)
        def _():
            acc_ref[...] += part

        @pl.when(k == nk - 1)
        def _():
            o_ref[...] = acc_ref[0:8, :]

    return pl.pallas_call(
        body, name="gw_in_gates", grid=(nk,),
        in_specs=[pl.BlockSpec((tk, 128), lambda k: (k, 0)), pl.BlockSpec((tk, D), lambda k: (k, 0)), ANY],
        out_specs=pl.BlockSpec((8, D), lambda k: (N_MAIN // 8, 0)),
        out_shape=jax.ShapeDtypeStruct(g.shape, g.dtype),
        scratch_shapes=[pltpu.VMEM((128, D), F32)],
        input_output_aliases={2: 0},
        compiler_params=_params(("arbitrary",)),
    )(d_if, u, g)


def _half_mean(v, low):
    s_lo = jnp.sum(jnp.where(low, v, 0.0), axis=1, keepdims=True)
    s_hi = jnp.sum(jnp.where(low, 0.0, v), axis=1, keepdims=True)
    return jnp.where(low, s_lo, s_hi) * (1.0 / 64.0)


def _silu_grad(z, s):
    return s * (1.0 + z * (1.0 - s))


def _tail(y_sb, h_ml, proj, x, p, target, sb_nw, ml_nw, w_out, post_w, w_gate, b_gate, w_up):
    S, D = x.shape
    tm = 256

    def body(ysb_ref, hml_ref, sbz_ref, mlo_ref, mlz_ref, x_ref, p_ref, tg_ref, sbw_ref, mlw_ref, wo_ref, pw_ref,
             wg_ref, bg_ref, wu_ref,
             dx_ref, dysb_ref, dhml_ref, dsbz_ref, dmlo_ref, dmlz_ref, mix_ref, dy_ref, h1_ref, dgp_ref, dpu_ref,
             small_ref):
        lane = lax.broadcasted_iota(jnp.int32, (tm, 128), 1)
        low = lane < 64
        sb_saved, ml_saved, mixed = [], [], []
        for s in range(4):
            sl = slice(128 * s, 128 * s + 128)
            y = ysb_ref[:, sl]
            rs = lax.rsqrt(_half_mean(y * y, low) + EPS)
            n = y * rs
            z = sbz_ref[:, sl]
            sg = jax.nn.sigmoid(z)
            w = sbw_ref[:, sl]
            mixed.append((n * w) * (z * sg))
            sb_saved.append((rs, n, z, sg, w))
        for s in range(4):
            sl = slice(128 * s, 128 * s + 128)
            og = jax.nn.sigmoid(mlo_ref[:, sl])
            hh = hml_ref[:, sl]
            t = og * hh
            rs = lax.rsqrt(jnp.mean(t * t, axis=1, keepdims=True) + EPS)
            n = t * rs
            z = mlz_ref[:, sl]
            sg = jax.nn.sigmoid(z)
            w = mlw_ref[:, sl]
            mixed.append((n * w) * (z * sg))
            ml_saved.append((rs, n, z, sg, w, og, hh))
        mix = jnp.concatenate(mixed, axis=1).astype(BF16)
        mix_ref[...] = mix
        y = _dot(mix, wo_ref[...])
        rs_y = lax.rsqrt(jnp.mean(y * y, axis=1, keepdims=True) + EPS)
        yn = y * rs_y
        pw = pw_ref[...]
        h1 = x_ref[...] + yn * pw
        h1b = h1.astype(BF16)
        h1_ref[...] = h1b
        gate = jax.nn.sigmoid(_dot(h1b, wg_ref[...]) + bg_ref[...])
        pu = _dot(p_ref[...].astype(BF16), wu_ref[...])
        err = (h1 + gate * pu) - tg_ref[...]
        loss = 0.5 * jnp.sum(jnp.sum(err * err, axis=1, keepdims=True) * (1.0 / D))
        d_out = err * (1.0 / D)
        dpu_ref[...] = (d_out * gate).astype(BF16)
        dgp = (d_out * pu) * (gate * (1.0 - gate))
        dgpb = dgp.astype(BF16)
        dgp_ref[...] = dgpb
        d_h1 = d_out + _dot_nt(dgpb, wg_ref[...])
        dx_ref[...] = d_h1
        gwy = d_h1 * pw
        d_y = rs_y * (gwy - yn * jnp.mean(gwy * yn, axis=1, keepdims=True))
        d_yb = d_y.astype(BF16)
        dy_ref[...] = d_yb
        d_mix = _dot_nt(d_yb, wo_ref[...])
        g_nw = []
        for s in range(4):
            sl = slice(128 * s, 128 * s + 128)
            rs, n, z, sg, w = sb_saved[s]
            da = d_mix[:, sl]
            act = z * sg
            dsbz_ref[:, sl] = (da * (n * w) * _silu_grad(z, sg)).astype(BF16)
            dn = da * w * act
            g_nw.append(jnp.sum(da * act * n, axis=0, keepdims=True))
            dysb_ref[:, sl] = rs * (dn - n * _half_mean(dn * n, low))
        for s in range(4):
            sl = slice(128 * s, 128 * s + 128)
            rs, n, z, sg, w, og, hh = ml_saved[s]
            da = d_mix[:, 512 + 128 * s:512 + 128 * s + 128]
            act = z * sg
            dmlz_ref[:, sl] = (da * (n * w) * _silu_grad(z, sg)).astype(BF16)
            dn = da * w * act
            g_nw.append(jnp.sum(da * act * n, axis=0, keepdims=True))
            dt = rs * (dn - n * jnp.mean(dn * n, axis=1, keepdims=True))
            dmlo_ref[:, sl] = (dt * hh * (og * (1.0 - og))).astype(BF16)
            dhml_ref[:, sl] = dt * og
        upd = jnp.concatenate([
            jnp.sum(d_h1 * yn, axis=0, keepdims=True),
            jnp.sum(dgp, axis=0, keepdims=True),
            jnp.concatenate(g_nw, axis=1),
            jnp.full((1, D), loss, F32),
            jnp.zeros((4, D), F32)], axis=0)

        @pl.when(pl.program_id(0) == 0)
        def _():
            small_ref[...] = upd

        @pl.when(pl.program_id(0) > 0)
        def _():
            small_ref[...] += upd

    def rows(width, col=0):
        return pl.BlockSpec((tm, width), lambda i: (i, col))

    def whole(a):
        return pl.BlockSpec(a.shape, lambda i: (0, 0))

    return pl.pallas_call(
        body, name="tail", grid=(S // tm,),
        in_specs=[rows(512), rows(512), rows(512, 3), rows(512, 7), rows(512, 8), rows(D), rows(256), rows(D),
                  whole(sb_nw), whole(ml_nw), whole(w_out), whole(post_w), whole(w_gate), whole(b_gate), whole(w_up)],
        out_specs=[rows(D), rows(512), rows(512), rows(512), rows(512), rows(512), rows(D), rows(D), rows(D), rows(D),
                   rows(D), pl.BlockSpec((8, D), lambda i: (0, 0))],
        out_shape=[jax.ShapeDtypeStruct((S, D), F32), jax.ShapeDtypeStruct((S, 512), F32),
                   jax.ShapeDtypeStruct((S, 512), F32)] + [jax.ShapeDtypeStruct((S, 512), BF16)] * 3
        + [jax.ShapeDtypeStruct((S, D), BF16)] * 5 + [jax.ShapeDtypeStruct((8, D), F32)],
        compiler_params=_params(("arbitrary",)),
    )(y_sb, h_ml, proj, proj, proj, x, p, target, sb_nw, ml_nw, w_out, post_w, w_gate, b_gate, w_up)


def _local_step(x, p, target, pre_w, w_t, wg_t, conv_w, conv_b, gbias, sb_nw, ml_nw, post_w, b_gate, late,
                exchange=None):
    if callable(late[1]):
        proj, proj_g, u, gathered = _inproj_fwd(x, pre_w, w_t, wg_t, late[0])
        w_out, w_gate, w_up = late[1](gathered)
    else:
        proj, proj_g, u = _inproj_fwd(x, pre_w, w_t, wg_t)
        w_out, w_gate, w_up = late
    y_sb, tot = _sb_fwd(proj)
    qkv = _ml_prep(proj, conv_w, conv_b)
    h_ml, cst, nm = _ml_fwd(qkv, proj_g, gbias)
    dx_tail, d_ysb, d_hml, d_sbz, d_mlo, d_mlz, mix, d_y, h1, dgp, dpu, small = _tail(
        y_sb, h_ml, proj, x, p, target, sb_nw, ml_nw, w_out, post_w, w_gate, b_gate, w_up)
    dq, dk, dv = _sb_bwd(proj, tot, d_ysb)
    dqc, dks, dmlv, dif, gif = _ml_bwd(qkv, proj_g, gbias, cst, nm, d_hml)
    dmlqk, g_cw, g_cb = _ml_prep_bwd(proj, conv_w, conv_b, dqc, dks)
    d_main = jnp.concatenate([dq.astype(BF16), dk.astype(BF16), dv.astype(BF16), d_sbz, dmlqk.astype(BF16),
                              dmlv, d_mlo, d_mlz], axis=1)
    d_if = dif.astype(BF16)
    grads = dict(
        w_in_t=_gate_rows_tn(d_if, u, _matmul_tn(d_main, u, "gw_in", N_IN)),
        w_out=_matmul_tn(mix, d_y, "gw_out"), w_gate=_matmul_tn(h1, dgp, "gw_gate"), w_up=_matmul_tn(p, dpu, "gw_up"),
        conv_w=g_cw, conv_b=g_cb, gif=gif)
    parts = exchange(grads) if exchange else None
    return _inproj_bwd(d_main, d_if, w_t, wg_t, x, pre_w, dx_tail, parts), grads, small


def _all_gather(a, b):
    def body(a_ref, b_ref, oa_ref, ob_ref, send_sems, recv_sems, local_sems):
        x, y, c = _place()
        me, sibling = (x, y, c), (x, y, 1 - c)
        chips = [(1 - x, y), (x, 1 - y), (1 - x, 1 - y)]
        pairs = ((a_ref, oa_ref), (b_ref, ob_ref))

        def copies(k, block, to, from_input=False):
            slot = _block_of(*block)
            return [pltpu.make_async_remote_copy(
                src_ref=src if from_input else out.at[slot], dst_ref=out.at[slot],
                send_sem=send_sems.at[t, k], recv_sem=recv_sems.at[t, k], device_id=to, device_id_type=MESH)
                for t, (src, out) in enumerate(pairs)]

        mine = [pltpu.make_async_copy(src, out.at[_block_of(*me)], local_sems.at[t])
                for t, (src, out) in enumerate(pairs)]
        for cp in mine:
            cp.start()
        first = copies(0, me, sibling, True)
        for j, chip in enumerate(chips):
            first += copies(1 + j, me, (*chip, c), True)
        for cp in first:
            cp.start()
        passed = []
        for j, chip in enumerate(chips):
            for cp in copies(1 + j, (*chip, c), me):
                cp.wait_recv()
            fwd = copies(4 + j, (*chip, c), sibling)
            for cp in fwd:
                cp.start()
            passed += fwd
        for cp in copies(0, sibling, me):
            cp.wait_recv()
        for j, chip in enumerate(chips):
            for cp in copies(4 + j, (*chip, 1 - c), me):
                cp.wait_recv()
        for cp in first + passed:
            cp.wait_send()
        for cp in mine:
            cp.wait()

    return pl.pallas_call(
        body, name="all_gather",
        in_specs=[ANY, ANY], out_specs=[ANY, ANY],
        out_shape=[jax.ShapeDtypeStruct((N_DEV,) + a.shape, a.dtype), jax.ShapeDtypeStruct((N_DEV,) + b.shape, b.dtype)],
        scratch_shapes=[pltpu.SemaphoreType.DMA((2, 7)), pltpu.SemaphoreType.DMA((2, 7)), pltpu.SemaphoreType.DMA((2,))],
    )(a, b)


def _exchange_pair(g):
    def body(g_ref, og_ref, send_sems, recv_sems):
        x, y, c = _place()
        sent = [pltpu.make_async_remote_copy(
            src_ref=g_ref.at[k, 1 - c], dst_ref=og_ref.at[k], send_sem=send_sems.at[k], recv_sem=recv_sems.at[k],
            device_id=(x, y, 1 - c), device_id_type=MESH) for k in range(4)]
        for cp in sent:
            cp.start()
        for cp in sent:
            cp.wait()

    return pl.pallas_call(
        body, name="exchange_pair", in_specs=[ANY], out_specs=ANY,
        out_shape=jax.ShapeDtypeStruct((4,) + g.shape[2:], g.dtype),
        scratch_shapes=[pltpu.SemaphoreType.DMA((4,)), pltpu.SemaphoreType.DMA((4,))],
    )(g)


def _pair_sum(g, r, core, tr):
    _, _, R, D = g.shape

    def body(c_ref, g_ref, r_ref, o_ref):
        o_ref[...] = (g_ref[...] + r_ref[...]).astype(BF16)

    return pl.pallas_call(
        body, name="pair_sum",
        grid_spec=pltpu.PrefetchScalarGridSpec(
            num_scalar_prefetch=1, grid=(4, R // tr),
            in_specs=[pl.BlockSpec((None, None, tr, D), lambda k, i, c: (k, c[0], i, 0)),
                      pl.BlockSpec((None, tr, D), lambda k, i, c: (k, i, 0))],
            out_specs=pl.BlockSpec((None, tr, D), lambda k, i, c: (k, i, 0))),
        out_shape=jax.ShapeDtypeStruct((4, R, D), BF16),
        compiler_params=_params(("arbitrary", "arbitrary")),
    )(core, g, r)


ADAM_LR, ADAM_B1, ADAM_B2, ADAM_EPS, ADAM_WD, ADAM_STEP = 0.001, 0.9, 0.999, 1e-08, 0.01, 10


def _adamw(w, g, m, v):
    m = ADAM_B1 * m + (1.0 - ADAM_B1) * g
    v = ADAM_B2 * v + (1.0 - ADAM_B2) * (g * g)
    m_hat = m / (1.0 - ADAM_B1 ** ADAM_STEP)
    v_hat = v / (1.0 - ADAM_B2 ** ADAM_STEP)
    return -ADAM_LR * (m_hat / (jnp.sqrt(v_hat) + ADAM_EPS) + ADAM_WD * w), m, v


def _adam(parts, w, m, v, tr, name, small=None):
    R, D = w.shape
    n = parts.shape[0]
    steps = R // tr

    def body(p_ref, w_ref, m_ref, v_ref, *rest):
        if small is None:
            g_ref, d_ref, nm_ref, nv_ref = rest
        else:
            s_ref, g_ref, d_ref, nm_ref, nv_ref, o_ref, send_sems, recv_sems, local_sem = rest
            i = pl.program_id(0)
            _around_grid(lambda: _copies_to_all(s_ref, o_ref, send_sems, recv_sems, local_sem), i == 0, i == steps - 1)
        g = p_ref[0].astype(F32)
        for k in range(1, n):
            g = g + p_ref[k].astype(F32)
        g_ref[...] = g
        d_ref[...], nm_ref[...], nv_ref[...] = _adamw(w_ref[...], g, m_ref[...], v_ref[...])

    blk = pl.BlockSpec((tr, D), lambda i: (i, 0))
    in_specs = [pl.BlockSpec((n, tr, D), lambda i: (0, i, 0)), blk, blk, blk]
    out_specs, out_shape = [blk] * 4, [jax.ShapeDtypeStruct((R, D), F32)] * 4
    operands, scratch = (parts, w, m, v), []
    if small is not None:
        in_specs, out_specs, operands = in_specs + [ANY], out_specs + [ANY], operands + (small,)
        out_shape = out_shape + [jax.ShapeDtypeStruct((N_DEV,) + small.shape, small.dtype)]
        scratch = [pltpu.SemaphoreType.DMA((7,)), pltpu.SemaphoreType.DMA((7,)), pltpu.SemaphoreType.DMA]
    return pl.pallas_call(
        body, name=name, grid=(steps,), in_specs=in_specs, out_specs=out_specs, out_shape=out_shape,
        scratch_shapes=scratch, compiler_params=_params(("arbitrary",)),
    )(*operands)


ROWS_IN = 592
ROWS_BF16 = ROWS_IN + 128 + 128 + 32
ROWS_CONV = 16
ROWS_ALL = ROWS_BF16 + ROWS_CONV
ROW_TILE = 224


def _pad_rows(a, rows):
    return jnp.pad(a, ((0, rows - a.shape[0]), (0, 0)))


def _pack_shards(w_in, w_out, w_gate, w_up, conv_w):
    return jnp.concatenate([
        _pad_rows(w_in[0].T, ROWS_IN), w_out[0], w_gate[0], w_up[0].reshape(32, D_MODEL),
        _pad_rows(jnp.pad(conv_w[0].reshape(1, 512), ((0, 0), (0, 512))), ROWS_CONV)], axis=0)


def _unpack_shards(a):
    return (a[:SHARD_IN].T[None], a[ROWS_IN:ROWS_IN + 128][None], a[ROWS_IN + 128:ROWS_IN + 256][None],
            a[ROWS_IN + 256:ROWS_BF16].reshape(1, 256, 128), a[ROWS_BF16, :512].reshape(1, 4, 128))


def _pack_small(pre_w, conv_b, i_bias, f_bias, sb_nw, ml_nw, post_w, b_gate):
    gates = jnp.pad(jnp.concatenate([i_bias, f_bias], axis=1), ((0, 0), (0, D_MODEL - 8)))
    return jnp.concatenate([post_w, b_gate, jnp.concatenate([sb_nw, ml_nw], axis=1), jnp.zeros((1, D_MODEL), F32),
                            pre_w, conv_b, gates, jnp.zeros((1, D_MODEL), F32)], axis=0)


def _unpack_small(a):
    return a[4:5], a[5:6], a[6:7, 0:4], a[6:7, 4:8], a[2:3, :512], a[2:3, 512:], a[0:1], a[1:2]


def kernel(x, p, pre_norm_w, w_in, ml_conv_w, ml_conv_b, ml_i_bias, ml_f_bias, sb_norm_w, ml_norm_w, w_out, post_norm_w, ple_w_up, ple_w_gate, ple_b_gate, loss_target, m_pre_norm_w, m_w_in, m_ml_conv_w, m_ml_conv_b, m_ml_i_bias, m_ml_f_bias, m_sb_norm_w, m_ml_norm_w, m_w_out, m_post_norm_w, m_ple_w_up, m_ple_w_gate, m_ple_b_gate, v_pre_norm_w, v_w_in, v_ml_conv_w, v_ml_conv_b, v_ml_i_bias, v_ml_f_bias, v_sb_norm_w, v_ml_norm_w, v_w_out, v_post_norm_w, v_ple_w_up, v_ple_w_gate, v_ple_b_gate):
    D = D_MODEL
    w_pk = _pack_shards(w_in, w_out, ple_w_gate, ple_w_up, ml_conv_w)
    m_pk = _pack_shards(m_w_in, m_w_out, m_ple_w_gate, m_ple_w_up, m_ml_conv_w)
    v_pk = _pack_shards(v_w_in, v_w_out, v_ple_w_gate, v_ple_w_up, v_ml_conv_w)
    w_sm = _pack_small(pre_norm_w, ml_conv_b, ml_i_bias, ml_f_bias, sb_norm_w, ml_norm_w, post_norm_w, ple_b_gate)
    m_sm = _pack_small(m_pre_norm_w, m_ml_conv_b, m_ml_i_bias, m_ml_f_bias, m_sb_norm_w, m_ml_norm_w, m_post_norm_w, m_ple_b_gate)
    v_sm = _pack_small(v_pre_norm_w, v_ml_conv_b, v_ml_i_bias, v_ml_f_bias, v_sb_norm_w, v_ml_norm_w, v_post_norm_w, v_ple_b_gate)

    w_bf = w_pk[:ROWS_BF16].astype(BF16)
    ga, gb = _all_gather(w_bf[:SHARD_IN], w_pk[ROWS_BF16:])
    w_in_t = ga.reshape(N_IN, D)
    wg_t = _pad_rows(w_in_t[N_MAIN:], 128)
    conv_w_f = gb[:, 0, :512].reshape(N_DEV, 4, 128).transpose(1, 0, 2).reshape(4, D)
    gbias = jnp.pad(jnp.concatenate([ml_i_bias, ml_f_bias], axis=1), ((0, 0), (0, 120)))

    def unpack_late(gl):
        return (gl[:, :128].reshape(D, D), gl[:, 128:256].reshape(D, D),
                gl[:, 256:].reshape(N_DEV, 256, 128).transpose(1, 0, 2).reshape(256, D))

    def exchange(g):
        g_in = g["w_in_t"].reshape(N_DEV, SHARD_IN, D)
        g_blocks = jnp.concatenate([
            jnp.pad(g_in, ((0, 0), (0, ROWS_IN - SHARD_IN), (0, 0))),
            g["w_out"].reshape(N_DEV, 128, D), g["w_gate"].reshape(N_DEV, 128, D),
            g["w_up"].reshape(256, N_DEV, 128).transpose(1, 0, 2).reshape(N_DEV, 32, D),
            jnp.pad(g["conv_w"].reshape(4, N_DEV, 128).transpose(1, 0, 2).reshape(N_DEV, 1, 512),
                    ((0, 0), (0, ROWS_CONV - 1), (0, 512))),
        ], axis=1).reshape(4, 2, ROWS_ALL, D)
        core = lax.axis_index("c").astype(jnp.int32).reshape(1)
        return _pair_sum(g_blocks, _exchange_pair(g_blocks), core, ROW_TILE)

    (dx, g_pre, parts), g, small = _local_step(
        x[0], p[0, 0], loss_target[0], pre_norm_w, w_in_t, wg_t, conv_w_f, ml_conv_b, gbias, sb_norm_w, ml_norm_w,
        post_norm_w, ple_b_gate, (w_bf[ROWS_IN:], unpack_late), exchange)

    g_small = jnp.concatenate([small[0:4], g_pre, g["conv_b"], jnp.pad(g["gif"][0:1], ((0, 0), (0, D - 128))),
                               jnp.zeros((1, D), F32)], axis=0)
    grad_pk, delta_pk, nm_pk, nv_pk, parts_sm = _adam(parts, w_pk, m_pk, v_pk, ROW_TILE, "adam", g_small)
    grad_sm, delta_sm, nm_sm, nv_sm = _adam(parts_sm, w_sm, m_sm, v_sm, 8, "adam_small")
    loss = grad_sm[3, 0]

    def ordered(pk, sm):
        win, wout, wgate, wup, convw = _unpack_shards(pk)
        pre_w, conv_b, i_b, f_b, sb_nw, ml_nw, post_w, b_gate = _unpack_small(sm)
        return [pre_w, win, convw, conv_b, i_b, f_b, sb_nw, ml_nw, wout, post_w, wup, wgate, b_gate]

    return (loss, dx[None], *ordered(grad_pk, grad_sm), *ordered(delta_pk, delta_sm), *ordered(nm_pk, nm_sm),
            *ordered(nv_pk, nv_sm))
```

```python
import functools

import jax
import jax.numpy as jnp
from jax import lax
from jax.experimental import pallas as pl
from jax.experimental.pallas import tpu as pltpu

F32 = jnp.float32
BF16 = jnp.bfloat16
EPS = 1e-6
D_MODEL = 1024
SB_W = 512
ML_W = 512
N_MAIN = 4608
N_IN = 4616
SHARD_IN = 577
SHARD_IN_PAD = 584
TQ = 1024
TK = 256
ND = TQ // TK
LCH = 128
VMEM_LIMIT = 56 * 1024 * 1024


def _dot(a, b):
    return jnp.dot(a, b, preferred_element_type=F32)


def _dot_nt(a, b):
    return lax.dot_general(a, b, (((1,), (1,)), ((), ())), preferred_element_type=F32)


def _dot_tn(a, b):
    return lax.dot_general(a, b, (((0,), (0,)), ((), ())), preferred_element_type=F32)


def _split2(x):
    hi = x.astype(BF16)
    lo = (x - hi.astype(F32)).astype(BF16)
    return hi, lo


def _split3(x):
    hi = x.astype(BF16)
    r = x - hi.astype(F32)
    mid = r.astype(BF16)
    lo = (r - mid.astype(F32)).astype(BF16)
    return hi, mid, lo


def _params(sem):
    return pltpu.CompilerParams(dimension_semantics=sem, vmem_limit_bytes=VMEM_LIMIT)


def _log_sigmoid_parts(z):
    e = jnp.exp(-jnp.abs(z))
    return jnp.minimum(z, 0.0) - jnp.log(1.0 + e)


def _neg_log_sigmoid(nz):
    nz = jnp.minimum(nz, 80.0)
    sp = jnp.log(1.0 + jnp.exp(nz))
    return sp, nz - sp


def _sb_fwd(proj):
    S = proj.shape[0]
    nq = S // TQ

    def body(q_ref, k_ref, v_ref, y_ref, t_ref, acc_ref, car_ref, zs_ref):
        i = pl.program_id(1)
        low = lax.broadcasted_iota(jnp.int32, (TQ, 128), 1) < 64
        row = lax.broadcasted_iota(jnp.int32, (TK, TK), 0)
        col = lax.broadcasted_iota(jnp.int32, (TK, TK), 1)
        uo = (row > col).astype(BF16)
        uo = jnp.concatenate([uo, uo], axis=0)
        q = q_ref[...] * 0.125
        qh = (jnp.where(low, -q, 0.0).astype(BF16), jnp.where(low, 0.0, -q).astype(BF16))
        acc_ref[...] = jnp.zeros_like(acc_ref)
        car_ref[...] = jnp.zeros_like(car_ref)

        def block(j, r0):
            diag = r0 is not None
            r0 = r0 or 0
            if diag:
                strict = (lax.broadcasted_iota(jnp.int32, (TQ - r0, TK), 1)
                          < lax.broadcasted_iota(jnp.int32, (TQ - r0, TK), 0))
            rows = pl.ds(pl.multiple_of(j * TK, TK), TK)
            kb = k_ref[rows, :].astype(BF16)
            vb = v_ref[rows, :].astype(BF16)
            if not diag:
                kn = k_ref[pl.ds(pl.multiple_of(jnp.maximum(j - 1, 0) * TK, TK), TK), :].astype(BF16)

            def first(h):
                if diag:
                    nz = _dot_nt(qh[h][r0:], kb)
                else:
                    nz = zs_ref[h]
                    zs_ref[h] = _dot_nt(qh[h], kn)
                sp, lk = _neg_log_sigmoid(nz)
                if diag:
                    lk = jnp.where(strict, lk, 0.0)
                hi, lo = _split2(lk)
                return sp, lk[:, 0:1], _dot(jnp.concatenate([hi, lo], axis=1), uo)

            def second(h, sp, lk0, rr):
                car = car_ref[h, r0:, :]
                a = jnp.exp((jnp.concatenate([car, car], axis=1) + rr) - sp)
                if diag:
                    a = jnp.where(strict, a, 0.0)
                acc_ref[h, r0:, :] += _dot(a.astype(BF16), vb)
                car_ref[h, r0:, :] = car + jnp.broadcast_to(rr[:, 0:1] + lk0, car.shape)

            if diag:
                halves = [first(h) for h in range(2)]
                for h in range(2):
                    second(h, *halves[h])
            else:
                for h in range(2):
                    second(h, *first(h))

        for d in reversed(range(ND)):
            block(ND * i + d, TK * d)

        @pl.when(i > 0)
        def _():
            k0 = k_ref[pl.ds(pl.multiple_of((ND * i - 1) * TK, TK), TK), :].astype(BF16)
            for h in range(2):
                zs_ref[h] = _dot_nt(qh[h], k0)

        def loop(n, c):
            block(ND * i - 1 - n, None)
            return c

        lax.fori_loop(0, ND * i, loop, 0)
        y_ref[...] = jnp.where(low, acc_ref[0], acc_ref[1])
        t_ref[...] = jnp.where(low, car_ref[0], car_ref[1])

    return pl.pallas_call(
        body, name="sb_fwd", grid=(4, nq),
        in_specs=[pl.BlockSpec((TQ, 128), lambda p, i: (i, p)),
                  pl.BlockSpec((S, 128), lambda p, i: (0, 4 + p)),
                  pl.BlockSpec((S, 128), lambda p, i: (0, 8 + p))],
        out_specs=[pl.BlockSpec((TQ, 128), lambda p, i: (i, p)),
                   pl.BlockSpec((TQ, 128), lambda p, i: (i, p))],
        out_shape=[jax.ShapeDtypeStruct((S, SB_W), F32), jax.ShapeDtypeStruct((S, SB_W), F32)],
        scratch_shapes=[pltpu.VMEM((2, TQ, 128), F32), pltpu.VMEM((2, TQ, 128), F32), pltpu.VMEM((2, TQ, TK), F32)],
        compiler_params=_params(("arbitrary", "arbitrary")),
    )(proj, proj, proj)


def _sb_bwd(proj, tot, dy):
    S = proj.shape[0]
    nq = S // TQ

    def body(q_ref, k_ref, v_ref, t_ref, dy_ref, dq_ref, dk_ref, dv_ref, dqa_ref, cp_ref, cg_ref, dkt_ref, dvt_ref):
        i = pl.program_id(1)
        low = lax.broadcasted_iota(jnp.int32, (TQ, 128), 1) < 64
        row = lax.broadcasted_iota(jnp.int32, (TK, TK), 0)
        col = lax.broadcasted_iota(jnp.int32, (TK, TK), 1)
        u_inc = (row <= col).astype(BF16)
        u_inc = jnp.concatenate([u_inc, u_inc], axis=0)
        u_exc = (row < col).astype(BF16)
        q = q_ref[...] * 0.125
        qh = (jnp.where(low, -q, 0.0).astype(BF16), jnp.where(low, 0.0, -q).astype(BF16))
        dy_ = dy_ref[...]
        dyh = (jnp.where(low, dy_, 0.0).astype(BF16), jnp.where(low, 0.0, dy_).astype(BF16))
        qt = tuple(jnp.transpose(x) for x in qh)
        dyt = tuple(jnp.transpose(x) for x in dyh)
        t_ = t_ref[...]
        t_sw = pltpu.roll(t_, 64, 1)
        th = (jnp.where(low, t_, t_sw), jnp.where(low, t_sw, t_))
        dqa_ref[...] = jnp.zeros_like(dqa_ref)
        cp_ref[...] = jnp.zeros_like(cp_ref)
        cg_ref[...] = jnp.zeros_like(cg_ref)

        @pl.when(i == 0)
        def _():
            dkt_ref[...] = jnp.zeros_like(dkt_ref)
            dvt_ref[...] = jnp.zeros_like(dvt_ref)

        def block(j, r0):
            diag = r0 is not None
            r0 = r0 or 0
            if diag:
                strict = (lax.broadcasted_iota(jnp.int32, (TQ - r0, TK), 1)
                          < lax.broadcasted_iota(jnp.int32, (TQ - r0, TK), 0))
            rows = pl.ds(pl.multiple_of(j * TK, TK), TK)
            kb = k_ref[rows, :].astype(BF16)
            vb = v_ref[rows, :].astype(BF16)
            dk_acc = jnp.zeros((128, TK), F32)
            dv_acc = jnp.zeros((128, TK), F32)
            for h in range(2):
                qr, dyr = qh[h][r0:], dyh[h][r0:]
                sp, lk = _neg_log_sigmoid(_dot_nt(qr, kb))
                if diag:
                    lk = jnp.where(strict, lk, 0.0)
                hi, lo = _split2(lk)
                pp = _dot(jnp.concatenate([hi, lo], axis=1), u_inc)
                cp, cg = cp_ref[h, r0:, :], cg_ref[h, r0:, :]
                rest = th[h][r0:] - cp
                a = jnp.exp((jnp.concatenate([rest, rest], axis=1) - pp) - sp)
                if diag:
                    a = jnp.where(strict, a, 0.0)
                g = _dot_nt(dyr, vb) * a
                gg = _dot(g.astype(BF16), u_exc)
                beta = jnp.exp(-sp)
                dz = g - beta * (g + (jnp.concatenate([cg, cg], axis=1) + gg))
                if diag:
                    dz = jnp.where(strict, dz, 0.0)
                dzb = dz.astype(BF16)
                dqa_ref[h, r0:, :] += _dot(dzb, kb)
                dk_acc += _dot(qt[h][:, r0:], dzb)
                dv_acc += _dot(dyt[h][:, r0:], a.astype(BF16))
                cp_ref[h, r0:, :] = cp + jnp.broadcast_to(pp[:, TK - 1:TK], cp.shape)
                cg_ref[h, r0:, :] = cg + jnp.broadcast_to(gg[:, TK - 1:TK] + g[:, TK - 1:TK], cg.shape)
            dkt_ref[:, rows] -= dk_acc
            dvt_ref[:, rows] += dv_acc

        def loop(j, c):
            block(j, None)
            return c

        lax.fori_loop(0, ND * i, loop, 0)
        for d in range(ND):
            block(ND * i + d, TK * d)
        dq_ref[...] = jnp.where(low, dqa_ref[0], dqa_ref[1]) * 0.125

        @pl.when(i == nq - 1)
        def _():
            dk_ref[...] = jnp.transpose(dkt_ref[...])
            dv_ref[...] = jnp.transpose(dvt_ref[...])

    return pl.pallas_call(
        body, name="sb_bwd", grid=(4, nq),
        in_specs=[pl.BlockSpec((TQ, 128), lambda p, i: (i, p)),
                  pl.BlockSpec((S, 128), lambda p, i: (0, 4 + p)),
                  pl.BlockSpec((S, 128), lambda p, i: (0, 8 + p)),
                  pl.BlockSpec((TQ, 128), lambda p, i: (i, p)),
                  pl.BlockSpec((TQ, 128), lambda p, i: (i, p))],
        out_specs=[pl.BlockSpec((TQ, 128), lambda p, i: (i, p)),
                   pl.BlockSpec((S, 128), lambda p, i: (0, p)),
                   pl.BlockSpec((S, 128), lambda p, i: (0, p))],
        out_shape=[jax.ShapeDtypeStruct((S, SB_W), F32)] * 3,
        scratch_shapes=[pltpu.VMEM((2, TQ, 128), F32)] * 3 + [pltpu.VMEM((128, S), F32)] * 2,
        compiler_params=_params(("arbitrary", "arbitrary")),
    )(proj, proj, proj, tot, dy)


ML_SCALE = 128 ** -0.5
RC = 256


def _conv_taps(cur, prev8, w):
    n = cur.shape[0]
    win = jnp.concatenate([prev8, cur], axis=0)
    out = w[3:4, :] * cur
    for j in range(3):
        out = out + w[j:j + 1, :] * pltpu.roll(win, 3 - j, 0)[8:8 + n]
    return out


def _ml_prep(proj, conv_w, conv_b):
    S = proj.shape[0]

    def body(x_ref, w_ref, b_ref, o_ref):
        c = pl.program_id(0)
        scale = jnp.where(c < 4, 1.0, ML_SCALE).astype(F32)
        w = w_ref[...]
        b = b_ref[...]

        @pl.when(c < 8)
        def _():
            for n in range(S // RC):
                cur = x_ref[n * RC:(n + 1) * RC, :]
                prev8 = x_ref[n * RC - 8:n * RC, :] if n else jnp.zeros((8, 128), F32)
                pre = b + _conv_taps(cur, prev8, w)
                o_ref[n * RC:(n + 1) * RC, :] = (pre * jax.nn.sigmoid(pre) * scale).astype(BF16)

        @pl.when(c >= 8)
        def _():
            o_ref[...] = x_ref[...].astype(BF16)

    return pl.pallas_call(
        body, name="ml_prep", grid=(12,),
        in_specs=[pl.BlockSpec((S, 128), lambda c: (0, 16 + c)),
                  pl.BlockSpec((4, 128), lambda c: (0, jnp.minimum(c, 7))),
                  pl.BlockSpec((1, 128), lambda c: (0, jnp.minimum(c, 7)))],
        out_specs=pl.BlockSpec((S, 128), lambda c: (0, c)),
        out_shape=jax.ShapeDtypeStruct((S, 1536), BF16),
        compiler_params=_params(("arbitrary",)),
    )(proj, conv_w, conv_b)


def _ml_prep_bwd(proj, conv_w, conv_b, dq, dk):
    S = proj.shape[0]

    def body(x_ref, w_ref, b_ref, dq_ref, dk_ref, dx_ref, gw_ref, gb_ref, dp_ref):
        c = pl.program_id(0)
        w = w_ref[...]
        b = b_ref[...]
        gw = [jnp.zeros((1, 128), F32) for _ in range(4)]
        gb = jnp.zeros((1, 128), F32)
        for n in range(S // RC):
            rows = slice(n * RC, (n + 1) * RC)
            cur = x_ref[rows, :]
            prev8 = x_ref[n * RC - 8:n * RC, :] if n else jnp.zeros((8, 128), F32)
            pre = b + _conv_taps(cur, prev8, w)
            s = jax.nn.sigmoid(pre)
            dpost = jnp.where(c < 4, dq_ref[rows, :].astype(F32), dk_ref[rows, :].astype(F32) * ML_SCALE)
            dpre = dpost * (s * (1.0 + pre * (1.0 - s)))
            dp_ref[rows, :] = dpre
            win = jnp.concatenate([prev8, cur], axis=0)
            gb = gb + jnp.sum(dpre, axis=0, keepdims=True)
            gw[3] = gw[3] + jnp.sum(dpre * cur, axis=0, keepdims=True)
            for j in range(3):
                gw[j] = gw[j] + jnp.sum(dpre * pltpu.roll(win, 3 - j, 0)[8:8 + RC], axis=0, keepdims=True)
        dp_ref[S:S + 8, :] = jnp.zeros((8, 128), F32)
        for n in range(S // RC):
            win = dp_ref[n * RC:(n + 1) * RC + 8, :]
            dx = w[3:4, :] * win[:RC]
            for j in range(3):
                dx = dx + w[j:j + 1, :] * pltpu.roll(win, RC + 8 - (3 - j), 0)[:RC]
            dx_ref[n * RC:(n + 1) * RC, :] = dx
        gw_ref[...] = jnp.concatenate(gw, axis=0)
        gb_ref[...] = gb

    return pl.pallas_call(
        body, name="ml_prep_bwd", grid=(8,),
        in_specs=[pl.BlockSpec((S, 128), lambda c: (0, 16 + c)),
                  pl.BlockSpec((4, 128), lambda c: (0, c)),
                  pl.BlockSpec((1, 128), lambda c: (0, c)),
                  pl.BlockSpec((S, 128), lambda c: (0, jnp.minimum(c, 3))),
                  pl.BlockSpec((S, 128), lambda c: (0, jnp.maximum(c - 4, 0)))],
        out_specs=[pl.BlockSpec((S, 128), lambda c: (0, c)),
                   pl.BlockSpec((4, 128), lambda c: (0, c)),
                   pl.BlockSpec((1, 128), lambda c: (0, c))],
        out_shape=[jax.ShapeDtypeStruct((S, 1024), F32), jax.ShapeDtypeStruct((4, 1024), F32),
                   jax.ShapeDtypeStruct((1, 1024), F32)],
        scratch_shapes=[pltpu.VMEM((S + 8, 128), F32)],
        compiler_params=_params(("arbitrary",)),
    )(proj, conv_w, conv_b, dq, dk)


NH = 4


def _heads(x):
    return jnp.stack([x[:, 128 * h:128 * (h + 1)] for h in range(NH)])


def _unheads(x):
    return jnp.concatenate([x[h] for h in range(NH)], axis=1)


def _per_head(f, *xs):
    return jnp.stack([f(*[x[h] for x in xs]) for h in range(NH)])


def _ml_gates(g):
    lg = _log_sigmoid_parts(g)
    ig = jnp.stack([jnp.broadcast_to(g[:, h:h + 1], (LCH, 128)) for h in range(NH)])
    lf = jnp.stack([jnp.broadcast_to(lg[:, 4 + h:5 + h], (LCH, 128)) for h in range(NH)])
    return ig, lf


def _tri_sum(tri, x, dot):
    return _per_head(lambda a, b, c: dot(tri, a) + dot(tri, b) + dot(tri, c), *_split3(x))


def _ml_chunk_fwd(q, k, v, ig, lf, ct, n_st, m_st, tri, causal):
    vf = v.astype(F32)
    b = _tri_sum(tri, lf, _dot)
    b_last = b[:, LCH - 1:LCH, :]
    g = b_last - b + ig
    m_loc = jnp.max(g, axis=1, keepdims=True)
    w = jnp.exp(g - m_loc)
    vwf = vf * w
    vw = vwf.astype(BF16)
    ct_loc = _per_head(_dot_tn, k, vw)
    kf = k.astype(F32)
    n_loc = jnp.sum(w * kf, axis=1, keepdims=True)
    r = _per_head(jnp.transpose, ig - b)
    d_log = jnp.where(causal, b + r, -jnp.inf)
    m_t = jnp.maximum(b + m_st, jnp.max(d_log, axis=2, keepdims=True))
    w_in = jnp.exp(d_log - m_t)
    qk = _per_head(_dot_nt, q, k)
    scores = qk * w_in
    cs = jnp.exp(b + m_st - m_t)
    ctb = ct.astype(BF16)
    qc = _per_head(_dot, q, ctb)
    qf = q.astype(F32)
    qn = jnp.sum(qf * n_st, axis=2, keepdims=True)
    num = _per_head(_dot, scores.astype(BF16), v) + cs * qc
    den = jnp.sum(scores, axis=2, keepdims=True) + cs * qn
    em = jnp.exp(-m_t)
    dd = jnp.maximum(jnp.abs(den), em)
    h = num / dd
    m_new = jnp.maximum(b_last + m_st, m_loc)
    a = jnp.exp(b_last + m_st - m_new)
    gg = jnp.exp(m_loc - m_new)
    return dict(vwf=vwf, vw=vw, kf=kf, qf=qf, ct_loc=ct_loc, n_loc=n_loc, w=w, w_in=w_in, qk=qk, scores=scores,
                cs=cs, ctb=ctb, qc=qc, qn=qn, den=den, em=em, dd=dd, h=h, m_new=m_new, a=a, gg=gg)


def _ml_consts():
    row = lax.broadcasted_iota(jnp.int32, (LCH, LCH), 0)
    col = lax.broadcasted_iota(jnp.int32, (LCH, LCH), 1)
    return row, (col <= row), (col <= row).astype(BF16)


def _ml_rows(qkv_ref, g_ref, gb_ref, c):
    rows = pl.ds(pl.multiple_of(c * LCH, LCH), LCH)
    q, k, v = (_heads(qkv_ref[rows, 512 * t:512 * (t + 1)]) for t in range(3))
    ig, lf = _ml_gates(g_ref[rows, :] + gb_ref[...])
    return rows, q, k, v, ig, lf


VMEM_SPEC = pl.BlockSpec(memory_space=pltpu.VMEM)


def _ml_fwd(qkv, proj_g, gbias):
    S = qkv.shape[0]
    nc = S // LCH

    def body(qkv_ref, g_ref, gb_ref, h_ref, cst_ref, nm_ref, ct_ref, n_ref, m_ref):
        _, causal, tri = _ml_consts()
        ct_ref[...] = jnp.zeros_like(ct_ref)
        n_ref[...] = jnp.zeros_like(n_ref)
        m_ref[...] = jnp.zeros_like(m_ref)

        def chunk(c, carry):
            rows, q, k, v, ig, lf = _ml_rows(qkv_ref, g_ref, gb_ref, c)
            ct, n_st, m_st = ct_ref[...], n_ref[:, 0:1, :], m_ref[:, 0:1, :]
            cst_ref[c] = ct
            nm_ref[c, :, 0:8, :] = n_ref[...]
            nm_ref[c, :, 8:16, :] = m_ref[...]
            f = _ml_chunk_fwd(q, k, v, ig, lf, ct, n_st, m_st, tri, causal)
            h_ref[rows, :] = _unheads(f["h"])
            ct_ref[...] = f["a"] * ct + f["gg"] * f["ct_loc"]
            n_ref[...] = jnp.broadcast_to(f["a"] * n_st + f["gg"] * f["n_loc"], (NH, 8, 128))
            m_ref[...] = jnp.broadcast_to(f["m_new"], (NH, 8, 128))
            return carry

        lax.fori_loop(0, nc, chunk, 0)

    return pl.pallas_call(
        body, name="ml_fwd", in_specs=[VMEM_SPEC] * 3, out_specs=[VMEM_SPEC] * 3,
        out_shape=[jax.ShapeDtypeStruct((S, ML_W), F32), jax.ShapeDtypeStruct((nc, NH, 128, 128), F32),
                   jax.ShapeDtypeStruct((nc, NH, 16, 128), F32)],
        scratch_shapes=[pltpu.VMEM((NH, 128, 128), F32), pltpu.VMEM((NH, 8, 128), F32), pltpu.VMEM((NH, 8, 128), F32)],
        compiler_params=pltpu.CompilerParams(vmem_limit_bytes=VMEM_LIMIT),
    )(qkv, proj_g, gbias)


def _ml_bwd(qkv, proj_g, gbias, cst, nm, dh):
    S = qkv.shape[0]
    nc = S // LCH

    def body(qkv_ref, g_ref, gb_ref, cst_ref, nm_ref, dh_ref, dq_ref, dk_ref, dv_ref, dif_ref, gsum_ref,
             dct_ref, dn_ref, gi_ref):
        row, causal, tri = _ml_consts()
        lane = lax.broadcasted_iota(jnp.int32, (LCH, 128), 1)
        last_row = row == LCH - 1
        dct_ref[...] = jnp.zeros_like(dct_ref)
        dn_ref[...] = jnp.zeros_like(dn_ref)
        gi_ref[...] = jnp.zeros_like(gi_ref)

        def chunk(t, carry):
            c = nc - 1 - t
            rows, q, k, v, ig, lf = _ml_rows(qkv_ref, g_ref, gb_ref, c)
            ct, n_st, m_st = cst_ref[c], nm_ref[c, :, 0:1, :], nm_ref[c, :, 8:9, :]
            f = _ml_chunk_fwd(q, k, v, ig, lf, ct, n_st, m_st, tri, causal)
            dh_ = _heads(dh_ref[rows, :])
            dct_new, dn_new = dct_ref[...], dn_ref[:, 0:1, :]
            e_num = dh_ / f["dd"]
            hdh = jnp.sum(f["h"] * dh_, axis=2, keepdims=True)
            free = jnp.abs(f["den"]) > f["em"]
            e_den = jnp.where(free, -hdh / f["dd"] * jnp.sign(f["den"]), 0.0)
            e_num_b = e_num.astype(BF16)
            ds_ = _per_head(_dot_nt, e_num_b, v) + e_den
            dqk = ds_ * f["w_in"]
            gam = dqk * f["qk"]
            dqk_b = dqk.astype(BF16)
            cse = f["cs"] * e_den
            dq = _per_head(_dot, dqk_b, k) + f["cs"] * _per_head(_dot_nt, e_num_b, f["ctb"]) + cse * n_st
            dk = _per_head(_dot_tn, dqk_b, q)
            dv = _per_head(_dot_tn, f["scores"].astype(BF16), e_num_b)
            dcl = (f["gg"] * dct_new).astype(BF16)
            dnl = f["gg"] * dn_new
            kd = _per_head(_dot, k, dcl)
            dv = dv + f["w"] * kd
            dk = dk + _per_head(_dot_nt, f["vw"], dcl) + f["w"] * dnl
            gam_s = (jnp.sum(kd * f["vwf"], axis=2, keepdims=True)
                     + f["w"] * jnp.sum(f["kf"] * dnl, axis=2, keepdims=True))
            col_g = jnp.sum(_per_head(jnp.transpose, gam), axis=2, keepdims=True) + gam_s
            db = (jnp.sum(gam, axis=2, keepdims=True) + jnp.sum(e_num * (f["cs"] * f["qc"]), axis=2, keepdims=True)
                  + cse * f["qn"] - col_g)
            state = jnp.sum(jnp.sum(dct_new * ct, axis=2, keepdims=True), axis=1, keepdims=True)
            state = state + jnp.sum(dn_new * n_st, axis=2, keepdims=True)
            db_last = jnp.sum(gam_s[:, :, 0:1], axis=1, keepdims=True) + f["a"][:, :, 0:1] * state
            db = jnp.where(last_row, db + db_last, db)
            dlf = _tri_sum(tri, db, _dot_tn)
            df = dlf * (1.0 - jnp.exp(lf))
            dq_ref[rows, :] = _unheads(dq).astype(BF16)
            dk_ref[rows, :] = _unheads(dk).astype(BF16)
            dv_ref[rows, :] = _unheads(dv).astype(BF16)
            dif = jnp.zeros((LCH, 128), F32)
            for h in range(NH):
                dif = dif + jnp.where(lane == h, col_g[h], 0.0) + jnp.where(lane == h + 4, df[h], 0.0)
            dif_ref[rows, :] = dif
            clamped = jnp.where(free, 0.0, hdh)
            gi_ref[...] += jnp.broadcast_to(jnp.sum(clamped, axis=1, keepdims=True), (NH, 8, 128))
            dct_ref[...] = f["a"] * dct_new + _per_head(_dot_tn, q, (f["cs"] * e_num).astype(BF16))
            dn_ref[...] = jnp.broadcast_to(f["a"] * dn_new + jnp.sum(cse * f["qf"], axis=1, keepdims=True), (NH, 8, 128))
            return carry

        lax.fori_loop(0, nc, chunk, 0)
        lane8 = lax.broadcasted_iota(jnp.int32, (8, 128), 1)
        gsum = jnp.where(lane8 >= 4, jnp.sum(dif_ref[...], axis=0, keepdims=True), 0.0)
        for h in range(NH):
            gsum = gsum + jnp.where(lane8 == h, gi_ref[h], 0.0)
        gsum_ref[...] = gsum

    return pl.pallas_call(
        body, name="ml_bwd", in_specs=[VMEM_SPEC] * 6, out_specs=[VMEM_SPEC] * 5,
        out_shape=[jax.ShapeDtypeStruct((S, ML_W), BF16)] * 3 + [jax.ShapeDtypeStruct((S, 128), F32),
                                                                  jax.ShapeDtypeStruct((8, 128), F32)],
        scratch_shapes=[pltpu.VMEM((NH, 128, 128), F32), pltpu.VMEM((NH, 8, 128), F32), pltpu.VMEM((NH, 8, 128), F32)],
        compiler_params=pltpu.CompilerParams(vmem_limit_bytes=VMEM_LIMIT),
    )(qkv, proj_g, gbias, cst, nm, dh)


MESH = pl.DeviceIdType.MESH
ANY = pl.BlockSpec(memory_space=pl.ANY)
N_DEV = 8


def _place():
    return lax.axis_index("x"), lax.axis_index("y"), lax.axis_index("c")


def _block_of(px, py, pc):
    return 4 * px + 2 * py + pc


def _copies_to_all(b_ref, o_ref, send_sems, recv_sems, local_sem):
    x, y, c = _place()
    mine = _block_of(x, y, c)
    copies = [pltpu.make_async_copy(b_ref, o_ref.at[mine], local_sem)]
    for k in range(1, N_DEV):
        peer = (x ^ (k >> 2), y ^ ((k >> 1) & 1), c ^ (k & 1))
        copies.append(pltpu.make_async_remote_copy(
            src_ref=b_ref, dst_ref=o_ref.at[mine], send_sem=send_sems.at[k - 1], recv_sem=recv_sems.at[k - 1],
            device_id=peer, device_id_type=MESH))
    return copies


def _copies_between_chips(p_ref, o_ref, send_sems, recv_sems, local_sem):
    x, y, c = _place()
    mine = 2 * x + y
    copies = [pltpu.make_async_copy(p_ref.at[mine], o_ref.at[mine], local_sem)]
    for k in range(1, 4):
        px, py = x ^ (k >> 1), y ^ (k & 1)
        copies.append(pltpu.make_async_remote_copy(
            src_ref=p_ref.at[2 * px + py], dst_ref=o_ref.at[mine], send_sem=send_sems.at[k - 1],
            recv_sem=recv_sems.at[k - 1], device_id=(px, py, c), device_id_type=MESH))
    return copies


def _around_grid(copies, first, last):
    @pl.when(first)
    def _():
        for cp in copies():
            cp.start()

    @pl.when(last)
    def _():
        for cp in copies():
            cp.wait()


def _inproj_fwd(x, pre_w, w_t, wg_t, blk=None):
    S, D = x.shape
    tm, tn = min(S, 1024), 1152
    ni, nj = S // tm, N_MAIN // tn

    def body(x_ref, pw_ref, w_ref, wg_ref, *rest):
        if blk is None:
            proj_ref, g_ref, u_ref = rest
        else:
            b_ref, proj_ref, g_ref, u_ref, o_ref, send_sems, recv_sems, local_sem = rest
            i, j = pl.program_id(0), pl.program_id(1)
            _around_grid(lambda: _copies_to_all(b_ref, o_ref, send_sems, recv_sems, local_sem),
                         (i == 0) & (j == 0), (i == ni - 1) & (j == nj - 1))

        @pl.when(pl.program_id(1) == 0)
        def _():
            xf = x_ref[...]
            r = lax.rsqrt(jnp.mean(xf * xf, axis=-1, keepdims=True) + EPS)
            u = (xf * r * pw_ref[...]).astype(BF16)
            u_ref[...] = u
            g_ref[...] = _dot_nt(u, wg_ref[...])

        proj_ref[...] = _dot_nt(u_ref[...], w_ref[...])

    in_specs = [pl.BlockSpec((tm, D), lambda i, j: (i, 0)),
                pl.BlockSpec((1, D), lambda i, j: (0, 0)),
                pl.BlockSpec((tn, D), lambda i, j: (j, 0)),
                pl.BlockSpec((128, D), lambda i, j: (0, 0))]
    out_specs = [pl.BlockSpec((tm, tn), lambda i, j: (i, j)),
                 pl.BlockSpec((tm, 128), lambda i, j: (i, 0)),
                 pl.BlockSpec((tm, D), lambda i, j: (i, 0))]
    out_shape = [jax.ShapeDtypeStruct((S, N_MAIN), F32), jax.ShapeDtypeStruct((S, 128), F32),
                 jax.ShapeDtypeStruct((S, D), BF16)]
    operands, scratch = (x, pre_w, w_t, wg_t), []
    if blk is not None:
        in_specs, out_specs, operands = in_specs + [ANY], out_specs + [ANY], operands + (blk,)
        out_shape = out_shape + [jax.ShapeDtypeStruct((N_DEV,) + blk.shape, blk.dtype)]
        scratch = [pltpu.SemaphoreType.DMA((7,)), pltpu.SemaphoreType.DMA((7,)), pltpu.SemaphoreType.DMA]
    return pl.pallas_call(
        body, name="inproj_fwd", grid=(ni, nj), in_specs=in_specs, out_specs=out_specs, out_shape=out_shape,
        scratch_shapes=scratch, compiler_params=_params(("arbitrary", "arbitrary")),
    )(*operands)


def _inproj_bwd(d_main, d_if, w_t, wg_t, x, pre_w, dx_tail, parts=None):
    S, D = x.shape
    tm, tk = min(S, 1024), 1152
    ni, nk = S // tm, N_MAIN // tk

    def body(d_ref, dg_ref, w_ref, wg_ref, x_ref, pw_ref, dt_ref, *rest):
        i, k = pl.program_id(0), pl.program_id(1)
        if parts is None:
            dx_ref, gpw_ref, acc_ref = rest
        else:
            p_ref, dx_ref, gpw_ref, o_ref, acc_ref, send_sems, recv_sems, local_sem = rest
            _around_grid(lambda: _copies_between_chips(p_ref, o_ref, send_sems, recv_sems, local_sem),
                         (i == 0) & (k == 0), (i == ni - 1) & (k == nk - 1))

        @pl.when(k == 0)
        def _():
            acc_ref[...] = _dot(dg_ref[...], wg_ref[...])

        acc_ref[...] += _dot(d_ref[...], w_ref[...])

        @pl.when(k == nk - 1)
        def _():
            xf = x_ref[...]
            r = lax.rsqrt(jnp.mean(xf * xf, axis=-1, keepdims=True) + EPS)
            xn = xf * r
            du = acc_ref[...]
            gw = du * pw_ref[...]
            dx_ref[...] = dt_ref[...] + r * (gw - xn * jnp.mean(gw * xn, axis=-1, keepdims=True))
            part = jnp.sum(du * xn, axis=0, keepdims=True)

            @pl.when(i == 0)
            def _():
                gpw_ref[...] = part

            @pl.when(i > 0)
            def _():
                gpw_ref[...] += part

    in_specs = [pl.BlockSpec((tm, tk), lambda i, k: (i, k)),
                pl.BlockSpec((tm, 128), lambda i, k: (i, 0)),
                pl.BlockSpec((tk, D), lambda i, k: (k, 0)),
                pl.BlockSpec((128, D), lambda i, k: (0, 0)),
                pl.BlockSpec((tm, D), lambda i, k: (i, 0)),
                pl.BlockSpec((1, D), lambda i, k: (0, 0)),
                pl.BlockSpec((tm, D), lambda i, k: (i, 0))]
    out_specs = [pl.BlockSpec((tm, D), lambda i, k: (i, 0)), pl.BlockSpec((1, D), lambda i, k: (0, 0))]
    out_shape = [jax.ShapeDtypeStruct((S, D), F32), jax.ShapeDtypeStruct((1, D), F32)]
    operands, scratch = (d_main, d_if, w_t, wg_t, x, pre_w, dx_tail), [pltpu.VMEM((tm, D), F32)]
    if parts is not None:
        in_specs, out_specs, operands = in_specs + [ANY], out_specs + [ANY], operands + (parts,)
        out_shape = out_shape + [jax.ShapeDtypeStruct(parts.shape, parts.dtype)]
        scratch = scratch + [pltpu.SemaphoreType.DMA((3,)), pltpu.SemaphoreType.DMA((3,)), pltpu.SemaphoreType.DMA]
    return pl.pallas_call(
        body, name="inproj_bwd", grid=(ni, nk), in_specs=in_specs, out_specs=out_specs, out_shape=out_shape,
        scratch_shapes=scratch, compiler_params=_params(("arbitrary", "arbitrary")),
    )(*operands)


def _matmul_tn(a, b, name, out_rows=None):
    S, M = a.shape
    N = b.shape[1]
    tmm = 1152 if M % 1152 == 0 else min(M, 1024)
    tk = min(S, 1024)
    nk = S // tk

    def body(a_ref, b_ref, o_ref):
        part = _dot_tn(a_ref[...].astype(BF16), b_ref[...].astype(BF16))

        @pl.when(pl.program_id(1) == 0)
        def _():
            o_ref[...] = part

        @pl.when(pl.program_id(1) > 0)
        def _():
            o_ref[...] += part

    return pl.pallas_call(
        body, name=name, grid=(M // tmm, nk),
        in_specs=[pl.BlockSpec((tk, tmm), lambda i, k: (k, i)),
                  pl.BlockSpec((tk, N), lambda i, k: (k, 0))],
        out_specs=pl.BlockSpec((tmm, N), lambda i, k: (i, 0)),
        out_shape=jax.ShapeDtypeStruct((out_rows or M, N), F32),
        compiler_params=_params(("arbitrary", "arbitrary")),
    )(a, b)


def _gate_rows_tn(d_if, u, g):
    S, D = u.shape
    tk = min(S, 1024)
    nk = S // tk

    def body(a_ref, b_ref, g_ref, o_ref, acc_ref):
        k = pl.program_id(0)
        part = _dot_tn(a_ref[...], b_ref[...])

        @pl.when(k == 0)
        def _():
            acc_ref[...] = part

        @pl.when(k > 0)
        def _():
            acc_ref[...] += part

        @pl.when(k == nk - 1)
        def _():
            o_ref[...] = acc_ref[0:8, :]

    return pl.pallas_call(
        body, name="gw_in_gates", grid=(nk,),
        in_specs=[pl.BlockSpec((tk, 128), lambda k: (k, 0)), pl.BlockSpec((tk, D), lambda k: (k, 0)), ANY],
        out_specs=pl.BlockSpec((8, D), lambda k: (N_MAIN // 8, 0)),
        out_shape=jax.ShapeDtypeStruct(g.shape, g.dtype),
        scratch_shapes=[pltpu.VMEM((128, D), F32)],
        input_output_aliases={2: 0},
        compiler_params=_params(("arbitrary",)),
    )(d_if, u, g)


def _half_mean(v, low):
    s_lo = jnp.sum(jnp.where(low, v, 0.0), axis=1, keepdims=True)
    s_hi = jnp.sum(jnp.where(low, 0.0, v), axis=1, keepdims=True)
    return jnp.where(low, s_lo, s_hi) * (1.0 / 64.0)


def _silu_grad(z, s):
    return s * (1.0 + z * (1.0 - s))


def _tail(y_sb, h_ml, proj, x, p, target, sb_nw, ml_nw, w_out, post_w, w_gate, b_gate, w_up):
    S, D = x.shape
    tm = 256

    def body(ysb_ref, hml_ref, sbz_ref, mlo_ref, mlz_ref, x_ref, p_ref, tg_ref, sbw_ref, mlw_ref, wo_ref, pw_ref,
             wg_ref, bg_ref, wu_ref,
             dx_ref, dysb_ref, dhml_ref, dsbz_ref, dmlo_ref, dmlz_ref, mix_ref, dy_ref, h1_ref, dgp_ref, dpu_ref,
             small_ref):
        lane = lax.broadcasted_iota(jnp.int32, (tm, 128), 1)
        low = lane < 64
        sb_saved, ml_saved, mixed = [], [], []
        for s in range(4):
            sl = slice(128 * s, 128 * s + 128)
            y = ysb_ref[:, sl]
            rs = lax.rsqrt(_half_mean(y * y, low) + EPS)
            n = y * rs
            z = sbz_ref[:, sl]
            sg = jax.nn.sigmoid(z)
            w = sbw_ref[:, sl]
            mixed.append((n * w) * (z * sg))
            sb_saved.append((rs, n, z, sg, w))
        for s in range(4):
            sl = slice(128 * s, 128 * s + 128)
            og = jax.nn.sigmoid(mlo_ref[:, sl])
            hh = hml_ref[:, sl]
            t = og * hh
            rs = lax.rsqrt(jnp.mean(t * t, axis=1, keepdims=True) + EPS)
            n = t * rs
            z = mlz_ref[:, sl]
            sg = jax.nn.sigmoid(z)
            w = mlw_ref[:, sl]
            mixed.append((n * w) * (z * sg))
            ml_saved.append((rs, n, z, sg, w, og, hh))
        mix = jnp.concatenate(mixed, axis=1).astype(BF16)
        mix_ref[...] = mix
        y = _dot(mix, wo_ref[...])
        rs_y = lax.rsqrt(jnp.mean(y * y, axis=1, keepdims=True) + EPS)
        yn = y * rs_y
        pw = pw_ref[...]
        h1 = x_ref[...] + yn * pw
        h1b = h1.astype(BF16)
        h1_ref[...] = h1b
        gate = jax.nn.sigmoid(_dot(h1b, wg_ref[...]) + bg_ref[...])
        pu = _dot(p_ref[...].astype(BF16), wu_ref[...])
        err = (h1 + gate * pu) - tg_ref[...]
        loss = 0.5 * jnp.sum(jnp.sum(err * err, axis=1, keepdims=True) * (1.0 / D))
        d_out = err * (1.0 / D)
        dpu_ref[...] = (d_out * gate).astype(BF16)
        dgp = (d_out * pu) * (gate * (1.0 - gate))
        dgpb = dgp.astype(BF16)
        dgp_ref[...] = dgpb
        d_h1 = d_out + _dot_nt(dgpb, wg_ref[...])
        dx_ref[...] = d_h1
        gwy = d_h1 * pw
        d_y = rs_y * (gwy - yn * jnp.mean(gwy * yn, axis=1, keepdims=True))
        d_yb = d_y.astype(BF16)
        dy_ref[...] = d_yb
        d_mix = _dot_nt(d_yb, wo_ref[...])
        g_nw = []
        for s in range(4):
            sl = slice(128 * s, 128 * s + 128)
            rs, n, z, sg, w = sb_saved[s]
            da = d_mix[:, sl]
            act = z * sg
            dsbz_ref[:, sl] = (da * (n * w) * _silu_grad(z, sg)).astype(BF16)
            dn = da * w * act
            g_nw.append(jnp.sum(da * act * n, axis=0, keepdims=True))
            dysb_ref[:, sl] = rs * (dn - n * _half_mean(dn * n, low))
        for s in range(4):
            sl = slice(128 * s, 128 * s + 128)
            rs, n, z, sg, w, og, hh = ml_saved[s]
            da = d_mix[:, 512 + 128 * s:512 + 128 * s + 128]
            act = z * sg
            dmlz_ref[:, sl] = (da * (n * w) * _silu_grad(z, sg)).astype(BF16)
            dn = da * w * act
            g_nw.append(jnp.sum(da * act * n, axis=0, keepdims=True))
            dt = rs * (dn - n * jnp.mean(dn * n, axis=1, keepdims=True))
            dmlo_ref[:, sl] = (dt * hh * (og * (1.0 - og))).astype(BF16)
            dhml_ref[:, sl] = dt * og
        upd = jnp.concatenate([
            jnp.sum(d_h1 * yn, axis=0, keepdims=True),
            jnp.sum(dgp, axis=0, keepdims=True),
            jnp.concatenate(g_nw, axis=1),
            jnp.full((1, D), loss, F32),
            jnp.zeros((4, D), F32)], axis=0)

        @pl.when(pl.program_id(0) == 0)
        def _():
            small_ref[...] = upd

        @pl.when(pl.program_id(0) > 0)
        def _():
            small_ref[...] += upd

    def rows(width, col=0):
        return pl.BlockSpec((tm, width), lambda i: (i, col))

    def whole(a):
        return pl.BlockSpec(a.shape, lambda i: (0, 0))

    return pl.pallas_call(
        body, name="tail", grid=(S // tm,),
        in_specs=[rows(512), rows(512), rows(512, 3), rows(512, 7), rows(512, 8), rows(D), rows(256), rows(D),
                  whole(sb_nw), whole(ml_nw), whole(w_out), whole(post_w), whole(w_gate), whole(b_gate), whole(w_up)],
        out_specs=[rows(D), rows(512), rows(512), rows(512), rows(512), rows(512), rows(D), rows(D), rows(D), rows(D),
                   rows(D), pl.BlockSpec((8, D), lambda i: (0, 0))],
        out_shape=[jax.ShapeDtypeStruct((S, D), F32), jax.ShapeDtypeStruct((S, 512), F32),
                   jax.ShapeDtypeStruct((S, 512), F32)] + [jax.ShapeDtypeStruct((S, 512), BF16)] * 3
        + [jax.ShapeDtypeStruct((S, D), BF16)] * 5 + [jax.ShapeDtypeStruct((8, D), F32)],
        compiler_params=_params(("arbitrary",)),
    )(y_sb, h_ml, proj, proj, proj, x, p, target, sb_nw, ml_nw, w_out, post_w, w_gate, b_gate, w_up)


def _local_step(x, p, target, pre_w, w_t, wg_t, conv_w, conv_b, gbias, sb_nw, ml_nw, post_w, b_gate, late,
                exchange=None):
    if callable(late[1]):
        proj, proj_g, u, gathered = _inproj_fwd(x, pre_w, w_t, wg_t, late[0])
        w_out, w_gate, w_up = late[1](gathered)
    else:
        proj, proj_g, u = _inproj_fwd(x, pre_w, w_t, wg_t)
        w_out, w_gate, w_up = late
    y_sb, tot = _sb_fwd(proj)
    qkv = _ml_prep(proj, conv_w, conv_b)
    h_ml, cst, nm = _ml_fwd(qkv, proj_g, gbias)
    dx_tail, d_ysb, d_hml, d_sbz, d_mlo, d_mlz, mix, d_y, h1, dgp, dpu, small = _tail(
        y_sb, h_ml, proj, x, p, target, sb_nw, ml_nw, w_out, post_w, w_gate, b_gate, w_up)
    dq, dk, dv = _sb_bwd(proj, tot, d_ysb)
    dqc, dks, dmlv, dif, gif = _ml_bwd(qkv, proj_g, gbias, cst, nm, d_hml)
    dmlqk, g_cw, g_cb = _ml_prep_bwd(proj, conv_w, conv_b, dqc, dks)
    d_main = jnp.concatenate([dq.astype(BF16), dk.astype(BF16), dv.astype(BF16), d_sbz, dmlqk.astype(BF16),
                              dmlv, d_mlo, d_mlz], axis=1)
    d_if = dif.astype(BF16)
    grads = dict(
        w_in_t=_gate_rows_tn(d_if, u, _matmul_tn(d_main, u, "gw_in", N_IN)),
        w_out=_matmul_tn(mix, d_y, "gw_out"), w_gate=_matmul_tn(h1, dgp, "gw_gate"), w_up=_matmul_tn(p, dpu, "gw_up"),
        conv_w=g_cw, conv_b=g_cb, gif=gif)
    parts = exchange(grads) if exchange else None
    return _inproj_bwd(d_main, d_if, w_t, wg_t, x, pre_w, dx_tail, parts), grads, small


def _all_gather(a, b):
    def body(a_ref, b_ref, oa_ref, ob_ref, send_sems, recv_sems, local_sems):
        x, y, c = _place()
        me, sibling = (x, y, c), (x, y, 1 - c)
        chips = [(1 - x, y), (x, 1 - y), (1 - x, 1 - y)]
        pairs = ((a_ref, oa_ref), (b_ref, ob_ref))

        def copies(k, block, to, from_input=False):
            slot = _block_of(*block)
            return [pltpu.make_async_remote_copy(
                src_ref=src if from_input else out.at[slot], dst_ref=out.at[slot],
                send_sem=send_sems.at[t, k], recv_sem=recv_sems.at[t, k], device_id=to, device_id_type=MESH)
                for t, (src, out) in enumerate(pairs)]

        mine = [pltpu.make_async_copy(src, out.at[_block_of(*me)], local_sems.at[t])
                for t, (src, out) in enumerate(pairs)]
        for cp in mine:
            cp.start()
        first = copies(0, me, sibling, True)
        for j, chip in enumerate(chips):
            first += copies(1 + j, me, (*chip, c), True)
        for cp in first:
            cp.start()
        passed = []
        for j, chip in enumerate(chips):
            for cp in copies(1 + j, (*chip, c), me):
                cp.wait_recv()
            fwd = copies(4 + j, (*chip, c), sibling)
            for cp in fwd:
                cp.start()
            passed += fwd
        for cp in copies(0, sibling, me):
            cp.wait_recv()
        for j, chip in enumerate(chips):
            for cp in copies(4 + j, (*chip, 1 - c), me):
                cp.wait_recv()
        for cp in first + passed:
            cp.wait_send()
        for cp in mine:
            cp.wait()

    return pl.pallas_call(
        body, name="all_gather",
        in_specs=[ANY, ANY], out_specs=[ANY, ANY],
        out_shape=[jax.ShapeDtypeStruct((N_DEV,) + a.shape, a.dtype), jax.ShapeDtypeStruct((N_DEV,) + b.shape, b.dtype)],
        scratch_shapes=[pltpu.SemaphoreType.DMA((2, 7)), pltpu.SemaphoreType.DMA((2, 7)), pltpu.SemaphoreType.DMA((2,))],
    )(a, b)


def _exchange_pair(g):
    def body(g_ref, og_ref, send_sems, recv_sems):
        x, y, c = _place()
        sent = [pltpu.make_async_remote_copy(
            src_ref=g_ref.at[k, 1 - c], dst_ref=og_ref.at[k], send_sem=send_sems.at[k], recv_sem=recv_sems.at[k],
            device_id=(x, y, 1 - c), device_id_type=MESH) for k in range(4)]
        for cp in sent:
            cp.start()
        for cp in sent:
            cp.wait()

    return pl.pallas_call(
        body, name="exchange_pair", in_specs=[ANY], out_specs=ANY,
        out_shape=jax.ShapeDtypeStruct((4,) + g.shape[2:], g.dtype),
        scratch_shapes=[pltpu.SemaphoreType.DMA((4,)), pltpu.SemaphoreType.DMA((4,))],
    )(g)


def _pair_sum(g, r, core, tr):
    _, _, R, D = g.shape

    def body(c_ref, g_ref, r_ref, o_ref):
        o_ref[...] = (g_ref[...] + r_ref[...]).astype(BF16)

    return pl.pallas_call(
        body, name="pair_sum",
        grid_spec=pltpu.PrefetchScalarGridSpec(
            num_scalar_prefetch=1, grid=(4, R // tr),
            in_specs=[pl.BlockSpec((None, None, tr, D), lambda k, i, c: (k, c[0], i, 0)),
                      pl.BlockSpec((None, tr, D), lambda k, i, c: (k, i, 0))],
            out_specs=pl.BlockSpec((None, tr, D), lambda k, i, c: (k, i, 0))),
        out_shape=jax.ShapeDtypeStruct((4, R, D), BF16),
        compiler_params=_params(("arbitrary", "arbitrary")),
    )(core, g, r)


ADAM_LR, ADAM_B1, ADAM_B2, ADAM_EPS, ADAM_WD, ADAM_STEP = 0.001, 0.9, 0.999, 1e-08, 0.01, 10


def _adamw(w, g, m, v):
    m = ADAM_B1 * m + (1.0 - ADAM_B1) * g
    v = ADAM_B2 * v + (1.0 - ADAM_B2) * (g * g)
    m_hat = m / (1.0 - ADAM_B1 ** ADAM_STEP)
    v_hat = v / (1.0 - ADAM_B2 ** ADAM_STEP)
    return -ADAM_LR * (m_hat / (jnp.sqrt(v_hat) + ADAM_EPS) + ADAM_WD * w), m, v


def _adam(parts, w, m, v, tr, name, small=None):
    R, D = w.shape
    n = parts.shape[0]
    steps = R // tr

    def body(p_ref, w_ref, m_ref, v_ref, *rest):
        if small is None:
            g_ref, d_ref, nm_ref, nv_ref = rest
        else:
            s_ref, g_ref, d_ref, nm_ref, nv_ref, o_ref, send_sems, recv_sems, local_sem = rest
            i = pl.program_id(0)
            _around_grid(lambda: _copies_to_all(s_ref, o_ref, send_sems, recv_sems, local_sem), i == 0, i == steps - 1)
        g = p_ref[0].astype(F32)
        for k in range(1, n):
            g = g + p_ref[k].astype(F32)
        g_ref[...] = g
        d_ref[...], nm_ref[...], nv_ref[...] = _adamw(w_ref[...], g, m_ref[...], v_ref[...])

    blk = pl.BlockSpec((tr, D), lambda i: (i, 0))
    in_specs = [pl.BlockSpec((n, tr, D), lambda i: (0, i, 0)), blk, blk, blk]
    out_specs, out_shape = [blk] * 4, [jax.ShapeDtypeStruct((R, D), F32)] * 4
    operands, scratch = (parts, w, m, v), []
    if small is not None:
        in_specs, out_specs, operands = in_specs + [ANY], out_specs + [ANY], operands + (small,)
        out_shape = out_shape + [jax.ShapeDtypeStruct((N_DEV,) + small.shape, small.dtype)]
        scratch = [pltpu.SemaphoreType.DMA((7,)), pltpu.SemaphoreType.DMA((7,)), pltpu.SemaphoreType.DMA]
    return pl.pallas_call(
        body, name=name, grid=(steps,), in_specs=in_specs, out_specs=out_specs, out_shape=out_shape,
        scratch_shapes=scratch, compiler_params=_params(("arbitrary",)),
    )(*operands)


ROWS_IN = 592
ROWS_BF16 = ROWS_IN + 128 + 128 + 32
ROWS_CONV = 16
ROWS_ALL = ROWS_BF16 + ROWS_CONV
ROW_TILE = 224


def _pad_rows(a, rows):
    return jnp.pad(a, ((0, rows - a.shape[0]), (0, 0)))


def _pack_shards(w_in, w_out, w_gate, w_up, conv_w):
    return jnp.concatenate([
        _pad_rows(w_in[0].T, ROWS_IN), w_out[0], w_gate[0], w_up[0].reshape(32, D_MODEL),
        _pad_rows(jnp.pad(conv_w[0].reshape(1, 512), ((0, 0), (0, 512))), ROWS_CONV)], axis=0)


def _unpack_shards(a):
    return (a[:SHARD_IN].T[None], a[ROWS_IN:ROWS_IN + 128][None], a[ROWS_IN + 128:ROWS_IN + 256][None],
            a[ROWS_IN + 256:ROWS_BF16].reshape(1, 256, 128), a[ROWS_BF16, :512].reshape(1, 4, 128))


def _adam_small(parts, ws, ms, vs):
    n = len(ws)

    def body(p_ref, *refs):
        ins, outs = refs[:3 * n], refs[3 * n:]
        g = p_ref[0]
        for k in range(1, N_DEV):
            g = g + p_ref[k]
        pieces = [g[4:5], g[5:6], g[6:7, 0:4], g[6:7, 4:8], g[2:3, :512], g[2:3, 512:], g[0:1], g[1:2]]
        for t, gt in enumerate(pieces):
            d, nm, nv = _adamw(ins[t][...], gt, ins[n + t][...], ins[2 * n + t][...])
            for o, val in zip(outs[4 * t:4 * t + 4], (gt, d, nm, nv)):
                o[...] = val
        outs[4 * n][...] = g[3:4, 0:1]

    shapes = [jax.ShapeDtypeStruct(w.shape, F32) for w in ws for _ in range(4)] + [jax.ShapeDtypeStruct((1, 1), F32)]
    res = pl.pallas_call(body, name="adam_small", out_shape=shapes)(parts, *ws, *ms, *vs)
    return [res[k:4 * n:4] for k in range(4)], res[4 * n]


def kernel(x, p, pre_norm_w, w_in, ml_conv_w, ml_conv_b, ml_i_bias, ml_f_bias, sb_norm_w, ml_norm_w, w_out, post_norm_w, ple_w_up, ple_w_gate, ple_b_gate, loss_target, m_pre_norm_w, m_w_in, m_ml_conv_w, m_ml_conv_b, m_ml_i_bias, m_ml_f_bias, m_sb_norm_w, m_ml_norm_w, m_w_out, m_post_norm_w, m_ple_w_up, m_ple_w_gate, m_ple_b_gate, v_pre_norm_w, v_w_in, v_ml_conv_w, v_ml_conv_b, v_ml_i_bias, v_ml_f_bias, v_sb_norm_w, v_ml_norm_w, v_w_out, v_post_norm_w, v_ple_w_up, v_ple_w_gate, v_ple_b_gate):
    D = D_MODEL
    w_pk = _pack_shards(w_in, w_out, ple_w_gate, ple_w_up, ml_conv_w)
    m_pk = _pack_shards(m_w_in, m_w_out, m_ple_w_gate, m_ple_w_up, m_ml_conv_w)
    v_pk = _pack_shards(v_w_in, v_w_out, v_ple_w_gate, v_ple_w_up, v_ml_conv_w)
    w_sm = [pre_norm_w, ml_conv_b, ml_i_bias, ml_f_bias, sb_norm_w, ml_norm_w, post_norm_w, ple_b_gate]
    m_sm = [m_pre_norm_w, m_ml_conv_b, m_ml_i_bias, m_ml_f_bias, m_sb_norm_w, m_ml_norm_w, m_post_norm_w, m_ple_b_gate]
    v_sm = [v_pre_norm_w, v_ml_conv_b, v_ml_i_bias, v_ml_f_bias, v_sb_norm_w, v_ml_norm_w, v_post_norm_w, v_ple_b_gate]

    w_bf = w_pk[:ROWS_BF16].astype(BF16)
    ga, gb = _all_gather(w_bf[:SHARD_IN], w_pk[ROWS_BF16:])
    w_in_t = ga.reshape(N_IN, D)
    wg_t = _pad_rows(w_in_t[N_MAIN:], 128)
    conv_w_f = gb[:, 0, :512].reshape(N_DEV, 4, 128).transpose(1, 0, 2).reshape(4, D)
    gbias = jnp.pad(jnp.concatenate([ml_i_bias, ml_f_bias], axis=1), ((0, 0), (0, 120)))

    def unpack_late(gl):
        return (gl[:, :128].reshape(D, D), gl[:, 128:256].reshape(D, D),
                gl[:, 256:].reshape(N_DEV, 256, 128).transpose(1, 0, 2).reshape(256, D))

    def exchange(g):
        g_in = g["w_in_t"].reshape(N_DEV, SHARD_IN, D)
        g_blocks = jnp.concatenate([
            jnp.pad(g_in, ((0, 0), (0, ROWS_IN - SHARD_IN), (0, 0))),
            g["w_out"].reshape(N_DEV, 128, D), g["w_gate"].reshape(N_DEV, 128, D),
            g["w_up"].reshape(256, N_DEV, 128).transpose(1, 0, 2).reshape(N_DEV, 32, D),
            jnp.pad(g["conv_w"].reshape(4, N_DEV, 128).transpose(1, 0, 2).reshape(N_DEV, 1, 512),
                    ((0, 0), (0, ROWS_CONV - 1), (0, 512))),
        ], axis=1).reshape(4, 2, ROWS_ALL, D)
        core = lax.axis_index("c").astype(jnp.int32).reshape(1)
        return _pair_sum(g_blocks, _exchange_pair(g_blocks), core, ROW_TILE)

    (dx, g_pre, parts), g, small = _local_step(
        x[0], p[0, 0], loss_target[0], pre_norm_w, w_in_t, wg_t, conv_w_f, ml_conv_b, gbias, sb_norm_w, ml_norm_w,
        post_norm_w, ple_b_gate, (w_bf[ROWS_IN:], unpack_late), exchange)

    g_small = jnp.concatenate([small[0:4], g_pre, g["conv_b"], jnp.pad(g["gif"][0:1], ((0, 0), (0, D - 128))),
                               jnp.zeros((1, D), F32)], axis=0)
    grad_pk, delta_pk, nm_pk, nv_pk, parts_sm = _adam(parts, w_pk, m_pk, v_pk, ROW_TILE, "adam", g_small)
    (grad_sm, delta_sm, nm_sm, nv_sm), loss = _adam_small(parts_sm, w_sm, m_sm, v_sm)

    def ordered(pk, sm):
        win, wout, wgate, wup, convw = _unpack_shards(pk)
        pre_w, conv_b, i_b, f_b, sb_nw, ml_nw, post_w, b_gate = sm
        return [pre_w, win, convw, conv_b, i_b, f_b, sb_nw, ml_nw, wout, post_w, wup, wgate, b_gate]

    return (loss[0, 0], dx[None], *ordered(grad_pk, grad_sm), *ordered(delta_pk, delta_sm), *ordered(nm_pk, nm_sm),
            *ordered(nv_pk, nv_sm))
```

```python
import functools

import jax
import jax.numpy as jnp
from jax import lax
from jax.experimental import pallas as pl
from jax.experimental.pallas import tpu as pltpu

F32 = jnp.float32
BF16 = jnp.bfloat16
EPS = 1e-6
D_MODEL = 1024
SB_W = 512
ML_W = 512
N_MAIN = 4608
N_IN = 4616
SHARD_IN = 577
SHARD_IN_PAD = 584
TQ = 1024
TK = 256
ND = TQ // TK
LCH = 128
VMEM_LIMIT = 56 * 1024 * 1024


def _dot(a, b):
    return jnp.dot(a, b, preferred_element_type=F32)


def _dot_nt(a, b):
    return lax.dot_general(a, b, (((1,), (1,)), ((), ())), preferred_element_type=F32)


def _dot_tn(a, b):
    return lax.dot_general(a, b, (((0,), (0,)), ((), ())), preferred_element_type=F32)


def _split2(x):
    hi = x.astype(BF16)
    lo = (x - hi.astype(F32)).astype(BF16)
    return hi, lo


def _split3(x):
    hi = x.astype(BF16)
    r = x - hi.astype(F32)
    mid = r.astype(BF16)
    lo = (r - mid.astype(F32)).astype(BF16)
    return hi, mid, lo


def _params(sem):
    return pltpu.CompilerParams(dimension_semantics=sem, vmem_limit_bytes=VMEM_LIMIT)


def _log_sigmoid_parts(z):
    e = jnp.exp(-jnp.abs(z))
    return jnp.minimum(z, 0.0) - jnp.log(1.0 + e)


def _neg_log_sigmoid(nz):
    nz = jnp.minimum(nz, 80.0)
    sp = jnp.log(1.0 + jnp.exp(nz))
    return sp, nz - sp


def _sb_fwd(proj):
    S = proj.shape[0]
    nq = S // TQ

    def body(q_ref, k_ref, v_ref, y_ref, t_ref, acc_ref, car_ref, zs_ref):
        i = pl.program_id(1)
        low = lax.broadcasted_iota(jnp.int32, (TQ, 128), 1) < 64
        row = lax.broadcasted_iota(jnp.int32, (TK, TK), 0)
        col = lax.broadcasted_iota(jnp.int32, (TK, TK), 1)
        uo = (row > col).astype(BF16)
        uo = jnp.concatenate([uo, uo], axis=0)
        q = q_ref[...] * 0.125
        qh = (jnp.where(low, -q, 0.0).astype(BF16), jnp.where(low, 0.0, -q).astype(BF16))
        acc_ref[...] = jnp.zeros_like(acc_ref)
        car_ref[...] = jnp.zeros_like(car_ref)

        def block(j, r0):
            diag = r0 is not None
            r0 = r0 or 0
            if diag:
                strict = (lax.broadcasted_iota(jnp.int32, (TQ - r0, TK), 1)
                          < lax.broadcasted_iota(jnp.int32, (TQ - r0, TK), 0))
            rows = pl.ds(pl.multiple_of(j * TK, TK), TK)
            kb = k_ref[rows, :].astype(BF16)
            vb = v_ref[rows, :].astype(BF16)
            if not diag:
                kn = k_ref[pl.ds(pl.multiple_of(jnp.maximum(j - 1, 0) * TK, TK), TK), :].astype(BF16)

            def first(h):
                if diag:
                    nz = _dot_nt(qh[h][r0:], kb)
                else:
                    nz = zs_ref[h]
                    zs_ref[h] = _dot_nt(qh[h], kn)
                sp, lk = _neg_log_sigmoid(nz)
                if diag:
                    lk = jnp.where(strict, lk, 0.0)
                hi, lo = _split2(lk)
                return sp, lk[:, 0:1], _dot(jnp.concatenate([hi, lo], axis=1), uo)

            def second(h, sp, lk0, rr):
                car = car_ref[h, r0:, :]
                a = jnp.exp((jnp.concatenate([car, car], axis=1) + rr) - sp)
                if diag:
                    a = jnp.where(strict, a, 0.0)
                acc_ref[h, r0:, :] += _dot(a.astype(BF16), vb)
                car_ref[h, r0:, :] = car + jnp.broadcast_to(rr[:, 0:1] + lk0, car.shape)

            if diag:
                halves = [first(h) for h in range(2)]
                for h in range(2):
                    second(h, *halves[h])
            else:
                for h in range(2):
                    second(h, *first(h))

        for d in reversed(range(ND)):
            block(ND * i + d, TK * d)

        @pl.when(i > 0)
        def _():
            k0 = k_ref[pl.ds(pl.multiple_of((ND * i - 1) * TK, TK), TK), :].astype(BF16)
            for h in range(2):
                zs_ref[h] = _dot_nt(qh[h], k0)

        def loop(n, c):
            block(ND * i - 1 - n, None)
            return c

        lax.fori_loop(0, ND * i, loop, 0)
        y_ref[...] = jnp.where(low, acc_ref[0], acc_ref[1])
        t_ref[...] = jnp.where(low, car_ref[0], car_ref[1])

    return pl.pallas_call(
        body, name="sb_fwd", grid=(4, nq),
        in_specs=[pl.BlockSpec((TQ, 128), lambda p, i: (i, p)),
                  pl.BlockSpec((S, 128), lambda p, i: (0, 4 + p)),
                  pl.BlockSpec((S, 128), lambda p, i: (0, 8 + p))],
        out_specs=[pl.BlockSpec((TQ, 128), lambda p, i: (i, p)),
                   pl.BlockSpec((TQ, 128), lambda p, i: (i, p))],
        out_shape=[jax.ShapeDtypeStruct((S, SB_W), F32), jax.ShapeDtypeStruct((S, SB_W), F32)],
        scratch_shapes=[pltpu.VMEM((2, TQ, 128), F32), pltpu.VMEM((2, TQ, 128), F32), pltpu.VMEM((2, TQ, TK), F32)],
        compiler_params=_params(("arbitrary", "arbitrary")),
    )(proj, proj, proj)


def _sb_bwd(proj, tot, dy):
    S = proj.shape[0]
    nq = S // TQ

    def body(q_ref, k_ref, v_ref, t_ref, dy_ref, dq_ref, dk_ref, dv_ref, dqa_ref, cp_ref, cg_ref, dkt_ref, dvt_ref):
        i = pl.program_id(1)
        low = lax.broadcasted_iota(jnp.int32, (TQ, 128), 1) < 64
        row = lax.broadcasted_iota(jnp.int32, (TK, TK), 0)
        col = lax.broadcasted_iota(jnp.int32, (TK, TK), 1)
        u_inc = (row <= col).astype(BF16)
        u_inc = jnp.concatenate([u_inc, u_inc], axis=0)
        u_exc = (row < col).astype(BF16)
        q = q_ref[...] * 0.125
        qh = (jnp.where(low, -q, 0.0).astype(BF16), jnp.where(low, 0.0, -q).astype(BF16))
        dy_ = dy_ref[...]
        dyh = (jnp.where(low, dy_, 0.0).astype(BF16), jnp.where(low, 0.0, dy_).astype(BF16))
        qt = tuple(jnp.transpose(x) for x in qh)
        dyt = tuple(jnp.transpose(x) for x in dyh)
        t_ = t_ref[...]
        t_sw = pltpu.roll(t_, 64, 1)
        th = (jnp.where(low, t_, t_sw), jnp.where(low, t_sw, t_))
        dqa_ref[...] = jnp.zeros_like(dqa_ref)
        cp_ref[...] = jnp.zeros_like(cp_ref)
        cg_ref[...] = jnp.zeros_like(cg_ref)

        @pl.when(i == 0)
        def _():
            dkt_ref[...] = jnp.zeros_like(dkt_ref)
            dvt_ref[...] = jnp.zeros_like(dvt_ref)

        def block(j, r0):
            diag = r0 is not None
            r0 = r0 or 0
            if diag:
                strict = (lax.broadcasted_iota(jnp.int32, (TQ - r0, TK), 1)
                          < lax.broadcasted_iota(jnp.int32, (TQ - r0, TK), 0))
            rows = pl.ds(pl.multiple_of(j * TK, TK), TK)
            kb = k_ref[rows, :].astype(BF16)
            vb = v_ref[rows, :].astype(BF16)
            dk_acc = jnp.zeros((128, TK), F32)
            dv_acc = jnp.zeros((128, TK), F32)
            for h in range(2):
                qr, dyr = qh[h][r0:], dyh[h][r0:]
                sp, lk = _neg_log_sigmoid(_dot_nt(qr, kb))
                if diag:
                    lk = jnp.where(strict, lk, 0.0)
                hi, lo = _split2(lk)
                pp = _dot(jnp.concatenate([hi, lo], axis=1), u_inc)
                cp, cg = cp_ref[h, r0:, :], cg_ref[h, r0:, :]
                rest = th[h][r0:] - cp
                a = jnp.exp((jnp.concatenate([rest, rest], axis=1) - pp) - sp)
                if diag:
                    a = jnp.where(strict, a, 0.0)
                g = _dot_nt(dyr, vb) * a
                gg = _dot(g.astype(BF16), u_exc)
                beta = jnp.exp(-sp)
                dz = g - beta * (g + (jnp.concatenate([cg, cg], axis=1) + gg))
                if diag:
                    dz = jnp.where(strict, dz, 0.0)
                dzb = dz.astype(BF16)
                dqa_ref[h, r0:, :] += _dot(dzb, kb)
                dk_acc += _dot(qt[h][:, r0:], dzb)
                dv_acc += _dot(dyt[h][:, r0:], a.astype(BF16))
                cp_ref[h, r0:, :] = cp + jnp.broadcast_to(pp[:, TK - 1:TK], cp.shape)
                cg_ref[h, r0:, :] = cg + jnp.broadcast_to(gg[:, TK - 1:TK] + g[:, TK - 1:TK], cg.shape)
            dkt_ref[:, rows] -= dk_acc
            dvt_ref[:, rows] += dv_acc

        def loop(j, c):
            block(j, None)
            return c

        lax.fori_loop(0, ND * i, loop, 0)
        for d in range(ND):
            block(ND * i + d, TK * d)
        dq_ref[...] = (jnp.where(low, dqa_ref[0], dqa_ref[1]) * 0.125).astype(BF16)

        @pl.when(i == nq - 1)
        def _():
            dk_ref[...] = jnp.transpose(dkt_ref[...]).astype(BF16)
            dv_ref[...] = jnp.transpose(dvt_ref[...]).astype(BF16)

    return pl.pallas_call(
        body, name="sb_bwd", grid=(4, nq),
        in_specs=[pl.BlockSpec((TQ, 128), lambda p, i: (i, p)),
                  pl.BlockSpec((S, 128), lambda p, i: (0, 4 + p)),
                  pl.BlockSpec((S, 128), lambda p, i: (0, 8 + p)),
                  pl.BlockSpec((TQ, 128), lambda p, i: (i, p)),
                  pl.BlockSpec((TQ, 128), lambda p, i: (i, p))],
        out_specs=[pl.BlockSpec((TQ, 128), lambda p, i: (i, p)),
                   pl.BlockSpec((S, 128), lambda p, i: (0, p)),
                   pl.BlockSpec((S, 128), lambda p, i: (0, p))],
        out_shape=[jax.ShapeDtypeStruct((S, SB_W), BF16)] * 3,
        scratch_shapes=[pltpu.VMEM((2, TQ, 128), F32)] * 3 + [pltpu.VMEM((128, S), F32)] * 2,
        compiler_params=_params(("arbitrary", "arbitrary")),
    )(proj, proj, proj, tot, dy)


ML_SCALE = 128 ** -0.5
RC = 256


def _conv_taps(cur, prev8, w):
    n = cur.shape[0]
    win = jnp.concatenate([prev8, cur], axis=0)
    out = w[3:4, :] * cur
    for j in range(3):
        out = out + w[j:j + 1, :] * pltpu.roll(win, 3 - j, 0)[8:8 + n]
    return out


def _ml_prep(proj, conv_w, conv_b):
    S = proj.shape[0]

    def body(x_ref, w_ref, b_ref, o_ref):
        c = pl.program_id(0)
        scale = jnp.where(c < 4, 1.0, ML_SCALE).astype(F32)
        w = w_ref[...]
        b = b_ref[...]

        @pl.when(c < 8)
        def _():
            for n in range(S // RC):
                cur = x_ref[n * RC:(n + 1) * RC, :]
                prev8 = x_ref[n * RC - 8:n * RC, :] if n else jnp.zeros((8, 128), F32)
                pre = b + _conv_taps(cur, prev8, w)
                o_ref[n * RC:(n + 1) * RC, :] = (pre * jax.nn.sigmoid(pre) * scale).astype(BF16)

        @pl.when(c >= 8)
        def _():
            o_ref[...] = x_ref[...].astype(BF16)

    return pl.pallas_call(
        body, name="ml_prep", grid=(12,),
        in_specs=[pl.BlockSpec((S, 128), lambda c: (0, 16 + c)),
                  pl.BlockSpec((4, 128), lambda c: (0, jnp.minimum(c, 7))),
                  pl.BlockSpec((1, 128), lambda c: (0, jnp.minimum(c, 7)))],
        out_specs=pl.BlockSpec((S, 128), lambda c: (0, c)),
        out_shape=jax.ShapeDtypeStruct((S, 1536), BF16),
        compiler_params=_params(("arbitrary",)),
    )(proj, conv_w, conv_b)


def _ml_prep_bwd(proj, conv_w, conv_b, dq, dk):
    S = proj.shape[0]

    def body(x_ref, w_ref, b_ref, dq_ref, dk_ref, dx_ref, gw_ref, gb_ref, dp_ref):
        c = pl.program_id(0)
        w = w_ref[...]
        b = b_ref[...]
        gw = [jnp.zeros((1, 128), F32) for _ in range(4)]
        gb = jnp.zeros((1, 128), F32)
        for n in range(S // RC):
            rows = slice(n * RC, (n + 1) * RC)
            cur = x_ref[rows, :]
            prev8 = x_ref[n * RC - 8:n * RC, :] if n else jnp.zeros((8, 128), F32)
            pre = b + _conv_taps(cur, prev8, w)
            s = jax.nn.sigmoid(pre)
            dpost = jnp.where(c < 4, dq_ref[rows, :].astype(F32), dk_ref[rows, :].astype(F32) * ML_SCALE)
            dpre = dpost * (s * (1.0 + pre * (1.0 - s)))
            dp_ref[rows, :] = dpre
            win = jnp.concatenate([prev8, cur], axis=0)
            gb = gb + jnp.sum(dpre, axis=0, keepdims=True)
            gw[3] = gw[3] + jnp.sum(dpre * cur, axis=0, keepdims=True)
            for j in range(3):
                gw[j] = gw[j] + jnp.sum(dpre * pltpu.roll(win, 3 - j, 0)[8:8 + RC], axis=0, keepdims=True)
        dp_ref[S:S + 8, :] = jnp.zeros((8, 128), F32)
        for n in range(S // RC):
            win = dp_ref[n * RC:(n + 1) * RC + 8, :]
            dx = w[3:4, :] * win[:RC]
            for j in range(3):
                dx = dx + w[j:j + 1, :] * pltpu.roll(win, RC + 8 - (3 - j), 0)[:RC]
            dx_ref[n * RC:(n + 1) * RC, :] = dx.astype(BF16)
        gw_ref[...] = jnp.concatenate(gw, axis=0)
        gb_ref[...] = gb

    return pl.pallas_call(
        body, name="ml_prep_bwd", grid=(8,),
        in_specs=[pl.BlockSpec((S, 128), lambda c: (0, 16 + c)),
                  pl.BlockSpec((4, 128), lambda c: (0, c)),
                  pl.BlockSpec((1, 128), lambda c: (0, c)),
                  pl.BlockSpec((S, 128), lambda c: (0, jnp.minimum(c, 3))),
                  pl.BlockSpec((S, 128), lambda c: (0, jnp.maximum(c - 4, 0)))],
        out_specs=[pl.BlockSpec((S, 128), lambda c: (0, c)),
                   pl.BlockSpec((4, 128), lambda c: (0, c)),
                   pl.BlockSpec((1, 128), lambda c: (0, c))],
        out_shape=[jax.ShapeDtypeStruct((S, 1024), BF16), jax.ShapeDtypeStruct((4, 1024), F32),
                   jax.ShapeDtypeStruct((1, 1024), F32)],
        scratch_shapes=[pltpu.VMEM((S + 8, 128), F32)],
        compiler_params=_params(("arbitrary",)),
    )(proj, conv_w, conv_b, dq, dk)


NH = 4


def _heads(x):
    return jnp.stack([x[:, 128 * h:128 * (h + 1)] for h in range(NH)])


def _unheads(x):
    return jnp.concatenate([x[h] for h in range(NH)], axis=1)


def _per_head(f, *xs):
    return jnp.stack([f(*[x[h] for x in xs]) for h in range(NH)])


def _ml_gates(g):
    lg = _log_sigmoid_parts(g)
    ig = jnp.stack([jnp.broadcast_to(g[:, h:h + 1], (LCH, 128)) for h in range(NH)])
    lf = jnp.stack([jnp.broadcast_to(lg[:, 4 + h:5 + h], (LCH, 128)) for h in range(NH)])
    return ig, lf


def _tri_sum(tri, x, dot):
    return _per_head(lambda a, b, c: dot(tri, a) + dot(tri, b) + dot(tri, c), *_split3(x))


def _ml_chunk_fwd(q, k, v, ig, lf, ct, n_st, m_st, tri, causal):
    vf = v.astype(F32)
    b = _tri_sum(tri, lf, _dot)
    b_last = b[:, LCH - 1:LCH, :]
    g = b_last - b + ig
    m_loc = jnp.max(g, axis=1, keepdims=True)
    w = jnp.exp(g - m_loc)
    vwf = vf * w
    vw = vwf.astype(BF16)
    ct_loc = _per_head(_dot_tn, k, vw)
    kf = k.astype(F32)
    n_loc = jnp.sum(w * kf, axis=1, keepdims=True)
    r = _per_head(jnp.transpose, ig - b)
    d_log = jnp.where(causal, b + r, -jnp.inf)
    m_t = jnp.maximum(b + m_st, jnp.max(d_log, axis=2, keepdims=True))
    w_in = jnp.exp(d_log - m_t)
    qk = _per_head(_dot_nt, q, k)
    scores = qk * w_in
    cs = jnp.exp(b + m_st - m_t)
    ctb = ct.astype(BF16)
    qc = _per_head(_dot, q, ctb)
    qf = q.astype(F32)
    qn = jnp.sum(qf * n_st, axis=2, keepdims=True)
    num = _per_head(_dot, scores.astype(BF16), v) + cs * qc
    den = jnp.sum(scores, axis=2, keepdims=True) + cs * qn
    em = jnp.exp(-m_t)
    dd = jnp.maximum(jnp.abs(den), em)
    h = num / dd
    m_new = jnp.maximum(b_last + m_st, m_loc)
    a = jnp.exp(b_last + m_st - m_new)
    gg = jnp.exp(m_loc - m_new)
    return dict(vwf=vwf, vw=vw, kf=kf, qf=qf, ct_loc=ct_loc, n_loc=n_loc, w=w, w_in=w_in, qk=qk, scores=scores,
                cs=cs, ctb=ctb, qc=qc, qn=qn, den=den, em=em, dd=dd, h=h, m_new=m_new, a=a, gg=gg)


def _ml_consts():
    row = lax.broadcasted_iota(jnp.int32, (LCH, LCH), 0)
    col = lax.broadcasted_iota(jnp.int32, (LCH, LCH), 1)
    return row, (col <= row), (col <= row).astype(BF16)


def _ml_rows(qkv_ref, g_ref, gb_ref, c):
    rows = pl.ds(pl.multiple_of(c * LCH, LCH), LCH)
    q, k, v = (_heads(qkv_ref[rows, 512 * t:512 * (t + 1)]) for t in range(3))
    ig, lf = _ml_gates(g_ref[rows, :] + gb_ref[...])
    return rows, q, k, v, ig, lf


VMEM_SPEC = pl.BlockSpec(memory_space=pltpu.VMEM)


def _ml_fwd(qkv, proj_g, gbias):
    S = qkv.shape[0]
    nc = S // LCH

    def body(qkv_ref, g_ref, gb_ref, h_ref, cst_ref, nm_ref, ct_ref, n_ref, m_ref):
        _, causal, tri = _ml_consts()
        ct_ref[...] = jnp.zeros_like(ct_ref)
        n_ref[...] = jnp.zeros_like(n_ref)
        m_ref[...] = jnp.zeros_like(m_ref)

        def chunk(c, carry):
            rows, q, k, v, ig, lf = _ml_rows(qkv_ref, g_ref, gb_ref, c)
            ct, n_st, m_st = ct_ref[...], n_ref[:, 0:1, :], m_ref[:, 0:1, :]
            cst_ref[c] = ct
            nm_ref[c, :, 0:8, :] = n_ref[...]
            nm_ref[c, :, 8:16, :] = m_ref[...]
            f = _ml_chunk_fwd(q, k, v, ig, lf, ct, n_st, m_st, tri, causal)
            h_ref[rows, :] = _unheads(f["h"])
            ct_ref[...] = f["a"] * ct + f["gg"] * f["ct_loc"]
            n_ref[...] = jnp.broadcast_to(f["a"] * n_st + f["gg"] * f["n_loc"], (NH, 8, 128))
            m_ref[...] = jnp.broadcast_to(f["m_new"], (NH, 8, 128))
            return carry

        lax.fori_loop(0, nc, chunk, 0)

    return pl.pallas_call(
        body, name="ml_fwd", in_specs=[VMEM_SPEC] * 3, out_specs=[VMEM_SPEC] * 3,
        out_shape=[jax.ShapeDtypeStruct((S, ML_W), F32), jax.ShapeDtypeStruct((nc, NH, 128, 128), F32),
                   jax.ShapeDtypeStruct((nc, NH, 16, 128), F32)],
        scratch_shapes=[pltpu.VMEM((NH, 128, 128), F32), pltpu.VMEM((NH, 8, 128), F32), pltpu.VMEM((NH, 8, 128), F32)],
        compiler_params=pltpu.CompilerParams(vmem_limit_bytes=VMEM_LIMIT),
    )(qkv, proj_g, gbias)


def _ml_bwd(qkv, proj_g, gbias, cst, nm, dh):
    S = qkv.shape[0]
    nc = S // LCH

    def body(qkv_ref, g_ref, gb_ref, cst_ref, nm_ref, dh_ref, dq_ref, dk_ref, dv_ref, dif_ref, gsum_ref,
             dct_ref, dn_ref, gi_ref):
        row, causal, tri = _ml_consts()
        lane = lax.broadcasted_iota(jnp.int32, (LCH, 128), 1)
        last_row = row == LCH - 1
        dct_ref[...] = jnp.zeros_like(dct_ref)
        dn_ref[...] = jnp.zeros_like(dn_ref)
        gi_ref[...] = jnp.zeros_like(gi_ref)

        def chunk(t, carry):
            c = nc - 1 - t
            rows, q, k, v, ig, lf = _ml_rows(qkv_ref, g_ref, gb_ref, c)
            ct, n_st, m_st = cst_ref[c], nm_ref[c, :, 0:1, :], nm_ref[c, :, 8:9, :]
            f = _ml_chunk_fwd(q, k, v, ig, lf, ct, n_st, m_st, tri, causal)
            dh_ = _heads(dh_ref[rows, :])
            dct_new, dn_new = dct_ref[...], dn_ref[:, 0:1, :]
            e_num = dh_ / f["dd"]
            hdh = jnp.sum(f["h"] * dh_, axis=2, keepdims=True)
            free = jnp.abs(f["den"]) > f["em"]
            e_den = jnp.where(free, -hdh / f["dd"] * jnp.sign(f["den"]), 0.0)
            e_num_b = e_num.astype(BF16)
            ds_ = _per_head(_dot_nt, e_num_b, v) + e_den
            dqk = ds_ * f["w_in"]
            gam = dqk * f["qk"]
            dqk_b = dqk.astype(BF16)
            cse = f["cs"] * e_den
            dq = _per_head(_dot, dqk_b, k) + f["cs"] * _per_head(_dot_nt, e_num_b, f["ctb"]) + cse * n_st
            dk = _per_head(_dot_tn, dqk_b, q)
            dv = _per_head(_dot_tn, f["scores"].astype(BF16), e_num_b)
            dcl = (f["gg"] * dct_new).astype(BF16)
            dnl = f["gg"] * dn_new
            kd = _per_head(_dot, k, dcl)
            dv = dv + f["w"] * kd
            dk = dk + _per_head(_dot_nt, f["vw"], dcl) + f["w"] * dnl
            gam_s = (jnp.sum(kd * f["vwf"], axis=2, keepdims=True)
                     + f["w"] * jnp.sum(f["kf"] * dnl, axis=2, keepdims=True))
            col_g = jnp.sum(_per_head(jnp.transpose, gam), axis=2, keepdims=True) + gam_s
            db = (jnp.sum(gam, axis=2, keepdims=True) + jnp.sum(e_num * (f["cs"] * f["qc"]), axis=2, keepdims=True)
                  + cse * f["qn"] - col_g)
            state = jnp.sum(jnp.sum(dct_new * ct, axis=2, keepdims=True), axis=1, keepdims=True)
            state = state + jnp.sum(dn_new * n_st, axis=2, keepdims=True)
            db_last = jnp.sum(gam_s[:, :, 0:1], axis=1, keepdims=True) + f["a"][:, :, 0:1] * state
            db = jnp.where(last_row, db + db_last, db)
            dlf = _tri_sum(tri, db, _dot_tn)
            df = dlf * (1.0 - jnp.exp(lf))
            dq_ref[rows, :] = _unheads(dq).astype(BF16)
            dk_ref[rows, :] = _unheads(dk).astype(BF16)
            dv_ref[rows, :] = _unheads(dv).astype(BF16)
            dif = jnp.zeros((LCH, 128), F32)
            for h in range(NH):
                dif = dif + jnp.where(lane == h, col_g[h], 0.0) + jnp.where(lane == h + 4, df[h], 0.0)
            dif_ref[rows, :] = dif
            clamped = jnp.where(free, 0.0, hdh)
            gi_ref[...] += jnp.broadcast_to(jnp.sum(clamped, axis=1, keepdims=True), (NH, 8, 128))
            dct_ref[...] = f["a"] * dct_new + _per_head(_dot_tn, q, (f["cs"] * e_num).astype(BF16))
            dn_ref[...] = jnp.broadcast_to(f["a"] * dn_new + jnp.sum(cse * f["qf"], axis=1, keepdims=True), (NH, 8, 128))
            return carry

        lax.fori_loop(0, nc, chunk, 0)
        lane8 = lax.broadcasted_iota(jnp.int32, (8, 128), 1)
        gsum = jnp.where(lane8 >= 4, jnp.sum(dif_ref[...], axis=0, keepdims=True), 0.0)
        for h in range(NH):
            gsum = gsum + jnp.where(lane8 == h, gi_ref[h], 0.0)
        gsum_ref[...] = gsum

    return pl.pallas_call(
        body, name="ml_bwd", in_specs=[VMEM_SPEC] * 6, out_specs=[VMEM_SPEC] * 5,
        out_shape=[jax.ShapeDtypeStruct((S, ML_W), BF16)] * 3 + [jax.ShapeDtypeStruct((S, 128), F32),
                                                                  jax.ShapeDtypeStruct((8, 128), F32)],
        scratch_shapes=[pltpu.VMEM((NH, 128, 128), F32), pltpu.VMEM((NH, 8, 128), F32), pltpu.VMEM((NH, 8, 128), F32)],
        compiler_params=pltpu.CompilerParams(vmem_limit_bytes=VMEM_LIMIT),
    )(qkv, proj_g, gbias, cst, nm, dh)


MESH = pl.DeviceIdType.MESH
ANY = pl.BlockSpec(memory_space=pl.ANY)
N_DEV = 8


def _place():
    return lax.axis_index("x"), lax.axis_index("y"), lax.axis_index("c")


def _block_of(px, py, pc):
    return 4 * px + 2 * py + pc


def _copies_to_all(b_ref, o_ref, send_sems, recv_sems, local_sem):
    x, y, c = _place()
    mine = _block_of(x, y, c)
    copies = [pltpu.make_async_copy(b_ref, o_ref.at[mine], local_sem)]
    for k in range(1, N_DEV):
        peer = (x ^ (k >> 2), y ^ ((k >> 1) & 1), c ^ (k & 1))
        copies.append(pltpu.make_async_remote_copy(
            src_ref=b_ref, dst_ref=o_ref.at[mine], send_sem=send_sems.at[k - 1], recv_sem=recv_sems.at[k - 1],
            device_id=peer, device_id_type=MESH))
    return copies


def _copies_between_chips(p_ref, o_ref, send_sems, recv_sems, local_sem):
    x, y, c = _place()
    mine = 2 * x + y
    copies = [pltpu.make_async_copy(p_ref.at[mine], o_ref.at[mine], local_sem)]
    for k in range(1, 4):
        px, py = x ^ (k >> 1), y ^ (k & 1)
        copies.append(pltpu.make_async_remote_copy(
            src_ref=p_ref.at[2 * px + py], dst_ref=o_ref.at[mine], send_sem=send_sems.at[k - 1],
            recv_sem=recv_sems.at[k - 1], device_id=(px, py, c), device_id_type=MESH))
    return copies


def _around_grid(copies, first, last):
    @pl.when(first)
    def _():
        for cp in copies():
            cp.start()

    @pl.when(last)
    def _():
        for cp in copies():
            cp.wait()


def _inproj_fwd(x, pre_w, w_t, wg_t, blk=None):
    S, D = x.shape
    tm, tn = min(S, 1024), 1152
    ni, nj = S // tm, N_MAIN // tn

    def body(x_ref, pw_ref, w_ref, wg_ref, *rest):
        if blk is None:
            proj_ref, g_ref, u_ref = rest
        else:
            b_ref, proj_ref, g_ref, u_ref, o_ref, send_sems, recv_sems, local_sem = rest
            i, j = pl.program_id(0), pl.program_id(1)
            _around_grid(lambda: _copies_to_all(b_ref, o_ref, send_sems, recv_sems, local_sem),
                         (i == 0) & (j == 0), (i == ni - 1) & (j == nj - 1))

        @pl.when(pl.program_id(1) == 0)
        def _():
            xf = x_ref[...]
            r = lax.rsqrt(jnp.mean(xf * xf, axis=-1, keepdims=True) + EPS)
            u = (xf * r * pw_ref[...]).astype(BF16)
            u_ref[...] = u
            g_ref[...] = _dot_nt(u, wg_ref[...])

        proj_ref[...] = _dot_nt(u_ref[...], w_ref[...])

    in_specs = [pl.BlockSpec((tm, D), lambda i, j: (i, 0)),
                pl.BlockSpec((1, D), lambda i, j: (0, 0)),
                pl.BlockSpec((tn, D), lambda i, j: (j, 0)),
                pl.BlockSpec((128, D), lambda i, j: (0, 0))]
    out_specs = [pl.BlockSpec((tm, tn), lambda i, j: (i, j)),
                 pl.BlockSpec((tm, 128), lambda i, j: (i, 0)),
                 pl.BlockSpec((tm, D), lambda i, j: (i, 0))]
    out_shape = [jax.ShapeDtypeStruct((S, N_MAIN), F32), jax.ShapeDtypeStruct((S, 128), F32),
                 jax.ShapeDtypeStruct((S, D), BF16)]
    operands, scratch = (x, pre_w, w_t, wg_t), []
    if blk is not None:
        in_specs, out_specs, operands = in_specs + [ANY], out_specs + [ANY], operands + (blk,)
        out_shape = out_shape + [jax.ShapeDtypeStruct((N_DEV,) + blk.shape, blk.dtype)]
        scratch = [pltpu.SemaphoreType.DMA((7,)), pltpu.SemaphoreType.DMA((7,)), pltpu.SemaphoreType.DMA]
    return pl.pallas_call(
        body, name="inproj_fwd", grid=(ni, nj), in_specs=in_specs, out_specs=out_specs, out_shape=out_shape,
        scratch_shapes=scratch, compiler_params=_params(("arbitrary", "arbitrary")),
    )(*operands)


def _inproj_bwd(d_main, d_if, w_t, wg_t, x, pre_w, dx_tail, parts=None):
    S, D = x.shape
    tm, tk = min(S, 1024), 1152
    ni, nk = S // tm, N_MAIN // tk

    def body(d_ref, dg_ref, w_ref, wg_ref, x_ref, pw_ref, dt_ref, *rest):
        i, k = pl.program_id(0), pl.program_id(1)
        if parts is None:
            dx_ref, gpw_ref, acc_ref = rest
        else:
            p_ref, dx_ref, gpw_ref, o_ref, acc_ref, send_sems, recv_sems, local_sem = rest
            _around_grid(lambda: _copies_between_chips(p_ref, o_ref, send_sems, recv_sems, local_sem),
                         (i == 0) & (k == 0), (i == ni - 1) & (k == nk - 1))

        @pl.when(k == 0)
        def _():
            acc_ref[...] = _dot(dg_ref[...], wg_ref[...])

        acc_ref[...] += _dot(d_ref[...], w_ref[...])

        @pl.when(k == nk - 1)
        def _():
            xf = x_ref[...]
            r = lax.rsqrt(jnp.mean(xf * xf, axis=-1, keepdims=True) + EPS)
            xn = xf * r
            du = acc_ref[...]
            gw = du * pw_ref[...]
            dx_ref[...] = dt_ref[...] + r * (gw - xn * jnp.mean(gw * xn, axis=-1, keepdims=True))
            part = jnp.sum(du * xn, axis=0, keepdims=True)

            @pl.when(i == 0)
            def _():
                gpw_ref[...] = part

            @pl.when(i > 0)
            def _():
                gpw_ref[...] += part

    in_specs = [pl.BlockSpec((tm, tk), lambda i, k: (i, k)),
                pl.BlockSpec((tm, 128), lambda i, k: (i, 0)),
                pl.BlockSpec((tk, D), lambda i, k: (k, 0)),
                pl.BlockSpec((128, D), lambda i, k: (0, 0)),
                pl.BlockSpec((tm, D), lambda i, k: (i, 0)),
                pl.BlockSpec((1, D), lambda i, k: (0, 0)),
                pl.BlockSpec((tm, D), lambda i, k: (i, 0))]
    out_specs = [pl.BlockSpec((tm, D), lambda i, k: (i, 0)), pl.BlockSpec((1, D), lambda i, k: (0, 0))]
    out_shape = [jax.ShapeDtypeStruct((S, D), F32), jax.ShapeDtypeStruct((1, D), F32)]
    operands, scratch = (d_main, d_if, w_t, wg_t, x, pre_w, dx_tail), [pltpu.VMEM((tm, D), F32)]
    if parts is not None:
        in_specs, out_specs, operands = in_specs + [ANY], out_specs + [ANY], operands + (parts,)
        out_shape = out_shape + [jax.ShapeDtypeStruct(parts.shape, parts.dtype)]
        scratch = scratch + [pltpu.SemaphoreType.DMA((3,)), pltpu.SemaphoreType.DMA((3,)), pltpu.SemaphoreType.DMA]
    return pl.pallas_call(
        body, name="inproj_bwd", grid=(ni, nk), in_specs=in_specs, out_specs=out_specs, out_shape=out_shape,
        scratch_shapes=scratch, compiler_params=_params(("arbitrary", "arbitrary")),
    )(*operands)


def _matmul_tn(a, b, name, out_rows=None):
    S, M = a.shape
    N = b.shape[1]
    tmm = 1152 if M % 1152 == 0 else min(M, 1024)
    tk = min(S, 1024)
    nk = S // tk

    def body(a_ref, b_ref, o_ref):
        part = _dot_tn(a_ref[...].astype(BF16), b_ref[...].astype(BF16))

        @pl.when(pl.program_id(1) == 0)
        def _():
            o_ref[...] = part

        @pl.when(pl.program_id(1) > 0)
        def _():
            o_ref[...] += part

    return pl.pallas_call(
        body, name=name, grid=(M // tmm, nk),
        in_specs=[pl.BlockSpec((tk, tmm), lambda i, k: (k, i)),
                  pl.BlockSpec((tk, N), lambda i, k: (k, 0))],
        out_specs=pl.BlockSpec((tmm, N), lambda i, k: (i, 0)),
        out_shape=jax.ShapeDtypeStruct((out_rows or M, N), F32),
        compiler_params=_params(("arbitrary", "arbitrary")),
    )(a, b)


def _gate_rows_tn(d_if, u, g):
    S, D = u.shape
    tk = min(S, 1024)
    nk = S // tk

    def body(a_ref, b_ref, g_ref, o_ref, acc_ref):
        k = pl.program_id(0)
        part = _dot_tn(a_ref[...], b_ref[...])

        @pl.when(k == 0)
        def _():
            acc_ref[...] = part

        @pl.when(k > 0)
        def _():
            acc_ref[...] += part

        @pl.when(k == nk - 1)
        def _():
            o_ref[...] = acc_ref[0:8, :]

    return pl.pallas_call(
        body, name="gw_in_gates", grid=(nk,),
        in_specs=[pl.BlockSpec((tk, 128), lambda k: (k, 0)), pl.BlockSpec((tk, D), lambda k: (k, 0)), ANY],
        out_specs=pl.BlockSpec((8, D), lambda k: (N_MAIN // 8, 0)),
        out_shape=jax.ShapeDtypeStruct(g.shape, g.dtype),
        scratch_shapes=[pltpu.VMEM((128, D), F32)],
        input_output_aliases={2: 0},
        compiler_params=_params(("arbitrary",)),
    )(d_if, u, g)


def _half_mean(v, low):
    s_lo = jnp.sum(jnp.where(low, v, 0.0), axis=1, keepdims=True)
    s_hi = jnp.sum(jnp.where(low, 0.0, v), axis=1, keepdims=True)
    return jnp.where(low, s_lo, s_hi) * (1.0 / 64.0)


def _silu_grad(z, s):
    return s * (1.0 + z * (1.0 - s))


def _tail(y_sb, h_ml, proj, x, p, target, sb_nw, ml_nw, w_out, post_w, w_gate, b_gate, w_up):
    S, D = x.shape
    tm = 256

    def body(ysb_ref, hml_ref, sbz_ref, mlo_ref, mlz_ref, x_ref, p_ref, tg_ref, sbw_ref, mlw_ref, wo_ref, pw_ref,
             wg_ref, bg_ref, wu_ref,
             dx_ref, dysb_ref, dhml_ref, dsbz_ref, dmlo_ref, dmlz_ref, mix_ref, dy_ref, h1_ref, dgp_ref, dpu_ref,
             small_ref):
        lane = lax.broadcasted_iota(jnp.int32, (tm, 128), 1)
        low = lane < 64
        sb_saved, ml_saved, mixed = [], [], []
        for s in range(4):
            sl = slice(128 * s, 128 * s + 128)
            y = ysb_ref[:, sl]
            rs = lax.rsqrt(_half_mean(y * y, low) + EPS)
            n = y * rs
            z = sbz_ref[:, sl]
            sg = jax.nn.sigmoid(z)
            w = sbw_ref[:, sl]
            mixed.append((n * w) * (z * sg))
            sb_saved.append((rs, n, z, sg, w))
        for s in range(4):
            sl = slice(128 * s, 128 * s + 128)
            og = jax.nn.sigmoid(mlo_ref[:, sl])
            hh = hml_ref[:, sl]
            t = og * hh
            rs = lax.rsqrt(jnp.mean(t * t, axis=1, keepdims=True) + EPS)
            n = t * rs
            z = mlz_ref[:, sl]
            sg = jax.nn.sigmoid(z)
            w = mlw_ref[:, sl]
            mixed.append((n * w) * (z * sg))
            ml_saved.append((rs, n, z, sg, w, og, hh))
        mix = jnp.concatenate(mixed, axis=1).astype(BF16)
        mix_ref[...] = mix
        y = _dot(mix, wo_ref[...])
        rs_y = lax.rsqrt(jnp.mean(y * y, axis=1, keepdims=True) + EPS)
        yn = y * rs_y
        pw = pw_ref[...]
        h1 = x_ref[...] + yn * pw
        h1b = h1.astype(BF16)
        h1_ref[...] = h1b
        gate = jax.nn.sigmoid(_dot(h1b, wg_ref[...]) + bg_ref[...])
        pu = _dot(p_ref[...].astype(BF16), wu_ref[...])
        err = (h1 + gate * pu) - tg_ref[...]
        loss = 0.5 * jnp.sum(jnp.sum(err * err, axis=1, keepdims=True) * (1.0 / D))
        d_out = err * (1.0 / D)
        dpu_ref[...] = (d_out * gate).astype(BF16)
        dgp = (d_out * pu) * (gate * (1.0 - gate))
        dgpb = dgp.astype(BF16)
        dgp_ref[...] = dgpb
        d_h1 = d_out + _dot_nt(dgpb, wg_ref[...])
        dx_ref[...] = d_h1
        gwy = d_h1 * pw
        d_y = rs_y * (gwy - yn * jnp.mean(gwy * yn, axis=1, keepdims=True))
        d_yb = d_y.astype(BF16)
        dy_ref[...] = d_yb
        d_mix = _dot_nt(d_yb, wo_ref[...])
        g_nw = []
        for s in range(4):
            sl = slice(128 * s, 128 * s + 128)
            rs, n, z, sg, w = sb_saved[s]
            da = d_mix[:, sl]
            act = z * sg
            dsbz_ref[:, sl] = (da * (n * w) * _silu_grad(z, sg)).astype(BF16)
            dn = da * w * act
            g_nw.append(jnp.sum(da * act * n, axis=0, keepdims=True))
            dysb_ref[:, sl] = rs * (dn - n * _half_mean(dn * n, low))
        for s in range(4):
            sl = slice(128 * s, 128 * s + 128)
            rs, n, z, sg, w, og, hh = ml_saved[s]
            da = d_mix[:, 512 + 128 * s:512 + 128 * s + 128]
            act = z * sg
            dmlz_ref[:, sl] = (da * (n * w) * _silu_grad(z, sg)).astype(BF16)
            dn = da * w * act
            g_nw.append(jnp.sum(da * act * n, axis=0, keepdims=True))
            dt = rs * (dn - n * jnp.mean(dn * n, axis=1, keepdims=True))
            dmlo_ref[:, sl] = (dt * hh * (og * (1.0 - og))).astype(BF16)
            dhml_ref[:, sl] = dt * og
        upd = jnp.concatenate([
            jnp.sum(d_h1 * yn, axis=0, keepdims=True),
            jnp.sum(dgp, axis=0, keepdims=True),
            jnp.concatenate(g_nw, axis=1),
            jnp.full((1, D), loss, F32),
            jnp.zeros((4, D), F32)], axis=0)

        @pl.when(pl.program_id(0) == 0)
        def _():
            small_ref[...] = upd

        @pl.when(pl.program_id(0) > 0)
        def _():
            small_ref[...] += upd

    def rows(width, col=0):
        return pl.BlockSpec((tm, width), lambda i: (i, col))

    def whole(a):
        return pl.BlockSpec(a.shape, lambda i: (0, 0))

    return pl.pallas_call(
        body, name="tail", grid=(S // tm,),
        in_specs=[rows(512), rows(512), rows(512, 3), rows(512, 7), rows(512, 8), rows(D), rows(256), rows(D),
                  whole(sb_nw), whole(ml_nw), whole(w_out), whole(post_w), whole(w_gate), whole(b_gate), whole(w_up)],
        out_specs=[rows(D), rows(512), rows(512), rows(512), rows(512), rows(512), rows(D), rows(D), rows(D), rows(D),
                   rows(D), pl.BlockSpec((8, D), lambda i: (0, 0))],
        out_shape=[jax.ShapeDtypeStruct((S, D), F32), jax.ShapeDtypeStruct((S, 512), F32),
                   jax.ShapeDtypeStruct((S, 512), F32)] + [jax.ShapeDtypeStruct((S, 512), BF16)] * 3
        + [jax.ShapeDtypeStruct((S, D), BF16)] * 5 + [jax.ShapeDtypeStruct((8, D), F32)],
        compiler_params=_params(("arbitrary",)),
    )(y_sb, h_ml, proj, proj, proj, x, p, target, sb_nw, ml_nw, w_out, post_w, w_gate, b_gate, w_up)


def _local_step(x, p, target, pre_w, w_t, wg_t, conv_w, conv_b, gbias, sb_nw, ml_nw, post_w, b_gate, late,
                exchange=None):
    if callable(late[1]):
        proj, proj_g, u, gathered = _inproj_fwd(x, pre_w, w_t, wg_t, late[0])
        w_out, w_gate, w_up = late[1](gathered)
    else:
        proj, proj_g, u = _inproj_fwd(x, pre_w, w_t, wg_t)
        w_out, w_gate, w_up = late
    y_sb, tot = _sb_fwd(proj)
    qkv = _ml_prep(proj, conv_w, conv_b)
    h_ml, cst, nm = _ml_fwd(qkv, proj_g, gbias)
    dx_tail, d_ysb, d_hml, d_sbz, d_mlo, d_mlz, mix, d_y, h1, dgp, dpu, small = _tail(
        y_sb, h_ml, proj, x, p, target, sb_nw, ml_nw, w_out, post_w, w_gate, b_gate, w_up)
    dq, dk, dv = _sb_bwd(proj, tot, d_ysb)
    dqc, dks, dmlv, dif, gif = _ml_bwd(qkv, proj_g, gbias, cst, nm, d_hml)
    dmlqk, g_cw, g_cb = _ml_prep_bwd(proj, conv_w, conv_b, dqc, dks)
    d_main = jnp.concatenate([dq, dk, dv, d_sbz, dmlqk, dmlv, d_mlo, d_mlz], axis=1)
    d_if = dif.astype(BF16)
    grads = dict(
        w_in_t=_gate_rows_tn(d_if, u, _matmul_tn(d_main, u, "gw_in", N_IN)),
        w_out=_matmul_tn(mix, d_y, "gw_out"), w_gate=_matmul_tn(h1, dgp, "gw_gate"), w_up=_matmul_tn(p, dpu, "gw_up"),
        conv_w=g_cw, conv_b=g_cb, gif=gif)
    parts = exchange(grads) if exchange else None
    return _inproj_bwd(d_main, d_if, w_t, wg_t, x, pre_w, dx_tail, parts), grads, small


def _all_gather(a, b):
    def body(a_ref, b_ref, oa_ref, ob_ref, send_sems, recv_sems, local_sems):
        x, y, c = _place()
        me, sibling = (x, y, c), (x, y, 1 - c)
        chips = [(1 - x, y), (x, 1 - y), (1 - x, 1 - y)]
        pairs = ((a_ref, oa_ref), (b_ref, ob_ref))

        def copies(k, block, to, from_input=False):
            slot = _block_of(*block)
            return [pltpu.make_async_remote_copy(
                src_ref=src if from_input else out.at[slot], dst_ref=out.at[slot],
                send_sem=send_sems.at[t, k], recv_sem=recv_sems.at[t, k], device_id=to, device_id_type=MESH)
                for t, (src, out) in enumerate(pairs)]

        mine = [pltpu.make_async_copy(src, out.at[_block_of(*me)], local_sems.at[t])
                for t, (src, out) in enumerate(pairs)]
        for cp in mine:
            cp.start()
        first = copies(0, me, sibling, True)
        for j, chip in enumerate(chips):
            first += copies(1 + j, me, (*chip, c), True)
        for cp in first:
            cp.start()
        passed = []
        for j, chip in enumerate(chips):
            for cp in copies(1 + j, (*chip, c), me):
                cp.wait_recv()
            fwd = copies(4 + j, (*chip, c), sibling)
            for cp in fwd:
                cp.start()
            passed += fwd
        for cp in copies(0, sibling, me):
            cp.wait_recv()
        for j, chip in enumerate(chips):
            for cp in copies(4 + j, (*chip, 1 - c), me):
                cp.wait_recv()
        for cp in first + passed:
            cp.wait_send()
        for cp in mine:
            cp.wait()

    return pl.pallas_call(
        body, name="all_gather",
        in_specs=[ANY, ANY], out_specs=[ANY, ANY],
        out_shape=[jax.ShapeDtypeStruct((N_DEV,) + a.shape, a.dtype), jax.ShapeDtypeStruct((N_DEV,) + b.shape, b.dtype)],
        scratch_shapes=[pltpu.SemaphoreType.DMA((2, 7)), pltpu.SemaphoreType.DMA((2, 7)), pltpu.SemaphoreType.DMA((2,))],
    )(a, b)


def _exchange_pair(g):
    def body(g_ref, og_ref, send_sems, recv_sems):
        x, y, c = _place()
        sent = [pltpu.make_async_remote_copy(
            src_ref=g_ref.at[k, 1 - c], dst_ref=og_ref.at[k], send_sem=send_sems.at[k], recv_sem=recv_sems.at[k],
            device_id=(x, y, 1 - c), device_id_type=MESH) for k in range(4)]
        for cp in sent:
            cp.start()
        for cp in sent:
            cp.wait()

    return pl.pallas_call(
        body, name="exchange_pair", in_specs=[ANY], out_specs=ANY,
        out_shape=jax.ShapeDtypeStruct((4,) + g.shape[2:], g.dtype),
        scratch_shapes=[pltpu.SemaphoreType.DMA((4,)), pltpu.SemaphoreType.DMA((4,))],
    )(g)


def _pair_sum(g, r, core, tr):
    _, _, R, D = g.shape

    def body(c_ref, g_ref, r_ref, o_ref):
        o_ref[...] = (g_ref[...] + r_ref[...]).astype(BF16)

    return pl.pallas_call(
        body, name="pair_sum",
        grid_spec=pltpu.PrefetchScalarGridSpec(
            num_scalar_prefetch=1, grid=(4, R // tr),
            in_specs=[pl.BlockSpec((None, None, tr, D), lambda k, i, c: (k, c[0], i, 0)),
                      pl.BlockSpec((None, tr, D), lambda k, i, c: (k, i, 0))],
            out_specs=pl.BlockSpec((None, tr, D), lambda k, i, c: (k, i, 0))),
        out_shape=jax.ShapeDtypeStruct((4, R, D), BF16),
        compiler_params=_params(("arbitrary", "arbitrary")),
    )(core, g, r)


ADAM_LR, ADAM_B1, ADAM_B2, ADAM_EPS, ADAM_WD, ADAM_STEP = 0.001, 0.9, 0.999, 1e-08, 0.01, 10


def _adamw(w, g, m, v):
    m = ADAM_B1 * m + (1.0 - ADAM_B1) * g
    v = ADAM_B2 * v + (1.0 - ADAM_B2) * (g * g)
    m_hat = m / (1.0 - ADAM_B1 ** ADAM_STEP)
    v_hat = v / (1.0 - ADAM_B2 ** ADAM_STEP)
    return -ADAM_LR * (m_hat / (jnp.sqrt(v_hat) + ADAM_EPS) + ADAM_WD * w), m, v


def _adam(parts, w, m, v, tr, name, small=None):
    R, D = w.shape
    n = parts.shape[0]
    steps = R // tr

    def body(p_ref, w_ref, m_ref, v_ref, *rest):
        if small is None:
            g_ref, d_ref, nm_ref, nv_ref = rest
        else:
            s_ref, g_ref, d_ref, nm_ref, nv_ref, o_ref, send_sems, recv_sems, local_sem = rest
            i = pl.program_id(0)
            _around_grid(lambda: _copies_to_all(s_ref, o_ref, send_sems, recv_sems, local_sem), i == 0, i == steps - 1)
        g = p_ref[0].astype(F32)
        for k in range(1, n):
            g = g + p_ref[k].astype(F32)
        g_ref[...] = g
        d_ref[...], nm_ref[...], nv_ref[...] = _adamw(w_ref[...], g, m_ref[...], v_ref[...])

    blk = pl.BlockSpec((tr, D), lambda i: (i, 0))
    in_specs = [pl.BlockSpec((n, tr, D), lambda i: (0, i, 0)), blk, blk, blk]
    out_specs, out_shape = [blk] * 4, [jax.ShapeDtypeStruct((R, D), F32)] * 4
    operands, scratch = (parts, w, m, v), []
    if small is not None:
        in_specs, out_specs, operands = in_specs + [ANY], out_specs + [ANY], operands + (small,)
        out_shape = out_shape + [jax.ShapeDtypeStruct((N_DEV,) + small.shape, small.dtype)]
        scratch = [pltpu.SemaphoreType.DMA((7,)), pltpu.SemaphoreType.DMA((7,)), pltpu.SemaphoreType.DMA]
    return pl.pallas_call(
        body, name=name, grid=(steps,), in_specs=in_specs, out_specs=out_specs, out_shape=out_shape,
        scratch_shapes=scratch, compiler_params=_params(("arbitrary",)),
    )(*operands)


ROWS_IN = 592
ROWS_BF16 = ROWS_IN + 128 + 128 + 32
ROWS_CONV = 16
ROWS_ALL = ROWS_BF16 + ROWS_CONV
ROW_TILE = 224


def _pad_rows(a, rows):
    return jnp.pad(a, ((0, rows - a.shape[0]), (0, 0)))


def _pack_shards(w_in, w_out, w_gate, w_up, conv_w):
    return jnp.concatenate([
        _pad_rows(w_in[0].T, ROWS_IN), w_out[0], w_gate[0], w_up[0].reshape(32, D_MODEL),
        _pad_rows(jnp.pad(conv_w[0].reshape(1, 512), ((0, 0), (0, 512))), ROWS_CONV)], axis=0)


def _unpack_shards(a):
    return (a[:SHARD_IN].T[None], a[ROWS_IN:ROWS_IN + 128][None], a[ROWS_IN + 128:ROWS_IN + 256][None],
            a[ROWS_IN + 256:ROWS_BF16].reshape(1, 256, 128), a[ROWS_BF16, :512].reshape(1, 4, 128))


def _adam_small(parts, ws, ms, vs):
    n = len(ws)

    def body(p_ref, *refs):
        ins, outs = refs[:3 * n], refs[3 * n:]
        g = p_ref[0]
        for k in range(1, N_DEV):
            g = g + p_ref[k]
        pieces = [g[4:5], g[5:6], g[6:7, 0:4], g[6:7, 4:8], g[2:3, :512], g[2:3, 512:], g[0:1], g[1:2]]
        for t, gt in enumerate(pieces):
            d, nm, nv = _adamw(ins[t][...], gt, ins[n + t][...], ins[2 * n + t][...])
            for o, val in zip(outs[4 * t:4 * t + 4], (gt, d, nm, nv)):
                o[...] = val
        outs[4 * n][...] = g[3:4, 0:1]

    shapes = [jax.ShapeDtypeStruct(w.shape, F32) for w in ws for _ in range(4)] + [jax.ShapeDtypeStruct((1, 1), F32)]
    res = pl.pallas_call(body, name="adam_small", out_shape=shapes)(parts, *ws, *ms, *vs)
    return [res[k:4 * n:4] for k in range(4)], res[4 * n]


def kernel(x, p, pre_norm_w, w_in, ml_conv_w, ml_conv_b, ml_i_bias, ml_f_bias, sb_norm_w, ml_norm_w, w_out, post_norm_w, ple_w_up, ple_w_gate, ple_b_gate, loss_target, m_pre_norm_w, m_w_in, m_ml_conv_w, m_ml_conv_b, m_ml_i_bias, m_ml_f_bias, m_sb_norm_w, m_ml_norm_w, m_w_out, m_post_norm_w, m_ple_w_up, m_ple_w_gate, m_ple_b_gate, v_pre_norm_w, v_w_in, v_ml_conv_w, v_ml_conv_b, v_ml_i_bias, v_ml_f_bias, v_sb_norm_w, v_ml_norm_w, v_w_out, v_post_norm_w, v_ple_w_up, v_ple_w_gate, v_ple_b_gate):
    D = D_MODEL
    w_pk = _pack_shards(w_in, w_out, ple_w_gate, ple_w_up, ml_conv_w)
    m_pk = _pack_shards(m_w_in, m_w_out, m_ple_w_gate, m_ple_w_up, m_ml_conv_w)
    v_pk = _pack_shards(v_w_in, v_w_out, v_ple_w_gate, v_ple_w_up, v_ml_conv_w)
    w_sm = [pre_norm_w, ml_conv_b, ml_i_bias, ml_f_bias, sb_norm_w, ml_norm_w, post_norm_w, ple_b_gate]
    m_sm = [m_pre_norm_w, m_ml_conv_b, m_ml_i_bias, m_ml_f_bias, m_sb_norm_w, m_ml_norm_w, m_post_norm_w, m_ple_b_gate]
    v_sm = [v_pre_norm_w, v_ml_conv_b, v_ml_i_bias, v_ml_f_bias, v_sb_norm_w, v_ml_norm_w, v_post_norm_w, v_ple_b_gate]

    w_bf = w_pk[:ROWS_BF16].astype(BF16)
    ga, gb = _all_gather(w_bf[:SHARD_IN], w_pk[ROWS_BF16:])
    w_in_t = ga.reshape(N_IN, D)
    wg_t = _pad_rows(w_in_t[N_MAIN:], 128)
    conv_w_f = gb[:, 0, :512].reshape(N_DEV, 4, 128).transpose(1, 0, 2).reshape(4, D)
    gbias = jnp.pad(jnp.concatenate([ml_i_bias, ml_f_bias], axis=1), ((0, 0), (0, 120)))

    def unpack_late(gl):
        return (gl[:, :128].reshape(D, D), gl[:, 128:256].reshape(D, D),
                gl[:, 256:].reshape(N_DEV, 256, 128).transpose(1, 0, 2).reshape(256, D))

    def exchange(g):
        g_in = g["w_in_t"].reshape(N_DEV, SHARD_IN, D)
        g_blocks = jnp.concatenate([
            jnp.pad(g_in, ((0, 0), (0, ROWS_IN - SHARD_IN), (0, 0))),
            g["w_out"].reshape(N_DEV, 128, D), g["w_gate"].reshape(N_DEV, 128, D),
            g["w_up"].reshape(256, N_DEV, 128).transpose(1, 0, 2).reshape(N_DEV, 32, D),
            jnp.pad(g["conv_w"].reshape(4, N_DEV, 128).transpose(1, 0, 2).reshape(N_DEV, 1, 512),
                    ((0, 0), (0, ROWS_CONV - 1), (0, 512))),
        ], axis=1).reshape(4, 2, ROWS_ALL, D)
        core = lax.axis_index("c").astype(jnp.int32).reshape(1)
        return _pair_sum(g_blocks, _exchange_pair(g_blocks), core, ROW_TILE)

    (dx, g_pre, parts), g, small = _local_step(
        x[0], p[0, 0], loss_target[0], pre_norm_w, w_in_t, wg_t, conv_w_f, ml_conv_b, gbias, sb_norm_w, ml_norm_w,
        post_norm_w, ple_b_gate, (w_bf[ROWS_IN:], unpack_late), exchange)

    g_small = jnp.concatenate([small[0:4], g_pre, g["conv_b"], jnp.pad(g["gif"][0:1], ((0, 0), (0, D - 128))),
                               jnp.zeros((1, D), F32)], axis=0)
    grad_pk, delta_pk, nm_pk, nv_pk, parts_sm = _adam(parts, w_pk, m_pk, v_pk, ROW_TILE, "adam", g_small)
    (grad_sm, delta_sm, nm_sm, nv_sm), loss = _adam_small(parts_sm, w_sm, m_sm, v_sm)

    def ordered(pk, sm):
        win, wout, wgate, wup, convw = _unpack_shards(pk)
        pre_w, conv_b, i_b, f_b, sb_nw, ml_nw, post_w, b_gate = sm
        return [pre_w, win, convw, conv_b, i_b, f_b, sb_nw, ml_nw, wout, post_w, wup, wgate, b_gate]

    return (loss[0, 0], dx[None], *ordered(grad_pk, grad_sm), *ordered(delta_pk, delta_sm), *ordered(nm_pk, nm_sm),
            *ordered(nv_pk, nv_sm))
```

```python
import jax
import jax.numpy as jnp
from jax import lax
from jax.experimental import pallas as pl
from jax.experimental.pallas import tpu as pltpu

F32 = jnp.float32
BF16 = jnp.bfloat16
EPS = 1e-6
D_MODEL = 1024
SB_W = 512
ML_W = 512
N_MAIN = 4608
N_IN = 4616
SHARD_IN = 577
TQ = 1024
TK = 256
ND = TQ // TK
LCH = 128
VMEM_LIMIT = 56 * 1024 * 1024


def _dot(a, b):
    return jnp.dot(a, b, preferred_element_type=F32)


def _dot_nt(a, b):
    return lax.dot_general(a, b, (((1,), (1,)), ((), ())), preferred_element_type=F32)


def _dot_tn(a, b):
    return lax.dot_general(a, b, (((0,), (0,)), ((), ())), preferred_element_type=F32)


def _split2(x):
    hi = x.astype(BF16)
    lo = (x - hi.astype(F32)).astype(BF16)
    return hi, lo


def _split3(x):
    hi = x.astype(BF16)
    r = x - hi.astype(F32)
    mid = r.astype(BF16)
    lo = (r - mid.astype(F32)).astype(BF16)
    return hi, mid, lo


def _params(sem):
    return pltpu.CompilerParams(dimension_semantics=sem, vmem_limit_bytes=VMEM_LIMIT)


def _log_sigmoid_parts(z):
    e = jnp.exp(-jnp.abs(z))
    return jnp.minimum(z, 0.0) - jnp.log(1.0 + e)


def _neg_log_sigmoid(nz):
    nz = jnp.minimum(nz, 80.0)
    sp = jnp.log(1.0 + jnp.exp(nz))
    return sp, nz - sp


def _sb_fwd(proj):
    S = proj.shape[0]
    nq = S // TQ

    def body(q_ref, k_ref, v_ref, y_ref, t_ref, acc_ref, car_ref, zs_ref):
        i = pl.program_id(1)
        low = lax.broadcasted_iota(jnp.int32, (TQ, 128), 1) < 64
        row = lax.broadcasted_iota(jnp.int32, (TK, TK), 0)
        col = lax.broadcasted_iota(jnp.int32, (TK, TK), 1)
        uo = (row > col).astype(BF16)
        uo = jnp.concatenate([uo, uo], axis=0)
        q = q_ref[...] * 0.125
        qh = (jnp.where(low, -q, 0.0).astype(BF16), jnp.where(low, 0.0, -q).astype(BF16))
        acc_ref[...] = jnp.zeros_like(acc_ref)
        car_ref[...] = jnp.zeros_like(car_ref)

        def block(j, r0):
            diag = r0 is not None
            r0 = r0 or 0
            if diag:
                strict = (lax.broadcasted_iota(jnp.int32, (TQ - r0, TK), 1)
                          < lax.broadcasted_iota(jnp.int32, (TQ - r0, TK), 0))
            rows = pl.ds(pl.multiple_of(j * TK, TK), TK)
            kb = k_ref[rows, :].astype(BF16)
            vb = v_ref[rows, :].astype(BF16)
            if not diag:
                kn = k_ref[pl.ds(pl.multiple_of(jnp.maximum(j - 1, 0) * TK, TK), TK), :].astype(BF16)

            def first(h):
                if diag:
                    nz = _dot_nt(qh[h][r0:], kb)
                else:
                    nz = zs_ref[h]
                    zs_ref[h] = _dot_nt(qh[h], kn)
                sp, lk = _neg_log_sigmoid(nz)
                if diag:
                    lk = jnp.where(strict, lk, 0.0)
                hi, lo = _split2(lk)
                return sp, lk[:, 0:1], _dot(jnp.concatenate([hi, lo], axis=1), uo)

            def second(h, sp, lk0, rr):
                car = car_ref[h, r0:, :]
                a = jnp.exp((jnp.concatenate([car, car], axis=1) + rr) - sp)
                if diag:
                    a = jnp.where(strict, a, 0.0)
                acc_ref[h, r0:, :] += _dot(a.astype(BF16), vb)
                car_ref[h, r0:, :] = car + jnp.broadcast_to(rr[:, 0:1] + lk0, car.shape)

            if diag:
                halves = [first(h) for h in range(2)]
                for h in range(2):
                    second(h, *halves[h])
            else:
                for h in range(2):
                    second(h, *first(h))

        for d in reversed(range(ND)):
            block(ND * i + d, TK * d)

        @pl.when(i > 0)
        def _():
            k0 = k_ref[pl.ds(pl.multiple_of((ND * i - 1) * TK, TK), TK), :].astype(BF16)
            for h in range(2):
                zs_ref[h] = _dot_nt(qh[h], k0)

        def loop(n, c):
            block(ND * i - 1 - 2 * n, None)
            block(ND * i - 2 - 2 * n, None)
            return c

        lax.fori_loop(0, (ND // 2) * i, loop, 0)
        y_ref[...] = jnp.where(low, acc_ref[0], acc_ref[1])
        t_ref[...] = jnp.where(low, car_ref[0], car_ref[1])

    return pl.pallas_call(
        body, name="sb_fwd", grid=(4, nq),
        in_specs=[pl.BlockSpec((TQ, 128), lambda p, i: (i, p)),
                  pl.BlockSpec((S, 128), lambda p, i: (0, 4 + p)),
                  pl.BlockSpec((S, 128), lambda p, i: (0, 8 + p))],
        out_specs=[pl.BlockSpec((TQ, 128), lambda p, i: (i, p)),
                   pl.BlockSpec((TQ, 128), lambda p, i: (i, p))],
        out_shape=[jax.ShapeDtypeStruct((S, SB_W), F32), jax.ShapeDtypeStruct((S, SB_W), F32)],
        scratch_shapes=[pltpu.VMEM((2, TQ, 128), F32), pltpu.VMEM((2, TQ, 128), F32), pltpu.VMEM((2, TQ, TK), F32)],
        compiler_params=_params(("arbitrary", "arbitrary")),
    )(proj, proj, proj)


def _sb_bwd(proj, tot, dy):
    S = proj.shape[0]
    nq = S // TQ

    def body(q_ref, k_ref, v_ref, t_ref, dy_ref, dq_ref, dk_ref, dv_ref, dqa_ref, cp_ref, cg_ref, dkt_ref, dvt_ref):
        i = pl.program_id(1)
        low = lax.broadcasted_iota(jnp.int32, (TQ, 128), 1) < 64
        row = lax.broadcasted_iota(jnp.int32, (TK, TK), 0)
        col = lax.broadcasted_iota(jnp.int32, (TK, TK), 1)
        u_inc = (row <= col).astype(BF16)
        u_inc = jnp.concatenate([u_inc, u_inc], axis=0)
        u_exc = (row < col).astype(BF16)
        q = q_ref[...] * 0.125
        qh = (jnp.where(low, -q, 0.0).astype(BF16), jnp.where(low, 0.0, -q).astype(BF16))
        dy_ = dy_ref[...]
        dyh = (jnp.where(low, dy_, 0.0).astype(BF16), jnp.where(low, 0.0, dy_).astype(BF16))
        qt = tuple(jnp.transpose(x) for x in qh)
        dyt = tuple(jnp.transpose(x) for x in dyh)
        t_ = t_ref[...]
        t_sw = pltpu.roll(t_, 64, 1)
        th = (jnp.where(low, t_, t_sw), jnp.where(low, t_sw, t_))
        dqa_ref[...] = jnp.zeros_like(dqa_ref)
        cp_ref[...] = jnp.zeros_like(cp_ref)
        cg_ref[...] = jnp.zeros_like(cg_ref)

        @pl.when(i == 0)
        def _():
            dkt_ref[...] = jnp.zeros_like(dkt_ref)
            dvt_ref[...] = jnp.zeros_like(dvt_ref)

        def block(j, r0):
            diag = r0 is not None
            r0 = r0 or 0
            if diag:
                strict = (lax.broadcasted_iota(jnp.int32, (TQ - r0, TK), 1)
                          < lax.broadcasted_iota(jnp.int32, (TQ - r0, TK), 0))
            rows = pl.ds(pl.multiple_of(j * TK, TK), TK)
            kb = k_ref[rows, :].astype(BF16)
            vb = v_ref[rows, :].astype(BF16)
            dk_acc = jnp.zeros((128, TK), F32)
            dv_acc = jnp.zeros((128, TK), F32)
            for h in range(2):
                qr, dyr = qh[h][r0:], dyh[h][r0:]
                sp, lk = _neg_log_sigmoid(_dot_nt(qr, kb))
                if diag:
                    lk = jnp.where(strict, lk, 0.0)
                hi, lo = _split2(lk)
                pp = _dot(jnp.concatenate([hi, lo], axis=1), u_inc)
                cp, cg = cp_ref[h, r0:, :], cg_ref[h, r0:, :]
                rest = th[h][r0:] - cp
                a = jnp.exp((jnp.concatenate([rest, rest], axis=1) - pp) - sp)
                if diag:
                    a = jnp.where(strict, a, 0.0)
                g = _dot_nt(dyr, vb) * a
                gg = _dot(g.astype(BF16), u_exc)
                beta = jnp.exp(-sp)
                dz = g - beta * (g + (jnp.concatenate([cg, cg], axis=1) + gg))
                if diag:
                    dz = jnp.where(strict, dz, 0.0)
                dzb = dz.astype(BF16)
                dqa_ref[h, r0:, :] += _dot(dzb, kb)
                dk_acc += _dot(qt[h][:, r0:], dzb)
                dv_acc += _dot(dyt[h][:, r0:], a.astype(BF16))
                cp_ref[h, r0:, :] = cp + jnp.broadcast_to(pp[:, TK - 1:TK], cp.shape)
                cg_ref[h, r0:, :] = cg + jnp.broadcast_to(gg[:, TK - 1:TK] + g[:, TK - 1:TK], cg.shape)
            dkt_ref[:, rows] -= dk_acc
            dvt_ref[:, rows] += dv_acc

        def loop(j, c):
            block(j, None)
            return c

        lax.fori_loop(0, ND * i, loop, 0)
        for d in range(ND):
            block(ND * i + d, TK * d)
        dq_ref[...] = (jnp.where(low, dqa_ref[0], dqa_ref[1]) * 0.125).astype(BF16)

        @pl.when(i == nq - 1)
        def _():
            dk_ref[...] = jnp.transpose(dkt_ref[...]).astype(BF16)
            dv_ref[...] = jnp.transpose(dvt_ref[...]).astype(BF16)

    return pl.pallas_call(
        body, name="sb_bwd", grid=(4, nq),
        in_specs=[pl.BlockSpec((TQ, 128), lambda p, i: (i, p)),
                  pl.BlockSpec((S, 128), lambda p, i: (0, 4 + p)),
                  pl.BlockSpec((S, 128), lambda p, i: (0, 8 + p)),
                  pl.BlockSpec((TQ, 128), lambda p, i: (i, p)),
                  pl.BlockSpec((TQ, 128), lambda p, i: (i, p))],
        out_specs=[pl.BlockSpec((TQ, 128), lambda p, i: (i, p)),
                   pl.BlockSpec((S, 128), lambda p, i: (0, p)),
                   pl.BlockSpec((S, 128), lambda p, i: (0, p))],
        out_shape=[jax.ShapeDtypeStruct((S, SB_W), BF16)] * 3,
        scratch_shapes=[pltpu.VMEM((2, TQ, 128), F32)] * 3 + [pltpu.VMEM((128, S), F32)] * 2,
        compiler_params=_params(("arbitrary", "arbitrary")),
    )(proj, proj, proj, tot, dy)


ML_SCALE = 128 ** -0.5
RC = 256


def _conv_taps(cur, prev8, w):
    n = cur.shape[0]
    win = jnp.concatenate([prev8, cur], axis=0)
    out = w[3:4, :] * cur
    for j in range(3):
        out = out + w[j:j + 1, :] * pltpu.roll(win, 3 - j, 0)[8:8 + n]
    return out


def _ml_prep(proj, conv_w, conv_b):
    S = proj.shape[0]

    def body(x_ref, w_ref, b_ref, o_ref):
        c = pl.program_id(0)
        scale = jnp.where(c < 4, 1.0, ML_SCALE).astype(F32)
        w = w_ref[...]
        b = b_ref[...]

        @pl.when(c < 8)
        def _():
            for n in range(S // RC):
                cur = x_ref[n * RC:(n + 1) * RC, :]
                prev8 = x_ref[n * RC - 8:n * RC, :] if n else jnp.zeros((8, 128), F32)
                pre = b + _conv_taps(cur, prev8, w)
                o_ref[n * RC:(n + 1) * RC, :] = (pre * jax.nn.sigmoid(pre) * scale).astype(BF16)

        @pl.when(c >= 8)
        def _():
            o_ref[...] = x_ref[...].astype(BF16)

    return pl.pallas_call(
        body, name="ml_prep", grid=(12,),
        in_specs=[pl.BlockSpec((S, 128), lambda c: (0, 16 + c)),
                  pl.BlockSpec((4, 128), lambda c: (0, jnp.minimum(c, 7))),
                  pl.BlockSpec((1, 128), lambda c: (0, jnp.minimum(c, 7)))],
        out_specs=pl.BlockSpec((S, 128), lambda c: (0, c)),
        out_shape=jax.ShapeDtypeStruct((S, 1536), BF16),
        compiler_params=_params(("arbitrary",)),
    )(proj, conv_w, conv_b)


def _ml_prep_bwd(proj, conv_w, conv_b, dq, dk):
    S = proj.shape[0]

    def body(x_ref, w_ref, b_ref, dq_ref, dk_ref, dx_ref, gw_ref, gb_ref, dp_ref):
        c = pl.program_id(0)
        w = w_ref[...]
        b = b_ref[...]
        gw = [jnp.zeros((1, 128), F32) for _ in range(4)]
        gb = jnp.zeros((1, 128), F32)
        for n in range(S // RC):
            rows = slice(n * RC, (n + 1) * RC)
            cur = x_ref[rows, :]
            prev8 = x_ref[n * RC - 8:n * RC, :] if n else jnp.zeros((8, 128), F32)
            pre = b + _conv_taps(cur, prev8, w)
            s = jax.nn.sigmoid(pre)
            dpost = jnp.where(c < 4, dq_ref[rows, :].astype(F32), dk_ref[rows, :].astype(F32) * ML_SCALE)
            dpre = dpost * (s * (1.0 + pre * (1.0 - s)))
            dp_ref[rows, :] = dpre
            win = jnp.concatenate([prev8, cur], axis=0)
            gb = gb + jnp.sum(dpre, axis=0, keepdims=True)
            gw[3] = gw[3] + jnp.sum(dpre * cur, axis=0, keepdims=True)
            for j in range(3):
                gw[j] = gw[j] + jnp.sum(dpre * pltpu.roll(win, 3 - j, 0)[8:8 + RC], axis=0, keepdims=True)
        dp_ref[S:S + 8, :] = jnp.zeros((8, 128), F32)
        for n in range(S // RC):
            win = dp_ref[n * RC:(n + 1) * RC + 8, :]
            dx = w[3:4, :] * win[:RC]
            for j in range(3):
                dx = dx + w[j:j + 1, :] * pltpu.roll(win, RC + 8 - (3 - j), 0)[:RC]
            dx_ref[n * RC:(n + 1) * RC, :] = dx.astype(BF16)
        gw_ref[...] = jnp.concatenate(gw, axis=0)
        gb_ref[...] = gb

    return pl.pallas_call(
        body, name="ml_prep_bwd", grid=(8,),
        in_specs=[pl.BlockSpec((S, 128), lambda c: (0, 16 + c)),
                  pl.BlockSpec((4, 128), lambda c: (0, c)),
                  pl.BlockSpec((1, 128), lambda c: (0, c)),
                  pl.BlockSpec((S, 128), lambda c: (0, jnp.minimum(c, 3))),
                  pl.BlockSpec((S, 128), lambda c: (0, jnp.maximum(c - 4, 0)))],
        out_specs=[pl.BlockSpec((S, 128), lambda c: (0, c)),
                   pl.BlockSpec((4, 128), lambda c: (0, c)),
                   pl.BlockSpec((1, 128), lambda c: (0, c))],
        out_shape=[jax.ShapeDtypeStruct((S, 1024), BF16), jax.ShapeDtypeStruct((4, 1024), F32),
                   jax.ShapeDtypeStruct((1, 1024), F32)],
        scratch_shapes=[pltpu.VMEM((S + 8, 128), F32)],
        compiler_params=_params(("arbitrary",)),
    )(proj, conv_w, conv_b, dq, dk)


NH = 4


def _heads(x):
    return jnp.stack([x[:, 128 * h:128 * (h + 1)] for h in range(NH)])


def _unheads(x):
    return jnp.concatenate([x[h] for h in range(NH)], axis=1)


def _per_head(f, *xs):
    return jnp.stack([f(*[x[h] for x in xs]) for h in range(NH)])


def _ml_gates(g):
    lg = _log_sigmoid_parts(g)
    ig = jnp.stack([jnp.broadcast_to(g[:, h:h + 1], (LCH, 128)) for h in range(NH)])
    lf = jnp.stack([jnp.broadcast_to(lg[:, 4 + h:5 + h], (LCH, 128)) for h in range(NH)])
    return ig, lf


def _tri_sum(tri, x, dot):
    return _per_head(lambda a, b, c: dot(tri, a) + dot(tri, b) + dot(tri, c), *_split3(x))


def _ml_chunk_fwd(q, k, v, ig, lf, ct, n_st, m_st, tri, causal):
    vf = v.astype(F32)
    b = _tri_sum(tri, lf, _dot)
    b_last = b[:, LCH - 1:LCH, :]
    g = b_last - b + ig
    m_loc = jnp.max(g, axis=1, keepdims=True)
    w = jnp.exp(g - m_loc)
    vwf = vf * w
    vw = vwf.astype(BF16)
    ct_loc = _per_head(_dot_tn, k, vw)
    kf = k.astype(F32)
    n_loc = jnp.sum(w * kf, axis=1, keepdims=True)
    r = _per_head(jnp.transpose, ig - b)
    d_log = jnp.where(causal, b + r, -jnp.inf)
    m_t = jnp.maximum(b + m_st, jnp.max(d_log, axis=2, keepdims=True))
    w_in = jnp.exp(d_log - m_t)
    qk = _per_head(_dot_nt, q, k)
    scores = qk * w_in
    cs = jnp.exp(b + m_st - m_t)
    ctb = ct.astype(BF16)
    qc = _per_head(_dot, q, ctb)
    qf = q.astype(F32)
    qn = jnp.sum(qf * n_st, axis=2, keepdims=True)
    num = _per_head(_dot, scores.astype(BF16), v) + cs * qc
    den = jnp.sum(scores, axis=2, keepdims=True) + cs * qn
    em = jnp.exp(-m_t)
    dd = jnp.maximum(jnp.abs(den), em)
    h = num / dd
    m_new = jnp.maximum(b_last + m_st, m_loc)
    a = jnp.exp(b_last + m_st - m_new)
    gg = jnp.exp(m_loc - m_new)
    return dict(vwf=vwf, vw=vw, kf=kf, qf=qf, ct_loc=ct_loc, n_loc=n_loc, w=w, w_in=w_in, qk=qk, scores=scores,
                cs=cs, ctb=ctb, qc=qc, qn=qn, den=den, em=em, dd=dd, h=h, m_new=m_new, a=a, gg=gg)


def _ml_consts():
    row = lax.broadcasted_iota(jnp.int32, (LCH, LCH), 0)
    col = lax.broadcasted_iota(jnp.int32, (LCH, LCH), 1)
    return row, (col <= row), (col <= row).astype(BF16)


def _ml_rows(qkv_ref, g_ref, gb_ref, c):
    rows = pl.ds(pl.multiple_of(c * LCH, LCH), LCH)
    q, k, v = (_heads(qkv_ref[rows, 512 * t:512 * (t + 1)]) for t in range(3))
    ig, lf = _ml_gates(g_ref[rows, :] + gb_ref[...])
    return rows, q, k, v, ig, lf


VMEM_SPEC = pl.BlockSpec(memory_space=pltpu.VMEM)


def _ml_fwd(qkv, proj_g, gbias):
    S = qkv.shape[0]
    nc = S // LCH

    def body(qkv_ref, g_ref, gb_ref, h_ref, cst_ref, nm_ref, ct_ref, n_ref, m_ref):
        _, causal, tri = _ml_consts()
        ct_ref[...] = jnp.zeros_like(ct_ref)
        n_ref[...] = jnp.zeros_like(n_ref)
        m_ref[...] = jnp.zeros_like(m_ref)

        def chunk(c, carry):
            rows, q, k, v, ig, lf = _ml_rows(qkv_ref, g_ref, gb_ref, c)
            ct, n_st, m_st = ct_ref[...], n_ref[:, 0:1, :], m_ref[:, 0:1, :]
            cst_ref[c] = ct
            nm_ref[c, :, 0:8, :] = n_ref[...]
            nm_ref[c, :, 8:16, :] = m_ref[...]
            f = _ml_chunk_fwd(q, k, v, ig, lf, ct, n_st, m_st, tri, causal)
            h_ref[rows, :] = _unheads(f["h"])
            ct_ref[...] = f["a"] * ct + f["gg"] * f["ct_loc"]
            n_ref[...] = jnp.broadcast_to(f["a"] * n_st + f["gg"] * f["n_loc"], (NH, 8, 128))
            m_ref[...] = jnp.broadcast_to(f["m_new"], (NH, 8, 128))
            return carry

        lax.fori_loop(0, nc, chunk, 0)

    return pl.pallas_call(
        body, name="ml_fwd", in_specs=[VMEM_SPEC] * 3, out_specs=[VMEM_SPEC] * 3,
        out_shape=[jax.ShapeDtypeStruct((S, ML_W), F32), jax.ShapeDtypeStruct((nc, NH, 128, 128), F32),
                   jax.ShapeDtypeStruct((nc, NH, 16, 128), F32)],
        scratch_shapes=[pltpu.VMEM((NH, 128, 128), F32), pltpu.VMEM((NH, 8, 128), F32), pltpu.VMEM((NH, 8, 128), F32)],
        compiler_params=pltpu.CompilerParams(vmem_limit_bytes=VMEM_LIMIT),
    )(qkv, proj_g, gbias)


def _ml_bwd(qkv, proj_g, gbias, cst, nm, dh):
    S = qkv.shape[0]
    nc = S // LCH

    def body(qkv_ref, g_ref, gb_ref, cst_ref, nm_ref, dh_ref, dq_ref, dk_ref, dv_ref, dif_ref, gsum_ref,
             dct_ref, dn_ref, gi_ref):
        row, causal, tri = _ml_consts()
        lane = lax.broadcasted_iota(jnp.int32, (LCH, 128), 1)
        last_row = row == LCH - 1
        dct_ref[...] = jnp.zeros_like(dct_ref)
        dn_ref[...] = jnp.zeros_like(dn_ref)
        gi_ref[...] = jnp.zeros_like(gi_ref)

        def chunk(t, carry):
            c = nc - 1 - t
            rows, q, k, v, ig, lf = _ml_rows(qkv_ref, g_ref, gb_ref, c)
            ct, n_st, m_st = cst_ref[c], nm_ref[c, :, 0:1, :], nm_ref[c, :, 8:9, :]
            f = _ml_chunk_fwd(q, k, v, ig, lf, ct, n_st, m_st, tri, causal)
            dh_ = _heads(dh_ref[rows, :])
            dct_new, dn_new = dct_ref[...], dn_ref[:, 0:1, :]
            e_num = dh_ / f["dd"]
            hdh = jnp.sum(f["h"] * dh_, axis=2, keepdims=True)
            free = jnp.abs(f["den"]) > f["em"]
            e_den = jnp.where(free, -hdh / f["dd"] * jnp.sign(f["den"]), 0.0)
            e_num_b = e_num.astype(BF16)
            ds_ = _per_head(_dot_nt, e_num_b, v) + e_den
            dqk = ds_ * f["w_in"]
            gam = dqk * f["qk"]
            dqk_b = dqk.astype(BF16)
            cse = f["cs"] * e_den
            dq = _per_head(_dot, dqk_b, k) + f["cs"] * _per_head(_dot_nt, e_num_b, f["ctb"]) + cse * n_st
            dk = _per_head(_dot_tn, dqk_b, q)
            dv = _per_head(_dot_tn, f["scores"].astype(BF16), e_num_b)
            dcl = (f["gg"] * dct_new).astype(BF16)
            dnl = f["gg"] * dn_new
            kd = _per_head(_dot, k, dcl)
            dv = dv + f["w"] * kd
            dk = dk + _per_head(_dot_nt, f["vw"], dcl) + f["w"] * dnl
            gam_s = (jnp.sum(kd * f["vwf"], axis=2, keepdims=True)
                     + f["w"] * jnp.sum(f["kf"] * dnl, axis=2, keepdims=True))
            col_g = jnp.sum(_per_head(jnp.transpose, gam), axis=2, keepdims=True) + gam_s
            db = (jnp.sum(gam, axis=2, keepdims=True) + jnp.sum(e_num * (f["cs"] * f["qc"]), axis=2, keepdims=True)
                  + cse * f["qn"] - col_g)
            state = jnp.sum(jnp.sum(dct_new * ct, axis=2, keepdims=True), axis=1, keepdims=True)
            state = state + jnp.sum(dn_new * n_st, axis=2, keepdims=True)
            db_last = jnp.sum(gam_s[:, :, 0:1], axis=1, keepdims=True) + f["a"][:, :, 0:1] * state
            db = jnp.where(last_row, db + db_last, db)
            dlf = _tri_sum(tri, db, _dot_tn)
            df = dlf * (1.0 - jnp.exp(lf))
            dq_ref[rows, :] = _unheads(dq).astype(BF16)
            dk_ref[rows, :] = _unheads(dk).astype(BF16)
            dv_ref[rows, :] = _unheads(dv).astype(BF16)
            dif = jnp.zeros((LCH, 128), F32)
            for h in range(NH):
                dif = dif + jnp.where(lane == h, col_g[h], 0.0) + jnp.where(lane == h + 4, df[h], 0.0)
            dif_ref[rows, :] = dif
            clamped = jnp.where(free, 0.0, hdh)
            gi_ref[...] += jnp.broadcast_to(jnp.sum(clamped, axis=1, keepdims=True), (NH, 8, 128))
            dct_ref[...] = f["a"] * dct_new + _per_head(_dot_tn, q, (f["cs"] * e_num).astype(BF16))
            dn_ref[...] = jnp.broadcast_to(f["a"] * dn_new + jnp.sum(cse * f["qf"], axis=1, keepdims=True), (NH, 8, 128))
            return carry

        lax.fori_loop(0, nc, chunk, 0)
        lane8 = lax.broadcasted_iota(jnp.int32, (8, 128), 1)
        gsum = jnp.where(lane8 >= 4, jnp.sum(dif_ref[...], axis=0, keepdims=True), 0.0)
        for h in range(NH):
            gsum = gsum + jnp.where(lane8 == h, gi_ref[h], 0.0)
        gsum_ref[...] = gsum

    return pl.pallas_call(
        body, name="ml_bwd", in_specs=[VMEM_SPEC] * 6, out_specs=[VMEM_SPEC] * 5,
        out_shape=[jax.ShapeDtypeStruct((S, ML_W), BF16)] * 3 + [jax.ShapeDtypeStruct((S, 128), F32),
                                                                  jax.ShapeDtypeStruct((8, 128), F32)],
        scratch_shapes=[pltpu.VMEM((NH, 128, 128), F32), pltpu.VMEM((NH, 8, 128), F32), pltpu.VMEM((NH, 8, 128), F32)],
        compiler_params=pltpu.CompilerParams(vmem_limit_bytes=VMEM_LIMIT),
    )(qkv, proj_g, gbias, cst, nm, dh)


MESH = pl.DeviceIdType.MESH
ANY = pl.BlockSpec(memory_space=pl.ANY)
N_DEV = 8


def _place():
    return lax.axis_index("x"), lax.axis_index("y"), lax.axis_index("c")


def _block_of(px, py, pc):
    return 4 * px + 2 * py + pc


def _copies_to_all(b_ref, o_ref, send_sems, recv_sems, local_sem):
    x, y, c = _place()
    mine = _block_of(x, y, c)
    copies = [pltpu.make_async_copy(b_ref, o_ref.at[mine], local_sem)]
    for k in range(1, N_DEV):
        peer = (x ^ (k >> 2), y ^ ((k >> 1) & 1), c ^ (k & 1))
        copies.append(pltpu.make_async_remote_copy(
            src_ref=b_ref, dst_ref=o_ref.at[mine], send_sem=send_sems.at[k - 1], recv_sem=recv_sems.at[k - 1],
            device_id=peer, device_id_type=MESH))
    return copies


def _copies_between_chips(p_ref, o_ref, send_sems, recv_sems, local_sem):
    x, y, c = _place()
    mine = 2 * x + y
    copies = [pltpu.make_async_copy(p_ref.at[mine], o_ref.at[mine], local_sem)]
    for k in range(1, 4):
        px, py = x ^ (k >> 1), y ^ (k & 1)
        copies.append(pltpu.make_async_remote_copy(
            src_ref=p_ref.at[2 * px + py], dst_ref=o_ref.at[mine], send_sem=send_sems.at[k - 1],
            recv_sem=recv_sems.at[k - 1], device_id=(px, py, c), device_id_type=MESH))
    return copies


def _around_grid(copies, first, last):
    @pl.when(first)
    def _():
        for cp in copies():
            cp.start()

    @pl.when(last)
    def _():
        for cp in copies():
            cp.wait()


def _inproj_fwd(x, pre_w, w_t, wg_t, blk=None):
    S, D = x.shape
    tm, tn = min(S, 1024), 1152
    ni, nj = S // tm, N_MAIN // tn

    def body(x_ref, pw_ref, w_ref, wg_ref, *rest):
        if blk is None:
            proj_ref, g_ref, u_ref = rest
        else:
            b_ref, proj_ref, g_ref, u_ref, o_ref, send_sems, recv_sems, local_sem = rest
            i, j = pl.program_id(0), pl.program_id(1)
            _around_grid(lambda: _copies_to_all(b_ref, o_ref, send_sems, recv_sems, local_sem),
                         (i == 0) & (j == 0), (i == ni - 1) & (j == nj - 1))

        @pl.when(pl.program_id(1) == 0)
        def _():
            xf = x_ref[...]
            r = lax.rsqrt(jnp.mean(xf * xf, axis=-1, keepdims=True) + EPS)
            u = (xf * r * pw_ref[...]).astype(BF16)
            u_ref[...] = u
            g_ref[...] = _dot_nt(u, wg_ref[...])

        proj_ref[...] = _dot_nt(u_ref[...], w_ref[...])

    in_specs = [pl.BlockSpec((tm, D), lambda i, j: (i, 0)),
                pl.BlockSpec((1, D), lambda i, j: (0, 0)),
                pl.BlockSpec((tn, D), lambda i, j: (j, 0)),
                pl.BlockSpec((128, D), lambda i, j: (0, 0))]
    out_specs = [pl.BlockSpec((tm, tn), lambda i, j: (i, j)),
                 pl.BlockSpec((tm, 128), lambda i, j: (i, 0)),
                 pl.BlockSpec((tm, D), lambda i, j: (i, 0))]
    out_shape = [jax.ShapeDtypeStruct((S, N_MAIN), F32), jax.ShapeDtypeStruct((S, 128), F32),
                 jax.ShapeDtypeStruct((S, D), BF16)]
    operands, scratch = (x, pre_w, w_t, wg_t), []
    if blk is not None:
        in_specs, out_specs, operands = in_specs + [ANY], out_specs + [ANY], operands + (blk,)
        out_shape = out_shape + [jax.ShapeDtypeStruct((N_DEV,) + blk.shape, blk.dtype)]
        scratch = [pltpu.SemaphoreType.DMA((7,)), pltpu.SemaphoreType.DMA((7,)), pltpu.SemaphoreType.DMA]
    return pl.pallas_call(
        body, name="inproj_fwd", grid=(ni, nj), in_specs=in_specs, out_specs=out_specs, out_shape=out_shape,
        scratch_shapes=scratch, compiler_params=_params(("arbitrary", "arbitrary")),
    )(*operands)


def _inproj_bwd(d_main, d_if, w_t, wg_t, x, pre_w, dx_tail, parts=None):
    S, D = x.shape
    tm, tk = min(S, 1024), 1152
    ni, nk = S // tm, N_MAIN // tk

    def body(d_ref, dg_ref, w_ref, wg_ref, x_ref, pw_ref, dt_ref, *rest):
        i, k = pl.program_id(0), pl.program_id(1)
        if parts is None:
            dx_ref, gpw_ref, acc_ref = rest
        else:
            p_ref, dx_ref, gpw_ref, o_ref, acc_ref, send_sems, recv_sems, local_sem = rest
            _around_grid(lambda: _copies_between_chips(p_ref, o_ref, send_sems, recv_sems, local_sem),
                         (i == 0) & (k == 0), (i == ni - 1) & (k == nk - 1))

        @pl.when(k == 0)
        def _():
            acc_ref[...] = _dot(dg_ref[...], wg_ref[...])

        acc_ref[...] += _dot(d_ref[...], w_ref[...])

        @pl.when(k == nk - 1)
        def _():
            xf = x_ref[...]
            r = lax.rsqrt(jnp.mean(xf * xf, axis=-1, keepdims=True) + EPS)
            xn = xf * r
            du = acc_ref[...]
            gw = du * pw_ref[...]
            dx_ref[...] = dt_ref[...] + r * (gw - xn * jnp.mean(gw * xn, axis=-1, keepdims=True))
            part = jnp.sum(du * xn, axis=0, keepdims=True)

            @pl.when(i == 0)
            def _():
                gpw_ref[...] = part

            @pl.when(i > 0)
            def _():
                gpw_ref[...] += part

    in_specs = [pl.BlockSpec((tm, tk), lambda i, k: (i, k)),
                pl.BlockSpec((tm, 128), lambda i, k: (i, 0)),
                pl.BlockSpec((tk, D), lambda i, k: (k, 0)),
                pl.BlockSpec((128, D), lambda i, k: (0, 0)),
                pl.BlockSpec((tm, D), lambda i, k: (i, 0)),
                pl.BlockSpec((1, D), lambda i, k: (0, 0)),
                pl.BlockSpec((tm, D), lambda i, k: (i, 0))]
    out_specs = [pl.BlockSpec((tm, D), lambda i, k: (i, 0)), pl.BlockSpec((1, D), lambda i, k: (0, 0))]
    out_shape = [jax.ShapeDtypeStruct((S, D), F32), jax.ShapeDtypeStruct((1, D), F32)]
    operands, scratch = (d_main, d_if, w_t, wg_t, x, pre_w, dx_tail), [pltpu.VMEM((tm, D), F32)]
    if parts is not None:
        in_specs, out_specs, operands = in_specs + [ANY], out_specs + [ANY], operands + (parts,)
        out_shape = out_shape + [jax.ShapeDtypeStruct(parts.shape, parts.dtype)]
        scratch = scratch + [pltpu.SemaphoreType.DMA((3,)), pltpu.SemaphoreType.DMA((3,)), pltpu.SemaphoreType.DMA]
    return pl.pallas_call(
        body, name="inproj_bwd", grid=(ni, nk), in_specs=in_specs, out_specs=out_specs, out_shape=out_shape,
        scratch_shapes=scratch, compiler_params=_params(("arbitrary", "arbitrary")),
    )(*operands)


def _matmul_tn(a, b, name, out_rows=None):
    S, M = a.shape
    N = b.shape[1]
    tmm = 1152 if M % 1152 == 0 else min(M, 1024)
    tk = min(S, 1024)
    nk = S // tk

    def body(a_ref, b_ref, o_ref):
        part = _dot_tn(a_ref[...].astype(BF16), b_ref[...].astype(BF16))

        @pl.when(pl.program_id(1) == 0)
        def _():
            o_ref[...] = part

        @pl.when(pl.program_id(1) > 0)
        def _():
            o_ref[...] += part

    return pl.pallas_call(
        body, name=name, grid=(M // tmm, nk),
        in_specs=[pl.BlockSpec((tk, tmm), lambda i, k: (k, i)),
                  pl.BlockSpec((tk, N), lambda i, k: (k, 0))],
        out_specs=pl.BlockSpec((tmm, N), lambda i, k: (i, 0)),
        out_shape=jax.ShapeDtypeStruct((out_rows or M, N), F32),
        compiler_params=_params(("arbitrary", "arbitrary")),
    )(a, b)


def _gate_rows_tn(d_if, u, g):
    S, D = u.shape
    tk = min(S, 1024)
    nk = S // tk

    def body(a_ref, b_ref, g_ref, o_ref, acc_ref):
        k = pl.program_id(0)
        part = _dot_tn(a_ref[...], b_ref[...])

        @pl.when(k == 0)
        def _():
            acc_ref[...] = part

        @pl.when(k > 0)
        def _():
            acc_ref[...] += part

        @pl.when(k == nk - 1)
        def _():
            o_ref[...] = acc_ref[0:8, :]

    return pl.pallas_call(
        body, name="gw_in_gates", grid=(nk,),
        in_specs=[pl.BlockSpec((tk, 128), lambda k: (k, 0)), pl.BlockSpec((tk, D), lambda k: (k, 0)), ANY],
        out_specs=pl.BlockSpec((8, D), lambda k: (N_MAIN // 8, 0)),
        out_shape=jax.ShapeDtypeStruct(g.shape, g.dtype),
        scratch_shapes=[pltpu.VMEM((128, D), F32)],
        input_output_aliases={2: 0},
        compiler_params=_params(("arbitrary",)),
    )(d_if, u, g)


def _half_mean(v, low):
    s_lo = jnp.sum(jnp.where(low, v, 0.0), axis=1, keepdims=True)
    s_hi = jnp.sum(jnp.where(low, 0.0, v), axis=1, keepdims=True)
    return jnp.where(low, s_lo, s_hi) * (1.0 / 64.0)


def _silu_grad(z, s):
    return s * (1.0 + z * (1.0 - s))


def _tail(y_sb, h_ml, proj, x, p, target, sb_nw, ml_nw, w_out, post_w, w_gate, b_gate, w_up):
    S, D = x.shape
    tm = 256

    def body(ysb_ref, hml_ref, sbz_ref, mlo_ref, mlz_ref, x_ref, p_ref, tg_ref, sbw_ref, mlw_ref, wo_ref, pw_ref,
             wg_ref, bg_ref, wu_ref,
             dx_ref, dysb_ref, dhml_ref, dsbz_ref, dmlo_ref, dmlz_ref, mix_ref, dy_ref, h1_ref, dgp_ref, dpu_ref,
             small_ref):
        lane = lax.broadcasted_iota(jnp.int32, (tm, 128), 1)
        low = lane < 64
        sb_saved, ml_saved, mixed = [], [], []
        for s in range(4):
            sl = slice(128 * s, 128 * s + 128)
            y = ysb_ref[:, sl]
            rs = lax.rsqrt(_half_mean(y * y, low) + EPS)
            n = y * rs
            z = sbz_ref[:, sl]
            sg = jax.nn.sigmoid(z)
            w = sbw_ref[:, sl]
            mixed.append((n * w) * (z * sg))
            sb_saved.append((rs, n, z, sg, w))
        for s in range(4):
            sl = slice(128 * s, 128 * s + 128)
            og = jax.nn.sigmoid(mlo_ref[:, sl])
            hh = hml_ref[:, sl]
            t = og * hh
            rs = lax.rsqrt(jnp.mean(t * t, axis=1, keepdims=True) + EPS)
            n = t * rs
            z = mlz_ref[:, sl]
            sg = jax.nn.sigmoid(z)
            w = mlw_ref[:, sl]
            mixed.append((n * w) * (z * sg))
            ml_saved.append((rs, n, z, sg, w, og, hh))
        mix = jnp.concatenate(mixed, axis=1).astype(BF16)
        mix_ref[...] = mix
        y = _dot(mix, wo_ref[...])
        rs_y = lax.rsqrt(jnp.mean(y * y, axis=1, keepdims=True) + EPS)
        yn = y * rs_y
        pw = pw_ref[...]
        h1 = x_ref[...] + yn * pw
        h1b = h1.astype(BF16)
        h1_ref[...] = h1b
        gate = jax.nn.sigmoid(_dot(h1b, wg_ref[...]) + bg_ref[...])
        pu = _dot(p_ref[...].astype(BF16), wu_ref[...])
        err = (h1 + gate * pu) - tg_ref[...]
        loss = 0.5 * jnp.sum(jnp.sum(err * err, axis=1, keepdims=True) * (1.0 / D))
        d_out = err * (1.0 / D)
        dpu_ref[...] = (d_out * gate).astype(BF16)
        dgp = (d_out * pu) * (gate * (1.0 - gate))
        dgpb = dgp.astype(BF16)
        dgp_ref[...] = dgpb
        d_h1 = d_out + _dot_nt(dgpb, wg_ref[...])
        dx_ref[...] = d_h1
        gwy = d_h1 * pw
        d_y = rs_y * (gwy - yn * jnp.mean(gwy * yn, axis=1, keepdims=True))
        d_yb = d_y.astype(BF16)
        dy_ref[...] = d_yb
        d_mix = _dot_nt(d_yb, wo_ref[...])
        g_nw = []
        for s in range(4):
            sl = slice(128 * s, 128 * s + 128)
            rs, n, z, sg, w = sb_saved[s]
            da = d_mix[:, sl]
            act = z * sg
            dsbz_ref[:, sl] = (da * (n * w) * _silu_grad(z, sg)).astype(BF16)
            dn = da * w * act
            g_nw.append(jnp.sum(da * act * n, axis=0, keepdims=True))
            dysb_ref[:, sl] = rs * (dn - n * _half_mean(dn * n, low))
        for s in range(4):
            sl = slice(128 * s, 128 * s + 128)
            rs, n, z, sg, w, og, hh = ml_saved[s]
            da = d_mix[:, 512 + 128 * s:512 + 128 * s + 128]
            act = z * sg
            dmlz_ref[:, sl] = (da * (n * w) * _silu_grad(z, sg)).astype(BF16)
            dn = da * w * act
            g_nw.append(jnp.sum(da * act * n, axis=0, keepdims=True))
            dt = rs * (dn - n * jnp.mean(dn * n, axis=1, keepdims=True))
            dmlo_ref[:, sl] = (dt * hh * (og * (1.0 - og))).astype(BF16)
            dhml_ref[:, sl] = dt * og
        upd = jnp.concatenate([
            jnp.sum(d_h1 * yn, axis=0, keepdims=True),
            jnp.sum(dgp, axis=0, keepdims=True),
            jnp.concatenate(g_nw, axis=1),
            jnp.full((1, D), loss, F32),
            jnp.zeros((4, D), F32)], axis=0)

        @pl.when(pl.program_id(0) == 0)
        def _():
            small_ref[...] = upd

        @pl.when(pl.program_id(0) > 0)
        def _():
            small_ref[...] += upd

    def rows(width, col=0):
        return pl.BlockSpec((tm, width), lambda i: (i, col))

    def whole(a):
        return pl.BlockSpec(a.shape, lambda i: (0, 0))

    return pl.pallas_call(
        body, name="tail", grid=(S // tm,),
        in_specs=[rows(512), rows(512), rows(512, 3), rows(512, 7), rows(512, 8), rows(D), rows(256), rows(D),
                  whole(sb_nw), whole(ml_nw), whole(w_out), whole(post_w), whole(w_gate), whole(b_gate), whole(w_up)],
        out_specs=[rows(D), rows(512), rows(512), rows(512), rows(512), rows(512), rows(D), rows(D), rows(D), rows(D),
                   rows(D), pl.BlockSpec((8, D), lambda i: (0, 0))],
        out_shape=[jax.ShapeDtypeStruct((S, D), F32), jax.ShapeDtypeStruct((S, 512), F32),
                   jax.ShapeDtypeStruct((S, 512), F32)] + [jax.ShapeDtypeStruct((S, 512), BF16)] * 3
        + [jax.ShapeDtypeStruct((S, D), BF16)] * 5 + [jax.ShapeDtypeStruct((8, D), F32)],
        compiler_params=_params(("arbitrary",)),
    )(y_sb, h_ml, proj, proj, proj, x, p, target, sb_nw, ml_nw, w_out, post_w, w_gate, b_gate, w_up)


def _local_step(x, p, target, pre_w, w_t, wg_t, conv_w, conv_b, gbias, sb_nw, ml_nw, post_w, b_gate, late,
                exchange=None):
    if callable(late[1]):
        proj, proj_g, u, gathered = _inproj_fwd(x, pre_w, w_t, wg_t, late[0])
        w_out, w_gate, w_up = late[1](gathered)
    else:
        proj, proj_g, u = _inproj_fwd(x, pre_w, w_t, wg_t)
        w_out, w_gate, w_up = late
    y_sb, tot = _sb_fwd(proj)
    qkv = _ml_prep(proj, conv_w, conv_b)
    h_ml, cst, nm = _ml_fwd(qkv, proj_g, gbias)
    dx_tail, d_ysb, d_hml, d_sbz, d_mlo, d_mlz, mix, d_y, h1, dgp, dpu, small = _tail(
        y_sb, h_ml, proj, x, p, target, sb_nw, ml_nw, w_out, post_w, w_gate, b_gate, w_up)
    dq, dk, dv = _sb_bwd(proj, tot, d_ysb)
    dqc, dks, dmlv, dif, gif = _ml_bwd(qkv, proj_g, gbias, cst, nm, d_hml)
    dmlqk, g_cw, g_cb = _ml_prep_bwd(proj, conv_w, conv_b, dqc, dks)
    d_main = jnp.concatenate([dq, dk, dv, d_sbz, dmlqk, dmlv, d_mlo, d_mlz], axis=1)
    d_if = dif.astype(BF16)
    grads = dict(
        w_in_t=_gate_rows_tn(d_if, u, _matmul_tn(d_main, u, "gw_in", N_IN)),
        w_out=_matmul_tn(mix, d_y, "gw_out"), w_gate=_matmul_tn(h1, dgp, "gw_gate"), w_up=_matmul_tn(p, dpu, "gw_up"),
        conv_w=g_cw, conv_b=g_cb, gif=gif)
    parts = exchange(grads) if exchange else None
    return _inproj_bwd(d_main, d_if, w_t, wg_t, x, pre_w, dx_tail, parts), grads, small


def _all_gather(a, b):
    def body(a_ref, b_ref, oa_ref, ob_ref, send_sems, recv_sems, local_sems):
        x, y, c = _place()
        me, sibling = (x, y, c), (x, y, 1 - c)
        chips = [(1 - x, y), (x, 1 - y), (1 - x, 1 - y)]
        pairs = ((a_ref, oa_ref), (b_ref, ob_ref))

        def copies(k, block, to, from_input=False):
            slot = _block_of(*block)
            return [pltpu.make_async_remote_copy(
                src_ref=src if from_input else out.at[slot], dst_ref=out.at[slot],
                send_sem=send_sems.at[t, k], recv_sem=recv_sems.at[t, k], device_id=to, device_id_type=MESH)
                for t, (src, out) in enumerate(pairs)]

        mine = [pltpu.make_async_copy(src, out.at[_block_of(*me)], local_sems.at[t])
                for t, (src, out) in enumerate(pairs)]
        for cp in mine:
            cp.start()
        first = copies(0, me, sibling, True)
        for j, chip in enumerate(chips):
            first += copies(1 + j, me, (*chip, c), True)
        for cp in first:
            cp.start()
        passed = []
        for j, chip in enumerate(chips):
            for cp in copies(1 + j, (*chip, c), me):
                cp.wait_recv()
            fwd = copies(4 + j, (*chip, c), sibling)
            for cp in fwd:
                cp.start()
            passed += fwd
        for cp in copies(0, sibling, me):
            cp.wait_recv()
        for j, chip in enumerate(chips):
            for cp in copies(4 + j, (*chip, 1 - c), me):
                cp.wait_recv()
        for cp in first + passed:
            cp.wait_send()
        for cp in mine:
            cp.wait()

    return pl.pallas_call(
        body, name="all_gather",
        in_specs=[ANY, ANY], out_specs=[ANY, ANY],
        out_shape=[jax.ShapeDtypeStruct((N_DEV,) + a.shape, a.dtype), jax.ShapeDtypeStruct((N_DEV,) + b.shape, b.dtype)],
        scratch_shapes=[pltpu.SemaphoreType.DMA((2, 7)), pltpu.SemaphoreType.DMA((2, 7)), pltpu.SemaphoreType.DMA((2,))],
    )(a, b)


def _exchange_pair(g):
    def body(g_ref, og_ref, send_sems, recv_sems):
        x, y, c = _place()
        sent = [pltpu.make_async_remote_copy(
            src_ref=g_ref.at[k, 1 - c], dst_ref=og_ref.at[k], send_sem=send_sems.at[k], recv_sem=recv_sems.at[k],
            device_id=(x, y, 1 - c), device_id_type=MESH) for k in range(4)]
        for cp in sent:
            cp.start()
        for cp in sent:
            cp.wait()

    return pl.pallas_call(
        body, name="exchange_pair", in_specs=[ANY], out_specs=ANY,
        out_shape=jax.ShapeDtypeStruct((4,) + g.shape[2:], g.dtype),
        scratch_shapes=[pltpu.SemaphoreType.DMA((4,)), pltpu.SemaphoreType.DMA((4,))],
    )(g)


def _pair_sum(g, r, core, tr):
    _, _, R, D = g.shape

    def body(c_ref, g_ref, r_ref, o_ref):
        o_ref[...] = (g_ref[...] + r_ref[...]).astype(BF16)

    return pl.pallas_call(
        body, name="pair_sum",
        grid_spec=pltpu.PrefetchScalarGridSpec(
            num_scalar_prefetch=1, grid=(4, R // tr),
            in_specs=[pl.BlockSpec((None, None, tr, D), lambda k, i, c: (k, c[0], i, 0)),
                      pl.BlockSpec((None, tr, D), lambda k, i, c: (k, i, 0))],
            out_specs=pl.BlockSpec((None, tr, D), lambda k, i, c: (k, i, 0))),
        out_shape=jax.ShapeDtypeStruct((4, R, D), BF16),
        compiler_params=_params(("arbitrary", "arbitrary")),
    )(core, g, r)


ADAM_LR, ADAM_B1, ADAM_B2, ADAM_EPS, ADAM_WD, ADAM_STEP = 0.001, 0.9, 0.999, 1e-08, 0.01, 10


def _adamw(w, g, m, v):
    m = ADAM_B1 * m + (1.0 - ADAM_B1) * g
    v = ADAM_B2 * v + (1.0 - ADAM_B2) * (g * g)
    m_hat = m / (1.0 - ADAM_B1 ** ADAM_STEP)
    v_hat = v / (1.0 - ADAM_B2 ** ADAM_STEP)
    return -ADAM_LR * (m_hat / (jnp.sqrt(v_hat) + ADAM_EPS) + ADAM_WD * w), m, v


def _adam(parts, w, m, v, tr, name, small=None):
    R, D = w.shape
    n = parts.shape[0]
    steps = R // tr

    def body(p_ref, w_ref, m_ref, v_ref, *rest):
        if small is None:
            g_ref, d_ref, nm_ref, nv_ref = rest
        else:
            s_ref, g_ref, d_ref, nm_ref, nv_ref, o_ref, send_sems, recv_sems, local_sem = rest
            i = pl.program_id(0)
            _around_grid(lambda: _copies_to_all(s_ref, o_ref, send_sems, recv_sems, local_sem), i == 0, i == steps - 1)
        g = p_ref[0].astype(F32)
        for k in range(1, n):
            g = g + p_ref[k].astype(F32)
        g_ref[...] = g
        d_ref[...], nm_ref[...], nv_ref[...] = _adamw(w_ref[...], g, m_ref[...], v_ref[...])

    blk = pl.BlockSpec((tr, D), lambda i: (i, 0))
    in_specs = [pl.BlockSpec((n, tr, D), lambda i: (0, i, 0)), blk, blk, blk]
    out_specs, out_shape = [blk] * 4, [jax.ShapeDtypeStruct((R, D), F32)] * 4
    operands, scratch = (parts, w, m, v), []
    if small is not None:
        in_specs, out_specs, operands = in_specs + [ANY], out_specs + [ANY], operands + (small,)
        out_shape = out_shape + [jax.ShapeDtypeStruct((N_DEV,) + small.shape, small.dtype)]
        scratch = [pltpu.SemaphoreType.DMA((7,)), pltpu.SemaphoreType.DMA((7,)), pltpu.SemaphoreType.DMA]
    return pl.pallas_call(
        body, name=name, grid=(steps,), in_specs=in_specs, out_specs=out_specs, out_shape=out_shape,
        scratch_shapes=scratch, compiler_params=_params(("arbitrary",)),
    )(*operands)


ROWS_IN = 592
ROWS_BF16 = ROWS_IN + 128 + 128 + 32
ROWS_CONV = 16
ROWS_ALL = ROWS_BF16 + ROWS_CONV
ROW_TILE = 224


def _pad_rows(a, rows):
    return jnp.pad(a, ((0, rows - a.shape[0]), (0, 0)))


def _pack_shards(w_in, w_out, w_gate, w_up, conv_w):
    return jnp.concatenate([
        _pad_rows(w_in[0].T, ROWS_IN), w_out[0], w_gate[0], w_up[0].reshape(32, D_MODEL),
        _pad_rows(jnp.pad(conv_w[0].reshape(1, 512), ((0, 0), (0, 512))), ROWS_CONV)], axis=0)


def _unpack_shards(a):
    return (a[:SHARD_IN].T[None], a[ROWS_IN:ROWS_IN + 128][None], a[ROWS_IN + 128:ROWS_IN + 256][None],
            a[ROWS_IN + 256:ROWS_BF16].reshape(1, 256, 128), a[ROWS_BF16, :512].reshape(1, 4, 128))


def _adam_small(parts, ws, ms, vs):
    n = len(ws)

    def body(p_ref, *refs):
        ins, outs = refs[:3 * n], refs[3 * n:]
        g = p_ref[0]
        for k in range(1, N_DEV):
            g = g + p_ref[k]
        pieces = [g[4:5], g[5:6], g[6:7, 0:4], g[6:7, 4:8], g[2:3, :512], g[2:3, 512:], g[0:1], g[1:2]]
        for t, gt in enumerate(pieces):
            d, nm, nv = _adamw(ins[t][...], gt, ins[n + t][...], ins[2 * n + t][...])
            for o, val in zip(outs[4 * t:4 * t + 4], (gt, d, nm, nv)):
                o[...] = val
        outs[4 * n][...] = g[3:4, 0:1]

    shapes = [jax.ShapeDtypeStruct(w.shape, F32) for w in ws for _ in range(4)] + [jax.ShapeDtypeStruct((1, 1), F32)]
    res = pl.pallas_call(body, name="adam_small", out_shape=shapes)(parts, *ws, *ms, *vs)
    return [res[k:4 * n:4] for k in range(4)], res[4 * n]


def kernel(x, p, pre_norm_w, w_in, ml_conv_w, ml_conv_b, ml_i_bias, ml_f_bias, sb_norm_w, ml_norm_w, w_out, post_norm_w, ple_w_up, ple_w_gate, ple_b_gate, loss_target, m_pre_norm_w, m_w_in, m_ml_conv_w, m_ml_conv_b, m_ml_i_bias, m_ml_f_bias, m_sb_norm_w, m_ml_norm_w, m_w_out, m_post_norm_w, m_ple_w_up, m_ple_w_gate, m_ple_b_gate, v_pre_norm_w, v_w_in, v_ml_conv_w, v_ml_conv_b, v_ml_i_bias, v_ml_f_bias, v_sb_norm_w, v_ml_norm_w, v_w_out, v_post_norm_w, v_ple_w_up, v_ple_w_gate, v_ple_b_gate):
    D = D_MODEL
    w_pk = _pack_shards(w_in, w_out, ple_w_gate, ple_w_up, ml_conv_w)
    m_pk = _pack_shards(m_w_in, m_w_out, m_ple_w_gate, m_ple_w_up, m_ml_conv_w)
    v_pk = _pack_shards(v_w_in, v_w_out, v_ple_w_gate, v_ple_w_up, v_ml_conv_w)
    w_sm = [pre_norm_w, ml_conv_b, ml_i_bias, ml_f_bias, sb_norm_w, ml_norm_w, post_norm_w, ple_b_gate]
    m_sm = [m_pre_norm_w, m_ml_conv_b, m_ml_i_bias, m_ml_f_bias, m_sb_norm_w, m_ml_norm_w, m_post_norm_w, m_ple_b_gate]
    v_sm = [v_pre_norm_w, v_ml_conv_b, v_ml_i_bias, v_ml_f_bias, v_sb_norm_w, v_ml_norm_w, v_post_norm_w, v_ple_b_gate]

    w_bf = w_pk[:ROWS_BF16].astype(BF16)
    ga, gb = _all_gather(w_bf[:SHARD_IN], w_pk[ROWS_BF16:])
    w_in_t = ga.reshape(N_IN, D)
    wg_t = _pad_rows(w_in_t[N_MAIN:], 128)
    conv_w_f = gb[:, 0, :512].reshape(N_DEV, 4, 128).transpose(1, 0, 2).reshape(4, D)
    gbias = jnp.pad(jnp.concatenate([ml_i_bias, ml_f_bias], axis=1), ((0, 0), (0, 120)))

    def unpack_late(gl):
        return (gl[:, :128].reshape(D, D), gl[:, 128:256].reshape(D, D),
                gl[:, 256:].reshape(N_DEV, 256, 128).transpose(1, 0, 2).reshape(256, D))

    def exchange(g):
        g_in = g["w_in_t"].reshape(N_DEV, SHARD_IN, D)
        g_blocks = jnp.concatenate([
            jnp.pad(g_in, ((0, 0), (0, ROWS_IN - SHARD_IN), (0, 0))),
            g["w_out"].reshape(N_DEV, 128, D), g["w_gate"].reshape(N_DEV, 128, D),
            g["w_up"].reshape(256, N_DEV, 128).transpose(1, 0, 2).reshape(N_DEV, 32, D),
            jnp.pad(g["conv_w"].reshape(4, N_DEV, 128).transpose(1, 0, 2).reshape(N_DEV, 1, 512),
                    ((0, 0), (0, ROWS_CONV - 1), (0, 512))),
        ], axis=1).reshape(4, 2, ROWS_ALL, D)
        core = lax.axis_index("c").astype(jnp.int32).reshape(1)
        return _pair_sum(g_blocks, _exchange_pair(g_blocks), core, ROW_TILE)

    (dx, g_pre, parts), g, small = _local_step(
        x[0], p[0, 0], loss_target[0], pre_norm_w, w_in_t, wg_t, conv_w_f, ml_conv_b, gbias, sb_norm_w, ml_norm_w,
        post_norm_w, ple_b_gate, (w_bf[ROWS_IN:], unpack_late), exchange)

    g_small = jnp.concatenate([small[0:4], g_pre, g["conv_b"], jnp.pad(g["gif"][0:1], ((0, 0), (0, D - 128))),
                               jnp.zeros((1, D), F32)], axis=0)
    grad_pk, delta_pk, nm_pk, nv_pk, parts_sm = _adam(parts, w_pk, m_pk, v_pk, ROW_TILE, "adam", g_small)
    (grad_sm, delta_sm, nm_sm, nv_sm), loss = _adam_small(parts_sm, w_sm, m_sm, v_sm)

    def ordered(pk, sm):
        win, wout, wgate, wup, convw = _unpack_shards(pk)
        pre_w, conv_b, i_b, f_b, sb_nw, ml_nw, post_w, b_gate = sm
        return [pre_w, win, convw, conv_b, i_b, f_b, sb_nw, ml_nw, wout, post_w, wup, wgate, b_gate]

    return (loss[0, 0], dx[None], *ordered(grad_pk, grad_sm), *ordered(delta_pk, delta_sm), *ordered(nm_pk, nm_sm),
            *ordered(nv_pk, nv_sm))
```
